```python
import math
import jax, jax.numpy as jnp
from jax import lax
import numpy as np

D_MODEL = 1024
BATCH = 2
SEQ = 8192
DEPTH = 2
DEC_BATCH = 16
DEC_SEQ = 2048
PAST_LEN = 128

GROUP_W = 256
N_MIXERS = 4
MIX_W = N_MIXERS * GROUP_W
CONV_W = 4
RG_BLOCKS = 4
RG_BW = GROUP_W // RG_BLOCKS
RG_C = 8.0
POOL_WINDOWS = (2, 4, 8, 16)
POOL_GW = GROUP_W // len(POOL_WINDOWS)
DN_HEADS = 4
DN_DK = GROUP_W // DN_HEADS
DN_DV = GROUP_W // DN_HEADS
DN_CHUNK = 64
ATT_HEADS = 4
ATT_HD = GROUP_W // ATT_HEADS
ROT_DIM = ATT_HD // 4
ROPE_THETA = 500000.0
ATT_WINDOWS = (128, 512, 2048)
ATT_DILATIONS = (1, 4, 16)
ATT_BLK = 64
N_EXPERTS = 16
EXPERT_FF = 1024
EC_CAPACITY = 2
EPS = 1e-6
NEG = -1e30

IN_SPLITS = (GROUP_W, GROUP_W,
             GROUP_W,
             DN_HEADS * DN_DK, DN_HEADS * DN_DK, DN_HEADS * DN_DV,
             2 * DN_HEADS, 2 * DN_HEADS,
             DN_HEADS * DN_DV,
             ATT_HEADS * ATT_HD, ATT_HEADS * ATT_HD, ATT_HEADS * ATT_HD)
IN_COLS = sum(IN_SPLITS)

kernel_name = 'hybrid_bidir_encoder_parallel_groups'


def rmsnorm(x, g):
    xf = x.astype(jnp.float32)
    y = xf * lax.rsqrt(jnp.mean(xf * xf, axis=-1, keepdims=True) + EPS)
    return (y * g.astype(jnp.float32)).astype(x.dtype)


def group_rmsnorm(x, g, n_groups):
    B, S, W = x.shape
    xf = x.astype(jnp.float32).reshape(B, S, n_groups, W // n_groups)
    y = xf * lax.rsqrt(jnp.mean(xf * xf, axis=-1, keepdims=True) + EPS)
    return (y.reshape(B, S, W) * g.astype(jnp.float32)).astype(x.dtype)


def l2norm(x):
    return x * lax.rsqrt(jnp.sum(x * x, axis=-1, keepdims=True) + EPS)


def split_columns(p):
    offsets = [int(v) for v in np.cumsum(IN_SPLITS)[:-1]]
    return jnp.split(p, offsets, axis=-1)


def dwconv_centred(x, w):
    K = w.shape[0]
    S = x.shape[1]
    left = K // 2
    xp = jnp.pad(x, ((0, 0), (left, K - 1 - left), (0, 0)))
    y = xp[:, 0:S] * w[0]
    for j in range(1, K):
        y = y + xp[:, j:j + S] * w[j]
    return y


def linear_recurrence(a, b, reverse):
    def combine(lhs, rhs):
        a_l, b_l = lhs
        a_r, b_r = rhs
        return a_l * a_r, a_r * b_l + b_r
    _, h = lax.associative_scan(combine, (a, b), reverse=reverse, axis=1)
    return h


def rglru_mixer(x_rec, x_gate, conv_w, conv_b, wa, ba, wx, bx, lam):
    B, S, W = x_rec.shape
    u = dwconv_centred(x_rec, conv_w) + conv_b
    ub = u.reshape(B, S, RG_BLOCKS, RG_BW)
    uf = u.astype(jnp.float32)
    hs = []
    for d, rev in enumerate((False, True)):
        r = jax.nn.sigmoid((jnp.einsum('bsnc,ncd->bsnd', ub, wa[d]).reshape(B, S, W) + ba[d]).astype(jnp.float32))
        i = jax.nn.sigmoid((jnp.einsum('bsnc,ncd->bsnd', ub, wx[d]).reshape(B, S, W) + bx[d]).astype(jnp.float32))
        log_a = -RG_C * r * jax.nn.softplus(-lam[d].astype(jnp.float32))
        a = jnp.exp(log_a)
        b = jnp.sqrt(-jnp.expm1(2.0 * log_a)) * (i * uf)
        hs.append(linear_recurrence(a, b, rev))
    h = hs[0] + hs[1]
    return (jax.nn.gelu(x_gate.astype(jnp.float32)) * h).astype(x_rec.dtype)


def pool_mixer(xb, pool_w, pool_scale):
    B, S, W = xb.shape
    xf = xb.astype(jnp.float32)
    cs = jnp.concatenate([jnp.zeros((B, 1, W), jnp.float32), jnp.cumsum(xf, axis=1)], axis=1)
    t = jnp.arange(S)
    parts = []
    for gi, win in enumerate(POOL_WINDOWS):
        lo = jnp.maximum(t - win // 2, 0)
        hi = jnp.minimum(t + win - win // 2, S)
        c = slice(gi * POOL_GW, (gi + 1) * POOL_GW)
        mean = (cs[:, hi, c] - cs[:, lo, c]) / (hi - lo).astype(jnp.float32)[None, :, None]
        parts.append(mean - xf[:, :, c])
    p = jnp.stack(parts, axis=2).astype(xb.dtype)
    y = jnp.einsum('bsgc,gcd->bsgd', p, pool_w).reshape(B, S, W)
    return y * pool_scale


def chunk_gated_delta_rule(q, k, v, g, beta):
    B, S, H, DK = q.shape
    DV = v.shape[-1]
    C = DN_CHUNK
    N = S // C
    def chunks(x):
        return x.reshape(B, N, C, H, x.shape[-1]).transpose(0, 3, 1, 2, 4)
    qc, kc, vc = chunks(q), chunks(k), chunks(v)
    gc = g.reshape(B, N, C, H).transpose(0, 3, 1, 2)
    bc = beta.reshape(B, N, C, H).transpose(0, 3, 1, 2)
    gcum = jnp.cumsum(gc, axis=-1)
    tril = jnp.tril(jnp.ones((C, C), dtype=bool))
    strict = jnp.tril(jnp.ones((C, C), dtype=bool), -1)
    decay = jnp.exp(jnp.where(tril, gcum[..., :, None] - gcum[..., None, :], -jnp.inf))
    kb = kc * bc[..., None]
    lmat = jnp.where(strict, jnp.einsum('bhncd,bhnmd->bhncm', kb, kc) * decay, 0.0)
    amat = lmat + jnp.eye(C, dtype=jnp.float32)
    rhs = jnp.concatenate([vc * bc[..., None], kb * jnp.exp(gcum)[..., None]], axis=-1)
    sol = lax.linalg.triangular_solve(amat, rhs, left_side=True, lower=True, unit_diagonal=True)
    u, w = sol[..., :DV], sol[..., DV:]
    attn = jnp.where(tril, jnp.einsum('bhncd,bhnmd->bhncm', qc, kc) * decay, 0.0)
    qg = qc * jnp.exp(gcum)[..., None]
    kdec = kc * jnp.exp(gcum[..., -1:] - gcum)[..., None]
    glast = jnp.exp(gcum[..., -1])

    def step(state, xs):
        qg_i, kdec_i, u_i, w_i, attn_i, glast_i = xs
        v_new = u_i - jnp.einsum('bhck,bhkv->bhcv', w_i, state)
        o_i = jnp.einsum('bhck,bhkv->bhcv', qg_i, state) + jnp.einsum('bhcm,bhmv->bhcv', attn_i, v_new)
        state = state * glast_i[..., None, None] + jnp.einsum('bhck,bhcv->bhkv', kdec_i, v_new)
        return state, o_i

    xs = (jnp.moveaxis(qg, 2, 0), jnp.moveaxis(kdec, 2, 0), jnp.moveaxis(u, 2, 0),
          jnp.moveaxis(w, 2, 0), jnp.moveaxis(attn, 2, 0), jnp.moveaxis(glast, 2, 0))
    state0 = jnp.zeros((B, H, DK, DV), jnp.float32)
    _, o = lax.scan(step, state0, xs)
    return o.transpose(1, 0, 3, 2, 4).reshape(B, S, H, DV)


def gated_deltanet_mixer(q, k, v, gate, a_in, b_in, conv_w, A_log, dt_bias, norm_g):
    B, S, _ = q.shape
    H = DN_HEADS
    qkv = jax.nn.silu(dwconv_centred(jnp.concatenate([q, k, v], axis=-1), conv_w)).astype(jnp.float32)
    qf, kf, vf = jnp.split(qkv, [H * DN_DK, 2 * H * DN_DK], axis=-1)
    qf = l2norm(qf.reshape(B, S, H, DN_DK)) * (DN_DK ** -0.5)
    kf = l2norm(kf.reshape(B, S, H, DN_DK))
    vf = vf.reshape(B, S, H, DN_DV)
    a_f = a_in.astype(jnp.float32).reshape(B, S, 2, H)
    b_f = b_in.astype(jnp.float32).reshape(B, S, 2, H)
    outs = []
    for d, rev in enumerate((False, True)):
        g = -jnp.exp(A_log[d].astype(jnp.float32)) * jax.nn.softplus(a_f[:, :, d] + dt_bias[d].astype(jnp.float32))
        beta = jax.nn.sigmoid(b_f[:, :, d])
        if rev:
            od = chunk_gated_delta_rule(jnp.flip(qf, 1), jnp.flip(kf, 1), jnp.flip(vf, 1), jnp.flip(g, 1), jnp.flip(beta, 1))
            outs.append(jnp.flip(od, 1))
        else:
            outs.append(chunk_gated_delta_rule(qf, kf, vf, g, beta))
    o = outs[0] + outs[1]
    o = o * lax.rsqrt(jnp.mean(o * o, axis=-1, keepdims=True) + EPS) * norm_g.astype(jnp.float32)
    o = o.reshape(B, S, H * DN_DV) * jax.nn.silu(gate.astype(jnp.float32))
    return o.astype(q.dtype)


def rope_tables(S):
    inv = ROPE_THETA ** (-jnp.arange(0, ROT_DIM, 2, dtype=jnp.float32) / ROT_DIM)
    ang = jnp.arange(S, dtype=jnp.float32)[:, None] * inv[None, :]
    return jnp.cos(ang)[:, None, :], jnp.sin(ang)[:, None, :]


def partial_rope(x, cos, sin):
    half = ROT_DIM // 2
    x1 = x[..., :half]
    x2 = x[..., half:ROT_DIM]
    return jnp.concatenate([x1 * cos - x2 * sin, x2 * cos + x1 * sin, x[..., ROT_DIM:]], axis=-1)


def dilated_branch(q, k, v, dil, half):
    B, S, H, Dh = q.shape
    L = S // dil
    nblk = -(-L // ATT_BLK)
    Lp = nblk * ATT_BLK

    def to_sub(x):
        x = x.reshape(B, L, dil, H, Dh).transpose(0, 2, 3, 1, 4)
        return jnp.pad(x, ((0, 0), (0, 0), (0, 0), (0, Lp - L), (0, 0)))

    def band(x):
        xp = jnp.pad(to_sub(x), ((0, 0), (0, 0), (0, 0), (ATT_BLK, ATT_BLK), (0, 0)))
        return jnp.concatenate([xp[:, :, :, j * ATT_BLK:j * ATT_BLK + Lp].reshape(B, dil, H, nblk, ATT_BLK, Dh)
                                for j in range(3)], axis=4)

    qs = to_sub(q).reshape(B, dil, H, nblk, ATT_BLK, Dh)
    kb, vb = band(k), band(v)
    blk = jnp.arange(nblk)[:, None]
    qpos = blk * ATT_BLK + jnp.arange(ATT_BLK)[None, :]
    kpos = (blk - 1) * ATT_BLK + jnp.arange(3 * ATT_BLK)[None, :]
    rel = kpos[:, None, :] - qpos[:, :, None]
    valid = (jnp.abs(rel) <= half) & (kpos[:, None, :] >= 0) & (kpos[:, None, :] < L)
    scores = jnp.einsum('brhnqd,brhnkd->brhnqk', qs, kb)
    scores = jnp.where(valid, scores, NEG)
    m = jnp.max(scores, axis=-1)
    p = jnp.exp(scores - m[..., None])
    s = jnp.sum(p, axis=-1)
    o = jnp.einsum('brhnqk,brhnkd->brhnqd', p, vb)

    def from_sub(x):
        x = x.reshape((B, dil, H, Lp) + x.shape[5:])[:, :, :, :L]
        x = jnp.moveaxis(x, 3, 1)
        return x.reshape((B, S, H) + x.shape[4:])

    return from_sub(m), from_sub(s), from_sub(o)


def dilated_attention_mixer(q, k, v, cos, sin):
    B, S, _ = q.shape
    qf = partial_rope(q.reshape(B, S, ATT_HEADS, ATT_HD).astype(jnp.float32), cos, sin) * (ATT_HD ** -0.5)
    kf = partial_rope(k.reshape(B, S, ATT_HEADS, ATT_HD).astype(jnp.float32), cos, sin)
    vf = v.reshape(B, S, ATT_HEADS, ATT_HD).astype(jnp.float32)
    stats = [dilated_branch(qf, kf, vf, dil, win // (2 * dil)) for win, dil in zip(ATT_WINDOWS, ATT_DILATIONS)]
    m_all = stats[0][0]
    for m_i, _, _ in stats[1:]:
        m_all = jnp.maximum(m_all, m_i)
    w0 = jnp.exp(stats[0][0] - m_all)
    num = w0[..., None] * stats[0][2]
    den = w0 * stats[0][1]
    for m_i, s_i, o_i in stats[1:]:
        w_i = jnp.exp(m_i - m_all)
        num = num + w_i[..., None] * o_i
        den = den + w_i * s_i
    return (num / den[..., None]).reshape(B, S, ATT_HEADS * ATT_HD).astype(q.dtype)


def expert_choice_ffn(x, router_w, w_gate, w_up, w_down):
    B, S, D = x.shape
    T = B * S
    cap = max(1, EC_CAPACITY * T // N_EXPERTS)
    xt = x.reshape(T, D)
    aff = jax.nn.softmax((xt @ router_w).astype(jnp.float32), axis=-1)
    gsel, isel = lax.top_k(aff.T, cap)
    xe = xt[isel]
    h = jax.nn.silu(jnp.einsum('ecd,edf->ecf', xe, w_gate)) * jnp.einsum('ecd,edf->ecf', xe, w_up)
    ye = jnp.einsum('ecf,efd->ecd', h, w_down) * gsel[..., None].astype(x.dtype)
    y = jnp.zeros_like(xt).at[isel.reshape(-1)].add(ye.reshape(-1, D))
    return y.reshape(B, S, D)


def encoder(x, norm1_g, w_in, conv_a_w, conv_a_b, rg_wa, rg_ba, rg_wx, rg_bx, rg_lambda,
            pool_w, pool_scale, dn_conv_w, dn_A_log, dn_dt_bias, dn_norm_g, mix_norm_g, w_out,
            norm2_g, router_w, exp_w_gate, exp_w_up, exp_w_down, final_norm_g):
    cos, sin = rope_tables(x.shape[1])
    for l in range(DEPTH):
        h = rmsnorm(x, norm1_g[l])
        (a_x, a_g, b_x, c_q, c_k, c_v, c_a, c_b, c_g, d_q, d_k, d_v) = split_columns(h @ w_in[l])
        y_a = rglru_mixer(a_x, a_g, conv_a_w[l], conv_a_b[l], rg_wa[l], rg_ba[l], rg_wx[l], rg_bx[l], rg_lambda[l])
        y_b = pool_mixer(b_x, pool_w[l], pool_scale[l])
        y_c = gated_deltanet_mixer(c_q, c_k, c_v, c_g, c_a, c_b, dn_conv_w[l], dn_A_log[l], dn_dt_bias[l], dn_norm_g[l])
        y_d = dilated_attention_mixer(d_q, d_k, d_v, cos, sin)
        mix = group_rmsnorm(jnp.concatenate([y_a, y_b, y_c, y_d], axis=-1), mix_norm_g[l], N_MIXERS)
        x = x + mix @ w_out[l]
        x = x + expert_choice_ffn(rmsnorm(x, norm2_g[l]), router_w[l], exp_w_gate[l], exp_w_up[l], exp_w_down[l])
    return rmsnorm(x, final_norm_g)


def setup_inputs(seed: int = 0) -> dict:
    key = jax.random.key(seed)
    ks = jax.random.split(key, 26)
    f32 = jnp.float32
    L, D, F, E = DEPTH, D_MODEL, EXPERT_FF, N_EXPERTS

    def normal(i, shape, scale):
        return scale * jax.random.normal(ks[i], shape, f32)

    def gain(i, shape):
        return 1.0 + 0.02 * jax.random.normal(ks[i], shape, f32)

    a_c = jax.random.uniform(ks[10], (L, 2, GROUP_W), f32, 0.9, 0.999) ** (1.0 / RG_C)
    dt = jnp.exp(jax.random.uniform(ks[15], (L, 2, DN_HEADS), f32, math.log(1e-3), math.log(1e-1)))
    return {
        'x_prompt': normal(0, (BATCH, SEQ, D), 1.0),
        'x_sample': normal(1, (DEC_BATCH, DEC_SEQ, D), 1.0),
        'norm1_g': gain(2, (L, D)),
        'w_in': normal(3, (L, D, IN_COLS), D ** -0.5),
        'conv_a_w': normal(4, (L, CONV_W, GROUP_W), CONV_W ** -0.5),
        'conv_a_b': normal(5, (L, GROUP_W), 0.02),
        'rg_wa': normal(6, (L, 2, RG_BLOCKS, RG_BW, RG_BW), RG_BW ** -0.5),
        'rg_ba': normal(7, (L, 2, GROUP_W), 0.02),
        'rg_wx': normal(8, (L, 2, RG_BLOCKS, RG_BW, RG_BW), RG_BW ** -0.5),
        'rg_bx': normal(9, (L, 2, GROUP_W), 0.02),
        'rg_lambda': jnp.log(a_c) - jnp.log1p(-a_c),
        'pool_w': normal(11, (L, len(POOL_WINDOWS), POOL_GW, POOL_GW), POOL_GW ** -0.5),
        'pool_scale': gain(12, (L, GROUP_W)),
        'dn_conv_w': normal(13, (L, CONV_W, DN_HEADS * (2 * DN_DK + DN_DV)), CONV_W ** -0.5),
        'dn_A_log': jnp.log(jax.random.uniform(ks[14], (L, 2, DN_HEADS), f32, 1.0, 16.0)),
        'dn_dt_bias': dt + jnp.log(-jnp.expm1(-dt)),
        'dn_norm_g': gain(16, (L, DN_DV)),
        'mix_norm_g': gain(17, (L, MIX_W)),
        'w_out': normal(18, (L, MIX_W, D), MIX_W ** -0.5),
        'norm2_g': gain(19, (L, D)),
        'router_w': normal(20, (L, D, E), D ** -0.5),
        'exp_w_gate': normal(21, (L, E, D, F), D ** -0.5),
        'exp_w_up': normal(22, (L, E, D, F), D ** -0.5),
        'exp_w_down': normal(23, (L, E, F, D), F ** -0.5),
        'final_norm_g': gain(24, (D,)),
    }


def reference(x_prompt, x_sample, norm1_g, w_in, conv_a_w, conv_a_b, rg_wa, rg_ba, rg_wx, rg_bx, rg_lambda,
              pool_w, pool_scale, dn_conv_w, dn_A_log, dn_dt_bias, dn_norm_g, mix_norm_g, w_out,
              norm2_g, router_w, exp_w_gate, exp_w_up, exp_w_down, final_norm_g):
    weights = (norm1_g, w_in, conv_a_w, conv_a_b, rg_wa, rg_ba, rg_wx, rg_bx, rg_lambda,
               pool_w, pool_scale, dn_conv_w, dn_A_log, dn_dt_bias, dn_norm_g, mix_norm_g, w_out,
               norm2_g, router_w, exp_w_gate, exp_w_up, exp_w_down, final_norm_g)
    y_prompt = encoder(x_prompt, *weights)
    y_sample = encoder(x_sample, *weights)
    return (y_prompt, y_sample)
```

```python
import functools
import numpy as np
import jax
import jax.numpy as jnp
from jax import lax
from jax.experimental import pallas as pl
from jax.experimental.pallas import tpu as pltpu

F32, BF16, I32 = jnp.float32, jnp.bfloat16, jnp.int32
HI = lax.Precision.HIGHEST

D_MODEL = 1024
DEPTH = 2
GROUP_W = 256
RG_C = 8.0
POOL_WINDOWS = (2, 4, 8, 16)
DN_HEADS = 4
DN_DK = 64
DN_CHUNK = 64
ATT_HD = 64
ROT_DIM = 16
ROPE_THETA = 500000.0
ATT_WINDOWS = (128, 512, 2048)
ATT_DILATIONS = (1, 4, 16)
N_EXPERTS = 16
EC_CAPACITY = 2
EPS = 1e-6
NEG = -1e30

COL_A, COL_B, COL_CQ, COL_CG, COL_D, COL_CAB, COL_END = 0, 512, 768, 1536, 1792, 2560, 2688

ROW_TILE = 512
HALO = 8
DELTA_CB = 4
ATT_QB = 128
TOK_BLK = 256
GATHER_W = TOK_BLK + 16
VMEM_LIMIT = 56 * 1024 * 1024


def _cparams(sem):
    return pltpu.CompilerParams(dimension_semantics=sem, vmem_limit_bytes=VMEM_LIMIT)


def _dot(a, b):
    return jnp.dot(a.astype(BF16), b.astype(BF16), preferred_element_type=F32)


def _dot_hi(a, b):
    return jnp.dot(a, b, precision=HI, preferred_element_type=F32)


def _dot_nt(a, b):
    return lax.dot_general(a.astype(BF16), b.astype(BF16), (((1,), (1,)), ((), ())), preferred_element_type=F32)


def _dot_tn(a, b):
    return lax.dot_general(a.astype(BF16), b.astype(BF16), (((0,), (0,)), ((), ())), preferred_element_type=F32)


def _sigmoid(x):
    return 1.0 / (1.0 + jnp.exp(-x))


def _softplus(x):
    return jnp.maximum(x, 0.0) + jnp.log1p(jnp.exp(-jnp.abs(x)))


def _shift_rows(e, k):
    n = e.shape[0]
    return e if k % n == 0 else pltpu.roll(e, (-k) % n, axis=0)


def _with_halo(cur_ref, prev_ref, next_ref, first, last):
    prev = jnp.where(first, 0.0, prev_ref[0])
    nxt = jnp.where(last, 0.0, next_ref[0])
    return jnp.concatenate([prev, cur_ref[0], nxt], axis=0)


def _halo_specs(ts, width, S, blk_of):
    per = ts // HALO
    last = S // HALO - 1
    cur = pl.BlockSpec((1, ts, width), lambda b, c: (b, blk_of(c), 0))
    prev = pl.BlockSpec((1, HALO, width), lambda b, c: (b, jnp.maximum(blk_of(c) * per - 1, 0), 0))
    nxt = pl.BlockSpec((1, HALO, width), lambda b, c: (b, jnp.minimum((blk_of(c) + 1) * per, last), 0))
    return cur, prev, nxt


def _inproj_kernel(x_ref, g_ref, w_ref, c_ref, s1_ref, s2_ref, pa_ref, pb_ref, pq_ref, pg_ref, pd_ref, pcab_ref):
    x = x_ref[...]
    h = x * lax.rsqrt(jnp.mean(x * x, axis=-1, keepdims=True) + EPS) * g_ref[...]
    hb = h.astype(BF16)

    def mm(lo, hi):
        return jnp.dot(hb, w_ref[:, lo:hi], preferred_element_type=F32)

    pa_ref[...] = mm(COL_A, COL_B)
    pb_ref[...] = mm(COL_B, COL_CQ)
    pq_ref[...] = mm(COL_CQ, COL_CG)
    pg_ref[...] = mm(COL_CG, COL_D)
    pcab_ref[...] = mm(COL_CAB, COL_END)
    c, s1, s2 = c_ref[...], s1_ref[...], s2_ref[...]
    for blk in range(4):
        lo = COL_D + 128 * blk
        y = mm(lo, lo + 128)
        y = y * c + pltpu.roll(y, 128 - ROT_DIM // 2, axis=1) * s1 + pltpu.roll(y, ROT_DIM // 2, axis=1) * s2
        if blk < 2:
            y = y * (ATT_HD ** -0.5)
        pd_ref[:, 128 * blk:128 * blk + 128] = y
    pd_ref[:, 512:768] = mm(COL_D + 512, COL_D + 768)


def _inproj(xt, g1, w_cat, rope_c, rope_s1, rope_s2, S):
    T = xt.shape[0]
    tm = min(ROW_TILE, S)
    per_seq = S // tm
    widths = (512, 256, 768, 256, 768, 128)
    row = lambda w: pl.BlockSpec((tm, w), lambda i: (i, 0))
    rope = pl.BlockSpec((tm, 128), lambda i: (i % per_seq, 0))
    return pl.pallas_call(
        _inproj_kernel,
        out_shape=[jax.ShapeDtypeStruct((T, w), F32) for w in widths],
        grid=(T // tm,),
        in_specs=[row(D_MODEL), pl.BlockSpec((1, D_MODEL), lambda i: (0, 0)),
                  pl.BlockSpec((D_MODEL, COL_END), lambda i: (0, 0)), rope, rope, rope],
        out_specs=[row(w) for w in widths],
        compiler_params=_cparams(("parallel",)),
        name="inproj",
    )(xt, g1, w_cat, rope_c, rope_s1, rope_s2)


def _rglru_kernel(*refs, reverse, ts, nc):
    if reverse:
        (cur_ref, prev_ref, next_ref, gate_ref, hf_ref, cw_ref, cb_ref, wa_ref, ba_ref, wx_ref, bx_ref, lam_ref,
         out_ref, a_ref, b_ref, carry_ref) = refs
    else:
        (cur_ref, prev_ref, next_ref, cw_ref, cb_ref, wa_ref, ba_ref, wx_ref, bx_ref, lam_ref,
         out_ref, a_ref, b_ref, carry_ref) = refs
    c = pl.program_id(1)
    blk = (nc - 1 - c) if reverse else c
    e = _with_halo(cur_ref, prev_ref, next_ref, blk == 0, blk == nc - 1)
    cw = cw_ref[...]
    sl = slice(HALO, HALO + ts)
    u = (cw[0:1] * _shift_rows(e, -2)[sl] + cw[1:2] * _shift_rows(e, -1)[sl] + cw[2:3] * e[sl]
         + cw[3:4] * _shift_rows(e, 1)[sl]) + cb_ref[...]
    r = _sigmoid(_dot(u, wa_ref[...]) + ba_ref[...])
    i = _sigmoid(_dot(u, wx_ref[...]) + bx_ref[...])
    log_a = -RG_C * r * _softplus(-lam_ref[...])
    a_ref[...] = jnp.exp(log_a)
    b_ref[...] = jnp.sqrt(1.0 - jnp.exp(2.0 * log_a)) * (i * u)

    @pl.when(c == 0)
    def _():
        carry_ref[...] = jnp.zeros_like(carry_ref)

    row = lax.broadcasted_iota(I32, (HALO, GROUP_W), 0)
    nt = ts // HALO

    def body(it, carry):
        ti = (nt - 1 - it) if reverse else it
        st = pl.multiple_of(ti * HALO, HALO)
        a = a_ref[pl.ds(st, HALO), :]
        b = b_ref[pl.ds(st, HALO), :]
        for d in (1, 2, 4):
            k = d if reverse else -d
            valid = (row < HALO - d) if reverse else (row >= d)
            b = jnp.where(valid, a * _shift_rows(b, k) + b, b)
            a = jnp.where(valid, a * _shift_rows(a, k), a)
        h = a * carry + b
        if reverse:
            g = gate_ref[0, pl.ds(st, HALO), :]
            cdf = 0.5 * (1.0 + jnp.tanh(np.float32(np.sqrt(2.0 / np.pi)) * (g + 0.044715 * (g * g * g))))
            out_ref[0, pl.ds(st, HALO), :] = (g * cdf) * (hf_ref[0, pl.ds(st, HALO), :] + h)
            return jnp.broadcast_to(h[0:1], h.shape)
        out_ref[0, pl.ds(st, HALO), :] = h
        return jnp.broadcast_to(h[HALO - 1:HALO], h.shape)

    carry_ref[...] = lax.fori_loop(0, nt, body, carry_ref[...], unroll=4)


def _rglru(pa3, hf, cw, cb, wa, ba, wx, bx, lam, reverse):
    B, S, _ = pa3.shape
    ts = min(ROW_TILE, S)
    nc = S // ts
    blk_of = (lambda c: nc - 1 - c) if reverse else (lambda c: c)
    cur, prev, nxt = _halo_specs(ts, GROUP_W, S, blk_of)
    tile = pl.BlockSpec((1, ts, GROUP_W), lambda b, c: (b, blk_of(c), 0))
    const = lambda shape: pl.BlockSpec(shape, lambda b, c: (0,) * len(shape))
    in_specs = [cur, prev, nxt]
    args = [pa3, pa3, pa3]
    if reverse:
        in_specs += [pl.BlockSpec((1, ts, GROUP_W), lambda b, c: (b, blk_of(c), 1)), tile]
        args += [pa3, hf]
    in_specs += [const((4, GROUP_W)), const((1, GROUP_W)), const((GROUP_W, GROUP_W)), const((1, GROUP_W)),
                 const((GROUP_W, GROUP_W)), const((1, GROUP_W)), const((1, GROUP_W))]
    args += [cw, cb, wa, ba, wx, bx, lam]
    return pl.pallas_call(
        functools.partial(_rglru_kernel, reverse=reverse, ts=ts, nc=nc),
        out_shape=jax.ShapeDtypeStruct((B, S, GROUP_W), F32),
        grid=(B, nc),
        in_specs=in_specs,
        out_specs=tile,
        scratch_shapes=[pltpu.VMEM((ts, GROUP_W), F32), pltpu.VMEM((ts, GROUP_W), F32),
                        pltpu.VMEM((HALO, GROUP_W), F32)],
        compiler_params=_cparams(("parallel", "arbitrary")),
        name="rglru_bwd" if reverse else "rglru_fwd",
    )(*args)


def _pool_kernel(cur_ref, prev_ref, next_ref, w_ref, sc_ref, out_ref, *, ts, nc, S):
    c = pl.program_id(1)
    e = _with_halo(cur_ref, prev_ref, next_ref, c == 0, c == nc - 1)
    sl = slice(HALO, HALO + ts)
    a2 = e + _shift_rows(e, 1)
    a4 = a2 + _shift_rows(a2, 2)
    a8 = a4 + _shift_rows(a4, 4)
    a16 = a8 + _shift_rows(a8, 8)
    sums = [_shift_rows(a, -(w // 2))[sl] for a, w in zip((a2, a4, a8, a16), POOL_WINDOWS)]
    gi = lax.broadcasted_iota(I32, (ts, GROUP_W), 1) // (GROUP_W // len(POOL_WINDOWS))
    ssum = jnp.where(gi == 0, sums[0], jnp.where(gi == 1, sums[1], jnp.where(gi == 2, sums[2], sums[3])))
    hw = jnp.where(gi == 0, 1, jnp.where(gi == 1, 2, jnp.where(gi == 2, 4, 8)))
    t = c * ts + lax.broadcasted_iota(I32, (ts, GROUP_W), 0)
    cnt = (jnp.minimum(t + hw, S) - jnp.maximum(t - hw, 0)).astype(F32)
    p = ssum / cnt - e[sl]
    out_ref[0] = _dot(p, w_ref[...]) * sc_ref[...]


def _pool(pb3, w_bd, scale):
    B, S, _ = pb3.shape
    ts = min(ROW_TILE, S)
    nc = S // ts
    cur, prev, nxt = _halo_specs(ts, GROUP_W, S, lambda c: c)
    return pl.pallas_call(
        functools.partial(_pool_kernel, ts=ts, nc=nc, S=S),
        out_shape=jax.ShapeDtypeStruct((B, S, GROUP_W), F32),
        grid=(B, nc),
        in_specs=[cur, prev, nxt, pl.BlockSpec((GROUP_W, GROUP_W), lambda b, c: (0, 0)),
                  pl.BlockSpec((1, GROUP_W), lambda b, c: (0, 0))],
        out_specs=pl.BlockSpec((1, ts, GROUP_W), lambda b, c: (b, c, 0)),
        compiler_params=_cparams(("parallel", "parallel")),
        name="pool",
    )(pb3, pb3, pb3, w_bd, scale)


def _head_sum_matrix(n, group):
    r = lax.broadcasted_iota(I32, (n, n), 0) // group
    c = lax.broadcasted_iota(I32, (n, n), 1) // group
    return (r == c).astype(F32)


def _dnprep_kernel(cur_ref, prev_ref, next_ref, cab_ref, cw_ref, alog_ref, dtb_ref, eg_ref, eb_ref,
                   q_ref, k_ref, v_ref, gcf_ref, gcb_ref, bef_ref, beb_ref, *, ts, nc):
    c = pl.program_id(1)
    e = _with_halo(cur_ref, prev_ref, next_ref, c == 0, c == nc - 1)
    cw = cw_ref[...]
    sl = slice(HALO, HALO + ts)
    y = (cw[0:1] * _shift_rows(e, -2)[sl] + cw[1:2] * _shift_rows(e, -1)[sl] + cw[2:3] * e[sl]
         + cw[3:4] * _shift_rows(e, 1)[sl])
    y = y * _sigmoid(y)
    q, k = y[:, 0:GROUP_W], y[:, GROUP_W:2 * GROUP_W]
    hs = _head_sum_matrix(GROUP_W, DN_DK)
    q_ref[0] = q * lax.rsqrt(_dot_hi(q * q, hs) + EPS) * (DN_DK ** -0.5)
    k_ref[0] = k * lax.rsqrt(_dot_hi(k * k, hs) + EPS)
    v_ref[0] = y[:, 2 * GROUP_W:3 * GROUP_W]
    cab = cab_ref[0]
    g = -jnp.exp(alog_ref[...]) * _softplus(cab + dtb_ref[...])
    beta = _sigmoid(cab)
    bef_ref[0] = _dot_hi(beta, eb_ref[0])
    beb_ref[0] = _dot_hi(beta, eb_ref[1])
    r = lax.broadcasted_iota(I32, (ts, ts), 0)
    cc = lax.broadcasted_iota(I32, (ts, ts), 1)
    same = (r // DN_CHUNK) == (cc // DN_CHUNK)
    gcf_ref[0] = _dot_hi((same & (r >= cc)).astype(F32), _dot_hi(g, eg_ref[0]))
    gcb_ref[0] = _dot_hi((same & (r <= cc)).astype(F32), _dot_hi(g, eg_ref[1]))


def _dnprep(pq3, pcab3, cw, alog_row, dtb_row, eg, eb):
    B, S, _ = pq3.shape
    ts = min(ROW_TILE, S)
    nc = S // ts
    cur, prev, nxt = _halo_specs(ts, 3 * GROUP_W, S, lambda c: c)
    tile = pl.BlockSpec((1, ts, GROUP_W), lambda b, c: (b, c, 0))
    const = lambda shape: pl.BlockSpec(shape, lambda b, c: (0,) * len(shape))
    return pl.pallas_call(
        functools.partial(_dnprep_kernel, ts=ts, nc=nc),
        out_shape=[jax.ShapeDtypeStruct((B, S, GROUP_W), F32)] * 7,
        grid=(B, nc),
        in_specs=[cur, prev, nxt, pl.BlockSpec((1, ts, 128), lambda b, c: (b, c, 0)),
                  const((4, 3 * GROUP_W)), const((1, 128)), const((1, 128)),
                  const((2, 128, GROUP_W)), const((2, 128, GROUP_W))],
        out_specs=[tile] * 7,
        compiler_params=_cparams(("parallel", "parallel")),
        name="dnprep",
    )(pq3, pq3, pq3, pcab3, cw, alog_row, dtb_row, eg, eb)


def _delta_kernel(qf, kf, vf, gcf, bef, grf, qb, kb, vb, gcb, beb, grb, bd_ref, of_ref, ob_ref, s_ref, *, cb):
    j = pl.program_id(1)

    @pl.when(j == 0)
    def _():
        s_ref[...] = jnp.zeros_like(s_ref)

    C = DN_CHUNK
    bd = bd_ref[...] > 0.0
    c_idx = lax.broadcasted_iota(I32, (C, GROUP_W), 0)
    m_idx = lax.broadcasted_iota(I32, (C, GROUP_W), 1) % C
    eye = (c_idx == m_idx).astype(F32)

    def blockdiag(x):
        return jnp.where(bd, jnp.tile(x, (DN_HEADS, 1)), 0.0).astype(BF16)

    dirs = ((qf, kf, vf, gcf, bef, grf, of_ref), (qb, kb, vb, gcb, beb, grb, ob_ref))
    for d, (q_r, k_r, v_r, gc_r, be_r, gr_r, o_r) in enumerate(dirs):
        rev = d == 1
        tril = (c_idx <= m_idx) if rev else (c_idx >= m_idx)
        strict = (c_idx < m_idx) if rev else (c_idx > m_idx)
        last = 0 if rev else C - 1
        for ci in range(cb):
            cc = cb - 1 - ci if rev else ci
            rows = slice(cc * C, (cc + 1) * C)
            q, k, v = q_r[0, rows, :], k_r[0, rows, :], v_r[0, rows, :]
            gc, be, gr = gc_r[0, rows, :], be_r[0, rows, :], gr_r[0, cc]
            eg = jnp.exp(gc)
            kbeta = k * be
            kst = blockdiag(k)
            kkqk = _dot_nt(jnp.concatenate([kbeta, q], axis=0), kst)
            decay = jnp.exp(jnp.where(tril, gc - gr, -jnp.inf))
            lmat = jnp.where(strict, kkqk[:C] * decay, 0.0)
            attn = kkqk[C:] * decay
            p = -lmat
            x = eye + p
            p = _dot(p, blockdiag(p))
            for _ in range(4):
                px = _dot(jnp.concatenate([p, x], axis=0), blockdiag(p))
                p = px[:C]
                x = x + px[C:]
            x = x + _dot(x, blockdiag(p))
            u = _dot(x, blockdiag(v * be))
            w = _dot(x, blockdiag(kbeta * eg))
            state = s_ref[d]
            ws_qs = _dot(jnp.concatenate([w, q * eg], axis=0), state)
            v_new = u - ws_qs[:C]
            o_r[0, rows, :] = ws_qs[C:] + _dot(attn, blockdiag(v_new))
            kdec = k * jnp.exp(gc[last:last + 1] - gc)
            s_ref[d] = state * eg[last:last + 1] + jnp.where(bd, _dot_tn(kdec, v_new), 0.0)


def _delta(qn, kn, vv, gcf, gcb, bef, beb, grf, grb, bdmask):
    B, S, _ = qn.shape
    cb = DELTA_CB
    rb = cb * DN_CHUNK
    nb = S // rb
    f = lambda b, j: (b, j, 0)
    r = lambda b, j: (b, nb - 1 - j, 0)
    tf = pl.BlockSpec((1, rb, GROUP_W), f)
    tr = pl.BlockSpec((1, rb, GROUP_W), r)
    gf = pl.BlockSpec((1, cb, 1, GROUP_W), lambda b, j: (b, j, 0, 0))
    gr = pl.BlockSpec((1, cb, 1, GROUP_W), lambda b, j: (b, nb - 1 - j, 0, 0))
    return pl.pallas_call(
        functools.partial(_delta_kernel, cb=cb),
        out_shape=[jax.ShapeDtypeStruct((B, S, GROUP_W), F32)] * 2,
        grid=(B, nb),
        in_specs=[tf, tf, tf, tf, tf, gf, tr, tr, tr, tr, tr, gr,
                  pl.BlockSpec((GROUP_W, GROUP_W), lambda b, j: (0, 0))],
        out_specs=[tf, tr],
        scratch_shapes=[pltpu.VMEM((2, GROUP_W, GROUP_W), F32)],
        compiler_params=_cparams(("parallel", "arbitrary")),
        name="delta",
    )(qn, kn, vv, gcf, bef, grf, qn, kn, vv, gcb, beb, grb, bdmask)


def _attn_kernel(q_ref, k_ref, v_ref, out_ref, m_ref, l_ref, o_ref, *, S):
    head0 = lax.broadcasted_iota(I32, (1, 128), 1) < ATT_HD

    def rows(start, n, dil):
        return pl.ds(start, n) if dil == 1 else pl.ds(start, n, stride=dil)

    for bi, (win, dil) in enumerate(zip(ATT_WINDOWS, ATT_DILATIONS)):
        half = win // (2 * dil)
        L = S // dil
        qb_n = min(ATT_QB, L)
        kw = min(L, qb_n + 2 * half)
        nqb = L // qb_n

        def body(it, carry, bi=bi, dil=dil, half=half, L=L, qb_n=qb_n, kw=kw, nqb=nqb):
            r = it // nqb
            m0 = (it % nqb) * qb_n
            ks = jnp.clip(m0 - half, 0, L - kw)
            qsel = rows(r + m0 * dil, qb_n, dil)
            ksel = rows(r + ks * dil, kw, dil)
            q = q_ref[0, qsel, :]
            kk = k_ref[0, ksel, :].astype(BF16)
            vv = v_ref[0, ksel, :].astype(BF16)
            qpos = m0 + lax.broadcasted_iota(I32, (qb_n, kw), 0)
            kpos = ks + lax.broadcasted_iota(I32, (qb_n, kw), 1)
            valid = jnp.abs(kpos - qpos) <= half
            ms, ls, os_ = [], [], []
            for h in range(2):
                qh = jnp.where(head0 if h == 0 else ~head0, q, 0.0)
                s = jnp.where(valid, _dot_nt(qh, kk), NEG)
                m = jnp.max(s, axis=-1, keepdims=True)
                p = jnp.exp(s - m)
                ms.append(m)
                ls.append(jnp.sum(p, axis=-1, keepdims=True))
                os_.append(_dot(p, vv))
            m = jnp.where(head0, ms[0], ms[1])
            l = jnp.where(head0, ls[0], ls[1])
            o = jnp.where(head0, os_[0], os_[1])
            if bi > 0:
                m_old, l_old, o_old = m_ref[qsel, :], l_ref[qsel, :], o_ref[qsel, :]
                m_new = jnp.maximum(m_old, m)
                w_old, w_cur = jnp.exp(m_old - m_new), jnp.exp(m - m_new)
                l = w_old * l_old + w_cur * l
                o = w_old * o_old + w_cur * o
                m = m_new
            if bi == len(ATT_WINDOWS) - 1:
                out_ref[0, qsel, :] = o / l
            else:
                m_ref[qsel, :] = m
                l_ref[qsel, :] = l
                o_ref[qsel, :] = o
            return carry

        lax.fori_loop(0, dil * nqb, body, 0)


def _attention(pd3):
    B, S, _ = pd3.shape
    spec = lambda off: pl.BlockSpec((1, S, 128), lambda b, p: (b, 0, off + p))
    return pl.pallas_call(
        functools.partial(_attn_kernel, S=S),
        out_shape=jax.ShapeDtypeStruct((B, S, GROUP_W), F32),
        grid=(B, 2),
        in_specs=[spec(0), spec(2), spec(4)],
        out_specs=pl.BlockSpec((1, S, 128), lambda b, p: (b, 0, p)),
        scratch_shapes=[pltpu.VMEM((S, 128), F32)] * 3,
        compiler_params=_cparams(("parallel", "parallel")),
        name="attention",
    )(pd3, pd3, pd3)


def _outproj_kernel(ya_ref, yb_ref, of_ref, ob_ref, cg_ref, yd_ref, x_ref, dng_ref, mg_ref, wo_ref, g2_ref, rw_ref,
                    x2_ref, h2_ref, aff_ref):
    o = of_ref[...] + ob_ref[...]
    ms = _dot_hi(o * o, _head_sum_matrix(GROUP_W, DN_DK)) * (1.0 / DN_DK)
    cg = cg_ref[...]
    yc = (o * lax.rsqrt(ms + EPS) * dng_ref[...]) * (cg * _sigmoid(cg))
    acc = x_ref[...]
    for gi, y in enumerate((ya_ref[...], yb_ref[...], yc, yd_ref[...])):
        sl = slice(gi * GROUP_W, (gi + 1) * GROUP_W)
        mix = y * lax.rsqrt(jnp.mean(y * y, axis=-1, keepdims=True) + EPS) * mg_ref[:, sl]
        acc = acc + _dot(mix, wo_ref[sl, :])
    x2_ref[...] = acc
    h2 = acc * lax.rsqrt(jnp.mean(acc * acc, axis=-1, keepdims=True) + EPS) * g2_ref[...]
    h2_ref[...] = h2.astype(BF16)
    logits = _dot_hi(h2, rw_ref[...])
    logits = jnp.where(lax.broadcasted_iota(I32, logits.shape, 1) < N_EXPERTS, logits, -jnp.inf)
    ex = jnp.exp(logits - jnp.max(logits, axis=-1, keepdims=True))
    aff_ref[...] = ex / jnp.sum(ex, axis=-1, keepdims=True)


def _outproj(ya, yb, of, ob, cg, yd, xt, dng, mg, wo, g2, rw):
    T = xt.shape[0]
    tm = 256
    row = lambda w: pl.BlockSpec((tm, w), lambda i: (i, 0))
    const = lambda shape: pl.BlockSpec(shape, lambda i: (0, 0))
    return pl.pallas_call(
        _outproj_kernel,
        out_shape=[jax.ShapeDtypeStruct((T, D_MODEL), F32), jax.ShapeDtypeStruct((T, D_MODEL), BF16),
                   jax.ShapeDtypeStruct((T, 128), F32)],
        grid=(T // tm,),
        in_specs=[row(GROUP_W)] * 6 + [row(D_MODEL), const((1, GROUP_W)), const((1, D_MODEL)),
                                       const((D_MODEL, D_MODEL)), const((1, D_MODEL)), const((D_MODEL, 128))],
        out_specs=[row(D_MODEL), row(D_MODEL), row(128)],
        compiler_params=_cparams(("parallel",)),
        name="outproj",
    )(ya, yb, of, ob, cg, yd, xt, dng, mg, wo, g2, rw)


def _strict_upper(n):
    return (lax.broadcasted_iota(I32, (n, n), 0) < lax.broadcasted_iota(I32, (n, n), 1)).astype(BF16)


def _route_kernel(aff_ref, mask_ref, pos_ref, offs_ref, *, cap, nblk):
    keys = pltpu.bitcast(aff_ref[...], I32)

    def bit_body(i, thr):
        cand = thr | lax.shift_left(jnp.int32(1), 30 - i)
        cnt = jnp.sum((keys >= cand).astype(F32), axis=1, keepdims=True)
        return jnp.where(cnt >= cap, cand, thr)

    thr = lax.fori_loop(0, 31, bit_body, jnp.zeros((N_EXPERTS, 1), I32))
    need = cap - jnp.sum((keys > thr).astype(F32), axis=1, keepdims=True)
    su = _strict_upper(TOK_BLK)

    def blk_body(j, carry):
        ceq, csel = carry
        st = pl.multiple_of(j * TOK_BLK, TOK_BLK)
        kb = pltpu.bitcast(aff_ref[:, pl.ds(st, TOK_BLK)], I32)
        eqf = (kb == thr).astype(F32)
        rank = ceq + _dot(eqf, su)
        sel = ((kb > thr) | ((kb == thr) & (rank < need))).astype(F32)
        mask_ref[:, pl.ds(st, TOK_BLK)] = sel
        pos_ref[:, pl.ds(st, TOK_BLK)] = _dot(sel, su)
        offs_ref[j] = jnp.broadcast_to(csel.astype(I32), (N_EXPERTS, 128))
        return (ceq + jnp.sum(eqf, axis=1, keepdims=True), csel + jnp.sum(sel, axis=1, keepdims=True))

    zero = jnp.zeros((N_EXPERTS, 1), F32)
    lax.fori_loop(0, nblk, blk_body, (zero, zero))


def _route(aff_t, cap):
    E, T = aff_t.shape
    nblk = T // TOK_BLK
    full = lambda shape: pl.BlockSpec(shape, lambda i: (0,) * len(shape))
    return pl.pallas_call(
        functools.partial(_route_kernel, cap=cap, nblk=nblk),
        out_shape=[jax.ShapeDtypeStruct((E, T), F32), jax.ShapeDtypeStruct((E, T), F32),
                   jax.ShapeDtypeStruct((nblk, E, 128), I32)],
        grid=(1,),
        in_specs=[full((E, T))],
        out_specs=[full((E, T)), full((E, T)), full((nblk, E, 128))],
        compiler_params=_cparams(("arbitrary",)),
        name="route",
    )(aff_t)


def _gather_kernel(offs_ref, x_ref, m_ref, p_ref, out_ref):
    e, j = pl.program_id(0), pl.program_id(1)

    @pl.when(j == 0)
    def _():
        out_ref[...] = jnp.zeros_like(out_ref)

    off = offs_ref[j * N_EXPERTS + e]
    base = pl.multiple_of((off // 16) * 16, 16)
    slot = lax.broadcasted_iota(I32, (GATHER_W, TOK_BLK), 0).astype(F32)
    target = p_ref[0] + (off - base).astype(F32)
    onehot = jnp.where((slot == target) & (m_ref[0] > 0.0), 1.0, 0.0).astype(BF16)
    picked = jnp.dot(onehot, x_ref[...], preferred_element_type=F32).astype(BF16)
    out_ref[0, pl.ds(base, GATHER_W), :] += picked


def _gather(h2, mask3, pos3, offs, cap):
    T = h2.shape[0]
    nblk = T // TOK_BLK
    cap_x = cap + 2 * TOK_BLK
    grid_spec = pltpu.PrefetchScalarGridSpec(
        num_scalar_prefetch=1,
        grid=(N_EXPERTS, nblk),
        in_specs=[pl.BlockSpec((TOK_BLK, D_MODEL), lambda e, j, o: (j, 0)),
                  pl.BlockSpec((1, 1, TOK_BLK), lambda e, j, o: (e, 0, j)),
                  pl.BlockSpec((1, 1, TOK_BLK), lambda e, j, o: (e, 0, j))],
        out_specs=pl.BlockSpec((1, cap_x, D_MODEL), lambda e, j, o: (e, 0, 0)),
    )
    return pl.pallas_call(
        _gather_kernel,
        out_shape=jax.ShapeDtypeStruct((N_EXPERTS, cap_x, D_MODEL), BF16),
        grid_spec=grid_spec,
        compiler_params=_cparams(("parallel", "arbitrary")),
        name="gather",
    )(offs, h2, mask3, pos3)


def _ffn_kernel(x_ref, wg_ref, wu_ref, wd_ref, o_ref):
    x = x_ref[0]
    g = jnp.dot(x, wg_ref[0], preferred_element_type=F32)
    u = jnp.dot(x, wu_ref[0], preferred_element_type=F32)
    h = (g * _sigmoid(g)) * u
    o_ref[0] = _dot(h, wd_ref[0]).astype(BF16)


def _ffn(xe, wg, wu, wd, cap):
    tf = min(512, cap)
    wspec = pl.BlockSpec((1, D_MODEL, D_MODEL), lambda e, i: (e, 0, 0))
    return pl.pallas_call(
        _ffn_kernel,
        out_shape=jax.ShapeDtypeStruct((N_EXPERTS, cap, D_MODEL), BF16),
        grid=(N_EXPERTS, cap // tf),
        in_specs=[pl.BlockSpec((1, tf, D_MODEL), lambda e, i: (e, i, 0)), wspec, wspec, wspec],
        out_specs=pl.BlockSpec((1, tf, D_MODEL), lambda e, i: (e, i, 0)),
        compiler_params=_cparams(("parallel", "parallel")),
        name="ffn",
    )(xe, wg, wu, wd)


def _scatter_kernel(offs_ref, x2_ref, m_ref, p_ref, a_ref, gf_ref, *rest, nb, final):
    ye, out_ref = rest[:2 * N_EXPERTS], rest[2 * N_EXPERTS]
    j = pl.program_id(0)
    out_ref[...] = x2_ref[...]
    slot = lax.broadcasted_iota(I32, (TOK_BLK, 2 * TOK_BLK), 1).astype(F32)
    mk, ps, af = m_ref[...], p_ref[...], a_ref[...]
    for e in range(N_EXPERTS):
        off = offs_ref[j * N_EXPERTS + e]
        cnt = offs_ref[(j + 1) * N_EXPERTS + e] - off
        rel = off - jnp.minimum(off // TOK_BLK, nb - 1) * TOK_BLK
        target = ps[:, e:e + 1] + rel.astype(F32)
        weight = jnp.where(mk[:, e:e + 1] > 0.0, af[:, e:e + 1], 0.0)
        onehot = jnp.where(slot == target, weight, 0.0).astype(BF16)

        @pl.when(cnt > 0)
        def _(e=e, onehot=onehot):
            out_ref[...] += jnp.dot(onehot[:, :TOK_BLK], ye[2 * e][0], preferred_element_type=F32)

        @pl.when(rel + cnt > TOK_BLK)
        def _(e=e, onehot=onehot):
            out_ref[...] += jnp.dot(onehot[:, TOK_BLK:], ye[2 * e + 1][0], preferred_element_type=F32)

    if final:
        x = out_ref[...]
        out_ref[...] = x * lax.rsqrt(jnp.mean(x * x, axis=-1, keepdims=True) + EPS) * gf_ref[...]


def _scatter(x2, mask_c, pos_c, aff, ye, offs, gf, cap, final):
    T = x2.shape[0]
    nblk = T // TOK_BLK
    nb = cap // TOK_BLK

    def ye_spec(e, k):
        return pl.BlockSpec((1, TOK_BLK, D_MODEL),
                            lambda j, o: (e, jnp.minimum(o[j * N_EXPERTS + e] // TOK_BLK + k, nb - 1), 0))

    row = lambda w: pl.BlockSpec((TOK_BLK, w), lambda j, o: (j, 0))
    grid_spec = pltpu.PrefetchScalarGridSpec(
        num_scalar_prefetch=1,
        grid=(nblk,),
        in_specs=[row(D_MODEL), row(N_EXPERTS), row(N_EXPERTS), row(128),
                  pl.BlockSpec((1, D_MODEL), lambda j, o: (0, 0))]
                 + [ye_spec(e, k) for e in range(N_EXPERTS) for k in range(2)],
        out_specs=row(D_MODEL),
    )
    return pl.pallas_call(
        functools.partial(_scatter_kernel, nb=nb, final=final),
        out_shape=jax.ShapeDtypeStruct((T, D_MODEL), F32),
        grid_spec=grid_spec,
        compiler_params=_cparams(("arbitrary",)),
        name="scatter",
    )(offs, x2, mask_c, pos_c, aff, gf, *([ye] * (2 * N_EXPERTS)))


def _block_diag(w):
    n, a, b = w.shape
    out = jnp.zeros((n * a, n * b), w.dtype)
    for i in range(n):
        out = out.at[i * a:(i + 1) * a, i * b:(i + 1) * b].set(w[i])
    return out


def _rope_tables(S):
    inv = ROPE_THETA ** (-jnp.arange(0, ROT_DIM, 2, dtype=F32) / ROT_DIM)
    ang = jnp.arange(S, dtype=F32)[:, None] * inv[None, :]
    cos, sin = jnp.cos(ang), jnp.sin(ang)
    half = ROT_DIM // 2
    ones, zeros = jnp.ones((S, ATT_HD - ROT_DIM), F32), jnp.zeros((S, ATT_HD - ROT_DIM), F32)
    zh = jnp.zeros((S, half), F32)
    c = jnp.concatenate([cos, cos, ones], axis=1)
    s1 = jnp.concatenate([-sin, zh, zeros], axis=1)
    s2 = jnp.concatenate([zh, sin, zeros], axis=1)
    return tuple(jnp.tile(t, (1, 2)) for t in (c, s1, s2))


def _expanders():
    eg = np.zeros((2, 128, GROUP_W), np.float32)
    eb = np.zeros((2, 128, GROUP_W), np.float32)
    for d in range(2):
        for h in range(DN_HEADS):
            eg[d, d * DN_HEADS + h, h * DN_DK:(h + 1) * DN_DK] = 1.0
            eb[d, 2 * DN_HEADS + d * DN_HEADS + h, h * DN_DK:(h + 1) * DN_DK] = 1.0
    hd = np.arange(GROUP_W) // DN_DK
    bd = (hd[:, None] == hd[None, :]).astype(np.float32)
    return jnp.asarray(eg), jnp.asarray(eb), jnp.asarray(bd)


def _layer_params(l, norm1_g, w_in, conv_a_w, conv_a_b, rg_wa, rg_ba, rg_wx, rg_bx, rg_lambda, pool_w, pool_scale,
                  dn_conv_w, dn_A_log, dn_dt_bias, dn_norm_g, mix_norm_g, w_out, norm2_g, router_w,
                  exp_w_gate, exp_w_up, exp_w_down):
    w = w_in[l]
    w_cat = jnp.concatenate([w[:, 0:1536], w[:, 1552:2576], w[:, 1536:1552],
                             jnp.zeros((D_MODEL, COL_END - COL_CAB - 16), F32)], axis=1).astype(BF16)
    pad8 = lambda v: jnp.concatenate([v.reshape(1, 2 * DN_HEADS), jnp.zeros((1, 120), F32)], axis=1)
    return dict(
        g1=norm1_g[l].reshape(1, D_MODEL), w_cat=w_cat,
        conv_a_w=conv_a_w[l], conv_a_b=conv_a_b[l].reshape(1, GROUP_W),
        wa=[_block_diag(rg_wa[l, d]).astype(BF16) for d in range(2)],
        wx=[_block_diag(rg_wx[l, d]).astype(BF16) for d in range(2)],
        ba=[rg_ba[l, d].reshape(1, GROUP_W) for d in range(2)],
        bx=[rg_bx[l, d].reshape(1, GROUP_W) for d in range(2)],
        lam=[rg_lambda[l, d].reshape(1, GROUP_W) for d in range(2)],
        pool_w=_block_diag(pool_w[l]).astype(BF16), pool_scale=pool_scale[l].reshape(1, GROUP_W),
        dn_conv_w=dn_conv_w[l], alog=pad8(dn_A_log[l]), dtb=pad8(dn_dt_bias[l]),
        dng=jnp.tile(dn_norm_g[l], DN_HEADS).reshape(1, GROUP_W),
        mg=mix_norm_g[l].reshape(1, D_MODEL), wo=w_out[l].astype(BF16),
        g2=norm2_g[l].reshape(1, D_MODEL),
        rw=jnp.concatenate([router_w[l], jnp.zeros((D_MODEL, 128 - N_EXPERTS), F32)], axis=1),
        wg=exp_w_gate[l].astype(BF16), wu=exp_w_up[l].astype(BF16), wd=exp_w_down[l].astype(BF16),
    )


def _chunk_rows(gc):
    B, S, _ = gc.shape
    n = S // DN_CHUNK
    g = gc[:, :, ::DN_DK].reshape(B, n, DN_CHUNK, DN_HEADS)
    return g.transpose(0, 1, 3, 2).reshape(B, n, 1, GROUP_W)


def _encoder(x, layers, final_g, consts):
    B, S, D = x.shape
    T = B * S
    cap = max(1, EC_CAPACITY * T // N_EXPERTS)
    eg, eb, bdmask = consts
    rope_c, rope_s1, rope_s2 = _rope_tables(S)
    xt = x.reshape(T, D)
    for l, p in enumerate(layers):
        pa, pb, pq, pg, pd, pcab = _inproj(xt, p["g1"], p["w_cat"], rope_c, rope_s1, rope_s2, S)
        pa3 = pa.reshape(B, S, 2 * GROUP_W)
        rg = lambda d: (p["conv_a_w"], p["conv_a_b"], p["wa"][d], p["ba"][d], p["wx"][d], p["bx"][d], p["lam"][d])
        hf = _rglru(pa3, None, *rg(0), reverse=False)
        ya = _rglru(pa3, hf, *rg(1), reverse=True)
        yb = _pool(pb.reshape(B, S, GROUP_W), p["pool_w"], p["pool_scale"])
        qn, kn, vv, gcf, gcb, bef, beb = _dnprep(pq.reshape(B, S, 3 * GROUP_W), pcab.reshape(B, S, 128),
                                                 p["dn_conv_w"], p["alog"], p["dtb"], eg, eb)
        of, ob = _delta(qn, kn, vv, gcf, gcb, bef, beb, _chunk_rows(gcf), _chunk_rows(gcb), bdmask)
        yd = _attention(pd.reshape(B, S, 3 * GROUP_W))
        flat = lambda a: a.reshape(T, GROUP_W)
        x2, h2, aff = _outproj(flat(ya), flat(yb), flat(of), flat(ob), pg, flat(yd), xt,
                               p["dng"], p["mg"], p["wo"], p["g2"], p["rw"])
        mask, pos, offs3 = _route(aff[:, :N_EXPERTS].T, cap)
        offs = jnp.concatenate([offs3[:, :, 0].reshape(-1), jnp.full((N_EXPERTS,), cap, I32)])
        xe = _gather(h2, mask.reshape(N_EXPERTS, 1, T), pos.reshape(N_EXPERTS, 1, T), offs, cap)
        ye = _ffn(xe, p["wg"], p["wu"], p["wd"], cap)
        xt = _scatter(x2, mask.T, pos.T, aff, ye, offs, final_g, cap, final=(l == len(layers) - 1))
    return xt.reshape(B, S, D)


def kernel(x_prompt, x_sample, norm1_g, w_in, conv_a_w, conv_a_b, rg_wa, rg_ba, rg_wx, rg_bx, rg_lambda, pool_w, pool_scale, dn_conv_w, dn_A_log, dn_dt_bias, dn_norm_g, mix_norm_g, w_out, norm2_g, router_w, exp_w_gate, exp_w_up, exp_w_down, final_norm_g):
    layers = [_layer_params(l, norm1_g, w_in, conv_a_w, conv_a_b, rg_wa, rg_ba, rg_wx, rg_bx, rg_lambda, pool_w,
                            pool_scale, dn_conv_w, dn_A_log, dn_dt_bias, dn_norm_g, mix_norm_g, w_out, norm2_g,
                            router_w, exp_w_gate, exp_w_up, exp_w_down) for l in range(DEPTH)]
    consts = _expanders()
    final_g = final_norm_g.reshape(1, D_MODEL)
    return (_encoder(x_prompt, layers, final_g, consts), _encoder(x_sample, layers, final_g, consts))
```

```python
import functools
import numpy as np
import jax
import jax.numpy as jnp
from jax import lax
from jax.experimental import pallas as pl
from jax.experimental.pallas import tpu as pltpu

F32, BF16, I32 = jnp.float32, jnp.bfloat16, jnp.int32

D_MODEL = 1024
DEPTH = 2
GROUP_W = 256
RG_C = 8.0
POOL_WINDOWS = (2, 4, 8, 16)
DN_HEADS = 4
DN_DK = 64
DN_CHUNK = 64
ATT_HD = 64
ROT_DIM = 16
ROPE_THETA = 500000.0
ATT_WINDOWS = (128, 512, 2048)
ATT_DILATIONS = (1, 4, 16)
N_EXPERTS = 16
EC_CAPACITY = 2
EPS = 1e-6
NEG = -1e30

COL_A, COL_B, COL_CQ, COL_CG, COL_D, COL_CAB, COL_END = 0, 512, 768, 1536, 1792, 2560, 2688

ROW_TILE = 512
HALO = 8
DELTA_CB = 4
ATT_QB = 128
ATT_UNROLL = 2
TOK_BLK = 256
GATHER_W = 64
GATHER_JB = 4
SCATTER_W = 128
VMEM_LIMIT = 56 * 1024 * 1024


def _cparams(sem):
    return pltpu.CompilerParams(dimension_semantics=sem, vmem_limit_bytes=VMEM_LIMIT)


def _dot(a, b):
    return jnp.dot(a.astype(BF16), b.astype(BF16), preferred_element_type=F32)


def _split(a, pieces):
    out, rem = [], a
    for i in range(pieces):
        t = rem.astype(BF16)
        out.append(t)
        if i + 1 < pieces:
            rem = rem - t.astype(F32)
    return out


def _dot_sel(a, sel, pieces=3):
    sel = sel.astype(BF16)
    return sum(jnp.dot(t, sel, preferred_element_type=F32) for t in _split(a, pieces))


def _sel_dot(sel, a, pieces=3):
    sel = sel.astype(BF16)
    return sum(jnp.dot(sel, t, preferred_element_type=F32) for t in _split(a, pieces))


def _dot_nt(a, b):
    return lax.dot_general(a.astype(BF16), b.astype(BF16), (((1,), (1,)), ((), ())), preferred_element_type=F32)


def _dot_tn(a, b):
    return lax.dot_general(a.astype(BF16), b.astype(BF16), (((0,), (0,)), ((), ())), preferred_element_type=F32)


def _sigmoid(x):
    return 1.0 / (1.0 + jnp.exp(-x))


def _softplus(x):
    return jnp.maximum(x, 0.0) + jnp.log1p(jnp.exp(-jnp.abs(x)))


def _shift_rows(e, k):
    n = e.shape[0]
    return e if k % n == 0 else pltpu.roll(e, (-k) % n, axis=0)


def _with_halo(cur_ref, prev_ref, next_ref, first, last):
    prev = jnp.where(first, 0.0, prev_ref[0])
    nxt = jnp.where(last, 0.0, next_ref[0])
    return jnp.concatenate([prev, cur_ref[0], nxt], axis=0)


def _halo_specs(ts, width, S, blk_of):
    per = ts // HALO
    last = S // HALO - 1
    cur = pl.BlockSpec((1, ts, width), lambda b, c: (b, blk_of(c), 0))
    prev = pl.BlockSpec((1, HALO, width), lambda b, c: (b, jnp.maximum(blk_of(c) * per - 1, 0), 0))
    nxt = pl.BlockSpec((1, HALO, width), lambda b, c: (b, jnp.minimum((blk_of(c) + 1) * per, last), 0))
    return cur, prev, nxt


def _inproj_kernel(x_ref, g_ref, w_ref, c_ref, s1_ref, s2_ref, pa_ref, pb_ref, pq_ref, pg_ref, pd_ref, pcab_ref):
    x = x_ref[...]
    h = x * lax.rsqrt(jnp.mean(x * x, axis=-1, keepdims=True) + EPS) * g_ref[...]
    hb = h.astype(BF16)

    def mm(lo, hi):
        return jnp.dot(hb, w_ref[:, lo:hi], preferred_element_type=F32)

    pa_ref[...] = mm(COL_A, COL_B)
    pb_ref[...] = mm(COL_B, COL_CQ)
    pq_ref[...] = mm(COL_CQ, COL_CG)
    pg_ref[...] = mm(COL_CG, COL_D)
    pcab_ref[...] = mm(COL_CAB, COL_END)
    c, s1, s2 = c_ref[...], s1_ref[...], s2_ref[...]
    for blk in range(4):
        lo = COL_D + 128 * blk
        y = mm(lo, lo + 128)
        y = y * c + pltpu.roll(y, 128 - ROT_DIM // 2, axis=1) * s1 + pltpu.roll(y, ROT_DIM // 2, axis=1) * s2
        if blk < 2:
            y = y * (ATT_HD ** -0.5)
        pd_ref[:, 128 * blk:128 * blk + 128] = y
    pd_ref[:, 512:768] = mm(COL_D + 512, COL_D + 768)


def _inproj(xt, g1, w_cat, rope_c, rope_s1, rope_s2, S):
    T = xt.shape[0]
    tm = min(ROW_TILE, S)
    per_seq = S // tm
    widths = (512, 256, 768, 256, 768, 128)
    row = lambda w: pl.BlockSpec((tm, w), lambda i: (i, 0))
    rope = pl.BlockSpec((tm, 128), lambda i: (i % per_seq, 0))
    return pl.pallas_call(
        _inproj_kernel,
        out_shape=[jax.ShapeDtypeStruct((T, w), F32) for w in widths],
        grid=(T // tm,),
        in_specs=[row(D_MODEL), pl.BlockSpec((1, D_MODEL), lambda i: (0, 0)),
                  pl.BlockSpec((D_MODEL, COL_END), lambda i: (0, 0)), rope, rope, rope],
        out_specs=[row(w) for w in widths],
        compiler_params=_cparams(("parallel",)),
        name="inproj",
    )(xt, g1, w_cat, rope_c, rope_s1, rope_s2)


def _rglru_kernel(*refs, reverse, ts, nc):
    if reverse:
        (cur_ref, prev_ref, next_ref, gate_ref, hf_ref, cw_ref, cb_ref, wa_ref, ba_ref, wx_ref, bx_ref, lam_ref,
         out_ref, a_ref, b_ref, carry_ref) = refs
    else:
        (cur_ref, prev_ref, next_ref, cw_ref, cb_ref, wa_ref, ba_ref, wx_ref, bx_ref, lam_ref,
         out_ref, a_ref, b_ref, carry_ref) = refs
    c = pl.program_id(1)
    blk = (nc - 1 - c) if reverse else c
    e = _with_halo(cur_ref, prev_ref, next_ref, blk == 0, blk == nc - 1)
    cw = cw_ref[...]
    sl = slice(HALO, HALO + ts)
    u = (cw[0:1] * _shift_rows(e, -2)[sl] + cw[1:2] * _shift_rows(e, -1)[sl] + cw[2:3] * e[sl]
         + cw[3:4] * _shift_rows(e, 1)[sl]) + cb_ref[...]
    r = _sigmoid(_dot(u, wa_ref[...]) + ba_ref[...])
    i = _sigmoid(_dot(u, wx_ref[...]) + bx_ref[...])
    log_a = -RG_C * r * _softplus(-lam_ref[...])
    a_ref[...] = jnp.exp(log_a)
    b_ref[...] = jnp.sqrt(1.0 - jnp.exp(2.0 * log_a)) * (i * u)

    @pl.when(c == 0)
    def _():
        carry_ref[...] = jnp.zeros_like(carry_ref)

    row = lax.broadcasted_iota(I32, (HALO, GROUP_W), 0)
    nt = ts // HALO

    def body(it, carry):
        ti = (nt - 1 - it) if reverse else it
        st = pl.multiple_of(ti * HALO, HALO)
        a = a_ref[pl.ds(st, HALO), :]
        b = b_ref[pl.ds(st, HALO), :]
        for d in (1, 2, 4):
            k = d if reverse else -d
            valid = (row < HALO - d) if reverse else (row >= d)
            b = jnp.where(valid, a * _shift_rows(b, k) + b, b)
            a = jnp.where(valid, a * _shift_rows(a, k), a)
        h = a * carry + b
        if reverse:
            g = gate_ref[0, pl.ds(st, HALO), :]
            cdf = 0.5 * (1.0 + jnp.tanh(np.float32(np.sqrt(2.0 / np.pi)) * (g + 0.044715 * (g * g * g))))
            out_ref[0, pl.ds(st, HALO), :] = (g * cdf) * (hf_ref[0, pl.ds(st, HALO), :] + h)
            return jnp.broadcast_to(h[0:1], h.shape)
        out_ref[0, pl.ds(st, HALO), :] = h
        return jnp.broadcast_to(h[HALO - 1:HALO], h.shape)

    carry_ref[...] = lax.fori_loop(0, nt, body, carry_ref[...], unroll=4)


def _rglru(pa3, hf, cw, cb, wa, ba, wx, bx, lam, reverse):
    B, S, _ = pa3.shape
    ts = min(ROW_TILE, S)
    nc = S // ts
    blk_of = (lambda c: nc - 1 - c) if reverse else (lambda c: c)
    cur, prev, nxt = _halo_specs(ts, GROUP_W, S, blk_of)
    tile = pl.BlockSpec((1, ts, GROUP_W), lambda b, c: (b, blk_of(c), 0))
    const = lambda shape: pl.BlockSpec(shape, lambda b, c: (0,) * len(shape))
    in_specs = [cur, prev, nxt]
    args = [pa3, pa3, pa3]
    if reverse:
        in_specs += [pl.BlockSpec((1, ts, GROUP_W), lambda b, c: (b, blk_of(c), 1)), tile]
        args += [pa3, hf]
    in_specs += [const((4, GROUP_W)), const((1, GROUP_W)), const((GROUP_W, GROUP_W)), const((1, GROUP_W)),
                 const((GROUP_W, GROUP_W)), const((1, GROUP_W)), const((1, GROUP_W))]
    args += [cw, cb, wa, ba, wx, bx, lam]
    return pl.pallas_call(
        functools.partial(_rglru_kernel, reverse=reverse, ts=ts, nc=nc),
        out_shape=jax.ShapeDtypeStruct((B, S, GROUP_W), F32),
        grid=(B, nc),
        in_specs=in_specs,
        out_specs=tile,
        scratch_shapes=[pltpu.VMEM((ts, GROUP_W), F32), pltpu.VMEM((ts, GROUP_W), F32),
                        pltpu.VMEM((HALO, GROUP_W), F32)],
        compiler_params=_cparams(("parallel", "arbitrary")),
        name="rglru_bwd" if reverse else "rglru_fwd",
    )(*args)


def _pool_kernel(cur_ref, prev_ref, next_ref, w_ref, sc_ref, out_ref, *, ts, nc, S):
    c = pl.program_id(1)
    e = _with_halo(cur_ref, prev_ref, next_ref, c == 0, c == nc - 1)
    sl = slice(HALO, HALO + ts)
    a2 = e + _shift_rows(e, 1)
    a4 = a2 + _shift_rows(a2, 2)
    a8 = a4 + _shift_rows(a4, 4)
    a16 = a8 + _shift_rows(a8, 8)
    sums = [_shift_rows(a, -(w // 2))[sl] for a, w in zip((a2, a4, a8, a16), POOL_WINDOWS)]
    gi = lax.broadcasted_iota(I32, (ts, GROUP_W), 1) // (GROUP_W // len(POOL_WINDOWS))
    ssum = jnp.where(gi == 0, sums[0], jnp.where(gi == 1, sums[1], jnp.where(gi == 2, sums[2], sums[3])))
    hw = jnp.where(gi == 0, 1, jnp.where(gi == 1, 2, jnp.where(gi == 2, 4, 8)))
    t = c * ts + lax.broadcasted_iota(I32, (ts, GROUP_W), 0)
    cnt = (jnp.minimum(t + hw, S) - jnp.maximum(t - hw, 0)).astype(F32)
    p = ssum / cnt - e[sl]
    out_ref[0] = _dot(p, w_ref[...]) * sc_ref[...]


def _pool(pb3, w_bd, scale):
    B, S, _ = pb3.shape
    ts = min(ROW_TILE, S)
    nc = S // ts
    cur, prev, nxt = _halo_specs(ts, GROUP_W, S, lambda c: c)
    return pl.pallas_call(
        functools.partial(_pool_kernel, ts=ts, nc=nc, S=S),
        out_shape=jax.ShapeDtypeStruct((B, S, GROUP_W), F32),
        grid=(B, nc),
        in_specs=[cur, prev, nxt, pl.BlockSpec((GROUP_W, GROUP_W), lambda b, c: (0, 0)),
                  pl.BlockSpec((1, GROUP_W), lambda b, c: (0, 0))],
        out_specs=pl.BlockSpec((1, ts, GROUP_W), lambda b, c: (b, c, 0)),
        compiler_params=_cparams(("parallel", "parallel")),
        name="pool",
    )(pb3, pb3, pb3, w_bd, scale)


def _head_sum_matrix(n, group):
    r = lax.broadcasted_iota(I32, (n, n), 0) // group
    c = lax.broadcasted_iota(I32, (n, n), 1) // group
    return (r == c).astype(F32)


def _dnprep_kernel(cur_ref, prev_ref, next_ref, cab_ref, cw_ref, alog_ref, dtb_ref, eg_ref, eb_ref,
                   q_ref, k_ref, v_ref, gcf_ref, gcb_ref, bef_ref, beb_ref, gcn_ref, *, ts, nc):
    c = pl.program_id(1)
    e = _with_halo(cur_ref, prev_ref, next_ref, c == 0, c == nc - 1)
    cw = cw_ref[...]
    sl = slice(HALO, HALO + ts)
    y = (cw[0:1] * _shift_rows(e, -2)[sl] + cw[1:2] * _shift_rows(e, -1)[sl] + cw[2:3] * e[sl]
         + cw[3:4] * _shift_rows(e, 1)[sl])
    y = y * _sigmoid(y)
    q, k = y[:, 0:GROUP_W], y[:, GROUP_W:2 * GROUP_W]
    hs = _head_sum_matrix(GROUP_W, DN_DK)
    q_ref[0] = q * lax.rsqrt(_dot_sel(q * q, hs) + EPS) * (DN_DK ** -0.5)
    k_ref[0] = k * lax.rsqrt(_dot_sel(k * k, hs) + EPS)
    v_ref[0] = y[:, 2 * GROUP_W:3 * GROUP_W]
    cab = cab_ref[0]
    g = -jnp.exp(alog_ref[...]) * _softplus(cab + dtb_ref[...])
    beta = _sigmoid(cab)
    bef_ref[0] = _dot_sel(beta, eb_ref[0])
    beb_ref[0] = _dot_sel(beta, eb_ref[1])
    r = lax.broadcasted_iota(I32, (ts, ts), 0)
    cc = lax.broadcasted_iota(I32, (ts, ts), 1)
    same = (r // DN_CHUNK) == (cc // DN_CHUNK)
    lane = lax.broadcasted_iota(I32, (ts, 128), 1)
    gcn = jnp.where(lane < DN_HEADS, _sel_dot(same & (r >= cc), g), _sel_dot(same & (r <= cc), g))
    gcn_ref[0] = gcn
    gcf_ref[0] = _dot_sel(gcn, eg_ref[0])
    gcb_ref[0] = _dot_sel(gcn, eg_ref[1])


def _dnprep(pq3, pcab3, cw, alog_row, dtb_row, eg, eb):
    B, S, _ = pq3.shape
    ts = min(ROW_TILE, S)
    nc = S // ts
    cur, prev, nxt = _halo_specs(ts, 3 * GROUP_W, S, lambda c: c)
    tile = pl.BlockSpec((1, ts, GROUP_W), lambda b, c: (b, c, 0))
    const = lambda shape: pl.BlockSpec(shape, lambda b, c: (0,) * len(shape))
    return pl.pallas_call(
        functools.partial(_dnprep_kernel, ts=ts, nc=nc),
        out_shape=[jax.ShapeDtypeStruct((B, S, GROUP_W), F32)] * 7 + [jax.ShapeDtypeStruct((B, S, 128), F32)],
        grid=(B, nc),
        in_specs=[cur, prev, nxt, pl.BlockSpec((1, ts, 128), lambda b, c: (b, c, 0)),
                  const((4, 3 * GROUP_W)), const((1, 128)), const((1, 128)),
                  const((2, 128, GROUP_W)), const((2, 128, GROUP_W))],
        out_specs=[tile] * 7 + [pl.BlockSpec((1, ts, 128), lambda b, c: (b, c, 0))],
        compiler_params=_cparams(("parallel", "parallel")),
        name="dnprep",
    )(pq3, pq3, pq3, pcab3, cw, alog_row, dtb_row, eg, eb)


def _delta_kernel(qf, kf, vf, gcf, bef, grf, qb, kb, vb, gcb, beb, grb, bd_ref, of_ref, ob_ref,
                  s_ref, p_scr, x_scr, wq_scr, u_scr, at_scr, kd_scr, vb_scr, kg_scr, *, cb):
    j = pl.program_id(1)

    @pl.when(j == 0)
    def _():
        s_ref[...] = jnp.zeros_like(s_ref)

    C = DN_CHUNK
    bd = bd_ref[...] > 0.0
    c_idx = lax.broadcasted_iota(I32, (C, GROUP_W), 0)
    m_idx = lax.broadcasted_iota(I32, (C, GROUP_W), 1) % C
    eye = (c_idx == m_idx).astype(F32)

    def blockdiag(x):
        return jnp.where(bd, jnp.tile(x.astype(BF16), (DN_HEADS, 1)), jnp.zeros((), BF16))

    dirs = ((qf, kf, vf, gcf, bef, grf, of_ref), (qb, kb, vb, gcb, beb, grb, ob_ref))
    units = [(d, ci) for ci in range(cb) for d in range(2)]

    def chunk_of(d, ci):
        return cb - 1 - ci if d == 1 else ci

    for ui, (d, ci) in enumerate(units):
        q_r, k_r, v_r, gc_r, be_r, gr_r, _ = dirs[d]
        rev = d == 1
        tril = (c_idx <= m_idx) if rev else (c_idx >= m_idx)
        strict = (c_idx < m_idx) if rev else (c_idx > m_idx)
        last = 0 if rev else C - 1
        cc = chunk_of(d, ci)
        rows = slice(cc * C, (cc + 1) * C)
        q, k, v = q_r[0, rows, :], k_r[0, rows, :], v_r[0, rows, :]
        gc, be, gr = gc_r[0, rows, :], be_r[0, rows, :], gr_r[0, cc]
        eg = jnp.exp(gc)
        kbeta = k * be
        kkqk = _dot_nt(jnp.concatenate([kbeta, q], axis=0), blockdiag(k))
        decay = jnp.exp(jnp.where(tril, gc - gr, -jnp.inf))
        at_scr[ui] = kkqk[C:] * decay
        p = -jnp.where(strict, kkqk[:C] * decay, 0.0)
        p_scr[ui] = p
        x_scr[ui] = eye + p
        wq_scr[ui, C:2 * C, :] = q * eg
        kd_scr[ui] = k * jnp.exp(gc[last:last + 1] - gc)
        vb_scr[ui] = v * be
        kg_scr[ui] = kbeta * eg
    for ui in range(len(units)):
        p = p_scr[ui]
        p_scr[ui] = _dot(p, blockdiag(p))
    for _ in range(4):
        for ui in range(len(units)):
            p, x = p_scr[ui], x_scr[ui]
            px = _dot(jnp.concatenate([p, x], axis=0), blockdiag(p))
            p_scr[ui] = px[:C]
            x_scr[ui] = x + px[C:]
    for ui in range(len(units)):
        x = x_scr[ui]
        x = x + _dot(x, blockdiag(p_scr[ui]))
        u_scr[ui] = _dot(x, blockdiag(vb_scr[ui]))
        wq_scr[ui, 0:C, :] = _dot(x, blockdiag(kg_scr[ui]))
    for ui, (d, ci) in enumerate(units):
        gc_r, o_r = dirs[d][3], dirs[d][6]
        cc = chunk_of(d, ci)
        last = cc * C + (0 if d == 1 else C - 1)
        state = s_ref[d]
        ws_qs = _dot(wq_scr[ui], state)
        v_new = u_scr[ui] - ws_qs[:C]
        o_r[0, cc * C:(cc + 1) * C, :] = ws_qs[C:] + _dot(at_scr[ui], blockdiag(v_new))
        s_ref[d] = (state * jnp.exp(gc_r[0, last:last + 1, :])
                    + jnp.where(bd, _dot_tn(kd_scr[ui], v_new), 0.0))


def _delta(qn, kn, vv, gcf, gcb, bef, beb, grf, grb, bdmask):
    B, S, _ = qn.shape
    cb = DELTA_CB
    rb = cb * DN_CHUNK
    nb = S // rb
    f = lambda b, j: (b, j, 0)
    r = lambda b, j: (b, nb - 1 - j, 0)
    tf = pl.BlockSpec((1, rb, GROUP_W), f)
    tr = pl.BlockSpec((1, rb, GROUP_W), r)
    gf = pl.BlockSpec((1, cb, 1, GROUP_W), lambda b, j: (b, j, 0, 0))
    gr = pl.BlockSpec((1, cb, 1, GROUP_W), lambda b, j: (b, nb - 1 - j, 0, 0))
    return pl.pallas_call(
        functools.partial(_delta_kernel, cb=cb),
        out_shape=[jax.ShapeDtypeStruct((B, S, GROUP_W), F32)] * 2,
        grid=(B, nb),
        in_specs=[tf, tf, tf, tf, tf, gf, tr, tr, tr, tr, tr, gr,
                  pl.BlockSpec((GROUP_W, GROUP_W), lambda b, j: (0, 0))],
        out_specs=[tf, tr],
        scratch_shapes=[pltpu.VMEM((2, GROUP_W, GROUP_W), F32)]
                       + [pltpu.VMEM((2 * cb, n * DN_CHUNK, GROUP_W), F32) for n in (1, 1, 2, 1, 1, 1, 1, 1)],
        compiler_params=_cparams(("parallel", "arbitrary")),
        name="delta",
    )(qn, kn, vv, gcf, bef, grf, qn, kn, vv, gcb, beb, grb, bdmask)


def _attn_kernel(q_ref, k_ref, v_ref, out_ref, m_ref, l_ref, o_ref, *, S):
    head0 = lax.broadcasted_iota(I32, (1, 128), 1) < ATT_HD

    def rows(start, n, dil):
        return pl.ds(start, n) if dil == 1 else pl.ds(start, n, stride=dil)

    for bi, (win, dil) in enumerate(zip(ATT_WINDOWS, ATT_DILATIONS)):
        half = win // (2 * dil)
        L = S // dil
        qb_n = min(ATT_QB, L)
        kw = min(L, qb_n + 2 * half)
        nqb = L // qb_n

        def block_stats(idx, dil=dil, half=half, L=L, qb_n=qb_n, kw=kw, nqb=nqb):
            r = idx // nqb
            m0 = (idx % nqb) * qb_n
            ks = jnp.clip(m0 - half, 0, L - kw)
            qsel = rows(r + m0 * dil, qb_n, dil)
            ksel = rows(r + ks * dil, kw, dil)
            q = q_ref[0, qsel, :]
            kk = k_ref[0, ksel, :].astype(BF16)
            vv = v_ref[0, ksel, :].astype(BF16)
            qpos = m0 + lax.broadcasted_iota(I32, (qb_n, kw), 0)
            kpos = ks + lax.broadcasted_iota(I32, (qb_n, kw), 1)
            valid = jnp.abs(kpos - qpos) <= half
            ms, ls, os_ = [], [], []
            for h in range(2):
                qh = jnp.where(head0 if h == 0 else ~head0, q, 0.0)
                s = jnp.where(valid, _dot_nt(qh, kk), NEG)
                m = jnp.max(s, axis=-1, keepdims=True)
                p = jnp.exp(s - m)
                ms.append(m)
                ls.append(jnp.sum(p, axis=-1, keepdims=True))
                os_.append(_dot(p, vv))
            return (qsel, jnp.where(head0, ms[0], ms[1]), jnp.where(head0, ls[0], ls[1]),
                    jnp.where(head0, os_[0], os_[1]))

        def body(it, carry, bi=bi):
            stats = [block_stats(it * ATT_UNROLL + s) for s in range(ATT_UNROLL)]
            if bi > 0:
                olds = [(m_ref[qsel, :], l_ref[qsel, :], o_ref[qsel, :]) for qsel, _, _, _ in stats]
                merged = []
                for (qsel, m, l, o), (m_old, l_old, o_old) in zip(stats, olds):
                    m_new = jnp.maximum(m_old, m)
                    w_old, w_cur = jnp.exp(m_old - m_new), jnp.exp(m - m_new)
                    merged.append((qsel, m_new, w_old * l_old + w_cur * l, w_old * o_old + w_cur * o))
                stats = merged
            for qsel, m, l, o in stats:
                if bi == len(ATT_WINDOWS) - 1:
                    out_ref[0, qsel, :] = o / l
                else:
                    m_ref[qsel, :] = m
                    l_ref[qsel, :] = l
                    o_ref[qsel, :] = o
            return carry

        lax.fori_loop(0, dil * nqb // ATT_UNROLL, body, 0)


def _attention(pd3):
    B, S, _ = pd3.shape
    spec = lambda off: pl.BlockSpec((1, S, 128), lambda b, p: (b, 0, off + p))
    return pl.pallas_call(
        functools.partial(_attn_kernel, S=S),
        out_shape=jax.ShapeDtypeStruct((B, S, GROUP_W), F32),
        grid=(B, 2),
        in_specs=[spec(0), spec(2), spec(4)],
        out_specs=pl.BlockSpec((1, S, 128), lambda b, p: (b, 0, p)),
        scratch_shapes=[pltpu.VMEM((S, 128), F32)] * 3,
        compiler_params=_cparams(("parallel", "parallel")),
        name="attention",
    )(pd3, pd3, pd3)


def _outproj_kernel(ya_ref, yb_ref, of_ref, ob_ref, cg_ref, yd_ref, x_ref, dng_ref, mg_ref, wo_ref, g2_ref, rw_ref,
                    x2_ref, h2_ref, aff_ref):
    o = of_ref[...] + ob_ref[...]
    ms = _dot_sel(o * o, _head_sum_matrix(GROUP_W, DN_DK)) * (1.0 / DN_DK)
    cg = cg_ref[...]
    yc = (o * lax.rsqrt(ms + EPS) * dng_ref[...]) * (cg * _sigmoid(cg))
    acc = x_ref[...]
    for gi, y in enumerate((ya_ref[...], yb_ref[...], yc, yd_ref[...])):
        sl = slice(gi * GROUP_W, (gi + 1) * GROUP_W)
        mix = y * lax.rsqrt(jnp.mean(y * y, axis=-1, keepdims=True) + EPS) * mg_ref[:, sl]
        acc = acc + _dot(mix, wo_ref[sl, :])
    x2_ref[...] = acc
    h2 = acc * lax.rsqrt(jnp.mean(acc * acc, axis=-1, keepdims=True) + EPS) * g2_ref[...]
    h2_ref[...] = h2.astype(BF16)
    (h_hi, h_lo), (w_hi, w_lo) = _split(h2, 2), _split(rw_ref[...], 2)
    logits = (jnp.dot(h_hi, w_hi, preferred_element_type=F32) + jnp.dot(h_hi, w_lo, preferred_element_type=F32)
              + jnp.dot(h_lo, w_hi, preferred_element_type=F32))
    logits = jnp.where(lax.broadcasted_iota(I32, logits.shape, 1) < N_EXPERTS, logits, -jnp.inf)
    ex = jnp.exp(logits - jnp.max(logits, axis=-1, keepdims=True))
    aff_ref[...] = ex / jnp.sum(ex, axis=-1, keepdims=True)


def _outproj(ya, yb, of, ob, cg, yd, xt, dng, mg, wo, g2, rw):
    T = xt.shape[0]
    tm = 256
    row = lambda w: pl.BlockSpec((tm, w), lambda i: (i, 0))
    const = lambda shape: pl.BlockSpec(shape, lambda i: (0, 0))
    return pl.pallas_call(
        _outproj_kernel,
        out_shape=[jax.ShapeDtypeStruct((T, D_MODEL), F32), jax.ShapeDtypeStruct((T, D_MODEL), BF16),
                   jax.ShapeDtypeStruct((T, 128), F32)],
        grid=(T // tm,),
        in_specs=[row(GROUP_W)] * 6 + [row(D_MODEL), const((1, GROUP_W)), const((1, D_MODEL)),
                                       const((D_MODEL, D_MODEL)), const((1, D_MODEL)), const((D_MODEL, 128))],
        out_specs=[row(D_MODEL), row(D_MODEL), row(128)],
        compiler_params=_cparams(("parallel",)),
        name="outproj",
    )(ya, yb, of, ob, cg, yd, xt, dng, mg, wo, g2, rw)


def _strict_upper(n):
    return (lax.broadcasted_iota(I32, (n, n), 0) < lax.broadcasted_iota(I32, (n, n), 1)).astype(BF16)


def _route_kernel(aff_ref, mask_ref, pos_ref, offs_ref, *, cap, nblk):
    keys = pltpu.bitcast(aff_ref[...], I32)

    def bit_body(i, thr):
        cand = thr | lax.shift_left(jnp.int32(1), 30 - i)
        cnt = jnp.sum((keys >= cand).astype(F32), axis=1, keepdims=True)
        return jnp.where(cnt >= cap, cand, thr)

    thr = lax.fori_loop(0, 31, bit_body, jnp.zeros((N_EXPERTS, 1), I32))
    need = cap - jnp.sum((keys > thr).astype(F32), axis=1, keepdims=True)
    su = _strict_upper(TOK_BLK)

    def blk_body(j, carry):
        ceq, csel = carry
        st = pl.multiple_of(j * TOK_BLK, TOK_BLK)
        kb = pltpu.bitcast(aff_ref[:, pl.ds(st, TOK_BLK)], I32)
        eqf = (kb == thr).astype(F32)
        rank = ceq + _dot(eqf, su)
        sel = ((kb > thr) | ((kb == thr) & (rank < need))).astype(F32)
        mask_ref[:, pl.ds(st, TOK_BLK)] = sel
        pos_ref[:, pl.ds(st, TOK_BLK)] = _dot(sel, su)
        offs_ref[j] = jnp.broadcast_to(csel.astype(I32), (N_EXPERTS, 128))
        return (ceq + jnp.sum(eqf, axis=1, keepdims=True), csel + jnp.sum(sel, axis=1, keepdims=True))

    zero = jnp.zeros((N_EXPERTS, 1), F32)
    lax.fori_loop(0, nblk, blk_body, (zero, zero))


def _route(aff_t, cap):
    E, T = aff_t.shape
    nblk = T // TOK_BLK
    full = lambda shape: pl.BlockSpec(shape, lambda i: (0,) * len(shape))
    return pl.pallas_call(
        functools.partial(_route_kernel, cap=cap, nblk=nblk),
        out_shape=[jax.ShapeDtypeStruct((E, T), F32), jax.ShapeDtypeStruct((E, T), F32),
                   jax.ShapeDtypeStruct((nblk, E, 128), I32)],
        grid=(1,),
        in_specs=[full((E, T))],
        out_specs=[full((E, T)), full((E, T)), full((nblk, E, 128))],
        compiler_params=_cparams(("arbitrary",)),
        name="route",
    )(aff_t)


def _gather_kernel(offs_ref, x_ref, m_ref, p_ref, out_ref):
    e, j = pl.program_id(0), pl.program_id(1)

    @pl.when(j == 0)
    def _():
        out_ref[...] = jnp.zeros_like(out_ref)

    slot0 = lax.broadcasted_iota(I32, (GATHER_W, TOK_BLK), 0).astype(F32)
    for jb in range(GATHER_JB):
        blk = j * GATHER_JB + jb
        off = offs_ref[blk * N_EXPERTS + e]
        cnt = offs_ref[(blk + 1) * N_EXPERTS + e] - off
        base = (off // 16) * 16
        cols = slice(jb * TOK_BLK, (jb + 1) * TOK_BLK)
        target = jnp.where(m_ref[0, :, cols] > 0.0, p_ref[0, :, cols] + (off - base).astype(F32), -1.0)

        def one_pass(ps, carry, cols=cols, target=target, base=base):
            onehot = jnp.where(slot0 + (ps * GATHER_W).astype(F32) == target, 1.0, 0.0).astype(BF16)
            picked = jnp.dot(onehot, x_ref[cols, :], preferred_element_type=F32).astype(BF16)
            out_ref[0, pl.ds(pl.multiple_of(base + ps * GATHER_W, 16), GATHER_W), :] += picked
            return carry

        lax.fori_loop(0, (off - base + cnt + GATHER_W - 1) // GATHER_W, one_pass, 0)


def _gather(h2, mask3, pos3, offs, cap):
    T = h2.shape[0]
    nblk = T // TOK_BLK
    cap_x = cap + 2 * GATHER_W
    rows = GATHER_JB * TOK_BLK
    grid_spec = pltpu.PrefetchScalarGridSpec(
        num_scalar_prefetch=1,
        grid=(N_EXPERTS, nblk // GATHER_JB),
        in_specs=[pl.BlockSpec((rows, D_MODEL), lambda e, j, o: (j, 0)),
                  pl.BlockSpec((1, 1, rows), lambda e, j, o: (e, 0, j)),
                  pl.BlockSpec((1, 1, rows), lambda e, j, o: (e, 0, j))],
        out_specs=pl.BlockSpec((1, cap_x, D_MODEL), lambda e, j, o: (e, 0, 0)),
    )
    return pl.pallas_call(
        _gather_kernel,
        out_shape=jax.ShapeDtypeStruct((N_EXPERTS, cap_x, D_MODEL), BF16),
        grid_spec=grid_spec,
        compiler_params=_cparams(("parallel", "arbitrary")),
        name="gather",
    )(offs, h2, mask3, pos3)


def _ffn_kernel(x_ref, wg_ref, wu_ref, wd_ref, o_ref):
    x = x_ref[0]
    g = jnp.dot(x, wg_ref[0], preferred_element_type=F32)
    u = jnp.dot(x, wu_ref[0], preferred_element_type=F32)
    h = (g * _sigmoid(g)) * u
    o_ref[0] = _dot(h, wd_ref[0]).astype(BF16)


def _ffn(xe, wg, wu, wd, cap):
    tf = min(512, cap)
    wspec = pl.BlockSpec((1, D_MODEL, D_MODEL), lambda e, i: (e, 0, 0))
    return pl.pallas_call(
        _ffn_kernel,
        out_shape=jax.ShapeDtypeStruct((N_EXPERTS, cap, D_MODEL), BF16),
        grid=(N_EXPERTS, cap // tf),
        in_specs=[pl.BlockSpec((1, tf, D_MODEL), lambda e, i: (e, i, 0)), wspec, wspec, wspec],
        out_specs=pl.BlockSpec((1, tf, D_MODEL), lambda e, i: (e, i, 0)),
        compiler_params=_cparams(("parallel", "parallel")),
        name="ffn",
    )(xe, wg, wu, wd)


def _scatter_kernel(offs_ref, x2_ref, m_ref, p_ref, a_ref, gf_ref, *rest, nb, final):
    ye, out_ref, y_scr = rest[:2 * N_EXPERTS], rest[2 * N_EXPERTS], rest[2 * N_EXPERTS + 1]
    j = pl.program_id(0)
    mk, ps, af = m_ref[...], p_ref[...], a_ref[...]
    offs = [offs_ref[j * N_EXPERTS + e] for e in range(N_EXPERTS)]
    cnts = [offs_ref[(j + 1) * N_EXPERTS + e] - offs[e] for e in range(N_EXPERTS)]
    rels = [offs[e] - jnp.minimum(offs[e] // TOK_BLK, nb - 1) * TOK_BLK for e in range(N_EXPERTS)]
    weights = [jnp.where(mk[:, e:e + 1] > 0.0, af[:, e:e + 1], 0.0) for e in range(N_EXPERTS)]
    W = SCATTER_W
    small = cnts[0] <= W - 16
    for e in range(1, N_EXPERTS):
        small = jnp.logical_and(small, cnts[e] <= W - 16)

    def onehot(e, start, width):
        slot = lax.broadcasted_iota(I32, (TOK_BLK, width), 1).astype(F32)
        return jnp.where(slot == ps[:, e:e + 1] + (rels[e] - start).astype(F32), weights[e], 0.0).astype(BF16)

    @pl.when(small)
    def _():
        pieces = []
        for e in range(N_EXPERTS):
            r0 = pl.multiple_of(jnp.minimum((rels[e] // 16) * 16, TOK_BLK - W), 16)
            y_scr[e * W:(e + 1) * W, :] = ye[2 * e][0, pl.ds(r0, W), :]
            pieces.append(onehot(e, r0, W))
        out_ref[...] = x2_ref[...] + jnp.dot(jnp.concatenate(pieces, axis=1), y_scr[...],
                                             preferred_element_type=F32)
        for e in range(N_EXPERTS):
            @pl.when(rels[e] + cnts[e] > TOK_BLK)
            def _(e=e):
                out_ref[...] += jnp.dot(onehot(e, TOK_BLK, W), ye[2 * e + 1][0, 0:W, :],
                                        preferred_element_type=F32)

    @pl.when(jnp.logical_not(small))
    def _():
        out_ref[...] = x2_ref[...]
        for e in range(N_EXPERTS):
            @pl.when(cnts[e] > 0)
            def _(e=e):
                out_ref[...] += jnp.dot(onehot(e, 0, TOK_BLK), ye[2 * e][0], preferred_element_type=F32)

            @pl.when(rels[e] + cnts[e] > TOK_BLK)
            def _(e=e):
                out_ref[...] += jnp.dot(onehot(e, TOK_BLK, TOK_BLK), ye[2 * e + 1][0],
                                        preferred_element_type=F32)

    if final:
        x = out_ref[...]
        out_ref[...] = x * lax.rsqrt(jnp.mean(x * x, axis=-1, keepdims=True) + EPS) * gf_ref[...]


def _scatter(x2, mask_c, pos_c, aff, ye, offs, gf, cap, final):
    T = x2.shape[0]
    nblk = T // TOK_BLK
    nb = cap // TOK_BLK

    def ye_spec(e, k):
        return pl.BlockSpec((1, TOK_BLK, D_MODEL),
                            lambda j, o: (e, jnp.minimum(o[j * N_EXPERTS + e] // TOK_BLK + k, nb - 1), 0))

    row = lambda w: pl.BlockSpec((TOK_BLK, w), lambda j, o: (j, 0))
    grid_spec = pltpu.PrefetchScalarGridSpec(
        num_scalar_prefetch=1,
        grid=(nblk,),
        in_specs=[row(D_MODEL), row(N_EXPERTS), row(N_EXPERTS), row(128),
                  pl.BlockSpec((1, D_MODEL), lambda j, o: (0, 0))]
                 + [ye_spec(e, k) for e in range(N_EXPERTS) for k in range(2)],
        out_specs=row(D_MODEL),
        scratch_shapes=[pltpu.VMEM((N_EXPERTS * SCATTER_W, D_MODEL), BF16)],
    )
    return pl.pallas_call(
        functools.partial(_scatter_kernel, nb=nb, final=final),
        out_shape=jax.ShapeDtypeStruct((T, D_MODEL), F32),
        grid_spec=grid_spec,
        compiler_params=_cparams(("arbitrary",)),
        name="scatter",
    )(offs, x2, mask_c, pos_c, aff, gf, *([ye] * (2 * N_EXPERTS)))


def _block_diag(w):
    n, a, b = w.shape
    out = jnp.zeros((n * a, n * b), w.dtype)
    for i in range(n):
        out = out.at[i * a:(i + 1) * a, i * b:(i + 1) * b].set(w[i])
    return out


def _rope_tables(S):
    inv = ROPE_THETA ** (-jnp.arange(0, ROT_DIM, 2, dtype=F32) / ROT_DIM)
    ang = jnp.arange(S, dtype=F32)[:, None] * inv[None, :]
    cos, sin = jnp.cos(ang), jnp.sin(ang)
    half = ROT_DIM // 2
    ones, zeros = jnp.ones((S, ATT_HD - ROT_DIM), F32), jnp.zeros((S, ATT_HD - ROT_DIM), F32)
    zh = jnp.zeros((S, half), F32)
    c = jnp.concatenate([cos, cos, ones], axis=1)
    s1 = jnp.concatenate([-sin, zh, zeros], axis=1)
    s2 = jnp.concatenate([zh, sin, zeros], axis=1)
    return tuple(jnp.tile(t, (1, 2)) for t in (c, s1, s2))


def _expanders():
    eg = np.zeros((2, 128, GROUP_W), np.float32)
    eb = np.zeros((2, 128, GROUP_W), np.float32)
    for d in range(2):
        for h in range(DN_HEADS):
            eg[d, d * DN_HEADS + h, h * DN_DK:(h + 1) * DN_DK] = 1.0
            eb[d, 2 * DN_HEADS + d * DN_HEADS + h, h * DN_DK:(h + 1) * DN_DK] = 1.0
    hd = np.arange(GROUP_W) // DN_DK
    bd = (hd[:, None] == hd[None, :]).astype(np.float32)
    return jnp.asarray(eg), jnp.asarray(eb), jnp.asarray(bd)


def _layer_params(l, norm1_g, w_in, conv_a_w, conv_a_b, rg_wa, rg_ba, rg_wx, rg_bx, rg_lambda, pool_w, pool_scale,
                  dn_conv_w, dn_A_log, dn_dt_bias, dn_norm_g, mix_norm_g, w_out, norm2_g, router_w,
                  exp_w_gate, exp_w_up, exp_w_down):
    w = w_in[l]
    w_cat = jnp.concatenate([w[:, 0:1536], w[:, 1552:2576], w[:, 1536:1552],
                             jnp.zeros((D_MODEL, COL_END - COL_CAB - 16), F32)], axis=1).astype(BF16)
    pad8 = lambda v: jnp.concatenate([v.reshape(1, 2 * DN_HEADS), jnp.zeros((1, 120), F32)], axis=1)
    return dict(
        g1=norm1_g[l].reshape(1, D_MODEL), w_cat=w_cat,
        conv_a_w=conv_a_w[l], conv_a_b=conv_a_b[l].reshape(1, GROUP_W),
        wa=[_block_diag(rg_wa[l, d]).astype(BF16) for d in range(2)],
        wx=[_block_diag(rg_wx[l, d]).astype(BF16) for d in range(2)],
        ba=[rg_ba[l, d].reshape(1, GROUP_W) for d in range(2)],
        bx=[rg_bx[l, d].reshape(1, GROUP_W) for d in range(2)],
        lam=[rg_lambda[l, d].reshape(1, GROUP_W) for d in range(2)],
        pool_w=_block_diag(pool_w[l]).astype(BF16), pool_scale=pool_scale[l].reshape(1, GROUP_W),
        dn_conv_w=dn_conv_w[l], alog=pad8(dn_A_log[l]), dtb=pad8(dn_dt_bias[l]),
        dng=jnp.tile(dn_norm_g[l], DN_HEADS).reshape(1, GROUP_W),
        mg=mix_norm_g[l].reshape(1, D_MODEL), wo=w_out[l].astype(BF16),
        g2=norm2_g[l].reshape(1, D_MODEL),
        rw=jnp.concatenate([router_w[l], jnp.zeros((D_MODEL, 128 - N_EXPERTS), F32)], axis=1),
        wg=exp_w_gate[l].astype(BF16), wu=exp_w_up[l].astype(BF16), wd=exp_w_down[l].astype(BF16),
    )


def _chunk_rows(gcn, d):
    B, S, _ = gcn.shape
    n = S // DN_CHUNK
    g = gcn[:, :, d * DN_HEADS:(d + 1) * DN_HEADS].reshape(B, n, DN_CHUNK, DN_HEADS)
    return g.transpose(0, 1, 3, 2).reshape(B, n, 1, GROUP_W)


def _encoder(x, layers, final_g, consts):
    B, S, D = x.shape
    T = B * S
    cap = max(1, EC_CAPACITY * T // N_EXPERTS)
    eg, eb, bdmask = consts
    rope_c, rope_s1, rope_s2 = _rope_tables(S)
    xt = x.reshape(T, D)
    for l, p in enumerate(layers):
        pa, pb, pq, pg, pd, pcab = _inproj(xt, p["g1"], p["w_cat"], rope_c, rope_s1, rope_s2, S)
        pa3 = pa.reshape(B, S, 2 * GROUP_W)
        rg = lambda d: (p["conv_a_w"], p["conv_a_b"], p["wa"][d], p["ba"][d], p["wx"][d], p["bx"][d], p["lam"][d])
        hf = _rglru(pa3, None, *rg(0), reverse=False)
        ya = _rglru(pa3, hf, *rg(1), reverse=True)
        yb = _pool(pb.reshape(B, S, GROUP_W), p["pool_w"], p["pool_scale"])
        qn, kn, vv, gcf, gcb, bef, beb, gcn = _dnprep(pq.reshape(B, S, 3 * GROUP_W), pcab.reshape(B, S, 128),
                                                      p["dn_conv_w"], p["alog"], p["dtb"], eg, eb)
        of, ob = _delta(qn, kn, vv, gcf, gcb, bef, beb, _chunk_rows(gcn, 0), _chunk_rows(gcn, 1), bdmask)
        yd = _attention(pd.reshape(B, S, 3 * GROUP_W))
        flat = lambda a: a.reshape(T, GROUP_W)
        x2, h2, aff = _outproj(flat(ya), flat(yb), flat(of), flat(ob), pg, flat(yd), xt,
                               p["dng"], p["mg"], p["wo"], p["g2"], p["rw"])
        mask, pos, offs3 = _route(aff[:, :N_EXPERTS].T, cap)
        offs = jnp.concatenate([offs3[:, :, 0].reshape(-1), jnp.full((N_EXPERTS,), cap, I32)])
        xe = _gather(h2, mask.reshape(N_EXPERTS, 1, T), pos.reshape(N_EXPERTS, 1, T), offs, cap)
        ye = _ffn(xe, p["wg"], p["wu"], p["wd"], cap)
        xt = _scatter(x2, mask.T, pos.T, aff, ye, offs, final_g, cap, final=(l == len(layers) - 1))
    return xt.reshape(B, S, D)


def kernel(x_prompt, x_sample, norm1_g, w_in, conv_a_w, conv_a_b, rg_wa, rg_ba, rg_wx, rg_bx, rg_lambda, pool_w, pool_scale, dn_conv_w, dn_A_log, dn_dt_bias, dn_norm_g, mix_norm_g, w_out, norm2_g, router_w, exp_w_gate, exp_w_up, exp_w_down, final_norm_g):
    layers = [_layer_params(l, norm1_g, w_in, conv_a_w, conv_a_b, rg_wa, rg_ba, rg_wx, rg_bx, rg_lambda, pool_w,
                            pool_scale, dn_conv_w, dn_A_log, dn_dt_bias, dn_norm_g, mix_norm_g, w_out, norm2_g,
                            router_w, exp_w_gate, exp_w_up, exp_w_down) for l in range(DEPTH)]
    consts = _expanders()
    final_g = final_norm_g.reshape(1, D_MODEL)
    return (_encoder(x_prompt, layers, final_g, consts), _encoder(x_sample, layers, final_g, consts))
```

```python
import functools
import numpy as np
import jax
import jax.numpy as jnp
from jax import lax
from jax.experimental import pallas as pl
from jax.experimental.pallas import tpu as pltpu

F32, BF16, I32 = jnp.float32, jnp.bfloat16, jnp.int32

D_MODEL = 1024
DEPTH = 2
GROUP_W = 256
RG_C = 8.0
POOL_WINDOWS = (2, 4, 8, 16)
DN_HEADS = 4
DN_DK = 64
DN_CHUNK = 64
ATT_HD = 64
ROT_DIM = 16
ROPE_THETA = 500000.0
ATT_WINDOWS = (128, 512, 2048)
ATT_DILATIONS = (1, 4, 16)
N_EXPERTS = 16
EC_CAPACITY = 2
EPS = 1e-6
NEG = -1e30

COL_A, COL_B, COL_CQ, COL_CG, COL_D, COL_CAB, COL_END = 0, 512, 768, 1536, 1792, 2560, 2688

ROW_TILE = 512
HALO = 8
DELTA_CB = 4
ATT_QB = 128
ATT_UNROLL = 4
TOK_BLK = 256
GATHER_W = 64
GATHER_JB = 4
GATHER_E = 2
SCATTER_W = 64
SCATTER_G = 4
VMEM_LIMIT = 56 * 1024 * 1024


def _cparams(sem):
    return pltpu.CompilerParams(dimension_semantics=sem, vmem_limit_bytes=VMEM_LIMIT)


def _dot(a, b):
    return jnp.dot(a.astype(BF16), b.astype(BF16), preferred_element_type=F32)


def _split(a, pieces):
    out, rem = [], a
    for i in range(pieces):
        t = rem.astype(BF16)
        out.append(t)
        if i + 1 < pieces:
            rem = rem - t.astype(F32)
    return out


def _dot_sel(a, sel, pieces=3):
    sel = sel.astype(BF16)
    return sum(jnp.dot(t, sel, preferred_element_type=F32) for t in _split(a, pieces))


def _sel_dot(sel, a, pieces=3):
    sel = sel.astype(BF16)
    return sum(jnp.dot(sel, t, preferred_element_type=F32) for t in _split(a, pieces))


def _dot_nt(a, b):
    return lax.dot_general(a.astype(BF16), b.astype(BF16), (((1,), (1,)), ((), ())), preferred_element_type=F32)


def _dot_tn(a, b):
    return lax.dot_general(a.astype(BF16), b.astype(BF16), (((0,), (0,)), ((), ())), preferred_element_type=F32)


def _sigmoid(x):
    return 1.0 / (1.0 + jnp.exp(-x))


def _softplus(x):
    return jnp.maximum(x, 0.0) + jnp.log1p(jnp.exp(-jnp.abs(x)))


def _shift_rows(e, k):
    n = e.shape[0]
    return e if k % n == 0 else pltpu.roll(e, (-k) % n, axis=0)


def _with_halo(cur_ref, prev_ref, next_ref, first, last):
    prev = jnp.where(first, 0.0, prev_ref[0])
    nxt = jnp.where(last, 0.0, next_ref[0])
    return jnp.concatenate([prev, cur_ref[0], nxt], axis=0)


def _halo_specs(ts, width, S, blk_of):
    per = ts // HALO
    last = S // HALO - 1
    cur = pl.BlockSpec((1, ts, width), lambda b, c: (b, blk_of(c), 0))
    prev = pl.BlockSpec((1, HALO, width), lambda b, c: (b, jnp.maximum(blk_of(c) * per - 1, 0), 0))
    nxt = pl.BlockSpec((1, HALO, width), lambda b, c: (b, jnp.minimum((blk_of(c) + 1) * per, last), 0))
    return cur, prev, nxt


def _inproj_kernel(x_ref, g_ref, w_ref, c_ref, s1_ref, s2_ref, pa_ref, pb_ref, pq_ref, pg_ref, pd_ref, pcab_ref):
    x = x_ref[...]
    h = x * lax.rsqrt(jnp.mean(x * x, axis=-1, keepdims=True) + EPS) * g_ref[...]
    hb = h.astype(BF16)

    def mm(lo, hi):
        return jnp.dot(hb, w_ref[:, lo:hi], preferred_element_type=F32)

    pa_ref[...] = mm(COL_A, COL_B)
    pb_ref[...] = mm(COL_B, COL_CQ)
    pq_ref[...] = mm(COL_CQ, COL_CG)
    pg_ref[...] = mm(COL_CG, COL_D)
    pcab_ref[...] = mm(COL_CAB, COL_END)
    c, s1, s2 = c_ref[...], s1_ref[...], s2_ref[...]
    for blk in range(4):
        lo = COL_D + 128 * blk
        y = mm(lo, lo + 128)
        y = y * c + pltpu.roll(y, 128 - ROT_DIM // 2, axis=1) * s1 + pltpu.roll(y, ROT_DIM // 2, axis=1) * s2
        if blk < 2:
            y = y * (ATT_HD ** -0.5)
        pd_ref[:, 128 * blk:128 * blk + 128] = y
    pd_ref[:, 512:768] = mm(COL_D + 512, COL_D + 768)


def _inproj(xt, g1, w_cat, rope_c, rope_s1, rope_s2, S):
    T = xt.shape[0]
    tm = min(ROW_TILE, S)
    per_seq = S // tm
    widths = (512, 256, 768, 256, 768, 128)
    row = lambda w: pl.BlockSpec((tm, w), lambda i: (i, 0))
    rope = pl.BlockSpec((tm, 128), lambda i: (i % per_seq, 0))
    return pl.pallas_call(
        _inproj_kernel,
        out_shape=[jax.ShapeDtypeStruct((T, w), F32) for w in widths],
        grid=(T // tm,),
        in_specs=[row(D_MODEL), pl.BlockSpec((1, D_MODEL), lambda i: (0, 0)),
                  pl.BlockSpec((D_MODEL, COL_END), lambda i: (0, 0)), rope, rope, rope],
        out_specs=[row(w) for w in widths],
        compiler_params=_cparams(("parallel",)),
        name="inproj",
    )(xt, g1, w_cat, rope_c, rope_s1, rope_s2)


def _rglru_kernel(*refs, reverse, ts, nc):
    if reverse:
        (cur_ref, prev_ref, next_ref, gate_ref, hf_ref, cw_ref, cb_ref, wa_ref, ba_ref, wx_ref, bx_ref, lam_ref,
         out_ref, a_ref, b_ref, carry_ref) = refs
    else:
        (cur_ref, prev_ref, next_ref, cw_ref, cb_ref, wa_ref, ba_ref, wx_ref, bx_ref, lam_ref,
         out_ref, a_ref, b_ref, carry_ref) = refs
    c = pl.program_id(1)
    blk = (nc - 1 - c) if reverse else c
    e = _with_halo(cur_ref, prev_ref, next_ref, blk == 0, blk == nc - 1)
    cw = cw_ref[...]
    sl = slice(HALO, HALO + ts)
    u = (cw[0:1] * _shift_rows(e, -2)[sl] + cw[1:2] * _shift_rows(e, -1)[sl] + cw[2:3] * e[sl]
         + cw[3:4] * _shift_rows(e, 1)[sl]) + cb_ref[...]
    r = _sigmoid(_dot(u, wa_ref[...]) + ba_ref[...])
    i = _sigmoid(_dot(u, wx_ref[...]) + bx_ref[...])
    log_a = -RG_C * r * _softplus(-lam_ref[...])
    a_ref[...] = jnp.exp(log_a)
    b_ref[...] = jnp.sqrt(1.0 - jnp.exp(2.0 * log_a)) * (i * u)

    @pl.when(c == 0)
    def _():
        carry_ref[...] = jnp.zeros_like(carry_ref)

    row = lax.broadcasted_iota(I32, (HALO, GROUP_W), 0)
    nt = ts // HALO

    def body(it, carry):
        ti = (nt - 1 - it) if reverse else it
        st = pl.multiple_of(ti * HALO, HALO)
        a = a_ref[pl.ds(st, HALO), :]
        b = b_ref[pl.ds(st, HALO), :]
        for d in (1, 2, 4):
            k = d if reverse else -d
            valid = (row < HALO - d) if reverse else (row >= d)
            b = jnp.where(valid, a * _shift_rows(b, k) + b, b)
            a = jnp.where(valid, a * _shift_rows(a, k), a)
        h = a * carry + b
        if reverse:
            g = gate_ref[0, pl.ds(st, HALO), :]
            cdf = 0.5 * (1.0 + jnp.tanh(np.float32(np.sqrt(2.0 / np.pi)) * (g + 0.044715 * (g * g * g))))
            out_ref[0, pl.ds(st, HALO), :] = (g * cdf) * (hf_ref[0, pl.ds(st, HALO), :] + h)
            return jnp.broadcast_to(h[0:1], h.shape)
        out_ref[0, pl.ds(st, HALO), :] = h
        return jnp.broadcast_to(h[HALO - 1:HALO], h.shape)

    carry_ref[...] = lax.fori_loop(0, nt, body, carry_ref[...], unroll=4)


def _rglru(pa3, hf, cw, cb, wa, ba, wx, bx, lam, reverse):
    B, S, _ = pa3.shape
    ts = min(ROW_TILE, S)
    nc = S // ts
    blk_of = (lambda c: nc - 1 - c) if reverse else (lambda c: c)
    cur, prev, nxt = _halo_specs(ts, GROUP_W, S, blk_of)
    tile = pl.BlockSpec((1, ts, GROUP_W), lambda b, c: (b, blk_of(c), 0))
    const = lambda shape: pl.BlockSpec(shape, lambda b, c: (0,) * len(shape))
    in_specs = [cur, prev, nxt]
    args = [pa3, pa3, pa3]
    if reverse:
        in_specs += [pl.BlockSpec((1, ts, GROUP_W), lambda b, c: (b, blk_of(c), 1)), tile]
        args += [pa3, hf]
    in_specs += [const((4, GROUP_W)), const((1, GROUP_W)), const((GROUP_W, GROUP_W)), const((1, GROUP_W)),
                 const((GROUP_W, GROUP_W)), const((1, GROUP_W)), const((1, GROUP_W))]
    args += [cw, cb, wa, ba, wx, bx, lam]
    return pl.pallas_call(
        functools.partial(_rglru_kernel, reverse=reverse, ts=ts, nc=nc),
        out_shape=jax.ShapeDtypeStruct((B, S, GROUP_W), F32),
        grid=(B, nc),
        in_specs=in_specs,
        out_specs=tile,
        scratch_shapes=[pltpu.VMEM((ts, GROUP_W), F32), pltpu.VMEM((ts, GROUP_W), F32),
                        pltpu.VMEM((HALO, GROUP_W), F32)],
        compiler_params=_cparams(("parallel", "arbitrary")),
        name="rglru_bwd" if reverse else "rglru_fwd",
    )(*args)


def _pool_kernel(cur_ref, prev_ref, next_ref, w_ref, sc_ref, out_ref, *, ts, nc, S):
    c = pl.program_id(1)
    e = _with_halo(cur_ref, prev_ref, next_ref, c == 0, c == nc - 1)
    sl = slice(HALO, HALO + ts)
    a2 = e + _shift_rows(e, 1)
    a4 = a2 + _shift_rows(a2, 2)
    a8 = a4 + _shift_rows(a4, 4)
    a16 = a8 + _shift_rows(a8, 8)
    sums = [_shift_rows(a, -(w // 2))[sl] for a, w in zip((a2, a4, a8, a16), POOL_WINDOWS)]
    gi = lax.broadcasted_iota(I32, (ts, GROUP_W), 1) // (GROUP_W // len(POOL_WINDOWS))
    ssum = jnp.where(gi == 0, sums[0], jnp.where(gi == 1, sums[1], jnp.where(gi == 2, sums[2], sums[3])))
    hw = jnp.where(gi == 0, 1, jnp.where(gi == 1, 2, jnp.where(gi == 2, 4, 8)))
    t = c * ts + lax.broadcasted_iota(I32, (ts, GROUP_W), 0)
    cnt = (jnp.minimum(t + hw, S) - jnp.maximum(t - hw, 0)).astype(F32)
    p = ssum / cnt - e[sl]
    out_ref[0] = _dot(p, w_ref[...]) * sc_ref[...]


def _pool(pb3, w_bd, scale):
    B, S, _ = pb3.shape
    ts = min(ROW_TILE, S)
    nc = S // ts
    cur, prev, nxt = _halo_specs(ts, GROUP_W, S, lambda c: c)
    return pl.pallas_call(
        functools.partial(_pool_kernel, ts=ts, nc=nc, S=S),
        out_shape=jax.ShapeDtypeStruct((B, S, GROUP_W), F32),
        grid=(B, nc),
        in_specs=[cur, prev, nxt, pl.BlockSpec((GROUP_W, GROUP_W), lambda b, c: (0, 0)),
                  pl.BlockSpec((1, GROUP_W), lambda b, c: (0, 0))],
        out_specs=pl.BlockSpec((1, ts, GROUP_W), lambda b, c: (b, c, 0)),
        compiler_params=_cparams(("parallel", "parallel")),
        name="pool",
    )(pb3, pb3, pb3, w_bd, scale)


def _head_sum_matrix(n, group):
    r = lax.broadcasted_iota(I32, (n, n), 0) // group
    c = lax.broadcasted_iota(I32, (n, n), 1) // group
    return (r == c).astype(F32)


def _dnprep_kernel(cur_ref, prev_ref, next_ref, cab_ref, cw_ref, alog_ref, dtb_ref, eg_ref, eb_ref,
                   q_ref, k_ref, v_ref, gcf_ref, gcb_ref, bef_ref, beb_ref, gcn_ref, *, ts, nc):
    c = pl.program_id(1)
    e = _with_halo(cur_ref, prev_ref, next_ref, c == 0, c == nc - 1)
    cw = cw_ref[...]
    sl = slice(HALO, HALO + ts)
    y = (cw[0:1] * _shift_rows(e, -2)[sl] + cw[1:2] * _shift_rows(e, -1)[sl] + cw[2:3] * e[sl]
         + cw[3:4] * _shift_rows(e, 1)[sl])
    y = y * _sigmoid(y)
    q, k = y[:, 0:GROUP_W], y[:, GROUP_W:2 * GROUP_W]
    hs = _head_sum_matrix(GROUP_W, DN_DK)
    q_ref[0] = q * lax.rsqrt(_dot_sel(q * q, hs) + EPS) * (DN_DK ** -0.5)
    k_ref[0] = k * lax.rsqrt(_dot_sel(k * k, hs) + EPS)
    v_ref[0] = y[:, 2 * GROUP_W:3 * GROUP_W]
    cab = cab_ref[0]
    g = -jnp.exp(alog_ref[...]) * _softplus(cab + dtb_ref[...])
    beta = _sigmoid(cab)
    bef_ref[0] = _dot_sel(beta, eb_ref[0])
    beb_ref[0] = _dot_sel(beta, eb_ref[1])
    r = lax.broadcasted_iota(I32, (ts, ts), 0)
    cc = lax.broadcasted_iota(I32, (ts, ts), 1)
    same = (r // DN_CHUNK) == (cc // DN_CHUNK)
    lane = lax.broadcasted_iota(I32, (ts, 128), 1)
    gcn = jnp.where(lane < DN_HEADS, _sel_dot(same & (r >= cc), g), _sel_dot(same & (r <= cc), g))
    gcn_ref[0] = gcn
    gcf_ref[0] = _dot_sel(gcn, eg_ref[0])
    gcb_ref[0] = _dot_sel(gcn, eg_ref[1])


def _dnprep(pq3, pcab3, cw, alog_row, dtb_row, eg, eb):
    B, S, _ = pq3.shape
    ts = min(ROW_TILE, S)
    nc = S // ts
    cur, prev, nxt = _halo_specs(ts, 3 * GROUP_W, S, lambda c: c)
    tile = pl.BlockSpec((1, ts, GROUP_W), lambda b, c: (b, c, 0))
    const = lambda shape: pl.BlockSpec(shape, lambda b, c: (0,) * len(shape))
    return pl.pallas_call(
        functools.partial(_dnprep_kernel, ts=ts, nc=nc),
        out_shape=[jax.ShapeDtypeStruct((B, S, GROUP_W), F32)] * 7 + [jax.ShapeDtypeStruct((B, S, 128), F32)],
        grid=(B, nc),
        in_specs=[cur, prev, nxt, pl.BlockSpec((1, ts, 128), lambda b, c: (b, c, 0)),
                  const((4, 3 * GROUP_W)), const((1, 128)), const((1, 128)),
                  const((2, 128, GROUP_W)), const((2, 128, GROUP_W))],
        out_specs=[tile] * 7 + [pl.BlockSpec((1, ts, 128), lambda b, c: (b, c, 0))],
        compiler_params=_cparams(("parallel", "parallel")),
        name="dnprep",
    )(pq3, pq3, pq3, pcab3, cw, alog_row, dtb_row, eg, eb)


def _delta_kernel(qf, kf, vf, gcf, bef, grf, qb, kb, vb, gcb, beb, grb, bd_ref, of_ref, ob_ref,
                  s_ref, p_scr, x_scr, wq_scr, u_scr, at_scr, kd_scr, vb_scr, kg_scr, *, cb):
    j = pl.program_id(1)

    @pl.when(j == 0)
    def _():
        s_ref[...] = jnp.zeros_like(s_ref)

    C = DN_CHUNK
    bd = bd_ref[...] > 0.0
    c_idx = lax.broadcasted_iota(I32, (C, GROUP_W), 0)
    m_idx = lax.broadcasted_iota(I32, (C, GROUP_W), 1) % C
    eye = (c_idx == m_idx).astype(F32)

    def blockdiag(x):
        return jnp.where(bd, jnp.tile(x.astype(BF16), (DN_HEADS, 1)), jnp.zeros((), BF16))

    dirs = ((qf, kf, vf, gcf, bef, grf, of_ref), (qb, kb, vb, gcb, beb, grb, ob_ref))
    units = [(d, ci) for ci in range(cb) for d in range(2)]

    def chunk_of(d, ci):
        return cb - 1 - ci if d == 1 else ci

    for ui, (d, ci) in enumerate(units):
        q_r, k_r, v_r, gc_r, be_r, gr_r, _ = dirs[d]
        rev = d == 1
        tril = (c_idx <= m_idx) if rev else (c_idx >= m_idx)
        strict = (c_idx < m_idx) if rev else (c_idx > m_idx)
        last = 0 if rev else C - 1
        cc = chunk_of(d, ci)
        rows = slice(cc * C, (cc + 1) * C)
        q, k, v = q_r[0, rows, :], k_r[0, rows, :], v_r[0, rows, :]
        gc, be, gr = gc_r[0, rows, :], be_r[0, rows, :], gr_r[0, cc]
        eg = jnp.exp(gc)
        kbeta = k * be
        kkqk = _dot_nt(jnp.concatenate([kbeta, q], axis=0), blockdiag(k))
        decay = jnp.exp(jnp.where(tril, gc - gr, -jnp.inf))
        at_scr[ui] = kkqk[C:] * decay
        p = -jnp.where(strict, kkqk[:C] * decay, 0.0)
        p_scr[ui] = p
        x_scr[ui] = eye + p
        wq_scr[ui, C:2 * C, :] = q * eg
        kd_scr[ui] = k * jnp.exp(gc[last:last + 1] - gc)
        vb_scr[ui] = v * be
        kg_scr[ui] = kbeta * eg
    for ui in range(len(units)):
        p = p_scr[ui]
        p_scr[ui] = _dot(p, blockdiag(p))
    for _ in range(4):
        for ui in range(len(units)):
            p, x = p_scr[ui], x_scr[ui]
            px = _dot(jnp.concatenate([p, x], axis=0), blockdiag(p))
            p_scr[ui] = px[:C]
            x_scr[ui] = x + px[C:]
    for ui in range(len(units)):
        x = x_scr[ui]
        x = x + _dot(x, blockdiag(p_scr[ui]))
        u_scr[ui] = _dot(x, blockdiag(vb_scr[ui]))
        wq_scr[ui, 0:C, :] = _dot(x, blockdiag(kg_scr[ui]))
    for ui, (d, ci) in enumerate(units):
        gc_r, o_r = dirs[d][3], dirs[d][6]
        cc = chunk_of(d, ci)
        last = cc * C + (0 if d == 1 else C - 1)
        state = s_ref[d]
        ws_qs = _dot(wq_scr[ui], state)
        v_new = u_scr[ui] - ws_qs[:C]
        o_r[0, cc * C:(cc + 1) * C, :] = ws_qs[C:] + _dot(at_scr[ui], blockdiag(v_new))
        s_ref[d] = (state * jnp.exp(gc_r[0, last:last + 1, :])
                    + jnp.where(bd, _dot_tn(kd_scr[ui], v_new), 0.0))


def _delta(qn, kn, vv, gcf, gcb, bef, beb, grf, grb, bdmask):
    B, S, _ = qn.shape
    cb = DELTA_CB
    rb = cb * DN_CHUNK
    nb = S // rb
    f = lambda b, j: (b, j, 0)
    r = lambda b, j: (b, nb - 1 - j, 0)
    tf = pl.BlockSpec((1, rb, GROUP_W), f)
    tr = pl.BlockSpec((1, rb, GROUP_W), r)
    gf = pl.BlockSpec((1, cb, 1, GROUP_W), lambda b, j: (b, j, 0, 0))
    gr = pl.BlockSpec((1, cb, 1, GROUP_W), lambda b, j: (b, nb - 1 - j, 0, 0))
    return pl.pallas_call(
        functools.partial(_delta_kernel, cb=cb),
        out_shape=[jax.ShapeDtypeStruct((B, S, GROUP_W), F32)] * 2,
        grid=(B, nb),
        in_specs=[tf, tf, tf, tf, tf, gf, tr, tr, tr, tr, tr, gr,
                  pl.BlockSpec((GROUP_W, GROUP_W), lambda b, j: (0, 0))],
        out_specs=[tf, tr],
        scratch_shapes=[pltpu.VMEM((2, GROUP_W, GROUP_W), F32)]
                       + [pltpu.VMEM((2 * cb, n * DN_CHUNK, GROUP_W), F32) for n in (1, 1, 2, 1, 1, 1, 1, 1)],
        compiler_params=_cparams(("parallel", "arbitrary")),
        name="delta",
    )(qn, kn, vv, gcf, bef, grf, qn, kn, vv, gcb, beb, grb, bdmask)


def _attn_kernel(q_ref, k_ref, v_ref, out_ref, m_ref, l_ref, o_ref, *, S):
    head0 = lax.broadcasted_iota(I32, (1, 128), 1) < ATT_HD

    def rows(start, n, dil):
        return pl.ds(start, n) if dil == 1 else pl.ds(start, n, stride=dil)

    for bi, (win, dil) in enumerate(zip(ATT_WINDOWS, ATT_DILATIONS)):
        half = win // (2 * dil)
        L = S // dil
        qb_n = min(ATT_QB, L)
        kw = min(L, qb_n + 2 * half)
        nqb = L // qb_n

        def block_stats(idx, dil=dil, half=half, L=L, qb_n=qb_n, kw=kw, nqb=nqb):
            r = idx // nqb
            m0 = (idx % nqb) * qb_n
            ks = jnp.clip(m0 - half, 0, L - kw)
            qsel = rows(r + m0 * dil, qb_n, dil)
            ksel = rows(r + ks * dil, kw, dil)
            q = q_ref[0, qsel, :]
            kk = k_ref[0, ksel, :].astype(BF16)
            vv = v_ref[0, ksel, :].astype(BF16)
            qpos = m0 + lax.broadcasted_iota(I32, (qb_n, kw), 0)
            kpos = ks + lax.broadcasted_iota(I32, (qb_n, kw), 1)
            valid = jnp.abs(kpos - qpos) <= half
            ms, ls, os_ = [], [], []
            for h in range(2):
                qh = jnp.where(head0 if h == 0 else ~head0, q, 0.0)
                s = jnp.where(valid, _dot_nt(qh, kk), NEG)
                m = jnp.max(s, axis=-1, keepdims=True)
                p = jnp.exp(s - m)
                ms.append(m)
                ls.append(jnp.sum(p, axis=-1, keepdims=True))
                os_.append(_dot(p, vv))
            return (qsel, jnp.where(head0, ms[0], ms[1]), jnp.where(head0, ls[0], ls[1]),
                    jnp.where(head0, os_[0], os_[1]))

        def body(it, carry, bi=bi):
            stats = [block_stats(it * ATT_UNROLL + s) for s in range(ATT_UNROLL)]
            if bi > 0:
                olds = [(m_ref[qsel, :], l_ref[qsel, :], o_ref[qsel, :]) for qsel, _, _, _ in stats]
                merged = []
                for (qsel, m, l, o), (m_old, l_old, o_old) in zip(stats, olds):
                    m_new = jnp.maximum(m_old, m)
                    w_old, w_cur = jnp.exp(m_old - m_new), jnp.exp(m - m_new)
                    merged.append((qsel, m_new, w_old * l_old + w_cur * l, w_old * o_old + w_cur * o))
                stats = merged
            for qsel, m, l, o in stats:
                if bi == len(ATT_WINDOWS) - 1:
                    out_ref[0, qsel, :] = o / l
                else:
                    m_ref[qsel, :] = m
                    l_ref[qsel, :] = l
                    o_ref[qsel, :] = o
            return carry

        lax.fori_loop(0, dil * nqb // ATT_UNROLL, body, 0)


def _attention(pd3):
    B, S, _ = pd3.shape
    spec = lambda off: pl.BlockSpec((1, S, 128), lambda b, p: (b, 0, off + p))
    return pl.pallas_call(
        functools.partial(_attn_kernel, S=S),
        out_shape=jax.ShapeDtypeStruct((B, S, GROUP_W), F32),
        grid=(B, 2),
        in_specs=[spec(0), spec(2), spec(4)],
        out_specs=pl.BlockSpec((1, S, 128), lambda b, p: (b, 0, p)),
        scratch_shapes=[pltpu.VMEM((S, 128), F32)] * 3,
        compiler_params=_cparams(("parallel", "parallel")),
        name="attention",
    )(pd3, pd3, pd3)


def _outproj_kernel(ya_ref, yb_ref, of_ref, ob_ref, cg_ref, yd_ref, x_ref, dng_ref, mg_ref, wo_ref, g2_ref, rw_ref,
                    x2_ref, h2_ref, aff_ref):
    o = of_ref[...] + ob_ref[...]
    ms = _dot_sel(o * o, _head_sum_matrix(GROUP_W, DN_DK)) * (1.0 / DN_DK)
    cg = cg_ref[...]
    yc = (o * lax.rsqrt(ms + EPS) * dng_ref[...]) * (cg * _sigmoid(cg))
    acc = x_ref[...]
    for gi, y in enumerate((ya_ref[...], yb_ref[...], yc, yd_ref[...])):
        sl = slice(gi * GROUP_W, (gi + 1) * GROUP_W)
        mix = y * lax.rsqrt(jnp.mean(y * y, axis=-1, keepdims=True) + EPS) * mg_ref[:, sl]
        acc = acc + _dot(mix, wo_ref[sl, :])
    x2_ref[...] = acc
    h2 = acc * lax.rsqrt(jnp.mean(acc * acc, axis=-1, keepdims=True) + EPS) * g2_ref[...]
    h2_ref[...] = h2.astype(BF16)
    (h_hi, h_lo), (w_hi, w_lo) = _split(h2, 2), _split(rw_ref[...], 2)
    logits = (jnp.dot(h_hi, w_hi, preferred_element_type=F32) + jnp.dot(h_hi, w_lo, preferred_element_type=F32)
              + jnp.dot(h_lo, w_hi, preferred_element_type=F32))
    logits = jnp.where(lax.broadcasted_iota(I32, logits.shape, 1) < N_EXPERTS, logits, -jnp.inf)
    ex = jnp.exp(logits - jnp.max(logits, axis=-1, keepdims=True))
    aff_ref[...] = ex / jnp.sum(ex, axis=-1, keepdims=True)


def _outproj(ya, yb, of, ob, cg, yd, xt, dng, mg, wo, g2, rw):
    T = xt.shape[0]
    tm = 256
    row = lambda w: pl.BlockSpec((tm, w), lambda i: (i, 0))
    const = lambda shape: pl.BlockSpec(shape, lambda i: (0, 0))
    return pl.pallas_call(
        _outproj_kernel,
        out_shape=[jax.ShapeDtypeStruct((T, D_MODEL), F32), jax.ShapeDtypeStruct((T, D_MODEL), BF16),
                   jax.ShapeDtypeStruct((T, 128), F32)],
        grid=(T // tm,),
        in_specs=[row(GROUP_W)] * 6 + [row(D_MODEL), const((1, GROUP_W)), const((1, D_MODEL)),
                                       const((D_MODEL, D_MODEL)), const((1, D_MODEL)), const((D_MODEL, 128))],
        out_specs=[row(D_MODEL), row(D_MODEL), row(128)],
        compiler_params=_cparams(("parallel",)),
        name="outproj",
    )(ya, yb, of, ob, cg, yd, xt, dng, mg, wo, g2, rw)


def _strict_upper(n):
    return (lax.broadcasted_iota(I32, (n, n), 0) < lax.broadcasted_iota(I32, (n, n), 1)).astype(BF16)


def _route_kernel(aff_ref, mask_ref, pos_ref, wsel_ref, offs_ref, *, cap, nblk):
    keys = pltpu.bitcast(aff_ref[...], I32)

    def bit_body(i, thr):
        cand = thr | lax.shift_left(jnp.int32(1), 30 - i)
        cnt = jnp.sum((keys >= cand).astype(F32), axis=1, keepdims=True)
        return jnp.where(cnt >= cap, cand, thr)

    thr = lax.fori_loop(0, 31, bit_body, jnp.zeros((N_EXPERTS, 1), I32))
    need = cap - jnp.sum((keys > thr).astype(F32), axis=1, keepdims=True)
    su = _strict_upper(TOK_BLK)

    def blk_body(j, carry):
        ceq, csel = carry
        st = pl.multiple_of(j * TOK_BLK, TOK_BLK)
        kb = pltpu.bitcast(aff_ref[:, pl.ds(st, TOK_BLK)], I32)
        eqf = (kb == thr).astype(F32)
        rank = ceq + _dot(eqf, su)
        sel = ((kb > thr) | ((kb == thr) & (rank < need))).astype(F32)
        mask_ref[:, pl.ds(st, TOK_BLK)] = sel
        wsel_ref[:, pl.ds(st, TOK_BLK)] = sel * aff_ref[:, pl.ds(st, TOK_BLK)]
        pos_ref[:, pl.ds(st, TOK_BLK)] = _dot(sel, su)
        offs_ref[j] = jnp.broadcast_to(csel.astype(I32), (N_EXPERTS, 128))
        return (ceq + jnp.sum(eqf, axis=1, keepdims=True), csel + jnp.sum(sel, axis=1, keepdims=True))

    zero = jnp.zeros((N_EXPERTS, 1), F32)
    lax.fori_loop(0, nblk, blk_body, (zero, zero))


def _route(aff_t, cap):
    E, T = aff_t.shape
    nblk = T // TOK_BLK
    full = lambda shape: pl.BlockSpec(shape, lambda i: (0,) * len(shape))
    return pl.pallas_call(
        functools.partial(_route_kernel, cap=cap, nblk=nblk),
        out_shape=[jax.ShapeDtypeStruct((E, T), F32)] * 3 + [jax.ShapeDtypeStruct((nblk, E, 128), I32)],
        grid=(1,),
        in_specs=[full((E, T))],
        out_specs=[full((E, T))] * 3 + [full((nblk, E, 128))],
        compiler_params=_cparams(("arbitrary",)),
        name="route",
    )(aff_t)


def _gather_kernel(offs_ref, x_ref, m_ref, p_ref, out_ref):
    g, j = pl.program_id(0), pl.program_id(1)

    @pl.when(j == 0)
    def _():
        out_ref[...] = jnp.zeros_like(out_ref)

    W = GATHER_W
    slot0 = lax.broadcasted_iota(I32, (W, TOK_BLK), 0).astype(F32)
    for jb in range(GATHER_JB):
        blk = j * GATHER_JB + jb
        cols = slice(jb * TOK_BLK, (jb + 1) * TOK_BLK)
        targets, bases, passes = [], [], []
        for k in range(GATHER_E):
            off = offs_ref[blk * N_EXPERTS + g * GATHER_E + k]
            cnt = offs_ref[(blk + 1) * N_EXPERTS + g * GATHER_E + k] - off
            base = (off // 16) * 16
            targets.append(jnp.where(m_ref[k, :, cols] > 0.0, p_ref[k, :, cols] + (off - base).astype(F32), -1.0))
            bases.append(base)
            passes.append((off - base + cnt + W - 1) // W)
        onehot = jnp.concatenate([jnp.where(slot0 == t, 1.0, 0.0) for t in targets], axis=0).astype(BF16)
        picked = jnp.dot(onehot, x_ref[cols, :], preferred_element_type=F32).astype(BF16)
        for k in range(GATHER_E):
            out_ref[k, pl.ds(pl.multiple_of(bases[k], 16), W), :] += picked[k * W:(k + 1) * W]
        for k in range(GATHER_E):
            def more(ps, carry, k=k, cols=cols):
                hot = jnp.where(slot0 + (ps * W).astype(F32) == targets[k], 1.0, 0.0).astype(BF16)
                extra = jnp.dot(hot, x_ref[cols, :], preferred_element_type=F32).astype(BF16)
                out_ref[k, pl.ds(pl.multiple_of(bases[k] + ps * W, 16), W), :] += extra
                return carry

            lax.fori_loop(1, passes[k], more, 0)


def _gather(h2, mask3, pos3, offs, cap):
    T = h2.shape[0]
    nblk = T // TOK_BLK
    cap_x = cap + 2 * GATHER_W
    rows = GATHER_JB * TOK_BLK
    grid_spec = pltpu.PrefetchScalarGridSpec(
        num_scalar_prefetch=1,
        grid=(N_EXPERTS // GATHER_E, nblk // GATHER_JB),
        in_specs=[pl.BlockSpec((rows, D_MODEL), lambda g, j, o: (j, 0)),
                  pl.BlockSpec((GATHER_E, 1, rows), lambda g, j, o: (g, 0, j)),
                  pl.BlockSpec((GATHER_E, 1, rows), lambda g, j, o: (g, 0, j))],
        out_specs=pl.BlockSpec((GATHER_E, cap_x, D_MODEL), lambda g, j, o: (g, 0, 0)),
    )
    return pl.pallas_call(
        _gather_kernel,
        out_shape=jax.ShapeDtypeStruct((N_EXPERTS, cap_x, D_MODEL), BF16),
        grid_spec=grid_spec,
        compiler_params=_cparams(("parallel", "arbitrary")),
        name="gather",
    )(offs, h2, mask3, pos3)


def _ffn_kernel(x_ref, wg_ref, wu_ref, wd_ref, o_ref):
    x = x_ref[0]
    g = jnp.dot(x, wg_ref[0], preferred_element_type=F32)
    u = jnp.dot(x, wu_ref[0], preferred_element_type=F32)
    h = (g * _sigmoid(g)) * u
    o_ref[0] = _dot(h, wd_ref[0]).astype(BF16)


def _ffn(xe, wg, wu, wd, cap):
    tf = min(512, cap)
    wspec = pl.BlockSpec((1, D_MODEL, D_MODEL), lambda e, i: (e, 0, 0))
    return pl.pallas_call(
        _ffn_kernel,
        out_shape=jax.ShapeDtypeStruct((N_EXPERTS, cap, D_MODEL), BF16),
        grid=(N_EXPERTS, cap // tf),
        in_specs=[pl.BlockSpec((1, tf, D_MODEL), lambda e, i: (e, i, 0)), wspec, wspec, wspec],
        out_specs=pl.BlockSpec((1, tf, D_MODEL), lambda e, i: (e, i, 0)),
        compiler_params=_cparams(("parallel", "parallel")),
        name="ffn",
    )(xe, wg, wu, wd)


def _scatter_kernel(offs_ref, x2_ref, w_ref, p_ref, gf_ref, *rest, nb, final):
    ye, out_ref = rest[:2 * N_EXPERTS], rest[2 * N_EXPERTS]
    j = pl.program_id(0)
    ps, ws = p_ref[...], w_ref[...]
    offs = [offs_ref[j * N_EXPERTS + e] for e in range(N_EXPERTS)]
    cnts = [offs_ref[(j + 1) * N_EXPERTS + e] - offs[e] for e in range(N_EXPERTS)]
    rels = [offs[e] - jnp.minimum(offs[e] // TOK_BLK, nb - 1) * TOK_BLK for e in range(N_EXPERTS)]
    weights = [ws[:, e:e + 1] for e in range(N_EXPERTS)]
    W, SUB = SCATTER_W, 16
    small = cnts[0] <= W - SUB
    for e in range(1, N_EXPERTS):
        small = jnp.logical_and(small, cnts[e] <= W - SUB)

    def onehot(e, start, width):
        slot = lax.broadcasted_iota(I32, (TOK_BLK, width), 1).astype(F32)
        return jnp.where(slot == ps[:, e:e + 1] + (rels[e] - start).astype(F32), weights[e], 0.0).astype(BF16)

    @pl.when(small)
    def _():
        lane = lax.broadcasted_iota(I32, (TOK_BLK, 2 * W), 1).astype(F32)
        acc = x2_ref[...]
        for g0 in range(0, N_EXPERTS, SCATTER_G):
            windows, pieces = [], []
            for e in range(g0, g0 + SCATTER_G):
                r0 = (rels[e] // SUB) * SUB
                for i in range(W // SUB):
                    r = r0 + i * SUB
                    lo = ye[2 * e][0, pl.ds(pl.multiple_of(jnp.minimum(r, TOK_BLK - SUB), SUB), SUB), :]
                    hi = ye[2 * e + 1][0, pl.ds(pl.multiple_of(jnp.maximum(r - TOK_BLK, 0), SUB), SUB), :]
                    windows.append(jnp.where(r < TOK_BLK, lo, hi))
            for e in range(g0, g0 + SCATTER_G, 2):
                t0 = ps[:, e:e + 1] + (rels[e] - (rels[e] // SUB) * SUB).astype(F32)
                t1 = ps[:, e + 1:e + 2] + (rels[e + 1] - (rels[e + 1] // SUB) * SUB + W).astype(F32)
                pieces.append(jnp.where(lane < W, jnp.where(lane == t0, weights[e], 0.0),
                                        jnp.where(lane == t1, weights[e + 1], 0.0)).astype(BF16))
            acc = acc + jnp.dot(jnp.concatenate(pieces, axis=1), jnp.concatenate(windows, axis=0),
                                preferred_element_type=F32)
        out_ref[...] = acc

    @pl.when(jnp.logical_not(small))
    def _():
        out_ref[...] = x2_ref[...]
        for e in range(N_EXPERTS):
            @pl.when(cnts[e] > 0)
            def _(e=e):
                out_ref[...] += jnp.dot(onehot(e, 0, TOK_BLK), ye[2 * e][0], preferred_element_type=F32)

            @pl.when(rels[e] + cnts[e] > TOK_BLK)
            def _(e=e):
                out_ref[...] += jnp.dot(onehot(e, TOK_BLK, TOK_BLK), ye[2 * e + 1][0],
                                        preferred_element_type=F32)

    if final:
        x = out_ref[...]
        out_ref[...] = x * lax.rsqrt(jnp.mean(x * x, axis=-1, keepdims=True) + EPS) * gf_ref[...]


def _scatter(x2, wsel_c, pos_c, ye, offs, gf, cap, final):
    T = x2.shape[0]
    nblk = T // TOK_BLK
    nb = cap // TOK_BLK

    def ye_spec(e, k):
        return pl.BlockSpec((1, TOK_BLK, D_MODEL),
                            lambda j, o: (e, jnp.minimum(o[j * N_EXPERTS + e] // TOK_BLK + k, nb - 1), 0))

    row = lambda w: pl.BlockSpec((TOK_BLK, w), lambda j, o: (j, 0))
    grid_spec = pltpu.PrefetchScalarGridSpec(
        num_scalar_prefetch=1,
        grid=(nblk,),
        in_specs=[row(D_MODEL), row(N_EXPERTS), row(N_EXPERTS), pl.BlockSpec((1, D_MODEL), lambda j, o: (0, 0))]
                 + [ye_spec(e, k) for e in range(N_EXPERTS) for k in range(2)],
        out_specs=row(D_MODEL),
    )
    return pl.pallas_call(
        functools.partial(_scatter_kernel, nb=nb, final=final),
        out_shape=jax.ShapeDtypeStruct((T, D_MODEL), F32),
        grid_spec=grid_spec,
        compiler_params=_cparams(("arbitrary",)),
        name="scatter",
    )(offs, x2, wsel_c, pos_c, gf, *([ye] * (2 * N_EXPERTS)))


def _block_diag(w):
    n, a, b = w.shape
    out = jnp.zeros((n * a, n * b), w.dtype)
    for i in range(n):
        out = out.at[i * a:(i + 1) * a, i * b:(i + 1) * b].set(w[i])
    return out


def _rope_tables(S):
    inv = ROPE_THETA ** (-jnp.arange(0, ROT_DIM, 2, dtype=F32) / ROT_DIM)
    ang = jnp.arange(S, dtype=F32)[:, None] * inv[None, :]
    cos, sin = jnp.cos(ang), jnp.sin(ang)
    half = ROT_DIM // 2
    ones, zeros = jnp.ones((S, ATT_HD - ROT_DIM), F32), jnp.zeros((S, ATT_HD - ROT_DIM), F32)
    zh = jnp.zeros((S, half), F32)
    c = jnp.concatenate([cos, cos, ones], axis=1)
    s1 = jnp.concatenate([-sin, zh, zeros], axis=1)
    s2 = jnp.concatenate([zh, sin, zeros], axis=1)
    return tuple(jnp.tile(t, (1, 2)) for t in (c, s1, s2))


def _expanders():
    eg = np.zeros((2, 128, GROUP_W), np.float32)
    eb = np.zeros((2, 128, GROUP_W), np.float32)
    for d in range(2):
        for h in range(DN_HEADS):
            eg[d, d * DN_HEADS + h, h * DN_DK:(h + 1) * DN_DK] = 1.0
            eb[d, 2 * DN_HEADS + d * DN_HEADS + h, h * DN_DK:(h + 1) * DN_DK] = 1.0
    hd = np.arange(GROUP_W) // DN_DK
    bd = (hd[:, None] == hd[None, :]).astype(np.float32)
    return jnp.asarray(eg), jnp.asarray(eb), jnp.asarray(bd)


def _layer_params(l, norm1_g, w_in, conv_a_w, conv_a_b, rg_wa, rg_ba, rg_wx, rg_bx, rg_lambda, pool_w, pool_scale,
                  dn_conv_w, dn_A_log, dn_dt_bias, dn_norm_g, mix_norm_g, w_out, norm2_g, router_w,
                  exp_w_gate, exp_w_up, exp_w_down):
    w = w_in[l]
    w_cat = jnp.concatenate([w[:, 0:1536], w[:, 1552:2576], w[:, 1536:1552],
                             jnp.zeros((D_MODEL, COL_END - COL_CAB - 16), F32)], axis=1).astype(BF16)
    pad8 = lambda v: jnp.concatenate([v.reshape(1, 2 * DN_HEADS), jnp.zeros((1, 120), F32)], axis=1)
    return dict(
        g1=norm1_g[l].reshape(1, D_MODEL), w_cat=w_cat,
        conv_a_w=conv_a_w[l], conv_a_b=conv_a_b[l].reshape(1, GROUP_W),
        wa=[_block_diag(rg_wa[l, d]).astype(BF16) for d in range(2)],
        wx=[_block_diag(rg_wx[l, d]).astype(BF16) for d in range(2)],
        ba=[rg_ba[l, d].reshape(1, GROUP_W) for d in range(2)],
        bx=[rg_bx[l, d].reshape(1, GROUP_W) for d in range(2)],
        lam=[rg_lambda[l, d].reshape(1, GROUP_W) for d in range(2)],
        pool_w=_block_diag(pool_w[l]).astype(BF16), pool_scale=pool_scale[l].reshape(1, GROUP_W),
        dn_conv_w=dn_conv_w[l], alog=pad8(dn_A_log[l]), dtb=pad8(dn_dt_bias[l]),
        dng=jnp.tile(dn_norm_g[l], DN_HEADS).reshape(1, GROUP_W),
        mg=mix_norm_g[l].reshape(1, D_MODEL), wo=w_out[l].astype(BF16),
        g2=norm2_g[l].reshape(1, D_MODEL),
        rw=jnp.concatenate([router_w[l], jnp.zeros((D_MODEL, 128 - N_EXPERTS), F32)], axis=1),
        wg=exp_w_gate[l].astype(BF16), wu=exp_w_up[l].astype(BF16), wd=exp_w_down[l].astype(BF16),
    )


def _chunk_rows(gcn, d):
    B, S, _ = gcn.shape
    n = S // DN_CHUNK
    g = gcn[:, :, d * DN_HEADS:(d + 1) * DN_HEADS].reshape(B, n, DN_CHUNK, DN_HEADS)
    return g.transpose(0, 1, 3, 2).reshape(B, n, 1, GROUP_W)


def _encoder(x, layers, final_g, consts):
    B, S, D = x.shape
    T = B * S
    cap = max(1, EC_CAPACITY * T // N_EXPERTS)
    eg, eb, bdmask = consts
    rope_c, rope_s1, rope_s2 = _rope_tables(S)
    xt = x.reshape(T, D)
    for l, p in enumerate(layers):
        pa, pb, pq, pg, pd, pcab = _inproj(xt, p["g1"], p["w_cat"], rope_c, rope_s1, rope_s2, S)
        pa3 = pa.reshape(B, S, 2 * GROUP_W)
        rg = lambda d: (p["conv_a_w"], p["conv_a_b"], p["wa"][d], p["ba"][d], p["wx"][d], p["bx"][d], p["lam"][d])
        hf = _rglru(pa3, None, *rg(0), reverse=False)
        ya = _rglru(pa3, hf, *rg(1), reverse=True)
        yb = _pool(pb.reshape(B, S, GROUP_W), p["pool_w"], p["pool_scale"])
        qn, kn, vv, gcf, gcb, bef, beb, gcn = _dnprep(pq.reshape(B, S, 3 * GROUP_W), pcab.reshape(B, S, 128),
                                                      p["dn_conv_w"], p["alog"], p["dtb"], eg, eb)
        of, ob = _delta(qn, kn, vv, gcf, gcb, bef, beb, _chunk_rows(gcn, 0), _chunk_rows(gcn, 1), bdmask)
        yd = _attention(pd.reshape(B, S, 3 * GROUP_W))
        flat = lambda a: a.reshape(T, GROUP_W)
        x2, h2, aff = _outproj(flat(ya), flat(yb), flat(of), flat(ob), pg, flat(yd), xt,
                               p["dng"], p["mg"], p["wo"], p["g2"], p["rw"])
        mask, pos, wsel, offs3 = _route(aff[:, :N_EXPERTS].T, cap)
        offs = jnp.concatenate([offs3[:, :, 0].reshape(-1), jnp.full((N_EXPERTS,), cap, I32)])
        xe = _gather(h2, mask.reshape(N_EXPERTS, 1, T), pos.reshape(N_EXPERTS, 1, T), offs, cap)
        ye = _ffn(xe, p["wg"], p["wu"], p["wd"], cap)
        xt = _scatter(x2, wsel.T, pos.T, ye, offs, final_g, cap, final=(l == len(layers) - 1))
    return xt.reshape(B, S, D)


def kernel(x_prompt, x_sample, norm1_g, w_in, conv_a_w, conv_a_b, rg_wa, rg_ba, rg_wx, rg_bx, rg_lambda, pool_w, pool_scale, dn_conv_w, dn_A_log, dn_dt_bias, dn_norm_g, mix_norm_g, w_out, norm2_g, router_w, exp_w_gate, exp_w_up, exp_w_down, final_norm_g):
    layers = [_layer_params(l, norm1_g, w_in, conv_a_w, conv_a_b, rg_wa, rg_ba, rg_wx, rg_bx, rg_lambda, pool_w,
                            pool_scale, dn_conv_w, dn_A_log, dn_dt_bias, dn_norm_g, mix_norm_g, w_out, norm2_g,
                            router_w, exp_w_gate, exp_w_up, exp_w_down) for l in range(DEPTH)]
    consts = _expanders()
    final_g = final_norm_g.reshape(1, D_MODEL)
    return (_encoder(x_prompt, layers, final_g, consts), _encoder(x_sample, layers, final_g, consts))
```

```python
import functools
import numpy as np
import jax
import jax.numpy as jnp
from jax import lax
from jax.experimental import pallas as pl
from jax.experimental.pallas import tpu as pltpu

F32, BF16, I32 = jnp.float32, jnp.bfloat16, jnp.int32

D_MODEL = 1024
DEPTH = 2
GROUP_W = 256
RG_C = 8.0
POOL_WINDOWS = (2, 4, 8, 16)
DN_HEADS = 4
DN_DK = 64
DN_CHUNK = 64
ATT_HD = 64
ROT_DIM = 16
ROPE_THETA = 500000.0
ATT_WINDOWS = (128, 512, 2048)
ATT_DILATIONS = (1, 4, 16)
N_EXPERTS = 16
EC_CAPACITY = 2
EPS = 1e-6
NEG = -1e30

COL_A, COL_B, COL_CQ, COL_CG, COL_D, COL_CAB, COL_END = 0, 512, 768, 1536, 1792, 2560, 2688

ROW_TILE = 512
HALO = 8
DELTA_CB = 8
ATT_QB = 128
ATT_UNROLL = 4
TOK_BLK = 256
GATHER_W = 64
GATHER_JB = 4
GATHER_E = 2
SCATTER_W = 128
SCATTER_G = 2
VMEM_LIMIT = 56 * 1024 * 1024


def _cparams(sem):
    return pltpu.CompilerParams(dimension_semantics=sem, vmem_limit_bytes=VMEM_LIMIT)


def _dot(a, b):
    return jnp.dot(a.astype(BF16), b.astype(BF16), preferred_element_type=F32)


def _split(a, pieces):
    out, rem = [], a
    for i in range(pieces):
        t = rem.astype(BF16)
        out.append(t)
        if i + 1 < pieces:
            rem = rem - t.astype(F32)
    return out


def _dot_sel(a, sel, pieces=3):
    sel = sel.astype(BF16)
    return sum(jnp.dot(t, sel, preferred_element_type=F32) for t in _split(a, pieces))


def _sel_dot(sel, a, pieces=3):
    sel = sel.astype(BF16)
    return sum(jnp.dot(sel, t, preferred_element_type=F32) for t in _split(a, pieces))


def _dot_nt(a, b):
    return lax.dot_general(a.astype(BF16), b.astype(BF16), (((1,), (1,)), ((), ())), preferred_element_type=F32)


def _dot_tn(a, b):
    return lax.dot_general(a.astype(BF16), b.astype(BF16), (((0,), (0,)), ((), ())), preferred_element_type=F32)


def _sigmoid(x):
    return 1.0 / (1.0 + jnp.exp(-x))


def _softplus(x):
    return jnp.maximum(x, 0.0) + jnp.log1p(jnp.exp(-jnp.abs(x)))


def _shift_rows(e, k):
    n = e.shape[0]
    return e if k % n == 0 else pltpu.roll(e, (-k) % n, axis=0)


def _with_halo(cur_ref, prev_ref, next_ref, first, last):
    prev = jnp.where(first, 0.0, prev_ref[0])
    nxt = jnp.where(last, 0.0, next_ref[0])
    return jnp.concatenate([prev, cur_ref[0], nxt], axis=0)


def _halo_specs(ts, width, S, blk_of):
    per = ts // HALO
    last = S // HALO - 1
    cur = pl.BlockSpec((1, ts, width), lambda b, c: (b, blk_of(c), 0))
    prev = pl.BlockSpec((1, HALO, width), lambda b, c: (b, jnp.maximum(blk_of(c) * per - 1, 0), 0))
    nxt = pl.BlockSpec((1, HALO, width), lambda b, c: (b, jnp.minimum((blk_of(c) + 1) * per, last), 0))
    return cur, prev, nxt


def _inproj_kernel(x_ref, g_ref, w_ref, c_ref, s1_ref, s2_ref, pa_ref, pb_ref, pq_ref, pg_ref, pd_ref, pcab_ref):
    x = x_ref[...]
    h = x * lax.rsqrt(jnp.mean(x * x, axis=-1, keepdims=True) + EPS) * g_ref[...]
    hb = h.astype(BF16)

    def mm(lo, hi):
        return jnp.dot(hb, w_ref[:, lo:hi], preferred_element_type=F32)

    pa_ref[...] = mm(COL_A, COL_B)
    pb_ref[...] = mm(COL_B, COL_CQ)
    pq_ref[...] = mm(COL_CQ, COL_CG)
    pg_ref[...] = mm(COL_CG, COL_D)
    pcab_ref[...] = mm(COL_CAB, COL_END)
    c, s1, s2 = c_ref[...], s1_ref[...], s2_ref[...]
    for blk in range(4):
        lo = COL_D + 128 * blk
        y = mm(lo, lo + 128)
        y = y * c + pltpu.roll(y, 128 - ROT_DIM // 2, axis=1) * s1 + pltpu.roll(y, ROT_DIM // 2, axis=1) * s2
        if blk < 2:
            y = y * (ATT_HD ** -0.5)
        pd_ref[:, 128 * blk:128 * blk + 128] = y
    pd_ref[:, 512:768] = mm(COL_D + 512, COL_D + 768)


def _inproj(xt, g1, w_cat, rope_c, rope_s1, rope_s2, S):
    T = xt.shape[0]
    tm = min(ROW_TILE, S)
    per_seq = S // tm
    widths = (512, 256, 768, 256, 768, 128)
    row = lambda w: pl.BlockSpec((tm, w), lambda i: (i, 0))
    rope = pl.BlockSpec((tm, 128), lambda i: (i % per_seq, 0))
    return pl.pallas_call(
        _inproj_kernel,
        out_shape=[jax.ShapeDtypeStruct((T, w), F32) for w in widths],
        grid=(T // tm,),
        in_specs=[row(D_MODEL), pl.BlockSpec((1, D_MODEL), lambda i: (0, 0)),
                  pl.BlockSpec((D_MODEL, COL_END), lambda i: (0, 0)), rope, rope, rope],
        out_specs=[row(w) for w in widths],
        compiler_params=_cparams(("parallel",)),
        name="inproj",
    )(xt, g1, w_cat, rope_c, rope_s1, rope_s2)


def _rglru_kernel(*refs, reverse, ts, nc):
    if reverse:
        (cur_ref, prev_ref, next_ref, gate_ref, hf_ref, cw_ref, cb_ref, wa_ref, ba_ref, wx_ref, bx_ref, lam_ref,
         out_ref, a_ref, b_ref, carry_ref) = refs
    else:
        (cur_ref, prev_ref, next_ref, cw_ref, cb_ref, wa_ref, ba_ref, wx_ref, bx_ref, lam_ref,
         out_ref, a_ref, b_ref, carry_ref) = refs
    c = pl.program_id(1)
    blk = (nc - 1 - c) if reverse else c
    e = _with_halo(cur_ref, prev_ref, next_ref, blk == 0, blk == nc - 1)
    cw = cw_ref[...]
    sl = slice(HALO, HALO + ts)
    u = (cw[0:1] * _shift_rows(e, -2)[sl] + cw[1:2] * _shift_rows(e, -1)[sl] + cw[2:3] * e[sl]
         + cw[3:4] * _shift_rows(e, 1)[sl]) + cb_ref[...]
    r = _sigmoid(_dot(u, wa_ref[...]) + ba_ref[...])
    i = _sigmoid(_dot(u, wx_ref[...]) + bx_ref[...])
    log_a = -RG_C * r * _softplus(-lam_ref[...])
    a_ref[...] = jnp.exp(log_a)
    b_ref[...] = jnp.sqrt(1.0 - jnp.exp(2.0 * log_a)) * (i * u)

    @pl.when(c == 0)
    def _():
        carry_ref[...] = jnp.zeros_like(carry_ref)

    row = lax.broadcasted_iota(I32, (HALO, GROUP_W), 0)
    nt = ts // HALO

    def body(it, carry):
        ti = (nt - 1 - it) if reverse else it
        st = pl.multiple_of(ti * HALO, HALO)
        a = a_ref[pl.ds(st, HALO), :]
        b = b_ref[pl.ds(st, HALO), :]
        for d in (1, 2, 4):
            k = d if reverse else -d
            valid = (row < HALO - d) if reverse else (row >= d)
            b = jnp.where(valid, a * _shift_rows(b, k) + b, b)
            a = jnp.where(valid, a * _shift_rows(a, k), a)
        h = a * carry + b
        if reverse:
            g = gate_ref[0, pl.ds(st, HALO), :]
            cdf = 0.5 * (1.0 + jnp.tanh(np.float32(np.sqrt(2.0 / np.pi)) * (g + 0.044715 * (g * g * g))))
            out_ref[0, pl.ds(st, HALO), :] = (g * cdf) * (hf_ref[0, pl.ds(st, HALO), :] + h)
            return jnp.broadcast_to(h[0:1], h.shape)
        out_ref[0, pl.ds(st, HALO), :] = h
        return jnp.broadcast_to(h[HALO - 1:HALO], h.shape)

    carry_ref[...] = lax.fori_loop(0, nt, body, carry_ref[...], unroll=4)


def _rglru(pa3, hf, cw, cb, wa, ba, wx, bx, lam, reverse):
    B, S, _ = pa3.shape
    ts = min(ROW_TILE, S)
    nc = S // ts
    blk_of = (lambda c: nc - 1 - c) if reverse else (lambda c: c)
    cur, prev, nxt = _halo_specs(ts, GROUP_W, S, blk_of)
    tile = pl.BlockSpec((1, ts, GROUP_W), lambda b, c: (b, blk_of(c), 0))
    const = lambda shape: pl.BlockSpec(shape, lambda b, c: (0,) * len(shape))
    in_specs = [cur, prev, nxt]
    args = [pa3, pa3, pa3]
    if reverse:
        in_specs += [pl.BlockSpec((1, ts, GROUP_W), lambda b, c: (b, blk_of(c), 1)), tile]
        args += [pa3, hf]
    in_specs += [const((4, GROUP_W)), const((1, GROUP_W)), const((GROUP_W, GROUP_W)), const((1, GROUP_W)),
                 const((GROUP_W, GROUP_W)), const((1, GROUP_W)), const((1, GROUP_W))]
    args += [cw, cb, wa, ba, wx, bx, lam]
    return pl.pallas_call(
        functools.partial(_rglru_kernel, reverse=reverse, ts=ts, nc=nc),
        out_shape=jax.ShapeDtypeStruct((B, S, GROUP_W), F32),
        grid=(B, nc),
        in_specs=in_specs,
        out_specs=tile,
        scratch_shapes=[pltpu.VMEM((ts, GROUP_W), F32), pltpu.VMEM((ts, GROUP_W), F32),
                        pltpu.VMEM((HALO, GROUP_W), F32)],
        compiler_params=_cparams(("parallel", "arbitrary")),
        name="rglru_bwd" if reverse else "rglru_fwd",
    )(*args)


def _pool_kernel(cur_ref, prev_ref, next_ref, w_ref, sc_ref, out_ref, *, ts, nc, S):
    c = pl.program_id(1)
    e = _with_halo(cur_ref, prev_ref, next_ref, c == 0, c == nc - 1)
    sl = slice(HALO, HALO + ts)
    a2 = e + _shift_rows(e, 1)
    a4 = a2 + _shift_rows(a2, 2)
    a8 = a4 + _shift_rows(a4, 4)
    a16 = a8 + _shift_rows(a8, 8)
    sums = [_shift_rows(a, -(w // 2))[sl] for a, w in zip((a2, a4, a8, a16), POOL_WINDOWS)]
    gi = lax.broadcasted_iota(I32, (ts, GROUP_W), 1) // (GROUP_W // len(POOL_WINDOWS))
    ssum = jnp.where(gi == 0, sums[0], jnp.where(gi == 1, sums[1], jnp.where(gi == 2, sums[2], sums[3])))
    hw = jnp.where(gi == 0, 1, jnp.where(gi == 1, 2, jnp.where(gi == 2, 4, 8)))
    t = c * ts + lax.broadcasted_iota(I32, (ts, GROUP_W), 0)
    cnt = (jnp.minimum(t + hw, S) - jnp.maximum(t - hw, 0)).astype(F32)
    p = ssum / cnt - e[sl]
    out_ref[0] = _dot(p, w_ref[...]) * sc_ref[...]


def _pool(pb3, w_bd, scale):
    B, S, _ = pb3.shape
    ts = min(ROW_TILE, S)
    nc = S // ts
    cur, prev, nxt = _halo_specs(ts, GROUP_W, S, lambda c: c)
    return pl.pallas_call(
        functools.partial(_pool_kernel, ts=ts, nc=nc, S=S),
        out_shape=jax.ShapeDtypeStruct((B, S, GROUP_W), F32),
        grid=(B, nc),
        in_specs=[cur, prev, nxt, pl.BlockSpec((GROUP_W, GROUP_W), lambda b, c: (0, 0)),
                  pl.BlockSpec((1, GROUP_W), lambda b, c: (0, 0))],
        out_specs=pl.BlockSpec((1, ts, GROUP_W), lambda b, c: (b, c, 0)),
        compiler_params=_cparams(("parallel", "parallel")),
        name="pool",
    )(pb3, pb3, pb3, w_bd, scale)


def _head_sum_matrix(n, group):
    r = lax.broadcasted_iota(I32, (n, n), 0) // group
    c = lax.broadcasted_iota(I32, (n, n), 1) // group
    return (r == c).astype(F32)


def _dnprep_kernel(cur_ref, prev_ref, next_ref, cab_ref, cw_ref, alog_ref, dtb_ref, eg_ref, eb_ref,
                   q_ref, k_ref, v_ref, gcf_ref, gcb_ref, bef_ref, beb_ref, gcn_ref, *, ts, nc):
    c = pl.program_id(1)
    e = _with_halo(cur_ref, prev_ref, next_ref, c == 0, c == nc - 1)
    cw = cw_ref[...]
    sl = slice(HALO, HALO + ts)
    y = (cw[0:1] * _shift_rows(e, -2)[sl] + cw[1:2] * _shift_rows(e, -1)[sl] + cw[2:3] * e[sl]
         + cw[3:4] * _shift_rows(e, 1)[sl])
    y = y * _sigmoid(y)
    q, k = y[:, 0:GROUP_W], y[:, GROUP_W:2 * GROUP_W]
    hs = _head_sum_matrix(GROUP_W, DN_DK)
    q_ref[0] = q * lax.rsqrt(_dot_sel(q * q, hs) + EPS) * (DN_DK ** -0.5)
    k_ref[0] = k * lax.rsqrt(_dot_sel(k * k, hs) + EPS)
    v_ref[0] = y[:, 2 * GROUP_W:3 * GROUP_W]
    cab = cab_ref[0]
    g = -jnp.exp(alog_ref[...]) * _softplus(cab + dtb_ref[...])
    beta = _sigmoid(cab)
    bef_ref[0] = _dot_sel(beta, eb_ref[0])
    beb_ref[0] = _dot_sel(beta, eb_ref[1])
    r = lax.broadcasted_iota(I32, (ts, ts), 0)
    cc = lax.broadcasted_iota(I32, (ts, ts), 1)
    same = (r // DN_CHUNK) == (cc // DN_CHUNK)
    lane = lax.broadcasted_iota(I32, (ts, 128), 1)
    gcn = jnp.where(lane < DN_HEADS, _sel_dot(same & (r >= cc), g), _sel_dot(same & (r <= cc), g))
    gcn_ref[0] = gcn
    gcf_ref[0] = _dot_sel(gcn, eg_ref[0])
    gcb_ref[0] = _dot_sel(gcn, eg_ref[1])


def _dnprep(pq3, pcab3, cw, alog_row, dtb_row, eg, eb):
    B, S, _ = pq3.shape
    ts = min(ROW_TILE, S)
    nc = S // ts
    cur, prev, nxt = _halo_specs(ts, 3 * GROUP_W, S, lambda c: c)
    tile = pl.BlockSpec((1, ts, GROUP_W), lambda b, c: (b, c, 0))
    const = lambda shape: pl.BlockSpec(shape, lambda b, c: (0,) * len(shape))
    return pl.pallas_call(
        functools.partial(_dnprep_kernel, ts=ts, nc=nc),
        out_shape=[jax.ShapeDtypeStruct((B, S, GROUP_W), F32)] * 7 + [jax.ShapeDtypeStruct((B, S, 128), F32)],
        grid=(B, nc),
        in_specs=[cur, prev, nxt, pl.BlockSpec((1, ts, 128), lambda b, c: (b, c, 0)),
                  const((4, 3 * GROUP_W)), const((1, 128)), const((1, 128)),
                  const((2, 128, GROUP_W)), const((2, 128, GROUP_W))],
        out_specs=[tile] * 7 + [pl.BlockSpec((1, ts, 128), lambda b, c: (b, c, 0))],
        compiler_params=_cparams(("parallel", "parallel")),
        name="dnprep",
    )(pq3, pq3, pq3, pcab3, cw, alog_row, dtb_row, eg, eb)


def _delta_kernel(qf, kf, vf, gcf, bef, grf, qb, kb, vb, gcb, beb, grb, bd_ref, of_ref, ob_ref,
                  s_ref, p_scr, x_scr, wq_scr, u_scr, at_scr, kd_scr, vb_scr, kg_scr, *, cb):
    j = pl.program_id(1)

    @pl.when(j == 0)
    def _():
        s_ref[...] = jnp.zeros_like(s_ref)

    C = DN_CHUNK
    bd = bd_ref[...] > 0.0
    c_idx = lax.broadcasted_iota(I32, (C, GROUP_W), 0)
    m_idx = lax.broadcasted_iota(I32, (C, GROUP_W), 1) % C
    eye = (c_idx == m_idx).astype(F32)

    def blockdiag(x):
        return jnp.where(bd, jnp.tile(x.astype(BF16), (DN_HEADS, 1)), jnp.zeros((), BF16))

    dirs = ((qf, kf, vf, gcf, bef, grf, of_ref), (qb, kb, vb, gcb, beb, grb, ob_ref))
    units = [(d, ci) for ci in range(cb) for d in range(2)]

    def chunk_of(d, ci):
        return cb - 1 - ci if d == 1 else ci

    for ui, (d, ci) in enumerate(units):
        q_r, k_r, v_r, gc_r, be_r, gr_r, _ = dirs[d]
        rev = d == 1
        tril = (c_idx <= m_idx) if rev else (c_idx >= m_idx)
        strict = (c_idx < m_idx) if rev else (c_idx > m_idx)
        last = 0 if rev else C - 1
        cc = chunk_of(d, ci)
        rows = slice(cc * C, (cc + 1) * C)
        q, k, v = q_r[0, rows, :], k_r[0, rows, :], v_r[0, rows, :]
        gc, be, gr = gc_r[0, rows, :], be_r[0, rows, :], gr_r[0, cc]
        eg = jnp.exp(gc)
        kbeta = k * be
        kkqk = _dot_nt(jnp.concatenate([kbeta, q], axis=0), blockdiag(k))
        decay = jnp.exp(jnp.where(tril, gc - gr, -jnp.inf))
        at_scr[ui] = kkqk[C:] * decay
        p = -jnp.where(strict, kkqk[:C] * decay, 0.0)
        p_scr[ui] = p
        x_scr[ui] = eye + p
        wq_scr[ui, C:2 * C, :] = q * eg
        kd_scr[ui] = k * jnp.exp(gc[last:last + 1] - gc)
        vb_scr[ui] = v * be
        kg_scr[ui] = kbeta * eg
    for ui in range(len(units)):
        p = p_scr[ui]
        p_scr[ui] = _dot(p, blockdiag(p))
    for _ in range(4):
        for ui in range(len(units)):
            p, x = p_scr[ui], x_scr[ui]
            px = _dot(jnp.concatenate([p, x], axis=0), blockdiag(p))
            p_scr[ui] = px[:C]
            x_scr[ui] = x + px[C:]
    for ui in range(len(units)):
        x = x_scr[ui]
        x = x + _dot(x, blockdiag(p_scr[ui]))
        u_scr[ui] = _dot(x, blockdiag(vb_scr[ui]))
        wq_scr[ui, 0:C, :] = _dot(x, blockdiag(kg_scr[ui]))
    for ui, (d, ci) in enumerate(units):
        gc_r, o_r = dirs[d][3], dirs[d][6]
        cc = chunk_of(d, ci)
        last = cc * C + (0 if d == 1 else C - 1)
        state = s_ref[d]
        ws_qs = _dot(wq_scr[ui], state)
        v_new = u_scr[ui] - ws_qs[:C]
        o_r[0, cc * C:(cc + 1) * C, :] = ws_qs[C:] + _dot(at_scr[ui], blockdiag(v_new))
        s_ref[d] = (state * jnp.exp(gc_r[0, last:last + 1, :])
                    + jnp.where(bd, _dot_tn(kd_scr[ui], v_new), 0.0))


def _delta(qn, kn, vv, gcf, gcb, bef, beb, grf, grb, bdmask):
    B, S, _ = qn.shape
    cb = DELTA_CB
    rb = cb * DN_CHUNK
    nb = S // rb
    f = lambda b, j: (b, j, 0)
    r = lambda b, j: (b, nb - 1 - j, 0)
    tf = pl.BlockSpec((1, rb, GROUP_W), f)
    tr = pl.BlockSpec((1, rb, GROUP_W), r)
    gf = pl.BlockSpec((1, cb, 1, GROUP_W), lambda b, j: (b, j, 0, 0))
    gr = pl.BlockSpec((1, cb, 1, GROUP_W), lambda b, j: (b, nb - 1 - j, 0, 0))
    return pl.pallas_call(
        functools.partial(_delta_kernel, cb=cb),
        out_shape=[jax.ShapeDtypeStruct((B, S, GROUP_W), F32)] * 2,
        grid=(B, nb),
        in_specs=[tf, tf, tf, tf, tf, gf, tr, tr, tr, tr, tr, gr,
                  pl.BlockSpec((GROUP_W, GROUP_W), lambda b, j: (0, 0))],
        out_specs=[tf, tr],
        scratch_shapes=[pltpu.VMEM((2, GROUP_W, GROUP_W), F32)]
                       + [pltpu.VMEM((2 * cb, n * DN_CHUNK, GROUP_W), F32) for n in (1, 1, 2, 1, 1, 1, 1, 1)],
        compiler_params=_cparams(("parallel", "arbitrary")),
        name="delta",
    )(qn, kn, vv, gcf, bef, grf, qn, kn, vv, gcb, beb, grb, bdmask)


def _attn_kernel(q_ref, k_ref, v_ref, out_ref, m_ref, l_ref, o_ref, *, S):
    head0 = lax.broadcasted_iota(I32, (1, 128), 1) < ATT_HD

    def rows(start, n, dil):
        return pl.ds(start, n) if dil == 1 else pl.ds(start, n, stride=dil)

    for bi, (win, dil) in enumerate(zip(ATT_WINDOWS, ATT_DILATIONS)):
        half = win // (2 * dil)
        L = S // dil
        qb_n = min(ATT_QB, L)
        kw = min(L, qb_n + 2 * half)
        nqb = L // qb_n

        def block_stats(idx, dil=dil, half=half, L=L, qb_n=qb_n, kw=kw, nqb=nqb):
            r = idx // nqb
            m0 = (idx % nqb) * qb_n
            ks = jnp.clip(m0 - half, 0, L - kw)
            qsel = rows(r + m0 * dil, qb_n, dil)
            ksel = rows(r + ks * dil, kw, dil)
            q = q_ref[0, qsel, :]
            kk = k_ref[0, ksel, :].astype(BF16)
            vv = v_ref[0, ksel, :].astype(BF16)
            qpos = m0 + lax.broadcasted_iota(I32, (qb_n, kw), 0)
            kpos = ks + lax.broadcasted_iota(I32, (qb_n, kw), 1)
            valid = jnp.abs(kpos - qpos) <= half
            ms, ls, os_ = [], [], []
            for h in range(2):
                qh = jnp.where(head0 if h == 0 else ~head0, q, 0.0)
                s = jnp.where(valid, _dot_nt(qh, kk), NEG)
                m = jnp.max(s, axis=-1, keepdims=True)
                p = jnp.exp(s - m)
                ms.append(m)
                ls.append(jnp.sum(p, axis=-1, keepdims=True))
                os_.append(_dot(p, vv))
            return (qsel, jnp.where(head0, ms[0], ms[1]), jnp.where(head0, ls[0], ls[1]),
                    jnp.where(head0, os_[0], os_[1]))

        def body(it, carry, bi=bi):
            stats = [block_stats(it * ATT_UNROLL + s) for s in range(ATT_UNROLL)]
            if bi > 0:
                olds = [(m_ref[qsel, :], l_ref[qsel, :], o_ref[qsel, :]) for qsel, _, _, _ in stats]
                merged = []
                for (qsel, m, l, o), (m_old, l_old, o_old) in zip(stats, olds):
                    m_new = jnp.maximum(m_old, m)
                    w_old, w_cur = jnp.exp(m_old - m_new), jnp.exp(m - m_new)
                    merged.append((qsel, m_new, w_old * l_old + w_cur * l, w_old * o_old + w_cur * o))
                stats = merged
            for qsel, m, l, o in stats:
                if bi == len(ATT_WINDOWS) - 1:
                    out_ref[0, qsel, :] = o / l
                else:
                    m_ref[qsel, :] = m
                    l_ref[qsel, :] = l
                    o_ref[qsel, :] = o
            return carry

        lax.fori_loop(0, dil * nqb // ATT_UNROLL, body, 0)


def _attention(pd3):
    B, S, _ = pd3.shape
    spec = lambda off: pl.BlockSpec((1, S, 128), lambda b, p: (b, 0, off + p))
    return pl.pallas_call(
        functools.partial(_attn_kernel, S=S),
        out_shape=jax.ShapeDtypeStruct((B, S, GROUP_W), F32),
        grid=(B, 2),
        in_specs=[spec(0), spec(2), spec(4)],
        out_specs=pl.BlockSpec((1, S, 128), lambda b, p: (b, 0, p)),
        scratch_shapes=[pltpu.VMEM((S, 128), F32)] * 3,
        compiler_params=_cparams(("parallel", "parallel")),
        name="attention",
    )(pd3, pd3, pd3)


def _outproj_kernel(ya_ref, yb_ref, of_ref, ob_ref, cg_ref, yd_ref, x_ref, dng_ref, mg_ref, wo_ref, g2_ref, rw_ref,
                    x2_ref, h2_ref, aff_ref):
    o = of_ref[...] + ob_ref[...]
    ms = _dot_sel(o * o, _head_sum_matrix(GROUP_W, DN_DK)) * (1.0 / DN_DK)
    cg = cg_ref[...]
    yc = (o * lax.rsqrt(ms + EPS) * dng_ref[...]) * (cg * _sigmoid(cg))
    acc = x_ref[...]
    for gi, y in enumerate((ya_ref[...], yb_ref[...], yc, yd_ref[...])):
        sl = slice(gi * GROUP_W, (gi + 1) * GROUP_W)
        mix = y * lax.rsqrt(jnp.mean(y * y, axis=-1, keepdims=True) + EPS) * mg_ref[:, sl]
        acc = acc + _dot(mix, wo_ref[sl, :])
    x2_ref[...] = acc
    h2 = acc * lax.rsqrt(jnp.mean(acc * acc, axis=-1, keepdims=True) + EPS) * g2_ref[...]
    h2_ref[...] = h2.astype(BF16)
    (h_hi, h_lo), (w_hi, w_lo) = _split(h2, 2), _split(rw_ref[...], 2)
    logits = (jnp.dot(h_hi, w_hi, preferred_element_type=F32) + jnp.dot(h_hi, w_lo, preferred_element_type=F32)
              + jnp.dot(h_lo, w_hi, preferred_element_type=F32))
    logits = jnp.where(lax.broadcasted_iota(I32, logits.shape, 1) < N_EXPERTS, logits, -jnp.inf)
    ex = jnp.exp(logits - jnp.max(logits, axis=-1, keepdims=True))
    aff_ref[...] = ex / jnp.sum(ex, axis=-1, keepdims=True)


def _outproj(ya, yb, of, ob, cg, yd, xt, dng, mg, wo, g2, rw):
    T = xt.shape[0]
    tm = 256
    row = lambda w: pl.BlockSpec((tm, w), lambda i: (i, 0))
    const = lambda shape: pl.BlockSpec(shape, lambda i: (0, 0))
    return pl.pallas_call(
        _outproj_kernel,
        out_shape=[jax.ShapeDtypeStruct((T, D_MODEL), F32), jax.ShapeDtypeStruct((T, D_MODEL), BF16),
                   jax.ShapeDtypeStruct((T, 128), F32)],
        grid=(T // tm,),
        in_specs=[row(GROUP_W)] * 6 + [row(D_MODEL), const((1, GROUP_W)), const((1, D_MODEL)),
                                       const((D_MODEL, D_MODEL)), const((1, D_MODEL)), const((D_MODEL, 128))],
        out_specs=[row(D_MODEL), row(D_MODEL), row(128)],
        compiler_params=_cparams(("parallel",)),
        name="outproj",
    )(ya, yb, of, ob, cg, yd, xt, dng, mg, wo, g2, rw)


def _strict_upper(n):
    return (lax.broadcasted_iota(I32, (n, n), 0) < lax.broadcasted_iota(I32, (n, n), 1)).astype(BF16)


def _route_kernel(aff_ref, mask_ref, pos_ref, wsel_ref, offs_ref, *, cap, nblk):
    keys = pltpu.bitcast(aff_ref[...], I32)

    def bit_body(i, thr):
        cand = thr | lax.shift_left(jnp.int32(1), 30 - i)
        cnt = jnp.sum((keys >= cand).astype(F32), axis=1, keepdims=True)
        return jnp.where(cnt >= cap, cand, thr)

    thr = lax.fori_loop(0, 31, bit_body, jnp.zeros((N_EXPERTS, 1), I32))
    need = cap - jnp.sum((keys > thr).astype(F32), axis=1, keepdims=True)
    su = _strict_upper(TOK_BLK)

    def blk_body(j, carry):
        ceq, csel = carry
        st = pl.multiple_of(j * TOK_BLK, TOK_BLK)
        kb = pltpu.bitcast(aff_ref[:, pl.ds(st, TOK_BLK)], I32)
        eqf = (kb == thr).astype(F32)
        rank = ceq + _dot(eqf, su)
        sel = ((kb > thr) | ((kb == thr) & (rank < need))).astype(F32)
        mask_ref[:, pl.ds(st, TOK_BLK)] = sel
        wsel_ref[:, pl.ds(st, TOK_BLK)] = sel * aff_ref[:, pl.ds(st, TOK_BLK)]
        pos_ref[:, pl.ds(st, TOK_BLK)] = _dot(sel, su)
        offs_ref[j] = jnp.broadcast_to(csel.astype(I32), (N_EXPERTS, 128))
        return (ceq + jnp.sum(eqf, axis=1, keepdims=True), csel + jnp.sum(sel, axis=1, keepdims=True))

    zero = jnp.zeros((N_EXPERTS, 1), F32)
    lax.fori_loop(0, nblk, blk_body, (zero, zero))


def _route(aff_t, cap):
    E, T = aff_t.shape
    nblk = T // TOK_BLK
    full = lambda shape: pl.BlockSpec(shape, lambda i: (0,) * len(shape))
    return pl.pallas_call(
        functools.partial(_route_kernel, cap=cap, nblk=nblk),
        out_shape=[jax.ShapeDtypeStruct((E, T), F32)] * 3 + [jax.ShapeDtypeStruct((nblk, E, 128), I32)],
        grid=(1,),
        in_specs=[full((E, T))],
        out_specs=[full((E, T))] * 3 + [full((nblk, E, 128))],
        compiler_params=_cparams(("arbitrary",)),
        name="route",
    )(aff_t)


def _gather_kernel(offs_ref, x_ref, m_ref, p_ref, out_ref):
    g, j = pl.program_id(0), pl.program_id(1)

    @pl.when(j == 0)
    def _():
        out_ref[...] = jnp.zeros_like(out_ref)

    W = GATHER_W
    slot0 = lax.broadcasted_iota(I32, (W, TOK_BLK), 0).astype(F32)
    for jb in range(GATHER_JB):
        blk = j * GATHER_JB + jb
        cols = slice(jb * TOK_BLK, (jb + 1) * TOK_BLK)
        targets, bases, passes = [], [], []
        for k in range(GATHER_E):
            off = offs_ref[blk * N_EXPERTS + g * GATHER_E + k]
            cnt = offs_ref[(blk + 1) * N_EXPERTS + g * GATHER_E + k] - off
            base = (off // 16) * 16
            targets.append(jnp.where(m_ref[k, :, cols] > 0.0, p_ref[k, :, cols] + (off - base).astype(F32), -1.0))
            bases.append(base)
            passes.append((off - base + cnt + W - 1) // W)
        onehot = jnp.concatenate([jnp.where(slot0 == t, 1.0, 0.0) for t in targets], axis=0).astype(BF16)
        picked = jnp.dot(onehot, x_ref[cols, :], preferred_element_type=F32).astype(BF16)
        for k in range(GATHER_E):
            out_ref[k, pl.ds(pl.multiple_of(bases[k], 16), W), :] += picked[k * W:(k + 1) * W]
        for k in range(GATHER_E):
            def more(ps, carry, k=k, cols=cols):
                hot = jnp.where(slot0 + (ps * W).astype(F32) == targets[k], 1.0, 0.0).astype(BF16)
                extra = jnp.dot(hot, x_ref[cols, :], preferred_element_type=F32).astype(BF16)
                out_ref[k, pl.ds(pl.multiple_of(bases[k] + ps * W, 16), W), :] += extra
                return carry

            lax.fori_loop(1, passes[k], more, 0)


def _gather(h2, mask3, pos3, offs, cap):
    T = h2.shape[0]
    nblk = T // TOK_BLK
    cap_x = cap + 2 * GATHER_W
    rows = GATHER_JB * TOK_BLK
    grid_spec = pltpu.PrefetchScalarGridSpec(
        num_scalar_prefetch=1,
        grid=(N_EXPERTS // GATHER_E, nblk // GATHER_JB),
        in_specs=[pl.BlockSpec((rows, D_MODEL), lambda g, j, o: (j, 0)),
                  pl.BlockSpec((GATHER_E, 1, rows), lambda g, j, o: (g, 0, j)),
                  pl.BlockSpec((GATHER_E, 1, rows), lambda g, j, o: (g, 0, j))],
        out_specs=pl.BlockSpec((GATHER_E, cap_x, D_MODEL), lambda g, j, o: (g, 0, 0)),
    )
    return pl.pallas_call(
        _gather_kernel,
        out_shape=jax.ShapeDtypeStruct((N_EXPERTS, cap_x, D_MODEL), BF16),
        grid_spec=grid_spec,
        compiler_params=_cparams(("parallel", "arbitrary")),
        name="gather",
    )(offs, h2, mask3, pos3)


def _ffn_kernel(x_ref, wg_ref, wu_ref, wd_ref, o_ref):
    x = x_ref[0]
    g = jnp.dot(x, wg_ref[0], preferred_element_type=F32)
    u = jnp.dot(x, wu_ref[0], preferred_element_type=F32)
    h = (g * _sigmoid(g)) * u
    o_ref[0] = _dot(h, wd_ref[0]).astype(BF16)


def _ffn(xe, wg, wu, wd, cap):
    tf = min(512, cap)
    wspec = pl.BlockSpec((1, D_MODEL, D_MODEL), lambda e, i: (e, 0, 0))
    return pl.pallas_call(
        _ffn_kernel,
        out_shape=jax.ShapeDtypeStruct((N_EXPERTS, cap, D_MODEL), BF16),
        grid=(N_EXPERTS, cap // tf),
        in_specs=[pl.BlockSpec((1, tf, D_MODEL), lambda e, i: (e, i, 0)), wspec, wspec, wspec],
        out_specs=pl.BlockSpec((1, tf, D_MODEL), lambda e, i: (e, i, 0)),
        compiler_params=_cparams(("parallel", "parallel")),
        name="ffn",
    )(xe, wg, wu, wd)


def _slot_window_start(off, cap):
    return jnp.minimum((off // TOK_BLK) * TOK_BLK, cap - 2 * TOK_BLK)


def _scatter_kernel(offs_ref, x2_ref, w_ref, p_ref, gf_ref, *rest, cap, final):
    ye, out_ref = rest[:N_EXPERTS], rest[N_EXPERTS]
    j = pl.program_id(0)
    ps, ws = p_ref[...], w_ref[...]
    offs = [offs_ref[j * N_EXPERTS + e] for e in range(N_EXPERTS)]
    cnts = [offs_ref[(j + 1) * N_EXPERTS + e] - offs[e] for e in range(N_EXPERTS)]
    rels = [offs[e] - _slot_window_start(offs[e], cap) for e in range(N_EXPERTS)]
    weights = [ws[:, e:e + 1] for e in range(N_EXPERTS)]
    W, SUB = SCATTER_W, 16
    small = cnts[0] <= W - SUB
    for e in range(1, N_EXPERTS):
        small = jnp.logical_and(small, cnts[e] <= W - SUB)

    def onehot(e, start, width):
        slot = lax.broadcasted_iota(I32, (TOK_BLK, width), 1).astype(F32)
        return jnp.where(slot == ps[:, e:e + 1] + (rels[e] - start).astype(F32), weights[e], 0.0).astype(BF16)

    @pl.when(small)
    def _():
        acc = x2_ref[...]
        for g0 in range(0, N_EXPERTS, SCATTER_G):
            windows, pieces = [], []
            for e in range(g0, g0 + SCATTER_G):
                r0 = pl.multiple_of(jnp.minimum((rels[e] // SUB) * SUB, 2 * TOK_BLK - W), SUB)
                windows.append(ye[e][0, pl.ds(r0, W), :])
                pieces.append(onehot(e, r0, W))
            acc = acc + jnp.dot(jnp.concatenate(pieces, axis=1), jnp.concatenate(windows, axis=0),
                                preferred_element_type=F32)
        out_ref[...] = acc

    @pl.when(jnp.logical_not(small))
    def _():
        out_ref[...] = x2_ref[...]
        for e in range(N_EXPERTS):
            @pl.when(cnts[e] > 0)
            def _(e=e):
                out_ref[...] += jnp.dot(onehot(e, 0, 2 * TOK_BLK), ye[e][0], preferred_element_type=F32)

    if final:
        x = out_ref[...]
        out_ref[...] = x * lax.rsqrt(jnp.mean(x * x, axis=-1, keepdims=True) + EPS) * gf_ref[...]


def _scatter(x2, wsel_c, pos_c, ye, offs, gf, cap, final):
    T = x2.shape[0]
    nblk = T // TOK_BLK
    assert cap >= 2 * TOK_BLK and cap % TOK_BLK == 0

    def ye_spec(e):
        return pl.BlockSpec((pl.Element(1), pl.Element(2 * TOK_BLK), pl.Element(D_MODEL)),
                            lambda j, o: (e, pl.multiple_of(_slot_window_start(o[j * N_EXPERTS + e], cap), TOK_BLK),
                                          0))

    row = lambda w: pl.BlockSpec((TOK_BLK, w), lambda j, o: (j, 0))
    grid_spec = pltpu.PrefetchScalarGridSpec(
        num_scalar_prefetch=1,
        grid=(nblk,),
        in_specs=[row(D_MODEL), row(N_EXPERTS), row(N_EXPERTS), pl.BlockSpec((1, D_MODEL), lambda j, o: (0, 0))]
                 + [ye_spec(e) for e in range(N_EXPERTS)],
        out_specs=row(D_MODEL),
    )
    return pl.pallas_call(
        functools.partial(_scatter_kernel, cap=cap, final=final),
        out_shape=jax.ShapeDtypeStruct((T, D_MODEL), F32),
        grid_spec=grid_spec,
        compiler_params=_cparams(("arbitrary",)),
        name="scatter",
    )(offs, x2, wsel_c, pos_c, gf, *([ye] * N_EXPERTS))


def _block_diag(w):
    n, a, b = w.shape
    out = jnp.zeros((n * a, n * b), w.dtype)
    for i in range(n):
        out = out.at[i * a:(i + 1) * a, i * b:(i + 1) * b].set(w[i])
    return out


def _rope_tables(S):
    inv = ROPE_THETA ** (-jnp.arange(0, ROT_DIM, 2, dtype=F32) / ROT_DIM)
    ang = jnp.arange(S, dtype=F32)[:, None] * inv[None, :]
    cos, sin = jnp.cos(ang), jnp.sin(ang)
    half = ROT_DIM // 2
    ones, zeros = jnp.ones((S, ATT_HD - ROT_DIM), F32), jnp.zeros((S, ATT_HD - ROT_DIM), F32)
    zh = jnp.zeros((S, half), F32)
    c = jnp.concatenate([cos, cos, ones], axis=1)
    s1 = jnp.concatenate([-sin, zh, zeros], axis=1)
    s2 = jnp.concatenate([zh, sin, zeros], axis=1)
    return tuple(jnp.tile(t, (1, 2)) for t in (c, s1, s2))


def _expanders():
    eg = np.zeros((2, 128, GROUP_W), np.float32)
    eb = np.zeros((2, 128, GROUP_W), np.float32)
    for d in range(2):
        for h in range(DN_HEADS):
            eg[d, d * DN_HEADS + h, h * DN_DK:(h + 1) * DN_DK] = 1.0
            eb[d, 2 * DN_HEADS + d * DN_HEADS + h, h * DN_DK:(h + 1) * DN_DK] = 1.0
    hd = np.arange(GROUP_W) // DN_DK
    bd = (hd[:, None] == hd[None, :]).astype(np.float32)
    return jnp.asarray(eg), jnp.asarray(eb), jnp.asarray(bd)


def _layer_params(l, norm1_g, w_in, conv_a_w, conv_a_b, rg_wa, rg_ba, rg_wx, rg_bx, rg_lambda, pool_w, pool_scale,
                  dn_conv_w, dn_A_log, dn_dt_bias, dn_norm_g, mix_norm_g, w_out, norm2_g, router_w,
                  exp_w_gate, exp_w_up, exp_w_down):
    w = w_in[l]
    w_cat = jnp.concatenate([w[:, 0:1536], w[:, 1552:2576], w[:, 1536:1552],
                             jnp.zeros((D_MODEL, COL_END - COL_CAB - 16), F32)], axis=1).astype(BF16)
    pad8 = lambda v: jnp.concatenate([v.reshape(1, 2 * DN_HEADS), jnp.zeros((1, 120), F32)], axis=1)
    return dict(
        g1=norm1_g[l].reshape(1, D_MODEL), w_cat=w_cat,
        conv_a_w=conv_a_w[l], conv_a_b=conv_a_b[l].reshape(1, GROUP_W),
        wa=[_block_diag(rg_wa[l, d]).astype(BF16) for d in range(2)],
        wx=[_block_diag(rg_wx[l, d]).astype(BF16) for d in range(2)],
        ba=[rg_ba[l, d].reshape(1, GROUP_W) for d in range(2)],
        bx=[rg_bx[l, d].reshape(1, GROUP_W) for d in range(2)],
        lam=[rg_lambda[l, d].reshape(1, GROUP_W) for d in range(2)],
        pool_w=_block_diag(pool_w[l]).astype(BF16), pool_scale=pool_scale[l].reshape(1, GROUP_W),
        dn_conv_w=dn_conv_w[l], alog=pad8(dn_A_log[l]), dtb=pad8(dn_dt_bias[l]),
        dng=jnp.tile(dn_norm_g[l], DN_HEADS).reshape(1, GROUP_W),
        mg=mix_norm_g[l].reshape(1, D_MODEL), wo=w_out[l].astype(BF16),
        g2=norm2_g[l].reshape(1, D_MODEL),
        rw=jnp.concatenate([router_w[l], jnp.zeros((D_MODEL, 128 - N_EXPERTS), F32)], axis=1),
        wg=exp_w_gate[l].astype(BF16), wu=exp_w_up[l].astype(BF16), wd=exp_w_down[l].astype(BF16),
    )


def _chunk_rows(gcn, d):
    B, S, _ = gcn.shape
    n = S // DN_CHUNK
    g = gcn[:, :, d * DN_HEADS:(d + 1) * DN_HEADS].reshape(B, n, DN_CHUNK, DN_HEADS)
    return g.transpose(0, 1, 3, 2).reshape(B, n, 1, GROUP_W)


def _encoder(x, layers, final_g, consts):
    B, S, D = x.shape
    T = B * S
    cap = max(1, EC_CAPACITY * T // N_EXPERTS)
    eg, eb, bdmask = consts
    rope_c, rope_s1, rope_s2 = _rope_tables(S)
    xt = x.reshape(T, D)
    for l, p in enumerate(layers):
        pa, pb, pq, pg, pd, pcab = _inproj(xt, p["g1"], p["w_cat"], rope_c, rope_s1, rope_s2, S)
        pa3 = pa.reshape(B, S, 2 * GROUP_W)
        rg = lambda d: (p["conv_a_w"], p["conv_a_b"], p["wa"][d], p["ba"][d], p["wx"][d], p["bx"][d], p["lam"][d])
        hf = _rglru(pa3, None, *rg(0), reverse=False)
        ya = _rglru(pa3, hf, *rg(1), reverse=True)
        yb = _pool(pb.reshape(B, S, GROUP_W), p["pool_w"], p["pool_scale"])
        qn, kn, vv, gcf, gcb, bef, beb, gcn = _dnprep(pq.reshape(B, S, 3 * GROUP_W), pcab.reshape(B, S, 128),
                                                      p["dn_conv_w"], p["alog"], p["dtb"], eg, eb)
        of, ob = _delta(qn, kn, vv, gcf, gcb, bef, beb, _chunk_rows(gcn, 0), _chunk_rows(gcn, 1), bdmask)
        yd = _attention(pd.reshape(B, S, 3 * GROUP_W))
        flat = lambda a: a.reshape(T, GROUP_W)
        x2, h2, aff = _outproj(flat(ya), flat(yb), flat(of), flat(ob), pg, flat(yd), xt,
                               p["dng"], p["mg"], p["wo"], p["g2"], p["rw"])
        mask, pos, wsel, offs3 = _route(aff[:, :N_EXPERTS].T, cap)
        offs = jnp.concatenate([offs3[:, :, 0].reshape(-1), jnp.full((N_EXPERTS,), cap, I32)])
        xe = _gather(h2, mask.reshape(N_EXPERTS, 1, T), pos.reshape(N_EXPERTS, 1, T), offs, cap)
        ye = _ffn(xe, p["wg"], p["wu"], p["wd"], cap)
        xt = _scatter(x2, wsel.T, pos.T, ye, offs, final_g, cap, final=(l == len(layers) - 1))
    return xt.reshape(B, S, D)


def kernel(x_prompt, x_sample, norm1_g, w_in, conv_a_w, conv_a_b, rg_wa, rg_ba, rg_wx, rg_bx, rg_lambda, pool_w, pool_scale, dn_conv_w, dn_A_log, dn_dt_bias, dn_norm_g, mix_norm_g, w_out, norm2_g, router_w, exp_w_gate, exp_w_up, exp_w_down, final_norm_g):
    layers = [_layer_params(l, norm1_g, w_in, conv_a_w, conv_a_b, rg_wa, rg_ba, rg_wx, rg_bx, rg_lambda, pool_w,
                            pool_scale, dn_conv_w, dn_A_log, dn_dt_bias, dn_norm_g, mix_norm_g, w_out, norm2_g,
                            router_w, exp_w_gate, exp_w_up, exp_w_down) for l in range(DEPTH)]
    consts = _expanders()
    final_g = final_norm_g.reshape(1, D_MODEL)
    return (_encoder(x_prompt, layers, final_g, consts), _encoder(x_sample, layers, final_g, consts))
```

```python
import functools
import numpy as np
import jax
import jax.numpy as jnp
from jax import lax
from jax.experimental import pallas as pl
from jax.experimental.pallas import tpu as pltpu

F32, BF16, I32 = jnp.float32, jnp.bfloat16, jnp.int32

D_MODEL = 1024
DEPTH = 2
GROUP_W = 256
RG_C = 8.0
POOL_WINDOWS = (2, 4, 8, 16)
DN_HEADS = 4
DN_DK = 64
DN_CHUNK = 64
ATT_HD = 64
ROT_DIM = 16
ROPE_THETA = 500000.0
ATT_WINDOWS = (128, 512, 2048)
ATT_DILATIONS = (1, 4, 16)
N_EXPERTS = 16
EC_CAPACITY = 2
EPS = 1e-6
NEG = -1e30

COL_A, COL_B, COL_CQ, COL_CG, COL_D, COL_CAB, COL_END = 0, 512, 768, 1536, 1792, 2560, 2688

ROW_TILE = 512
HALO = 8
DELTA_CB = 8
ATT_QB = 128
ATT_UNROLL = 4
TOK_BLK = 256
GATHER_W = 64
GATHER_JB = 4
GATHER_E = 4
SCATTER_W = 128
SCATTER_G = 2
VMEM_LIMIT = 56 * 1024 * 1024


def _cparams(sem):
    return pltpu.CompilerParams(dimension_semantics=sem, vmem_limit_bytes=VMEM_LIMIT)


def _dot(a, b):
    return jnp.dot(a.astype(BF16), b.astype(BF16), preferred_element_type=F32)


def _split(a, pieces):
    out, rem = [], a
    for i in range(pieces):
        t = rem.astype(BF16)
        out.append(t)
        if i + 1 < pieces:
            rem = rem - t.astype(F32)
    return out


def _dot_sel(a, sel, pieces=3):
    sel = sel.astype(BF16)
    return sum(jnp.dot(t, sel, preferred_element_type=F32) for t in _split(a, pieces))


def _sel_dot(sel, a, pieces=3):
    sel = sel.astype(BF16)
    return sum(jnp.dot(sel, t, preferred_element_type=F32) for t in _split(a, pieces))


def _dot_nt(a, b):
    return lax.dot_general(a.astype(BF16), b.astype(BF16), (((1,), (1,)), ((), ())), preferred_element_type=F32)


def _dot_tn(a, b):
    return lax.dot_general(a.astype(BF16), b.astype(BF16), (((0,), (0,)), ((), ())), preferred_element_type=F32)


def _sigmoid(x):
    return 1.0 / (1.0 + jnp.exp(-x))


def _softplus(x):
    return jnp.maximum(x, 0.0) + jnp.log1p(jnp.exp(-jnp.abs(x)))


def _shift_rows(e, k):
    n = e.shape[0]
    return e if k % n == 0 else pltpu.roll(e, (-k) % n, axis=0)


def _with_halo(cur_ref, prev_ref, next_ref, first, last):
    prev = jnp.where(first, 0.0, prev_ref[0])
    nxt = jnp.where(last, 0.0, next_ref[0])
    return jnp.concatenate([prev, cur_ref[0], nxt], axis=0)


def _halo_specs(ts, width, S, blk_of):
    per = ts // HALO
    last = S // HALO - 1
    cur = pl.BlockSpec((1, ts, width), lambda b, c: (b, blk_of(c), 0))
    prev = pl.BlockSpec((1, HALO, width), lambda b, c: (b, jnp.maximum(blk_of(c) * per - 1, 0), 0))
    nxt = pl.BlockSpec((1, HALO, width), lambda b, c: (b, jnp.minimum((blk_of(c) + 1) * per, last), 0))
    return cur, prev, nxt


def _inproj_kernel(x_ref, g_ref, w_ref, c_ref, s1_ref, s2_ref, pa_ref, pb_ref, pq_ref, pg_ref, pd_ref, pcab_ref):
    x = x_ref[...]
    h = x * lax.rsqrt(jnp.mean(x * x, axis=-1, keepdims=True) + EPS) * g_ref[...]
    hb = h.astype(BF16)

    def mm(lo, hi):
        return jnp.dot(hb, w_ref[:, lo:hi], preferred_element_type=F32)

    pa_ref[...] = mm(COL_A, COL_B)
    pb_ref[...] = mm(COL_B, COL_CQ)
    pq_ref[...] = mm(COL_CQ, COL_CG)
    pg_ref[...] = mm(COL_CG, COL_D)
    pcab_ref[...] = mm(COL_CAB, COL_END)
    c, s1, s2 = c_ref[...], s1_ref[...], s2_ref[...]
    for blk in range(4):
        lo = COL_D + 128 * blk
        y = mm(lo, lo + 128)
        y = y * c + pltpu.roll(y, 128 - ROT_DIM // 2, axis=1) * s1 + pltpu.roll(y, ROT_DIM // 2, axis=1) * s2
        if blk < 2:
            y = y * (ATT_HD ** -0.5)
        pd_ref[:, 128 * blk:128 * blk + 128] = y
    pd_ref[:, 512:768] = mm(COL_D + 512, COL_D + 768)


def _inproj(xt, g1, w_cat, rope_c, rope_s1, rope_s2, S):
    T = xt.shape[0]
    tm = min(ROW_TILE, S)
    per_seq = S // tm
    widths = (512, 256, 768, 256, 768, 128)
    row = lambda w: pl.BlockSpec((tm, w), lambda i: (i, 0))
    rope = pl.BlockSpec((tm, 128), lambda i: (i % per_seq, 0))
    return pl.pallas_call(
        _inproj_kernel,
        out_shape=[jax.ShapeDtypeStruct((T, w), F32) for w in widths],
        grid=(T // tm,),
        in_specs=[row(D_MODEL), pl.BlockSpec((1, D_MODEL), lambda i: (0, 0)),
                  pl.BlockSpec((D_MODEL, COL_END), lambda i: (0, 0)), rope, rope, rope],
        out_specs=[row(w) for w in widths],
        compiler_params=_cparams(("parallel",)),
        name="inproj",
    )(xt, g1, w_cat, rope_c, rope_s1, rope_s2)


def _rglru_kernel(*refs, reverse, ts, nc):
    if reverse:
        (cur_ref, prev_ref, next_ref, gate_ref, hf_ref, cw_ref, cb_ref, wa_ref, ba_ref, wx_ref, bx_ref, lam_ref,
         out_ref, a_ref, b_ref, carry_ref) = refs
    else:
        (cur_ref, prev_ref, next_ref, cw_ref, cb_ref, wa_ref, ba_ref, wx_ref, bx_ref, lam_ref,
         out_ref, a_ref, b_ref, carry_ref) = refs
    c = pl.program_id(1)
    blk = (nc - 1 - c) if reverse else c
    e = _with_halo(cur_ref, prev_ref, next_ref, blk == 0, blk == nc - 1)
    cw = cw_ref[...]
    sl = slice(HALO, HALO + ts)
    u = (cw[0:1] * _shift_rows(e, -2)[sl] + cw[1:2] * _shift_rows(e, -1)[sl] + cw[2:3] * e[sl]
         + cw[3:4] * _shift_rows(e, 1)[sl]) + cb_ref[...]
    r = _sigmoid(_dot(u, wa_ref[...]) + ba_ref[...])
    i = _sigmoid(_dot(u, wx_ref[...]) + bx_ref[...])
    log_a = -RG_C * r * _softplus(-lam_ref[...])
    a_ref[...] = jnp.exp(log_a)
    b_ref[...] = jnp.sqrt(1.0 - jnp.exp(2.0 * log_a)) * (i * u)

    @pl.when(c == 0)
    def _():
        carry_ref[...] = jnp.zeros_like(carry_ref)

    row = lax.broadcasted_iota(I32, (HALO, GROUP_W), 0)
    nt = ts // HALO

    def body(it, carry):
        ti = (nt - 1 - it) if reverse else it
        st = pl.multiple_of(ti * HALO, HALO)
        a = a_ref[pl.ds(st, HALO), :]
        b = b_ref[pl.ds(st, HALO), :]
        for d in (1, 2, 4):
            k = d if reverse else -d
            valid = (row < HALO - d) if reverse else (row >= d)
            b = jnp.where(valid, a * _shift_rows(b, k) + b, b)
            a = jnp.where(valid, a * _shift_rows(a, k), a)
        h = a * carry + b
        if reverse:
            g = gate_ref[0, pl.ds(st, HALO), :]
            cdf = 0.5 * (1.0 + jnp.tanh(np.float32(np.sqrt(2.0 / np.pi)) * (g + 0.044715 * (g * g * g))))
            out_ref[0, pl.ds(st, HALO), :] = (g * cdf) * (hf_ref[0, pl.ds(st, HALO), :] + h)
            return jnp.broadcast_to(h[0:1], h.shape)
        out_ref[0, pl.ds(st, HALO), :] = h
        return jnp.broadcast_to(h[HALO - 1:HALO], h.shape)

    carry_ref[...] = lax.fori_loop(0, nt, body, carry_ref[...], unroll=4)


def _rglru(pa3, hf, cw, cb, wa, ba, wx, bx, lam, reverse):
    B, S, _ = pa3.shape
    ts = min(ROW_TILE, S)
    nc = S // ts
    blk_of = (lambda c: nc - 1 - c) if reverse else (lambda c: c)
    cur, prev, nxt = _halo_specs(ts, GROUP_W, S, blk_of)
    tile = pl.BlockSpec((1, ts, GROUP_W), lambda b, c: (b, blk_of(c), 0))
    const = lambda shape: pl.BlockSpec(shape, lambda b, c: (0,) * len(shape))
    in_specs = [cur, prev, nxt]
    args = [pa3, pa3, pa3]
    if reverse:
        in_specs += [pl.BlockSpec((1, ts, GROUP_W), lambda b, c: (b, blk_of(c), 1)), tile]
        args += [pa3, hf]
    in_specs += [const((4, GROUP_W)), const((1, GROUP_W)), const((GROUP_W, GROUP_W)), const((1, GROUP_W)),
                 const((GROUP_W, GROUP_W)), const((1, GROUP_W)), const((1, GROUP_W))]
    args += [cw, cb, wa, ba, wx, bx, lam]
    return pl.pallas_call(
        functools.partial(_rglru_kernel, reverse=reverse, ts=ts, nc=nc),
        out_shape=jax.ShapeDtypeStruct((B, S, GROUP_W), F32),
        grid=(B, nc),
        in_specs=in_specs,
        out_specs=tile,
        scratch_shapes=[pltpu.VMEM((ts, GROUP_W), F32), pltpu.VMEM((ts, GROUP_W), F32),
                        pltpu.VMEM((HALO, GROUP_W), F32)],
        compiler_params=_cparams(("parallel", "arbitrary")),
        name="rglru_bwd" if reverse else "rglru_fwd",
    )(*args)


def _pool_kernel(cur_ref, prev_ref, next_ref, w_ref, sc_ref, out_ref, *, ts, nc, S):
    c = pl.program_id(1)
    e = _with_halo(cur_ref, prev_ref, next_ref, c == 0, c == nc - 1)
    sl = slice(HALO, HALO + ts)
    a2 = e + _shift_rows(e, 1)
    a4 = a2 + _shift_rows(a2, 2)
    a8 = a4 + _shift_rows(a4, 4)
    a16 = a8 + _shift_rows(a8, 8)
    sums = [_shift_rows(a, -(w // 2))[sl] for a, w in zip((a2, a4, a8, a16), POOL_WINDOWS)]
    gi = lax.broadcasted_iota(I32, (ts, GROUP_W), 1) // (GROUP_W // len(POOL_WINDOWS))
    ssum = jnp.where(gi == 0, sums[0], jnp.where(gi == 1, sums[1], jnp.where(gi == 2, sums[2], sums[3])))
    hw = jnp.where(gi == 0, 1, jnp.where(gi == 1, 2, jnp.where(gi == 2, 4, 8)))
    t = c * ts + lax.broadcasted_iota(I32, (ts, GROUP_W), 0)
    cnt = (jnp.minimum(t + hw, S) - jnp.maximum(t - hw, 0)).astype(F32)
    p = ssum / cnt - e[sl]
    out_ref[0] = _dot(p, w_ref[...]) * sc_ref[...]


def _pool(pb3, w_bd, scale):
    B, S, _ = pb3.shape
    ts = min(ROW_TILE, S)
    nc = S // ts
    cur, prev, nxt = _halo_specs(ts, GROUP_W, S, lambda c: c)
    return pl.pallas_call(
        functools.partial(_pool_kernel, ts=ts, nc=nc, S=S),
        out_shape=jax.ShapeDtypeStruct((B, S, GROUP_W), F32),
        grid=(B, nc),
        in_specs=[cur, prev, nxt, pl.BlockSpec((GROUP_W, GROUP_W), lambda b, c: (0, 0)),
                  pl.BlockSpec((1, GROUP_W), lambda b, c: (0, 0))],
        out_specs=pl.BlockSpec((1, ts, GROUP_W), lambda b, c: (b, c, 0)),
        compiler_params=_cparams(("parallel", "parallel")),
        name="pool",
    )(pb3, pb3, pb3, w_bd, scale)


def _head_sum_matrix(n, group):
    r = lax.broadcasted_iota(I32, (n, n), 0) // group
    c = lax.broadcasted_iota(I32, (n, n), 1) // group
    return (r == c).astype(F32)


def _dnprep_kernel(cur_ref, prev_ref, next_ref, cab_ref, cw_ref, alog_ref, dtb_ref, eg_ref, eb_ref,
                   q_ref, k_ref, v_ref, gcf_ref, gcb_ref, bef_ref, beb_ref, gcn_ref, *, ts, nc):
    c = pl.program_id(1)
    e = _with_halo(cur_ref, prev_ref, next_ref, c == 0, c == nc - 1)
    cw = cw_ref[...]
    sl = slice(HALO, HALO + ts)
    y = (cw[0:1] * _shift_rows(e, -2)[sl] + cw[1:2] * _shift_rows(e, -1)[sl] + cw[2:3] * e[sl]
         + cw[3:4] * _shift_rows(e, 1)[sl])
    y = y * _sigmoid(y)
    q, k = y[:, 0:GROUP_W], y[:, GROUP_W:2 * GROUP_W]
    hs = _head_sum_matrix(GROUP_W, DN_DK)
    q_ref[0] = q * lax.rsqrt(_dot_sel(q * q, hs) + EPS) * (DN_DK ** -0.5)
    k_ref[0] = k * lax.rsqrt(_dot_sel(k * k, hs) + EPS)
    v_ref[0] = y[:, 2 * GROUP_W:3 * GROUP_W]
    cab = cab_ref[0]
    g = -jnp.exp(alog_ref[...]) * _softplus(cab + dtb_ref[...])
    beta = _sigmoid(cab)
    bef_ref[0] = _dot_sel(beta, eb_ref[0])
    beb_ref[0] = _dot_sel(beta, eb_ref[1])
    r = lax.broadcasted_iota(I32, (ts, ts), 0)
    cc = lax.broadcasted_iota(I32, (ts, ts), 1)
    same = (r // DN_CHUNK) == (cc // DN_CHUNK)
    lane = lax.broadcasted_iota(I32, (ts, 128), 1)
    gcn = jnp.where(lane < DN_HEADS, _sel_dot(same & (r >= cc), g), _sel_dot(same & (r <= cc), g))
    gcn_ref[0] = gcn
    gcf_ref[0] = _dot_sel(gcn, eg_ref[0])
    gcb_ref[0] = _dot_sel(gcn, eg_ref[1])


def _dnprep(pq3, pcab3, cw, alog_row, dtb_row, eg, eb):
    B, S, _ = pq3.shape
    ts = min(ROW_TILE, S)
    nc = S // ts
    cur, prev, nxt = _halo_specs(ts, 3 * GROUP_W, S, lambda c: c)
    tile = pl.BlockSpec((1, ts, GROUP_W), lambda b, c: (b, c, 0))
    const = lambda shape: pl.BlockSpec(shape, lambda b, c: (0,) * len(shape))
    return pl.pallas_call(
        functools.partial(_dnprep_kernel, ts=ts, nc=nc),
        out_shape=[jax.ShapeDtypeStruct((B, S, GROUP_W), F32)] * 7 + [jax.ShapeDtypeStruct((B, S, 128), F32)],
        grid=(B, nc),
        in_specs=[cur, prev, nxt, pl.BlockSpec((1, ts, 128), lambda b, c: (b, c, 0)),
                  const((4, 3 * GROUP_W)), const((1, 128)), const((1, 128)),
                  const((2, 128, GROUP_W)), const((2, 128, GROUP_W))],
        out_specs=[tile] * 7 + [pl.BlockSpec((1, ts, 128), lambda b, c: (b, c, 0))],
        compiler_params=_cparams(("parallel", "parallel")),
        name="dnprep",
    )(pq3, pq3, pq3, pcab3, cw, alog_row, dtb_row, eg, eb)


def _delta_kernel(qf, kf, vf, gcf, bef, grf, qb, kb, vb, gcb, beb, grb, bd_ref, of_ref, ob_ref,
                  s_ref, p_scr, x_scr, wq_scr, u_scr, at_scr, kd_scr, vb_scr, kg_scr, *, cb):
    j = pl.program_id(1)

    @pl.when(j == 0)
    def _():
        s_ref[...] = jnp.zeros_like(s_ref)

    C = DN_CHUNK
    bd = bd_ref[...] > 0.0
    c_idx = lax.broadcasted_iota(I32, (C, GROUP_W), 0)
    m_idx = lax.broadcasted_iota(I32, (C, GROUP_W), 1) % C
    eye = (c_idx == m_idx).astype(F32)

    def blockdiag(x):
        return jnp.where(bd, jnp.tile(x.astype(BF16), (DN_HEADS, 1)), jnp.zeros((), BF16))

    dirs = ((qf, kf, vf, gcf, bef, grf, of_ref), (qb, kb, vb, gcb, beb, grb, ob_ref))
    units = [(d, ci) for ci in range(cb) for d in range(2)]

    def chunk_of(d, ci):
        return cb - 1 - ci if d == 1 else ci

    for ui, (d, ci) in enumerate(units):
        q_r, k_r, v_r, gc_r, be_r, gr_r, _ = dirs[d]
        rev = d == 1
        tril = (c_idx <= m_idx) if rev else (c_idx >= m_idx)
        strict = (c_idx < m_idx) if rev else (c_idx > m_idx)
        last = 0 if rev else C - 1
        cc = chunk_of(d, ci)
        rows = slice(cc * C, (cc + 1) * C)
        q, k, v = q_r[0, rows, :], k_r[0, rows, :], v_r[0, rows, :]
        gc, be, gr = gc_r[0, rows, :], be_r[0, rows, :], gr_r[0, cc]
        eg = jnp.exp(gc)
        kbeta = k * be
        kkqk = _dot_nt(jnp.concatenate([kbeta, q], axis=0), blockdiag(k))
        decay = jnp.exp(jnp.where(tril, gc - gr, -jnp.inf))
        at_scr[ui] = kkqk[C:] * decay
        p = -jnp.where(strict, kkqk[:C] * decay, 0.0)
        p_scr[ui] = p
        x_scr[ui] = eye + p
        wq_scr[ui, C:2 * C, :] = q * eg
        kd_scr[ui] = k * jnp.exp(gc[last:last + 1] - gc)
        vb_scr[ui] = v * be
        kg_scr[ui] = kbeta * eg
    for ui in range(len(units)):
        p = p_scr[ui]
        p_scr[ui] = _dot(p, blockdiag(p))
    for _ in range(4):
        for ui in range(len(units)):
            p, x = p_scr[ui], x_scr[ui]
            px = _dot(jnp.concatenate([p, x], axis=0), blockdiag(p))
            p_scr[ui] = px[:C]
            x_scr[ui] = x + px[C:]
    for ui in range(len(units)):
        x = x_scr[ui]
        x = x + _dot(x, blockdiag(p_scr[ui]))
        u_scr[ui] = _dot(x, blockdiag(vb_scr[ui]))
        wq_scr[ui, 0:C, :] = _dot(x, blockdiag(kg_scr[ui]))
    for ui, (d, ci) in enumerate(units):
        gc_r, o_r = dirs[d][3], dirs[d][6]
        cc = chunk_of(d, ci)
        last = cc * C + (0 if d == 1 else C - 1)
        state = s_ref[d]
        ws_qs = _dot(wq_scr[ui], state)
        v_new = u_scr[ui] - ws_qs[:C]
        o_r[0, cc * C:(cc + 1) * C, :] = ws_qs[C:] + _dot(at_scr[ui], blockdiag(v_new))
        s_ref[d] = (state * jnp.exp(gc_r[0, last:last + 1, :])
                    + jnp.where(bd, _dot_tn(kd_scr[ui], v_new), 0.0))


def _delta(qn, kn, vv, gcf, gcb, bef, beb, grf, grb, bdmask):
    B, S, _ = qn.shape
    cb = DELTA_CB
    rb = cb * DN_CHUNK
    nb = S // rb
    f = lambda b, j: (b, j, 0)
    r = lambda b, j: (b, nb - 1 - j, 0)
    tf = pl.BlockSpec((1, rb, GROUP_W), f)
    tr = pl.BlockSpec((1, rb, GROUP_W), r)
    gf = pl.BlockSpec((1, cb, 1, GROUP_W), lambda b, j: (b, j, 0, 0))
    gr = pl.BlockSpec((1, cb, 1, GROUP_W), lambda b, j: (b, nb - 1 - j, 0, 0))
    return pl.pallas_call(
        functools.partial(_delta_kernel, cb=cb),
        out_shape=[jax.ShapeDtypeStruct((B, S, GROUP_W), F32)] * 2,
        grid=(B, nb),
        in_specs=[tf, tf, tf, tf, tf, gf, tr, tr, tr, tr, tr, gr,
                  pl.BlockSpec((GROUP_W, GROUP_W), lambda b, j: (0, 0))],
        out_specs=[tf, tr],
        scratch_shapes=[pltpu.VMEM((2, GROUP_W, GROUP_W), F32)]
                       + [pltpu.VMEM((2 * cb, n * DN_CHUNK, GROUP_W), F32) for n in (1, 1, 2, 1, 1, 1, 1, 1)],
        compiler_params=_cparams(("parallel", "arbitrary")),
        name="delta",
    )(qn, kn, vv, gcf, bef, grf, qn, kn, vv, gcb, beb, grb, bdmask)


def _attn_kernel(q_ref, k_ref, v_ref, out_ref, m_ref, l_ref, o_ref, *, S):
    head0 = lax.broadcasted_iota(I32, (1, 128), 1) < ATT_HD

    def rows(start, n, dil):
        return pl.ds(start, n) if dil == 1 else pl.ds(start, n, stride=dil)

    for bi, (win, dil) in enumerate(zip(ATT_WINDOWS, ATT_DILATIONS)):
        half = win // (2 * dil)
        L = S // dil
        qb_n = min(ATT_QB, L)
        kw = min(L, qb_n + 2 * half)
        nqb = L // qb_n

        def block_stats(idx, dil=dil, half=half, L=L, qb_n=qb_n, kw=kw, nqb=nqb):
            r = idx // nqb
            m0 = (idx % nqb) * qb_n
            ks = jnp.clip(m0 - half, 0, L - kw)
            qsel = rows(r + m0 * dil, qb_n, dil)
            ksel = rows(r + ks * dil, kw, dil)
            q = q_ref[0, qsel, :]
            kk = k_ref[0, ksel, :].astype(BF16)
            vv = v_ref[0, ksel, :].astype(BF16)
            qpos = m0 + lax.broadcasted_iota(I32, (qb_n, kw), 0)
            kpos = ks + lax.broadcasted_iota(I32, (qb_n, kw), 1)
            valid = jnp.abs(kpos - qpos) <= half
            ms, ls, os_ = [], [], []
            for h in range(2):
                qh = jnp.where(head0 if h == 0 else ~head0, q, 0.0)
                s = jnp.where(valid, _dot_nt(qh, kk), NEG)
                m = jnp.max(s, axis=-1, keepdims=True)
                p = jnp.exp(s - m)
                ms.append(m)
                ls.append(jnp.sum(p, axis=-1, keepdims=True))
                os_.append(_dot(p, vv))
            return (qsel, jnp.where(head0, ms[0], ms[1]), jnp.where(head0, ls[0], ls[1]),
                    jnp.where(head0, os_[0], os_[1]))

        def body(it, carry, bi=bi):
            stats = [block_stats(it * ATT_UNROLL + s) for s in range(ATT_UNROLL)]
            if bi > 0:
                olds = [(m_ref[qsel, :], l_ref[qsel, :], o_ref[qsel, :]) for qsel, _, _, _ in stats]
                merged = []
                for (qsel, m, l, o), (m_old, l_old, o_old) in zip(stats, olds):
                    m_new = jnp.maximum(m_old, m)
                    w_old, w_cur = jnp.exp(m_old - m_new), jnp.exp(m - m_new)
                    merged.append((qsel, m_new, w_old * l_old + w_cur * l, w_old * o_old + w_cur * o))
                stats = merged
            for qsel, m, l, o in stats:
                if bi == len(ATT_WINDOWS) - 1:
                    out_ref[0, qsel, :] = o / l
                else:
                    m_ref[qsel, :] = m
                    l_ref[qsel, :] = l
                    o_ref[qsel, :] = o
            return carry

        lax.fori_loop(0, dil * nqb // ATT_UNROLL, body, 0)


def _attention(pd3):
    B, S, _ = pd3.shape
    spec = lambda off: pl.BlockSpec((1, S, 128), lambda b, p: (b, 0, off + p))
    return pl.pallas_call(
        functools.partial(_attn_kernel, S=S),
        out_shape=jax.ShapeDtypeStruct((B, S, GROUP_W), F32),
        grid=(B, 2),
        in_specs=[spec(0), spec(2), spec(4)],
        out_specs=pl.BlockSpec((1, S, 128), lambda b, p: (b, 0, p)),
        scratch_shapes=[pltpu.VMEM((S, 128), F32)] * 3,
        compiler_params=_cparams(("parallel", "parallel")),
        name="attention",
    )(pd3, pd3, pd3)


def _outproj_kernel(ya_ref, yb_ref, of_ref, ob_ref, cg_ref, yd_ref, x_ref, dng_ref, mg_ref, wo_ref, g2_ref, rw_ref,
                    x2_ref, h2_ref, aff_ref):
    o = of_ref[...] + ob_ref[...]
    ms = _dot_sel(o * o, _head_sum_matrix(GROUP_W, DN_DK)) * (1.0 / DN_DK)
    cg = cg_ref[...]
    yc = (o * lax.rsqrt(ms + EPS) * dng_ref[...]) * (cg * _sigmoid(cg))
    acc = x_ref[...]
    for gi, y in enumerate((ya_ref[...], yb_ref[...], yc, yd_ref[...])):
        sl = slice(gi * GROUP_W, (gi + 1) * GROUP_W)
        mix = y * lax.rsqrt(jnp.mean(y * y, axis=-1, keepdims=True) + EPS) * mg_ref[:, sl]
        acc = acc + _dot(mix, wo_ref[sl, :])
    x2_ref[...] = acc
    h2 = acc * lax.rsqrt(jnp.mean(acc * acc, axis=-1, keepdims=True) + EPS) * g2_ref[...]
    h2_ref[...] = h2.astype(BF16)
    (h_hi, h_lo), (w_hi, w_lo) = _split(h2, 2), _split(rw_ref[...], 2)
    logits = (jnp.dot(h_hi, w_hi, preferred_element_type=F32) + jnp.dot(h_hi, w_lo, preferred_element_type=F32)
              + jnp.dot(h_lo, w_hi, preferred_element_type=F32))
    logits = jnp.where(lax.broadcasted_iota(I32, logits.shape, 1) < N_EXPERTS, logits, -jnp.inf)
    ex = jnp.exp(logits - jnp.max(logits, axis=-1, keepdims=True))
    aff = ex / jnp.sum(ex, axis=-1, keepdims=True)
    aff_ref[...] = aff.T[0:N_EXPERTS, :]


def _outproj(ya, yb, of, ob, cg, yd, xt, dng, mg, wo, g2, rw):
    T = xt.shape[0]
    tm = 256
    row = lambda w: pl.BlockSpec((tm, w), lambda i: (i, 0))
    const = lambda shape: pl.BlockSpec(shape, lambda i: (0, 0))
    return pl.pallas_call(
        _outproj_kernel,
        out_shape=[jax.ShapeDtypeStruct((T, D_MODEL), F32), jax.ShapeDtypeStruct((T, D_MODEL), BF16),
                   jax.ShapeDtypeStruct((N_EXPERTS, T), F32)],
        grid=(T // tm,),
        in_specs=[row(GROUP_W)] * 6 + [row(D_MODEL), const((1, GROUP_W)), const((1, D_MODEL)),
                                       const((D_MODEL, D_MODEL)), const((1, D_MODEL)), const((D_MODEL, 128))],
        out_specs=[row(D_MODEL), row(D_MODEL), pl.BlockSpec((N_EXPERTS, tm), lambda i: (0, i))],
        compiler_params=_cparams(("parallel",)),
        name="outproj",
    )(ya, yb, of, ob, cg, yd, xt, dng, mg, wo, g2, rw)


def _strict_upper(n):
    return (lax.broadcasted_iota(I32, (n, n), 0) < lax.broadcasted_iota(I32, (n, n), 1)).astype(BF16)


def _route_kernel(aff_ref, mask_ref, pos_ref, wsel_ref, offs_ref, *, cap, nblk):
    keys = pltpu.bitcast(aff_ref[...], I32)

    def bit_body(i, thr):
        cand = thr | lax.shift_left(jnp.int32(1), 30 - i)
        cnt = jnp.sum((keys >= cand).astype(F32), axis=1, keepdims=True)
        return jnp.where(cnt >= cap, cand, thr)

    thr = lax.fori_loop(0, 31, bit_body, jnp.zeros((N_EXPERTS, 1), I32))
    need = cap - jnp.sum((keys > thr).astype(F32), axis=1, keepdims=True)
    su = _strict_upper(TOK_BLK)

    def blk_body(j, carry):
        ceq, csel = carry
        st = pl.multiple_of(j * TOK_BLK, TOK_BLK)
        kb = pltpu.bitcast(aff_ref[:, pl.ds(st, TOK_BLK)], I32)
        eqf = (kb == thr).astype(F32)
        rank = ceq + _dot(eqf, su)
        sel = ((kb > thr) | ((kb == thr) & (rank < need))).astype(F32)
        mask_ref[:, pl.ds(st, TOK_BLK)] = sel
        wsel_ref[:, pl.ds(st, TOK_BLK)] = sel * aff_ref[:, pl.ds(st, TOK_BLK)]
        pos_ref[:, pl.ds(st, TOK_BLK)] = _dot(sel, su)
        offs_ref[j] = jnp.broadcast_to(csel.astype(I32), (N_EXPERTS, 128))
        return (ceq + jnp.sum(eqf, axis=1, keepdims=True), csel + jnp.sum(sel, axis=1, keepdims=True))

    zero = jnp.zeros((N_EXPERTS, 1), F32)
    lax.fori_loop(0, nblk, blk_body, (zero, zero))


def _route(aff_t, cap):
    E, T = aff_t.shape
    nblk = T // TOK_BLK
    full = lambda shape: pl.BlockSpec(shape, lambda i: (0,) * len(shape))
    return pl.pallas_call(
        functools.partial(_route_kernel, cap=cap, nblk=nblk),
        out_shape=[jax.ShapeDtypeStruct((E, T), F32)] * 3 + [jax.ShapeDtypeStruct((nblk, E, 128), I32)],
        grid=(1,),
        in_specs=[full((E, T))],
        out_specs=[full((E, T))] * 3 + [full((nblk, E, 128))],
        compiler_params=_cparams(("arbitrary",)),
        name="route",
    )(aff_t)


def _gather_kernel(offs_ref, x_ref, m_ref, p_ref, out_ref):
    g, j = pl.program_id(0), pl.program_id(1)

    @pl.when(j == 0)
    def _():
        out_ref[...] = jnp.zeros_like(out_ref)

    W = GATHER_W
    slot0 = lax.broadcasted_iota(I32, (W, TOK_BLK), 0).astype(F32)
    for jb in range(GATHER_JB):
        blk = j * GATHER_JB + jb
        cols = slice(jb * TOK_BLK, (jb + 1) * TOK_BLK)
        targets, bases, passes = [], [], []
        for k in range(GATHER_E):
            off = offs_ref[blk * N_EXPERTS + g * GATHER_E + k]
            cnt = offs_ref[(blk + 1) * N_EXPERTS + g * GATHER_E + k] - off
            base = (off // 16) * 16
            targets.append(jnp.where(m_ref[k, :, cols] > 0.0, p_ref[k, :, cols] + (off - base).astype(F32), -1.0))
            bases.append(base)
            passes.append((off - base + cnt + W - 1) // W)
        onehot = jnp.concatenate([jnp.where(slot0 == t, 1.0, 0.0) for t in targets], axis=0).astype(BF16)
        picked = jnp.dot(onehot, x_ref[cols, :], preferred_element_type=F32).astype(BF16)
        for k in range(GATHER_E):
            out_ref[k, pl.ds(pl.multiple_of(bases[k], 16), W), :] += picked[k * W:(k + 1) * W]
        for k in range(GATHER_E):
            def more(ps, carry, k=k, cols=cols):
                hot = jnp.where(slot0 + (ps * W).astype(F32) == targets[k], 1.0, 0.0).astype(BF16)
                extra = jnp.dot(hot, x_ref[cols, :], preferred_element_type=F32).astype(BF16)
                out_ref[k, pl.ds(pl.multiple_of(bases[k] + ps * W, 16), W), :] += extra
                return carry

            lax.fori_loop(1, passes[k], more, 0)


def _gather(h2, mask3, pos3, offs, cap):
    T = h2.shape[0]
    nblk = T // TOK_BLK
    cap_x = cap + 2 * GATHER_W
    rows = GATHER_JB * TOK_BLK
    grid_spec = pltpu.PrefetchScalarGridSpec(
        num_scalar_prefetch=1,
        grid=(N_EXPERTS // GATHER_E, nblk // GATHER_JB),
        in_specs=[pl.BlockSpec((rows, D_MODEL), lambda g, j, o: (j, 0)),
                  pl.BlockSpec((GATHER_E, 1, rows), lambda g, j, o: (g, 0, j)),
                  pl.BlockSpec((GATHER_E, 1, rows), lambda g, j, o: (g, 0, j))],
        out_specs=pl.BlockSpec((GATHER_E, cap_x, D_MODEL), lambda g, j, o: (g, 0, 0),
                               pipeline_mode=pl.Buffered(1)),
    )
    return pl.pallas_call(
        _gather_kernel,
        out_shape=jax.ShapeDtypeStruct((N_EXPERTS, cap_x, D_MODEL), BF16),
        grid_spec=grid_spec,
        compiler_params=_cparams(("parallel", "arbitrary")),
        name="gather",
    )(offs, h2, mask3, pos3)


def _ffn_kernel(x_ref, wg_ref, wu_ref, wd_ref, o_ref, w_scr):
    @pl.when(pl.program_id(1) == 0)
    def _():
        for n, w_ref in enumerate((wg_ref, wu_ref, wd_ref)):
            w_scr[n] = w_ref[0].astype(BF16)

    x = x_ref[0]
    g = jnp.dot(x, w_scr[0], preferred_element_type=F32)
    u = jnp.dot(x, w_scr[1], preferred_element_type=F32)
    h = (g * _sigmoid(g)) * u
    o_ref[0] = _dot(h, w_scr[2]).astype(BF16)


def _ffn(xe, wg, wu, wd, cap):
    tf = min(512, cap)
    wspec = pl.BlockSpec((1, D_MODEL, D_MODEL), lambda e, i: (e, 0, 0))
    return pl.pallas_call(
        _ffn_kernel,
        out_shape=jax.ShapeDtypeStruct((N_EXPERTS, cap, D_MODEL), BF16),
        grid=(N_EXPERTS, cap // tf),
        in_specs=[pl.BlockSpec((1, tf, D_MODEL), lambda e, i: (e, i, 0)), wspec, wspec, wspec],
        out_specs=pl.BlockSpec((1, tf, D_MODEL), lambda e, i: (e, i, 0)),
        scratch_shapes=[pltpu.VMEM((3, D_MODEL, D_MODEL), BF16)],
        compiler_params=_cparams(("parallel", "arbitrary")),
        name="ffn",
    )(xe, wg, wu, wd)


def _slot_window_start(off, cap):
    return jnp.minimum((off // TOK_BLK) * TOK_BLK, cap - 2 * TOK_BLK)


def _scatter_kernel(offs_ref, x2_ref, w_ref, p_ref, gf_ref, *rest, cap, final):
    ye, out_ref = rest[:N_EXPERTS], rest[N_EXPERTS]
    j = pl.program_id(0)
    ps, ws = p_ref[...].T, w_ref[...].T
    offs = [offs_ref[j * N_EXPERTS + e] for e in range(N_EXPERTS)]
    cnts = [offs_ref[(j + 1) * N_EXPERTS + e] - offs[e] for e in range(N_EXPERTS)]
    rels = [offs[e] - _slot_window_start(offs[e], cap) for e in range(N_EXPERTS)]
    weights = [ws[:, e:e + 1] for e in range(N_EXPERTS)]
    W, SUB = SCATTER_W, 16
    small = cnts[0] <= W - SUB
    for e in range(1, N_EXPERTS):
        small = jnp.logical_and(small, cnts[e] <= W - SUB)

    def onehot(e, start, width):
        slot = lax.broadcasted_iota(I32, (TOK_BLK, width), 1).astype(F32)
        return jnp.where(slot == ps[:, e:e + 1] + (rels[e] - start).astype(F32), weights[e], 0.0).astype(BF16)

    @pl.when(small)
    def _():
        acc = x2_ref[...]
        for g0 in range(0, N_EXPERTS, SCATTER_G):
            windows, pieces = [], []
            for e in range(g0, g0 + SCATTER_G):
                r0 = pl.multiple_of(jnp.minimum((rels[e] // SUB) * SUB, 2 * TOK_BLK - W), SUB)
                windows.append(ye[e][0, pl.ds(r0, W), :])
                pieces.append(onehot(e, r0, W))
            acc = acc + jnp.dot(jnp.concatenate(pieces, axis=1), jnp.concatenate(windows, axis=0),
                                preferred_element_type=F32)
        out_ref[...] = acc

    @pl.when(jnp.logical_not(small))
    def _():
        out_ref[...] = x2_ref[...]
        for e in range(N_EXPERTS):
            @pl.when(cnts[e] > 0)
            def _(e=e):
                out_ref[...] += jnp.dot(onehot(e, 0, 2 * TOK_BLK), ye[e][0], preferred_element_type=F32)

    if final:
        x = out_ref[...]
        out_ref[...] = x * lax.rsqrt(jnp.mean(x * x, axis=-1, keepdims=True) + EPS) * gf_ref[...]


def _scatter(x2, wsel, pos, ye, offs, gf, cap, final):
    T = x2.shape[0]
    nblk = T // TOK_BLK
    assert cap >= 2 * TOK_BLK and cap % TOK_BLK == 0

    def ye_spec(e):
        return pl.BlockSpec((pl.Element(1), pl.Element(2 * TOK_BLK), pl.Element(D_MODEL)),
                            lambda j, o: (e, pl.multiple_of(_slot_window_start(o[j * N_EXPERTS + e], cap), TOK_BLK),
                                          0))

    row = lambda w: pl.BlockSpec((TOK_BLK, w), lambda j, o: (j, 0))
    col = pl.BlockSpec((N_EXPERTS, TOK_BLK), lambda j, o: (0, j))
    grid_spec = pltpu.PrefetchScalarGridSpec(
        num_scalar_prefetch=1,
        grid=(nblk,),
        in_specs=[row(D_MODEL), col, col, pl.BlockSpec((1, D_MODEL), lambda j, o: (0, 0))]
                 + [ye_spec(e) for e in range(N_EXPERTS)],
        out_specs=row(D_MODEL),
    )
    return pl.pallas_call(
        functools.partial(_scatter_kernel, cap=cap, final=final),
        out_shape=jax.ShapeDtypeStruct((T, D_MODEL), F32),
        grid_spec=grid_spec,
        compiler_params=_cparams(("arbitrary",)),
        name="scatter",
    )(offs, x2, wsel, pos, gf, *([ye] * N_EXPERTS))


def _block_diag(w):
    n, a, b = w.shape
    out = jnp.zeros((n * a, n * b), w.dtype)
    for i in range(n):
        out = out.at[i * a:(i + 1) * a, i * b:(i + 1) * b].set(w[i])
    return out


def _rope_tables(S):
    inv = ROPE_THETA ** (-jnp.arange(0, ROT_DIM, 2, dtype=F32) / ROT_DIM)
    ang = jnp.arange(S, dtype=F32)[:, None] * inv[None, :]
    cos, sin = jnp.cos(ang), jnp.sin(ang)
    half = ROT_DIM // 2
    ones, zeros = jnp.ones((S, ATT_HD - ROT_DIM), F32), jnp.zeros((S, ATT_HD - ROT_DIM), F32)
    zh = jnp.zeros((S, half), F32)
    c = jnp.concatenate([cos, cos, ones], axis=1)
    s1 = jnp.concatenate([-sin, zh, zeros], axis=1)
    s2 = jnp.concatenate([zh, sin, zeros], axis=1)
    return tuple(jnp.tile(t, (1, 2)) for t in (c, s1, s2))


def _expanders():
    eg = np.zeros((2, 128, GROUP_W), np.float32)
    eb = np.zeros((2, 128, GROUP_W), np.float32)
    for d in range(2):
        for h in range(DN_HEADS):
            eg[d, d * DN_HEADS + h, h * DN_DK:(h + 1) * DN_DK] = 1.0
            eb[d, 2 * DN_HEADS + d * DN_HEADS + h, h * DN_DK:(h + 1) * DN_DK] = 1.0
    hd = np.arange(GROUP_W) // DN_DK
    bd = (hd[:, None] == hd[None, :]).astype(np.float32)
    return jnp.asarray(eg), jnp.asarray(eb), jnp.asarray(bd)


def _layer_params(l, norm1_g, w_in, conv_a_w, conv_a_b, rg_wa, rg_ba, rg_wx, rg_bx, rg_lambda, pool_w, pool_scale,
                  dn_conv_w, dn_A_log, dn_dt_bias, dn_norm_g, mix_norm_g, w_out, norm2_g, router_w,
                  exp_w_gate, exp_w_up, exp_w_down):
    w = w_in[l]
    w_cat = jnp.concatenate([w[:, 0:1536], w[:, 1552:2576], w[:, 1536:1552],
                             jnp.zeros((D_MODEL, COL_END - COL_CAB - 16), F32)], axis=1).astype(BF16)
    pad8 = lambda v: jnp.concatenate([v.reshape(1, 2 * DN_HEADS), jnp.zeros((1, 120), F32)], axis=1)
    return dict(
        g1=norm1_g[l].reshape(1, D_MODEL), w_cat=w_cat,
        conv_a_w=conv_a_w[l], conv_a_b=conv_a_b[l].reshape(1, GROUP_W),
        wa=[_block_diag(rg_wa[l, d]).astype(BF16) for d in range(2)],
        wx=[_block_diag(rg_wx[l, d]).astype(BF16) for d in range(2)],
        ba=[rg_ba[l, d].reshape(1, GROUP_W) for d in range(2)],
        bx=[rg_bx[l, d].reshape(1, GROUP_W) for d in range(2)],
        lam=[rg_lambda[l, d].reshape(1, GROUP_W) for d in range(2)],
        pool_w=_block_diag(pool_w[l]).astype(BF16), pool_scale=pool_scale[l].reshape(1, GROUP_W),
        dn_conv_w=dn_conv_w[l], alog=pad8(dn_A_log[l]), dtb=pad8(dn_dt_bias[l]),
        dng=jnp.tile(dn_norm_g[l], DN_HEADS).reshape(1, GROUP_W),
        mg=mix_norm_g[l].reshape(1, D_MODEL), wo=w_out[l].astype(BF16),
        g2=norm2_g[l].reshape(1, D_MODEL),
        rw=jnp.concatenate([router_w[l], jnp.zeros((D_MODEL, 128 - N_EXPERTS), F32)], axis=1),
        wg=exp_w_gate[l], wu=exp_w_up[l], wd=exp_w_down[l],
    )


def _chunk_rows(gcn, d):
    B, S, _ = gcn.shape
    n = S // DN_CHUNK
    g = gcn[:, :, d * DN_HEADS:(d + 1) * DN_HEADS].reshape(B, n, DN_CHUNK, DN_HEADS)
    return g.transpose(0, 1, 3, 2).reshape(B, n, 1, GROUP_W)


def _encoder(x, layers, final_g, consts):
    B, S, D = x.shape
    T = B * S
    cap = max(1, EC_CAPACITY * T // N_EXPERTS)
    eg, eb, bdmask = consts
    rope_c, rope_s1, rope_s2 = _rope_tables(S)
    xt = x.reshape(T, D)
    for l, p in enumerate(layers):
        pa, pb, pq, pg, pd, pcab = _inproj(xt, p["g1"], p["w_cat"], rope_c, rope_s1, rope_s2, S)
        pa3 = pa.reshape(B, S, 2 * GROUP_W)
        rg = lambda d: (p["conv_a_w"], p["conv_a_b"], p["wa"][d], p["ba"][d], p["wx"][d], p["bx"][d], p["lam"][d])
        hf = _rglru(pa3, None, *rg(0), reverse=False)
        ya = _rglru(pa3, hf, *rg(1), reverse=True)
        yb = _pool(pb.reshape(B, S, GROUP_W), p["pool_w"], p["pool_scale"])
        qn, kn, vv, gcf, gcb, bef, beb, gcn = _dnprep(pq.reshape(B, S, 3 * GROUP_W), pcab.reshape(B, S, 128),
                                                      p["dn_conv_w"], p["alog"], p["dtb"], eg, eb)
        of, ob = _delta(qn, kn, vv, gcf, gcb, bef, beb, _chunk_rows(gcn, 0), _chunk_rows(gcn, 1), bdmask)
        yd = _attention(pd.reshape(B, S, 3 * GROUP_W))
        flat = lambda a: a.reshape(T, GROUP_W)
        x2, h2, aff_t = _outproj(flat(ya), flat(yb), flat(of), flat(ob), pg, flat(yd), xt,
                                 p["dng"], p["mg"], p["wo"], p["g2"], p["rw"])
        mask, pos, wsel, offs3 = _route(aff_t, cap)
        offs = jnp.concatenate([offs3[:, :, 0].reshape(-1), jnp.full((N_EXPERTS,), cap, I32)])
        xe = _gather(h2, mask.reshape(N_EXPERTS, 1, T), pos.reshape(N_EXPERTS, 1, T), offs, cap)
        ye = _ffn(xe, p["wg"], p["wu"], p["wd"], cap)
        xt = _scatter(x2, wsel, pos, ye, offs, final_g, cap, final=(l == len(layers) - 1))
    return xt.reshape(B, S, D)


def kernel(x_prompt, x_sample, norm1_g, w_in, conv_a_w, conv_a_b, rg_wa, rg_ba, rg_wx, rg_bx, rg_lambda, pool_w, pool_scale, dn_conv_w, dn_A_log, dn_dt_bias, dn_norm_g, mix_norm_g, w_out, norm2_g, router_w, exp_w_gate, exp_w_up, exp_w_down, final_norm_g):
    layers = [_layer_params(l, norm1_g, w_in, conv_a_w, conv_a_b, rg_wa, rg_ba, rg_wx, rg_bx, rg_lambda, pool_w,
                            pool_scale, dn_conv_w, dn_A_log, dn_dt_bias, dn_norm_g, mix_norm_g, w_out, norm2_g,
                            router_w, exp_w_gate, exp_w_up, exp_w_down) for l in range(DEPTH)]
    consts = _expanders()
    final_g = final_norm_g.reshape(1, D_MODEL)
    return (_encoder(x_prompt, layers, final_g, consts), _encoder(x_sample, layers, final_g, consts))
```

```python
import functools
import numpy as np
import jax
import jax.numpy as jnp
from jax import lax
from jax.experimental import pallas as pl
from jax.experimental.pallas import tpu as pltpu

F32, BF16, I32 = jnp.float32, jnp.bfloat16, jnp.int32

D_MODEL = 1024
DEPTH = 2
GROUP_W = 256
RG_C = 8.0
POOL_WINDOWS = (2, 4, 8, 16)
DN_HEADS = 4
DN_DK = 64
DN_CHUNK = 64
ATT_HD = 64
ROT_DIM = 16
ROPE_THETA = 500000.0
ATT_WINDOWS = (128, 512, 2048)
ATT_DILATIONS = (1, 4, 16)
N_EXPERTS = 16
EC_CAPACITY = 2
EPS = 1e-6
NEG = -1e30

COL_A, COL_B, COL_CQ, COL_CG, COL_D, COL_CAB, COL_END = 0, 512, 768, 1536, 1792, 2560, 2688

ROW_TILE = 512
HALO = 8
DELTA_CB = 8
DELTA_GROUP = 8
ATT_QB = 128
ATT_UNROLL = 4
TOK_BLK = 256
GATHER_W = 64
GATHER_JB = 4
GATHER_E = 4
SCATTER_W = 128
SCATTER_G = 2
VMEM_LIMIT = 56 * 1024 * 1024


def _cparams(sem):
    return pltpu.CompilerParams(dimension_semantics=sem, vmem_limit_bytes=VMEM_LIMIT)


def _dot(a, b):
    return jnp.dot(a.astype(BF16), b.astype(BF16), preferred_element_type=F32)


def _split(a, pieces):
    out, rem = [], a
    for i in range(pieces):
        t = rem.astype(BF16)
        out.append(t)
        if i + 1 < pieces:
            rem = rem - t.astype(F32)
    return out


def _dot_sel(a, sel, pieces=3):
    sel = sel.astype(BF16)
    return sum(jnp.dot(t, sel, preferred_element_type=F32) for t in _split(a, pieces))


def _sel_dot(sel, a, pieces=3):
    sel = sel.astype(BF16)
    return sum(jnp.dot(sel, t, preferred_element_type=F32) for t in _split(a, pieces))


def _dot_nt(a, b):
    return lax.dot_general(a.astype(BF16), b.astype(BF16), (((1,), (1,)), ((), ())), preferred_element_type=F32)


def _dot_tn(a, b):
    return lax.dot_general(a.astype(BF16), b.astype(BF16), (((0,), (0,)), ((), ())), preferred_element_type=F32)


def _sigmoid(x):
    return 1.0 / (1.0 + jnp.exp(-x))


def _softplus(x):
    return jnp.maximum(x, 0.0) + jnp.log1p(jnp.exp(-jnp.abs(x)))


def _shift_rows(e, k):
    n = e.shape[0]
    return e if k % n == 0 else pltpu.roll(e, (-k) % n, axis=0)


def _with_halo(cur_ref, prev_ref, next_ref, first, last):
    prev = jnp.where(first, 0.0, prev_ref[0])
    nxt = jnp.where(last, 0.0, next_ref[0])
    return jnp.concatenate([prev, cur_ref[0], nxt], axis=0)


def _halo_specs(ts, width, S, blk_of):
    per = ts // HALO
    last = S // HALO - 1
    cur = pl.BlockSpec((1, ts, width), lambda b, c: (b, blk_of(c), 0))
    prev = pl.BlockSpec((1, HALO, width), lambda b, c: (b, jnp.maximum(blk_of(c) * per - 1, 0), 0))
    nxt = pl.BlockSpec((1, HALO, width), lambda b, c: (b, jnp.minimum((blk_of(c) + 1) * per, last), 0))
    return cur, prev, nxt


def _inproj_kernel(x_ref, g_ref, w_ref, c_ref, s1_ref, s2_ref, pa_ref, pb_ref, pq_ref, pg_ref, pd_ref, pcab_ref):
    x = x_ref[...]
    h = x * lax.rsqrt(jnp.mean(x * x, axis=-1, keepdims=True) + EPS) * g_ref[...]
    hb = h.astype(BF16)

    def mm(lo, hi):
        return jnp.dot(hb, w_ref[:, lo:hi], preferred_element_type=F32)

    pa_ref[...] = mm(COL_A, COL_B)
    pb_ref[...] = mm(COL_B, COL_CQ)
    pq_ref[...] = mm(COL_CQ, COL_CG)
    pg_ref[...] = mm(COL_CG, COL_D)
    c, s1, s2 = c_ref[...], s1_ref[...], s2_ref[...]
    for part in range(2):
        yy = mm(COL_D + 256 * part, COL_D + 256 * part + 256)
        for half in range(2):
            y = yy[:, 128 * half:128 * half + 128]
            y = y * c + pltpu.roll(y, 128 - ROT_DIM // 2, axis=1) * s1 + pltpu.roll(y, ROT_DIM // 2, axis=1) * s2
            if part == 0:
                y = y * (ATT_HD ** -0.5)
            pd_ref[:, 256 * part + 128 * half:256 * part + 128 * half + 128] = y
    vcab = mm(COL_D + 512, COL_END)
    pd_ref[:, 512:768] = vcab[:, 0:256]
    pcab_ref[...] = vcab[:, 256:384]


def _inproj(xt, g1, w_cat, rope_c, rope_s1, rope_s2, S):
    T = xt.shape[0]
    tm = min(ROW_TILE, S)
    per_seq = S // tm
    widths = (512, 256, 768, 256, 768, 128)
    row = lambda w: pl.BlockSpec((tm, w), lambda i: (i, 0))
    rope = pl.BlockSpec((tm, 128), lambda i: (i % per_seq, 0))
    return pl.pallas_call(
        _inproj_kernel,
        out_shape=[jax.ShapeDtypeStruct((T, w), F32) for w in widths],
        grid=(T // tm,),
        in_specs=[row(D_MODEL), pl.BlockSpec((1, D_MODEL), lambda i: (0, 0)),
                  pl.BlockSpec((D_MODEL, COL_END), lambda i: (0, 0)), rope, rope, rope],
        out_specs=[row(w) for w in widths],
        compiler_params=_cparams(("parallel",)),
        name="inproj",
    )(xt, g1, w_cat, rope_c, rope_s1, rope_s2)


def _rglru_kernel(*refs, reverse, ts, nc):
    if reverse:
        (cur_ref, prev_ref, next_ref, gate_ref, hf_ref, cw_ref, cb_ref, wa_ref, ba_ref, wx_ref, bx_ref, lam_ref,
         out_ref, a_ref, b_ref, carry_ref) = refs
    else:
        (cur_ref, prev_ref, next_ref, cw_ref, cb_ref, wa_ref, ba_ref, wx_ref, bx_ref, lam_ref,
         out_ref, a_ref, b_ref, carry_ref) = refs
    c = pl.program_id(1)
    blk = (nc - 1 - c) if reverse else c
    e = _with_halo(cur_ref, prev_ref, next_ref, blk == 0, blk == nc - 1)
    cw = cw_ref[...]
    sl = slice(HALO, HALO + ts)
    u = (cw[0:1] * _shift_rows(e, -2)[sl] + cw[1:2] * _shift_rows(e, -1)[sl] + cw[2:3] * e[sl]
         + cw[3:4] * _shift_rows(e, 1)[sl]) + cb_ref[...]
    r = _sigmoid(_dot(u, wa_ref[...]) + ba_ref[...])
    i = _sigmoid(_dot(u, wx_ref[...]) + bx_ref[...])
    log_a = -RG_C * r * _softplus(-lam_ref[...])
    a_ref[...] = jnp.exp(log_a)
    b_ref[...] = jnp.sqrt(1.0 - jnp.exp(2.0 * log_a)) * (i * u)

    @pl.when(c == 0)
    def _():
        carry_ref[...] = jnp.zeros_like(carry_ref)

    row = lax.broadcasted_iota(I32, (HALO, GROUP_W), 0)
    nt = ts // HALO

    def body(it, carry):
        ti = (nt - 1 - it) if reverse else it
        st = pl.multiple_of(ti * HALO, HALO)
        a = a_ref[pl.ds(st, HALO), :]
        b = b_ref[pl.ds(st, HALO), :]
        for d in (1, 2, 4):
            k = d if reverse else -d
            valid = (row < HALO - d) if reverse else (row >= d)
            b = jnp.where(valid, a * _shift_rows(b, k) + b, b)
            a = jnp.where(valid, a * _shift_rows(a, k), a)
        h = a * carry + b
        if reverse:
            g = gate_ref[0, pl.ds(st, HALO), :]
            cdf = 0.5 * (1.0 + jnp.tanh(np.float32(np.sqrt(2.0 / np.pi)) * (g + 0.044715 * (g * g * g))))
            out_ref[0, pl.ds(st, HALO), :] = (g * cdf) * (hf_ref[0, pl.ds(st, HALO), :] + h)
            return jnp.broadcast_to(h[0:1], h.shape)
        out_ref[0, pl.ds(st, HALO), :] = h
        return jnp.broadcast_to(h[HALO - 1:HALO], h.shape)

    carry_ref[...] = lax.fori_loop(0, nt, body, carry_ref[...], unroll=4)


def _rglru(pa3, hf, cw, cb, wa, ba, wx, bx, lam, reverse):
    B, S, _ = pa3.shape
    ts = min(ROW_TILE, S)
    nc = S // ts
    blk_of = (lambda c: nc - 1 - c) if reverse else (lambda c: c)
    cur, prev, nxt = _halo_specs(ts, GROUP_W, S, blk_of)
    tile = pl.BlockSpec((1, ts, GROUP_W), lambda b, c: (b, blk_of(c), 0))
    const = lambda shape: pl.BlockSpec(shape, lambda b, c: (0,) * len(shape))
    in_specs = [cur, prev, nxt]
    args = [pa3, pa3, pa3]
    if reverse:
        in_specs += [pl.BlockSpec((1, ts, GROUP_W), lambda b, c: (b, blk_of(c), 1)), tile]
        args += [pa3, hf]
    in_specs += [const((4, GROUP_W)), const((1, GROUP_W)), const((GROUP_W, GROUP_W)), const((1, GROUP_W)),
                 const((GROUP_W, GROUP_W)), const((1, GROUP_W)), const((1, GROUP_W))]
    args += [cw, cb, wa, ba, wx, bx, lam]
    return pl.pallas_call(
        functools.partial(_rglru_kernel, reverse=reverse, ts=ts, nc=nc),
        out_shape=jax.ShapeDtypeStruct((B, S, GROUP_W), F32),
        grid=(B, nc),
        in_specs=in_specs,
        out_specs=tile,
        scratch_shapes=[pltpu.VMEM((ts, GROUP_W), F32), pltpu.VMEM((ts, GROUP_W), F32),
                        pltpu.VMEM((HALO, GROUP_W), F32)],
        compiler_params=_cparams(("parallel", "arbitrary")),
        name="rglru_bwd" if reverse else "rglru_fwd",
    )(*args)


def _pool_kernel(cur_ref, prev_ref, next_ref, w_ref, sc_ref, out_ref, *, ts, nc, S):
    c = pl.program_id(1)
    e = _with_halo(cur_ref, prev_ref, next_ref, c == 0, c == nc - 1)
    sl = slice(HALO, HALO + ts)
    a2 = e + _shift_rows(e, 1)
    a4 = a2 + _shift_rows(a2, 2)
    a8 = a4 + _shift_rows(a4, 4)
    a16 = a8 + _shift_rows(a8, 8)
    sums = [_shift_rows(a, -(w // 2))[sl] for a, w in zip((a2, a4, a8, a16), POOL_WINDOWS)]
    gi = lax.broadcasted_iota(I32, (ts, GROUP_W), 1) // (GROUP_W // len(POOL_WINDOWS))
    ssum = jnp.where(gi == 0, sums[0], jnp.where(gi == 1, sums[1], jnp.where(gi == 2, sums[2], sums[3])))
    hw = jnp.where(gi == 0, 1, jnp.where(gi == 1, 2, jnp.where(gi == 2, 4, 8)))
    t = c * ts + lax.broadcasted_iota(I32, (ts, GROUP_W), 0)
    cnt = (jnp.minimum(t + hw, S) - jnp.maximum(t - hw, 0)).astype(F32)
    p = ssum / cnt - e[sl]
    out_ref[0] = _dot(p, w_ref[...]) * sc_ref[...]


def _pool(pb3, w_bd, scale):
    B, S, _ = pb3.shape
    ts = min(ROW_TILE, S)
    nc = S // ts
    cur, prev, nxt = _halo_specs(ts, GROUP_W, S, lambda c: c)
    return pl.pallas_call(
        functools.partial(_pool_kernel, ts=ts, nc=nc, S=S),
        out_shape=jax.ShapeDtypeStruct((B, S, GROUP_W), F32),
        grid=(B, nc),
        in_specs=[cur, prev, nxt, pl.BlockSpec((GROUP_W, GROUP_W), lambda b, c: (0, 0)),
                  pl.BlockSpec((1, GROUP_W), lambda b, c: (0, 0))],
        out_specs=pl.BlockSpec((1, ts, GROUP_W), lambda b, c: (b, c, 0)),
        compiler_params=_cparams(("parallel", "parallel")),
        name="pool",
    )(pb3, pb3, pb3, w_bd, scale)


def _head_sum_matrix(n, group):
    r = lax.broadcasted_iota(I32, (n, n), 0) // group
    c = lax.broadcasted_iota(I32, (n, n), 1) // group
    return (r == c).astype(F32)


def _dnprep_kernel(cur_ref, prev_ref, next_ref, cab_ref, cw_ref, alog_ref, dtb_ref, eg_ref, eb_ref,
                   q_ref, k_ref, v_ref, gcf_ref, gcb_ref, bef_ref, beb_ref, gcn_ref, *, ts, nc):
    c = pl.program_id(1)
    e = _with_halo(cur_ref, prev_ref, next_ref, c == 0, c == nc - 1)
    cw = cw_ref[...]
    sl = slice(HALO, HALO + ts)
    y = (cw[0:1] * _shift_rows(e, -2)[sl] + cw[1:2] * _shift_rows(e, -1)[sl] + cw[2:3] * e[sl]
         + cw[3:4] * _shift_rows(e, 1)[sl])
    y = y * _sigmoid(y)
    q, k = y[:, 0:GROUP_W], y[:, GROUP_W:2 * GROUP_W]
    hs = _head_sum_matrix(GROUP_W, DN_DK)
    q_ref[0] = q * lax.rsqrt(_dot_sel(q * q, hs) + EPS) * (DN_DK ** -0.5)
    k_ref[0] = k * lax.rsqrt(_dot_sel(k * k, hs) + EPS)
    v_ref[0] = y[:, 2 * GROUP_W:3 * GROUP_W]
    cab = cab_ref[0]
    g = -jnp.exp(alog_ref[...]) * _softplus(cab + dtb_ref[...])
    beta = _sigmoid(cab)
    bef_ref[0] = _dot_sel(beta, eb_ref[0])
    beb_ref[0] = _dot_sel(beta, eb_ref[1])
    r = lax.broadcasted_iota(I32, (ts, ts), 0)
    cc = lax.broadcasted_iota(I32, (ts, ts), 1)
    same = (r // DN_CHUNK) == (cc // DN_CHUNK)
    lane = lax.broadcasted_iota(I32, (ts, 128), 1)
    gcn = jnp.where(lane < DN_HEADS, _sel_dot(same & (r >= cc), g), _sel_dot(same & (r <= cc), g))
    gcn_ref[0] = gcn
    gcf_ref[0] = _dot_sel(gcn, eg_ref[0])
    gcb_ref[0] = _dot_sel(gcn, eg_ref[1])


def _dnprep(pq3, pcab3, cw, alog_row, dtb_row, eg, eb):
    B, S, _ = pq3.shape
    ts = min(ROW_TILE, S)
    nc = S // ts
    cur, prev, nxt = _halo_specs(ts, 3 * GROUP_W, S, lambda c: c)
    tile = pl.BlockSpec((1, ts, GROUP_W), lambda b, c: (b, c, 0))
    const = lambda shape: pl.BlockSpec(shape, lambda b, c: (0,) * len(shape))
    return pl.pallas_call(
        functools.partial(_dnprep_kernel, ts=ts, nc=nc),
        out_shape=[jax.ShapeDtypeStruct((B, S, GROUP_W), F32)] * 7 + [jax.ShapeDtypeStruct((B, S, 128), F32)],
        grid=(B, nc),
        in_specs=[cur, prev, nxt, pl.BlockSpec((1, ts, 128), lambda b, c: (b, c, 0)),
                  const((4, 3 * GROUP_W)), const((1, 128)), const((1, 128)),
                  const((2, 128, GROUP_W)), const((2, 128, GROUP_W))],
        out_specs=[tile] * 7 + [pl.BlockSpec((1, ts, 128), lambda b, c: (b, c, 0))],
        compiler_params=_cparams(("parallel", "parallel")),
        name="dnprep",
    )(pq3, pq3, pq3, pcab3, cw, alog_row, dtb_row, eg, eb)


def _delta_kernel(qf, kf, vf, gcf, bef, grf, qb, kb, vb, gcb, beb, grb, bd_ref, of_ref, ob_ref,
                  s_ref, p_scr, x_scr, wq_scr, u_scr, at_scr, kd_scr, vb_scr, kg_scr, *, cb):
    j = pl.program_id(1)

    @pl.when(j == 0)
    def _():
        s_ref[...] = jnp.zeros_like(s_ref)

    C = DN_CHUNK
    bd = bd_ref[...] > 0.0
    c_idx = lax.broadcasted_iota(I32, (C, GROUP_W), 0)
    m_idx = lax.broadcasted_iota(I32, (C, GROUP_W), 1) % C
    eye = (c_idx == m_idx).astype(F32)

    def blockdiag(x):
        return jnp.where(bd, jnp.tile(x.astype(BF16), (DN_HEADS, 1)), jnp.zeros((), BF16))

    dirs = ((qf, kf, vf, gcf, bef, grf, of_ref), (qb, kb, vb, gcb, beb, grb, ob_ref))
    units = [(d, ci) for ci in range(cb) for d in range(2)]
    per_group = 2 * DELTA_GROUP
    groups = [list(range(g, g + per_group)) for g in range(0, len(units), per_group)]

    def chunk_of(d, ci):
        return cb - 1 - ci if d == 1 else ci

    def prepare(ui):
        d, ci = units[ui]
        q_r, k_r, v_r, gc_r, be_r, gr_r, _ = dirs[d]
        rev = d == 1
        tril = (c_idx <= m_idx) if rev else (c_idx >= m_idx)
        strict = (c_idx < m_idx) if rev else (c_idx > m_idx)
        last = 0 if rev else C - 1
        cc = chunk_of(d, ci)
        rows = slice(cc * C, (cc + 1) * C)
        q, k, v = q_r[0, rows, :], k_r[0, rows, :], v_r[0, rows, :]
        gc, be, gr = gc_r[0, rows, :], be_r[0, rows, :], gr_r[0, cc]
        eg = jnp.exp(gc)
        kbeta = k * be
        kkqk = _dot_nt(jnp.concatenate([kbeta, q], axis=0), blockdiag(k))
        decay = jnp.exp(jnp.where(tril, gc - gr, -jnp.inf))
        at_scr[ui] = kkqk[C:] * decay
        p = -jnp.where(strict, kkqk[:C] * decay, 0.0)
        p_scr[ui] = p
        x_scr[ui] = eye + p
        wq_scr[ui, C:2 * C, :] = q * eg
        kd_scr[ui] = k * jnp.exp(gc[last:last + 1] - gc)
        vb_scr[ui] = v * be
        kg_scr[ui] = kbeta * eg

    def solve(group):
        for ui in group:
            p = p_scr[ui]
            p_scr[ui] = _dot(p, blockdiag(p))
        for _ in range(4):
            for ui in group:
                p, x = p_scr[ui], x_scr[ui]
                px = _dot(jnp.concatenate([p, x], axis=0), blockdiag(p))
                p_scr[ui] = px[:C]
                x_scr[ui] = x + px[C:]
        for ui in group:
            x = x_scr[ui]
            x = x + _dot(x, blockdiag(p_scr[ui]))
            u_scr[ui] = _dot(x, blockdiag(vb_scr[ui]))
            wq_scr[ui, 0:C, :] = _dot(x, blockdiag(kg_scr[ui]))

    def recur(ui):
        d, ci = units[ui]
        gc_r, o_r = dirs[d][3], dirs[d][6]
        cc = chunk_of(d, ci)
        last = cc * C + (0 if d == 1 else C - 1)
        state = s_ref[d]
        ws_qs = _dot(wq_scr[ui], state)
        v_new = u_scr[ui] - ws_qs[:C]
        o_r[0, cc * C:(cc + 1) * C, :] = ws_qs[C:] + _dot(at_scr[ui], blockdiag(v_new))
        s_ref[d] = (state * jnp.exp(gc_r[0, last:last + 1, :])
                    + jnp.where(bd, _dot_tn(kd_scr[ui], v_new), 0.0))

    for gi, group in enumerate(groups):
        for ui in group:
            prepare(ui)
        solve(group)
        if gi > 0:
            for ui in groups[gi - 1]:
                recur(ui)
    for ui in groups[-1]:
        recur(ui)


def _delta(qn, kn, vv, gcf, gcb, bef, beb, grf, grb, bdmask):
    B, S, _ = qn.shape
    cb = DELTA_CB
    rb = cb * DN_CHUNK
    nb = S // rb
    f = lambda b, j: (b, j, 0)
    r = lambda b, j: (b, nb - 1 - j, 0)
    tf = pl.BlockSpec((1, rb, GROUP_W), f)
    tr = pl.BlockSpec((1, rb, GROUP_W), r)
    gf = pl.BlockSpec((1, cb, 1, GROUP_W), lambda b, j: (b, j, 0, 0))
    gr = pl.BlockSpec((1, cb, 1, GROUP_W), lambda b, j: (b, nb - 1 - j, 0, 0))
    return pl.pallas_call(
        functools.partial(_delta_kernel, cb=cb),
        out_shape=[jax.ShapeDtypeStruct((B, S, GROUP_W), F32)] * 2,
        grid=(B, nb),
        in_specs=[tf, tf, tf, tf, tf, gf, tr, tr, tr, tr, tr, gr,
                  pl.BlockSpec((GROUP_W, GROUP_W), lambda b, j: (0, 0))],
        out_specs=[tf, tr],
        scratch_shapes=[pltpu.VMEM((2, GROUP_W, GROUP_W), F32)]
                       + [pltpu.VMEM((2 * cb, n * DN_CHUNK, GROUP_W), F32) for n in (1, 1, 2, 1, 1, 1, 1, 1)],
        compiler_params=_cparams(("parallel", "arbitrary")),
        name="delta",
    )(qn, kn, vv, gcf, bef, grf, qn, kn, vv, gcb, beb, grb, bdmask)


def _attn_kernel(q_ref, k_ref, v_ref, out_ref, m_ref, l_ref, o_ref, *, S):
    head0 = lax.broadcasted_iota(I32, (1, 128), 1) < ATT_HD

    def rows(start, n, dil):
        return pl.ds(start, n) if dil == 1 else pl.ds(start, n, stride=dil)

    for bi, (win, dil) in enumerate(zip(ATT_WINDOWS, ATT_DILATIONS)):
        half = win // (2 * dil)
        L = S // dil
        qb_n = min(ATT_QB, L)
        kw = min(L, qb_n + 2 * half)
        nqb = L // qb_n

        def block_stats(idx, dil=dil, half=half, L=L, qb_n=qb_n, kw=kw, nqb=nqb):
            r = idx // nqb
            m0 = (idx % nqb) * qb_n
            ks = jnp.clip(m0 - half, 0, L - kw)
            qsel = rows(r + m0 * dil, qb_n, dil)
            ksel = rows(r + ks * dil, kw, dil)
            q = q_ref[0, qsel, :]
            kk = k_ref[0, ksel, :].astype(BF16)
            vv = v_ref[0, ksel, :].astype(BF16)
            qpos = m0 + lax.broadcasted_iota(I32, (qb_n, kw), 0)
            kpos = ks + lax.broadcasted_iota(I32, (qb_n, kw), 1)
            valid = jnp.abs(kpos - qpos) <= half
            ms, ls, os_ = [], [], []
            for h in range(2):
                qh = jnp.where(head0 if h == 0 else ~head0, q, 0.0)
                s = jnp.where(valid, _dot_nt(qh, kk), NEG)
                m = jnp.max(s, axis=-1, keepdims=True)
                p = jnp.exp(s - m)
                ms.append(m)
                ls.append(jnp.sum(p, axis=-1, keepdims=True))
                os_.append(_dot(p, vv))
            return (qsel, jnp.where(head0, ms[0], ms[1]), jnp.where(head0, ls[0], ls[1]),
                    jnp.where(head0, os_[0], os_[1]))

        def body(it, carry, bi=bi):
            stats = [block_stats(it * ATT_UNROLL + s) for s in range(ATT_UNROLL)]
            if bi > 0:
                olds = [(m_ref[qsel, :], l_ref[qsel, :], o_ref[qsel, :]) for qsel, _, _, _ in stats]
                merged = []
                for (qsel, m, l, o), (m_old, l_old, o_old) in zip(stats, olds):
                    m_new = jnp.maximum(m_old, m)
                    w_old, w_cur = jnp.exp(m_old - m_new), jnp.exp(m - m_new)
                    merged.append((qsel, m_new, w_old * l_old + w_cur * l, w_old * o_old + w_cur * o))
                stats = merged
            for qsel, m, l, o in stats:
                if bi == len(ATT_WINDOWS) - 1:
                    out_ref[0, qsel, :] = o / l
                else:
                    m_ref[qsel, :] = m
                    l_ref[qsel, :] = l
                    o_ref[qsel, :] = o
            return carry

        lax.fori_loop(0, dil * nqb // ATT_UNROLL, body, 0)


def _attention(pd3):
    B, S, _ = pd3.shape
    spec = lambda off: pl.BlockSpec((1, S, 128), lambda b, p: (b, 0, off + p))
    return pl.pallas_call(
        functools.partial(_attn_kernel, S=S),
        out_shape=jax.ShapeDtypeStruct((B, S, GROUP_W), F32),
        grid=(B, 2),
        in_specs=[spec(0), spec(2), spec(4)],
        out_specs=pl.BlockSpec((1, S, 128), lambda b, p: (b, 0, p)),
        scratch_shapes=[pltpu.VMEM((S, 128), F32)] * 3,
        compiler_params=_cparams(("parallel", "parallel")),
        name="attention",
    )(pd3, pd3, pd3)


def _outproj_kernel(ya_ref, yb_ref, of_ref, ob_ref, cg_ref, yd_ref, x_ref, dng_ref, mg_ref, wo_ref, g2_ref, rw_ref,
                    x2_ref, h2_ref, aff_ref):
    o = of_ref[...] + ob_ref[...]
    ms = _dot_sel(o * o, _head_sum_matrix(GROUP_W, DN_DK)) * (1.0 / DN_DK)
    cg = cg_ref[...]
    yc = (o * lax.rsqrt(ms + EPS) * dng_ref[...]) * (cg * _sigmoid(cg))
    acc = x_ref[...]
    for gi, y in enumerate((ya_ref[...], yb_ref[...], yc, yd_ref[...])):
        sl = slice(gi * GROUP_W, (gi + 1) * GROUP_W)
        mix = y * lax.rsqrt(jnp.mean(y * y, axis=-1, keepdims=True) + EPS) * mg_ref[:, sl]
        acc = acc + _dot(mix, wo_ref[sl, :])
    x2_ref[...] = acc
    h2 = acc * lax.rsqrt(jnp.mean(acc * acc, axis=-1, keepdims=True) + EPS) * g2_ref[...]
    h2_ref[...] = h2.astype(BF16)
    h_hi, h_lo = _split(h2, 2)
    z = jnp.dot(h_hi, rw_ref[...], preferred_element_type=F32) + jnp.dot(h_lo, rw_ref[...], preferred_element_type=F32)
    logits = z + pltpu.roll(z, 128 - N_EXPERTS, axis=1)
    logits = jnp.where(lax.broadcasted_iota(I32, logits.shape, 1) < N_EXPERTS, logits, -jnp.inf)
    ex = jnp.exp(logits - jnp.max(logits, axis=-1, keepdims=True))
    aff = ex / jnp.sum(ex, axis=-1, keepdims=True)
    aff_ref[...] = aff.T[0:N_EXPERTS, :]


def _outproj(ya, yb, of, ob, cg, yd, xt, dng, mg, wo, g2, rw):
    T = xt.shape[0]
    tm = 256
    row = lambda w: pl.BlockSpec((tm, w), lambda i: (i, 0))
    const = lambda shape: pl.BlockSpec(shape, lambda i: (0, 0))
    return pl.pallas_call(
        _outproj_kernel,
        out_shape=[jax.ShapeDtypeStruct((T, D_MODEL), F32), jax.ShapeDtypeStruct((T, D_MODEL), BF16),
                   jax.ShapeDtypeStruct((N_EXPERTS, T), F32)],
        grid=(T // tm,),
        in_specs=[row(GROUP_W)] * 6 + [row(D_MODEL), const((1, GROUP_W)), const((1, D_MODEL)),
                                       const((D_MODEL, D_MODEL)), const((1, D_MODEL)), const((D_MODEL, 128))],
        out_specs=[row(D_MODEL), row(D_MODEL), pl.BlockSpec((N_EXPERTS, tm), lambda i: (0, i))],
        compiler_params=_cparams(("parallel",)),
        name="outproj",
    )(ya, yb, of, ob, cg, yd, xt, dng, mg, wo, g2, rw)


def _strict_upper(n):
    return (lax.broadcasted_iota(I32, (n, n), 0) < lax.broadcasted_iota(I32, (n, n), 1)).astype(BF16)


def _route_kernel(aff_ref, mask_ref, pos_ref, wsel_ref, offs_ref, *, cap, nblk):
    keys = pltpu.bitcast(aff_ref[...], I32)

    def bit_body(i, thr):
        cand = thr | lax.shift_left(jnp.int32(1), 30 - i)
        cnt = jnp.sum((keys >= cand).astype(F32), axis=1, keepdims=True)
        return jnp.where(cnt >= cap, cand, thr)

    thr = lax.fori_loop(0, 31, bit_body, jnp.zeros((N_EXPERTS, 1), I32))
    need = cap - jnp.sum((keys > thr).astype(F32), axis=1, keepdims=True)
    su = _strict_upper(TOK_BLK)

    def blk_body(j, carry):
        ceq, csel = carry
        st = pl.multiple_of(j * TOK_BLK, TOK_BLK)
        kb = pltpu.bitcast(aff_ref[:, pl.ds(st, TOK_BLK)], I32)
        eqf = (kb == thr).astype(F32)
        rank = ceq + _dot(eqf, su)
        sel = ((kb > thr) | ((kb == thr) & (rank < need))).astype(F32)
        mask_ref[:, pl.ds(st, TOK_BLK)] = sel
        wsel_ref[:, pl.ds(st, TOK_BLK)] = sel * aff_ref[:, pl.ds(st, TOK_BLK)]
        pos_ref[:, pl.ds(st, TOK_BLK)] = _dot(sel, su)
        offs_ref[j] = jnp.broadcast_to(csel.astype(I32), (N_EXPERTS, 128))
        return (ceq + jnp.sum(eqf, axis=1, keepdims=True), csel + jnp.sum(sel, axis=1, keepdims=True))

    zero = jnp.zeros((N_EXPERTS, 1), F32)
    lax.fori_loop(0, nblk, blk_body, (zero, zero))


def _route(aff_t, cap):
    E, T = aff_t.shape
    nblk = T // TOK_BLK
    full = lambda shape: pl.BlockSpec(shape, lambda i: (0,) * len(shape))
    return pl.pallas_call(
        functools.partial(_route_kernel, cap=cap, nblk=nblk),
        out_shape=[jax.ShapeDtypeStruct((E, T), F32)] * 3 + [jax.ShapeDtypeStruct((nblk, E, 128), I32)],
        grid=(1,),
        in_specs=[full((E, T))],
        out_specs=[full((E, T))] * 3 + [full((nblk, E, 128))],
        compiler_params=_cparams(("arbitrary",)),
        name="route",
    )(aff_t)


def _gather_kernel(offs_ref, x_ref, m_ref, p_ref, out_ref):
    g, j = pl.program_id(0), pl.program_id(1)

    @pl.when(j == 0)
    def _():
        out_ref[...] = jnp.zeros_like(out_ref)

    W = GATHER_W
    slot0 = lax.broadcasted_iota(I32, (W, TOK_BLK), 0).astype(F32)
    for jb in range(GATHER_JB):
        blk = j * GATHER_JB + jb
        cols = slice(jb * TOK_BLK, (jb + 1) * TOK_BLK)
        targets, bases, passes = [], [], []
        for k in range(GATHER_E):
            off = offs_ref[blk * N_EXPERTS + g * GATHER_E + k]
            cnt = offs_ref[(blk + 1) * N_EXPERTS + g * GATHER_E + k] - off
            base = (off // 16) * 16
            targets.append(jnp.where(m_ref[k, :, cols] > 0.0, p_ref[k, :, cols] + (off - base).astype(F32), -1.0))
            bases.append(base)
            passes.append((off - base + cnt + W - 1) // W)
        onehot = jnp.concatenate([jnp.where(slot0 == t, 1.0, 0.0) for t in targets], axis=0).astype(BF16)
        picked = jnp.dot(onehot, x_ref[cols, :], preferred_element_type=F32).astype(BF16)
        for k in range(GATHER_E):
            out_ref[k, pl.ds(pl.multiple_of(bases[k], 16), W), :] += picked[k * W:(k + 1) * W]
        for k in range(GATHER_E):
            def more(ps, carry, k=k, cols=cols):
                hot = jnp.where(slot0 + (ps * W).astype(F32) == targets[k], 1.0, 0.0).astype(BF16)
                extra = jnp.dot(hot, x_ref[cols, :], preferred_element_type=F32).astype(BF16)
                out_ref[k, pl.ds(pl.multiple_of(bases[k] + ps * W, 16), W), :] += extra
                return carry

            lax.fori_loop(1, passes[k], more, 0)


def _gather(h2, mask3, pos3, offs, cap):
    T = h2.shape[0]
    nblk = T // TOK_BLK
    cap_x = cap + 2 * GATHER_W
    rows = GATHER_JB * TOK_BLK
    grid_spec = pltpu.PrefetchScalarGridSpec(
        num_scalar_prefetch=1,
        grid=(N_EXPERTS // GATHER_E, nblk // GATHER_JB),
        in_specs=[pl.BlockSpec((rows, D_MODEL), lambda g, j, o: (j, 0)),
                  pl.BlockSpec((GATHER_E, 1, rows), lambda g, j, o: (g, 0, j)),
                  pl.BlockSpec((GATHER_E, 1, rows), lambda g, j, o: (g, 0, j))],
        out_specs=pl.BlockSpec((GATHER_E, cap_x, D_MODEL), lambda g, j, o: (g, 0, 0),
                               pipeline_mode=pl.Buffered(1)),
    )
    return pl.pallas_call(
        _gather_kernel,
        out_shape=jax.ShapeDtypeStruct((N_EXPERTS, cap_x, D_MODEL), BF16),
        grid_spec=grid_spec,
        compiler_params=_cparams(("parallel", "arbitrary")),
        name="gather",
    )(offs, h2, mask3, pos3)


def _ffn_kernel(x_ref, wg_ref, wu_ref, wd_ref, o_ref, w_scr):
    @pl.when(pl.program_id(1) == 0)
    def _():
        for n, w_ref in enumerate((wg_ref, wu_ref, wd_ref)):
            w_scr[n] = w_ref[0, 0].astype(BF16)

    x = x_ref[0]
    g = jnp.dot(x, w_scr[0], preferred_element_type=F32)
    u = jnp.dot(x, w_scr[1], preferred_element_type=F32)
    h = (g * _sigmoid(g)) * u
    o_ref[0] = _dot(h, w_scr[2]).astype(BF16)


def _ffn(xe, wg, wu, wd, layer, cap):
    tf = min(512, cap)
    wspec = pl.BlockSpec((1, 1, D_MODEL, D_MODEL), lambda e, i: (layer, e, 0, 0))
    return pl.pallas_call(
        _ffn_kernel,
        out_shape=jax.ShapeDtypeStruct((N_EXPERTS, cap, D_MODEL), BF16),
        grid=(N_EXPERTS, cap // tf),
        in_specs=[pl.BlockSpec((1, tf, D_MODEL), lambda e, i: (e, i, 0)), wspec, wspec, wspec],
        out_specs=pl.BlockSpec((1, tf, D_MODEL), lambda e, i: (e, i, 0)),
        scratch_shapes=[pltpu.VMEM((3, D_MODEL, D_MODEL), BF16)],
        compiler_params=_cparams(("parallel", "arbitrary")),
        name="ffn",
    )(xe, wg, wu, wd)


def _slot_window_start(off, cap):
    return jnp.minimum((off // TOK_BLK) * TOK_BLK, cap - 2 * TOK_BLK)


def _scatter_kernel(offs_ref, x2_ref, w_ref, p_ref, gf_ref, *rest, cap, final):
    ye, out_ref = rest[:N_EXPERTS], rest[N_EXPERTS]
    j = pl.program_id(0)
    ps, ws = p_ref[...].T, w_ref[...].T
    offs = [offs_ref[j * N_EXPERTS + e] for e in range(N_EXPERTS)]
    cnts = [offs_ref[(j + 1) * N_EXPERTS + e] - offs[e] for e in range(N_EXPERTS)]
    rels = [offs[e] - _slot_window_start(offs[e], cap) for e in range(N_EXPERTS)]
    weights = [ws[:, e:e + 1] for e in range(N_EXPERTS)]
    W, SUB = SCATTER_W, 16
    small = cnts[0] <= W - SUB
    for e in range(1, N_EXPERTS):
        small = jnp.logical_and(small, cnts[e] <= W - SUB)

    def onehot(e, start, width):
        slot = lax.broadcasted_iota(I32, (TOK_BLK, width), 1).astype(F32)
        return jnp.where(slot == ps[:, e:e + 1] + (rels[e] - start).astype(F32), weights[e], 0.0).astype(BF16)

    @pl.when(small)
    def _():
        acc = x2_ref[...]
        for g0 in range(0, N_EXPERTS, SCATTER_G):
            windows, pieces = [], []
            for e in range(g0, g0 + SCATTER_G):
                r0 = pl.multiple_of(jnp.minimum((rels[e] // SUB) * SUB, 2 * TOK_BLK - W), SUB)
                windows.append(ye[e][0, pl.ds(r0, W), :])
                pieces.append(onehot(e, r0, W))
            acc = acc + jnp.dot(jnp.concatenate(pieces, axis=1), jnp.concatenate(windows, axis=0),
                                preferred_element_type=F32)
        out_ref[...] = acc

    @pl.when(jnp.logical_not(small))
    def _():
        out_ref[...] = x2_ref[...]
        for e in range(N_EXPERTS):
            @pl.when(cnts[e] > 0)
            def _(e=e):
                out_ref[...] += jnp.dot(onehot(e, 0, 2 * TOK_BLK), ye[e][0], preferred_element_type=F32)

    if final:
        x = out_ref[...]
        out_ref[...] = x * lax.rsqrt(jnp.mean(x * x, axis=-1, keepdims=True) + EPS) * gf_ref[...]


def _scatter(x2, wsel, pos, ye, offs, gf, cap, final):
    T = x2.shape[0]
    nblk = T // TOK_BLK
    assert cap >= 2 * TOK_BLK and cap % TOK_BLK == 0

    def ye_spec(e):
        return pl.BlockSpec((pl.Element(1), pl.Element(2 * TOK_BLK), pl.Element(D_MODEL)),
                            lambda j, o: (e, pl.multiple_of(_slot_window_start(o[j * N_EXPERTS + e], cap), TOK_BLK),
                                          0))

    row = lambda w: pl.BlockSpec((TOK_BLK, w), lambda j, o: (j, 0))
    col = pl.BlockSpec((N_EXPERTS, TOK_BLK), lambda j, o: (0, j))
    grid_spec = pltpu.PrefetchScalarGridSpec(
        num_scalar_prefetch=1,
        grid=(nblk,),
        in_specs=[row(D_MODEL), col, col, pl.BlockSpec((1, D_MODEL), lambda j, o: (0, 0))]
                 + [ye_spec(e) for e in range(N_EXPERTS)],
        out_specs=row(D_MODEL),
    )
    return pl.pallas_call(
        functools.partial(_scatter_kernel, cap=cap, final=final),
        out_shape=jax.ShapeDtypeStruct((T, D_MODEL), F32),
        grid_spec=grid_spec,
        compiler_params=_cparams(("arbitrary",)),
        name="scatter",
    )(offs, x2, wsel, pos, gf, *([ye] * N_EXPERTS))


def _block_diag(w):
    n, a, b = w.shape
    out = jnp.zeros((n * a, n * b), w.dtype)
    for i in range(n):
        out = out.at[i * a:(i + 1) * a, i * b:(i + 1) * b].set(w[i])
    return out


def _rope_tables(S):
    inv = ROPE_THETA ** (-jnp.arange(0, ROT_DIM, 2, dtype=F32) / ROT_DIM)
    ang = jnp.arange(S, dtype=F32)[:, None] * inv[None, :]
    cos, sin = jnp.cos(ang), jnp.sin(ang)
    half = ROT_DIM // 2
    ones, zeros = jnp.ones((S, ATT_HD - ROT_DIM), F32), jnp.zeros((S, ATT_HD - ROT_DIM), F32)
    zh = jnp.zeros((S, half), F32)
    c = jnp.concatenate([cos, cos, ones], axis=1)
    s1 = jnp.concatenate([-sin, zh, zeros], axis=1)
    s2 = jnp.concatenate([zh, sin, zeros], axis=1)
    return tuple(jnp.tile(t, (1, 2)) for t in (c, s1, s2))


def _expanders():
    eg = np.zeros((2, 128, GROUP_W), np.float32)
    eb = np.zeros((2, 128, GROUP_W), np.float32)
    for d in range(2):
        for h in range(DN_HEADS):
            eg[d, d * DN_HEADS + h, h * DN_DK:(h + 1) * DN_DK] = 1.0
            eb[d, 2 * DN_HEADS + d * DN_HEADS + h, h * DN_DK:(h + 1) * DN_DK] = 1.0
    hd = np.arange(GROUP_W) // DN_DK
    bd = (hd[:, None] == hd[None, :]).astype(np.float32)
    return jnp.asarray(eg), jnp.asarray(eb), jnp.asarray(bd)


def _layer_params(l, norm1_g, w_in, conv_a_w, conv_a_b, rg_wa, rg_ba, rg_wx, rg_bx, rg_lambda, pool_w, pool_scale,
                  dn_conv_w, dn_A_log, dn_dt_bias, dn_norm_g, mix_norm_g, w_out, norm2_g, router_w,
                  exp_w_gate, exp_w_up, exp_w_down):
    w = w_in[l]
    w_cat = jnp.concatenate([w[:, 0:1536], w[:, 1552:2576], w[:, 1536:1552],
                             jnp.zeros((D_MODEL, COL_END - COL_CAB - 16), F32)], axis=1).astype(BF16)
    pad8 = lambda v: jnp.concatenate([v.reshape(1, 2 * DN_HEADS), jnp.zeros((1, 120), F32)], axis=1)
    return dict(
        g1=norm1_g[l].reshape(1, D_MODEL), w_cat=w_cat,
        conv_a_w=conv_a_w[l], conv_a_b=conv_a_b[l].reshape(1, GROUP_W),
        wa=[_block_diag(rg_wa[l, d]).astype(BF16) for d in range(2)],
        wx=[_block_diag(rg_wx[l, d]).astype(BF16) for d in range(2)],
        ba=[rg_ba[l, d].reshape(1, GROUP_W) for d in range(2)],
        bx=[rg_bx[l, d].reshape(1, GROUP_W) for d in range(2)],
        lam=[rg_lambda[l, d].reshape(1, GROUP_W) for d in range(2)],
        pool_w=_block_diag(pool_w[l]).astype(BF16), pool_scale=pool_scale[l].reshape(1, GROUP_W),
        dn_conv_w=dn_conv_w[l], alog=pad8(dn_A_log[l]), dtb=pad8(dn_dt_bias[l]),
        dng=jnp.tile(dn_norm_g[l], DN_HEADS).reshape(1, GROUP_W),
        mg=mix_norm_g[l].reshape(1, D_MODEL), wo=w_out[l].astype(BF16),
        g2=norm2_g[l].reshape(1, D_MODEL),
        rw=jnp.concatenate(_split(router_w[l], 2) + [jnp.zeros((D_MODEL, 128 - 2 * N_EXPERTS), BF16)], axis=1),
        wg=exp_w_gate, wu=exp_w_up, wd=exp_w_down,
    )


def _chunk_rows(gcn, d):
    B, S, _ = gcn.shape
    n = S // DN_CHUNK
    g = gcn[:, :, d * DN_HEADS:(d + 1) * DN_HEADS].reshape(B, n, DN_CHUNK, DN_HEADS)
    return g.transpose(0, 1, 3, 2).reshape(B, n, 1, GROUP_W)


def _encoder(x, layers, final_g, consts):
    B, S, D = x.shape
    T = B * S
    cap = max(1, EC_CAPACITY * T // N_EXPERTS)
    eg, eb, bdmask = consts
    rope_c, rope_s1, rope_s2 = _rope_tables(S)
    xt = x.reshape(T, D)
    for l, p in enumerate(layers):
        pa, pb, pq, pg, pd, pcab = _inproj(xt, p["g1"], p["w_cat"], rope_c, rope_s1, rope_s2, S)
        pa3 = pa.reshape(B, S, 2 * GROUP_W)
        rg = lambda d: (p["conv_a_w"], p["conv_a_b"], p["wa"][d], p["ba"][d], p["wx"][d], p["bx"][d], p["lam"][d])
        hf = _rglru(pa3, None, *rg(0), reverse=False)
        ya = _rglru(pa3, hf, *rg(1), reverse=True)
        yb = _pool(pb.reshape(B, S, GROUP_W), p["pool_w"], p["pool_scale"])
        qn, kn, vv, gcf, gcb, bef, beb, gcn = _dnprep(pq.reshape(B, S, 3 * GROUP_W), pcab.reshape(B, S, 128),
                                                      p["dn_conv_w"], p["alog"], p["dtb"], eg, eb)
        of, ob = _delta(qn, kn, vv, gcf, gcb, bef, beb, _chunk_rows(gcn, 0), _chunk_rows(gcn, 1), bdmask)
        yd = _attention(pd.reshape(B, S, 3 * GROUP_W))
        flat = lambda a: a.reshape(T, GROUP_W)
        x2, h2, aff_t = _outproj(flat(ya), flat(yb), flat(of), flat(ob), pg, flat(yd), xt,
                                 p["dng"], p["mg"], p["wo"], p["g2"], p["rw"])
        mask, pos, wsel, offs3 = _route(aff_t, cap)
        offs = jnp.concatenate([offs3[:, :, 0].reshape(-1), jnp.full((N_EXPERTS,), cap, I32)])
        xe = _gather(h2, mask.reshape(N_EXPERTS, 1, T), pos.reshape(N_EXPERTS, 1, T), offs, cap)
        ye = _ffn(xe, p["wg"], p["wu"], p["wd"], l, cap)
        xt = _scatter(x2, wsel, pos, ye, offs, final_g, cap, final=(l == len(layers) - 1))
    return xt.reshape(B, S, D)


def kernel(x_prompt, x_sample, norm1_g, w_in, conv_a_w, conv_a_b, rg_wa, rg_ba, rg_wx, rg_bx, rg_lambda, pool_w, pool_scale, dn_conv_w, dn_A_log, dn_dt_bias, dn_norm_g, mix_norm_g, w_out, norm2_g, router_w, exp_w_gate, exp_w_up, exp_w_down, final_norm_g):
    layers = [_layer_params(l, norm1_g, w_in, conv_a_w, conv_a_b, rg_wa, rg_ba, rg_wx, rg_bx, rg_lambda, pool_w,
                            pool_scale, dn_conv_w, dn_A_log, dn_dt_bias, dn_norm_g, mix_norm_g, w_out, norm2_g,
                            router_w, exp_w_gate, exp_w_up, exp_w_down) for l in range(DEPTH)]
    consts = _expanders()
    final_g = final_norm_g.reshape(1, D_MODEL)
    return (_encoder(x_prompt, layers, final_g, consts), _encoder(x_sample, layers, final_g, consts))
```

```python
import functools
import numpy as np
import jax
import jax.numpy as jnp
from jax import lax
from jax.experimental import pallas as pl
from jax.experimental.pallas import tpu as pltpu

F32, BF16, I32 = jnp.float32, jnp.bfloat16, jnp.int32

D_MODEL = 1024
DEPTH = 2
GROUP_W = 256
RG_C = 8.0
POOL_WINDOWS = (2, 4, 8, 16)
DN_HEADS = 4
DN_DK = 64
DN_CHUNK = 64
ATT_HD = 64
ROT_DIM = 16
ROPE_THETA = 500000.0
ATT_WINDOWS = (128, 512, 2048)
ATT_DILATIONS = (1, 4, 16)
N_EXPERTS = 16
EC_CAPACITY = 2
EPS = 1e-6
NEG = -1e30

COL_A, COL_B, COL_CQ, COL_CG, COL_D, COL_CAB, COL_END = 0, 512, 768, 1536, 1792, 2560, 2688

ROW_TILE = 512
HALO = 8
DELTA_CB = 8
ATT_QB = 128
ATT_UNROLL = 4
TOK_BLK = 256
GATHER_W = 64
GATHER_JB = 8
GATHER_E = 4
SCATTER_W = 128
VMEM_LIMIT = 56 * 1024 * 1024


def _cparams(sem):
    return pltpu.CompilerParams(dimension_semantics=sem, vmem_limit_bytes=VMEM_LIMIT)


def _dot(a, b):
    return jnp.dot(a.astype(BF16), b.astype(BF16), preferred_element_type=F32)


def _split(a, pieces):
    out, rem = [], a
    for i in range(pieces):
        t = rem.astype(BF16)
        out.append(t)
        if i + 1 < pieces:
            rem = rem - t.astype(F32)
    return out


def _dot_sel(a, sel, pieces=3):
    sel = sel.astype(BF16)
    return sum(jnp.dot(t, sel, preferred_element_type=F32) for t in _split(a, pieces))


def _sel_dot(sel, a, pieces=3):
    sel = sel.astype(BF16)
    return sum(jnp.dot(sel, t, preferred_element_type=F32) for t in _split(a, pieces))


def _dot_nt(a, b):
    return lax.dot_general(a.astype(BF16), b.astype(BF16), (((1,), (1,)), ((), ())), preferred_element_type=F32)


def _dot_tn(a, b):
    return lax.dot_general(a.astype(BF16), b.astype(BF16), (((0,), (0,)), ((), ())), preferred_element_type=F32)


def _sigmoid(x):
    return 1.0 / (1.0 + jnp.exp(-x))


def _softplus(x):
    return jnp.maximum(x, 0.0) + jnp.log1p(jnp.exp(-jnp.abs(x)))


def _shift_rows(e, k):
    n = e.shape[0]
    return e if k % n == 0 else pltpu.roll(e, (-k) % n, axis=0)


def _with_halo(cur_ref, prev_ref, next_ref, first, last):
    prev = jnp.where(first, 0.0, prev_ref[0])
    nxt = jnp.where(last, 0.0, next_ref[0])
    return jnp.concatenate([prev, cur_ref[0], nxt], axis=0)


def _halo_specs(ts, width, S, blk_of):
    per = ts // HALO
    last = S // HALO - 1
    cur = pl.BlockSpec((1, ts, width), lambda b, c: (b, blk_of(c), 0))
    prev = pl.BlockSpec((1, HALO, width), lambda b, c: (b, jnp.maximum(blk_of(c) * per - 1, 0), 0))
    nxt = pl.BlockSpec((1, HALO, width), lambda b, c: (b, jnp.minimum((blk_of(c) + 1) * per, last), 0))
    return cur, prev, nxt


def _inproj_kernel(x_ref, g_ref, w_ref, c_ref, s1_ref, s2_ref, pa_ref, pb_ref, pq_ref, pg_ref, pd_ref, pcab_ref):
    x = x_ref[...]
    h = x * lax.rsqrt(jnp.mean(x * x, axis=-1, keepdims=True) + EPS) * g_ref[...]
    hb = h.astype(BF16)

    def mm(lo, hi):
        return jnp.dot(hb, w_ref[:, lo:hi], preferred_element_type=F32)

    pa_ref[...] = mm(COL_A, COL_B)
    pb_ref[...] = mm(COL_B, COL_CQ)
    pq_ref[...] = mm(COL_CQ, COL_CG)
    pg_ref[...] = mm(COL_CG, COL_D)
    c, s1, s2 = c_ref[...], s1_ref[...], s2_ref[...]
    for part in range(2):
        yy = mm(COL_D + 256 * part, COL_D + 256 * part + 256)
        for half in range(2):
            y = yy[:, 128 * half:128 * half + 128]
            y = y * c + pltpu.roll(y, 128 - ROT_DIM // 2, axis=1) * s1 + pltpu.roll(y, ROT_DIM // 2, axis=1) * s2
            if part == 0:
                y = y * (ATT_HD ** -0.5)
            pd_ref[:, 256 * part + 128 * half:256 * part + 128 * half + 128] = y
    vcab = mm(COL_D + 512, COL_END)
    pd_ref[:, 512:768] = vcab[:, 0:256]
    pcab_ref[...] = vcab[:, 256:384]


def _inproj(xt, g1, w_cat, rope_c, rope_s1, rope_s2, S):
    T = xt.shape[0]
    tm = min(ROW_TILE, S)
    per_seq = S // tm
    widths = (512, 256, 768, 256, 768, 128)
    row = lambda w: pl.BlockSpec((tm, w), lambda i: (i, 0))
    rope = pl.BlockSpec((tm, 128), lambda i: (i % per_seq, 0))
    return pl.pallas_call(
        _inproj_kernel,
        out_shape=[jax.ShapeDtypeStruct((T, w), F32) for w in widths],
        grid=(T // tm,),
        in_specs=[row(D_MODEL), pl.BlockSpec((1, D_MODEL), lambda i: (0, 0)),
                  pl.BlockSpec((D_MODEL, COL_END), lambda i: (0, 0)), rope, rope, rope],
        out_specs=[row(w) for w in widths],
        compiler_params=_cparams(("parallel",)),
        name="inproj",
    )(xt, g1, w_cat, rope_c, rope_s1, rope_s2)


def _rglru_kernel(*refs, reverse, ts, nc):
    if reverse:
        (cur_ref, prev_ref, next_ref, gate_ref, hf_ref, cw_ref, cb_ref, wa_ref, ba_ref, wx_ref, bx_ref, lam_ref,
         out_ref, a_ref, b_ref, carry_ref) = refs
    else:
        (cur_ref, prev_ref, next_ref, cw_ref, cb_ref, wa_ref, ba_ref, wx_ref, bx_ref, lam_ref,
         out_ref, a_ref, b_ref, carry_ref) = refs
    c = pl.program_id(1)
    blk = (nc - 1 - c) if reverse else c
    e = _with_halo(cur_ref, prev_ref, next_ref, blk == 0, blk == nc - 1)
    cw = cw_ref[...]
    sl = slice(HALO, HALO + ts)
    u = (cw[0:1] * _shift_rows(e, -2)[sl] + cw[1:2] * _shift_rows(e, -1)[sl] + cw[2:3] * e[sl]
         + cw[3:4] * _shift_rows(e, 1)[sl]) + cb_ref[...]
    r = _sigmoid(_dot(u, wa_ref[...]) + ba_ref[...])
    i = _sigmoid(_dot(u, wx_ref[...]) + bx_ref[...])
    log_a = -RG_C * r * _softplus(-lam_ref[...])
    a_ref[...] = jnp.exp(log_a)
    b_ref[...] = jnp.sqrt(1.0 - jnp.exp(2.0 * log_a)) * (i * u)

    @pl.when(c == 0)
    def _():
        carry_ref[...] = jnp.zeros_like(carry_ref)

    row = lax.broadcasted_iota(I32, (HALO, GROUP_W), 0)
    nt = ts // HALO

    def body(it, carry):
        ti = (nt - 1 - it) if reverse else it
        st = pl.multiple_of(ti * HALO, HALO)
        a = a_ref[pl.ds(st, HALO), :]
        b = b_ref[pl.ds(st, HALO), :]
        for d in (1, 2, 4):
            k = d if reverse else -d
            valid = (row < HALO - d) if reverse else (row >= d)
            b = jnp.where(valid, a * _shift_rows(b, k) + b, b)
            a = jnp.where(valid, a * _shift_rows(a, k), a)
        h = a * carry + b
        if reverse:
            g = gate_ref[0, pl.ds(st, HALO), :]
            cdf = 0.5 * (1.0 + jnp.tanh(np.float32(np.sqrt(2.0 / np.pi)) * (g + 0.044715 * (g * g * g))))
            out_ref[0, pl.ds(st, HALO), :] = (g * cdf) * (hf_ref[0, pl.ds(st, HALO), :] + h)
            return jnp.broadcast_to(h[0:1], h.shape)
        out_ref[0, pl.ds(st, HALO), :] = h
        return jnp.broadcast_to(h[HALO - 1:HALO], h.shape)

    carry_ref[...] = lax.fori_loop(0, nt, body, carry_ref[...], unroll=4)


def _rglru(pa3, hf, cw, cb, wa, ba, wx, bx, lam, reverse):
    B, S, _ = pa3.shape
    ts = min(ROW_TILE, S)
    nc = S // ts
    blk_of = (lambda c: nc - 1 - c) if reverse else (lambda c: c)
    cur, prev, nxt = _halo_specs(ts, GROUP_W, S, blk_of)
    tile = pl.BlockSpec((1, ts, GROUP_W), lambda b, c: (b, blk_of(c), 0))
    const = lambda shape: pl.BlockSpec(shape, lambda b, c: (0,) * len(shape))
    in_specs = [cur, prev, nxt]
    args = [pa3, pa3, pa3]
    if reverse:
        in_specs += [pl.BlockSpec((1, ts, GROUP_W), lambda b, c: (b, blk_of(c), 1)), tile]
        args += [pa3, hf]
    in_specs += [const((4, GROUP_W)), const((1, GROUP_W)), const((GROUP_W, GROUP_W)), const((1, GROUP_W)),
                 const((GROUP_W, GROUP_W)), const((1, GROUP_W)), const((1, GROUP_W))]
    args += [cw, cb, wa, ba, wx, bx, lam]
    return pl.pallas_call(
        functools.partial(_rglru_kernel, reverse=reverse, ts=ts, nc=nc),
        out_shape=jax.ShapeDtypeStruct((B, S, GROUP_W), F32),
        grid=(B, nc),
        in_specs=in_specs,
        out_specs=tile,
        scratch_shapes=[pltpu.VMEM((ts, GROUP_W), F32), pltpu.VMEM((ts, GROUP_W), F32),
                        pltpu.VMEM((HALO, GROUP_W), F32)],
        compiler_params=_cparams(("parallel", "arbitrary")),
        name="rglru_bwd" if reverse else "rglru_fwd",
    )(*args)


def _pool_kernel(cur_ref, prev_ref, next_ref, w_ref, sc_ref, out_ref, *, ts, nc, S):
    c = pl.program_id(1)
    e = _with_halo(cur_ref, prev_ref, next_ref, c == 0, c == nc - 1)
    sl = slice(HALO, HALO + ts)
    a2 = e + _shift_rows(e, 1)
    a4 = a2 + _shift_rows(a2, 2)
    a8 = a4 + _shift_rows(a4, 4)
    a16 = a8 + _shift_rows(a8, 8)
    sums = [_shift_rows(a, -(w // 2))[sl] for a, w in zip((a2, a4, a8, a16), POOL_WINDOWS)]
    gi = lax.broadcasted_iota(I32, (ts, GROUP_W), 1) // (GROUP_W // len(POOL_WINDOWS))
    ssum = jnp.where(gi == 0, sums[0], jnp.where(gi == 1, sums[1], jnp.where(gi == 2, sums[2], sums[3])))
    hw = jnp.where(gi == 0, 1, jnp.where(gi == 1, 2, jnp.where(gi == 2, 4, 8)))
    t = c * ts + lax.broadcasted_iota(I32, (ts, GROUP_W), 0)
    cnt = (jnp.minimum(t + hw, S) - jnp.maximum(t - hw, 0)).astype(F32)
    p = ssum / cnt - e[sl]
    out_ref[0] = _dot(p, w_ref[...]) * sc_ref[...]


def _pool(pb3, w_bd, scale):
    B, S, _ = pb3.shape
    ts = min(ROW_TILE, S)
    nc = S // ts
    cur, prev, nxt = _halo_specs(ts, GROUP_W, S, lambda c: c)
    return pl.pallas_call(
        functools.partial(_pool_kernel, ts=ts, nc=nc, S=S),
        out_shape=jax.ShapeDtypeStruct((B, S, GROUP_W), F32),
        grid=(B, nc),
        in_specs=[cur, prev, nxt, pl.BlockSpec((GROUP_W, GROUP_W), lambda b, c: (0, 0)),
                  pl.BlockSpec((1, GROUP_W), lambda b, c: (0, 0))],
        out_specs=pl.BlockSpec((1, ts, GROUP_W), lambda b, c: (b, c, 0)),
        compiler_params=_cparams(("parallel", "parallel")),
        name="pool",
    )(pb3, pb3, pb3, w_bd, scale)


def _head_sum_matrix(n, group):
    r = lax.broadcasted_iota(I32, (n, n), 0) // group
    c = lax.broadcasted_iota(I32, (n, n), 1) // group
    return (r == c).astype(F32)


def _dnprep_kernel(cur_ref, prev_ref, next_ref, cab_ref, cw_ref, alog_ref, dtb_ref, eg_ref, eb_ref,
                   q_ref, k_ref, v_ref, gcf_ref, gcb_ref, bef_ref, beb_ref, gcn_ref, *, ts, nc):
    c = pl.program_id(1)
    e = _with_halo(cur_ref, prev_ref, next_ref, c == 0, c == nc - 1)
    cw = cw_ref[...]
    sl = slice(HALO, HALO + ts)
    y = (cw[0:1] * _shift_rows(e, -2)[sl] + cw[1:2] * _shift_rows(e, -1)[sl] + cw[2:3] * e[sl]
         + cw[3:4] * _shift_rows(e, 1)[sl])
    y = y * _sigmoid(y)
    q, k = y[:, 0:GROUP_W], y[:, GROUP_W:2 * GROUP_W]
    hs = _head_sum_matrix(GROUP_W, DN_DK)
    q_ref[0] = q * lax.rsqrt(_dot_sel(q * q, hs, 2) + EPS) * (DN_DK ** -0.5)
    k_ref[0] = k * lax.rsqrt(_dot_sel(k * k, hs, 2) + EPS)
    v_ref[0] = y[:, 2 * GROUP_W:3 * GROUP_W]
    cab = cab_ref[0]
    g = -jnp.exp(alog_ref[...]) * _softplus(cab + dtb_ref[...])
    beta = _sigmoid(cab)
    bef_ref[0] = _dot_sel(beta, eb_ref[0])
    beb_ref[0] = _dot_sel(beta, eb_ref[1])
    r = lax.broadcasted_iota(I32, (ts, ts), 0)
    cc = lax.broadcasted_iota(I32, (ts, ts), 1)
    same = (r // DN_CHUNK) == (cc // DN_CHUNK)
    lane = lax.broadcasted_iota(I32, (ts, 128), 1)
    gcn = jnp.where(lane < DN_HEADS, _sel_dot(same & (r >= cc), g), _sel_dot(same & (r <= cc), g))
    gcn_ref[0] = gcn
    gcf_ref[0] = _dot_sel(gcn, eg_ref[0])
    gcb_ref[0] = _dot_sel(gcn, eg_ref[1])


def _dnprep(pq3, pcab3, cw, alog_row, dtb_row, eg, eb):
    B, S, _ = pq3.shape
    ts = min(ROW_TILE, S)
    nc = S // ts
    cur, prev, nxt = _halo_specs(ts, 3 * GROUP_W, S, lambda c: c)
    tile = pl.BlockSpec((1, ts, GROUP_W), lambda b, c: (b, c, 0))
    const = lambda shape: pl.BlockSpec(shape, lambda b, c: (0,) * len(shape))
    return pl.pallas_call(
        functools.partial(_dnprep_kernel, ts=ts, nc=nc),
        out_shape=[jax.ShapeDtypeStruct((B, S, GROUP_W), F32)] * 7 + [jax.ShapeDtypeStruct((B, S, 128), F32)],
        grid=(B, nc),
        in_specs=[cur, prev, nxt, pl.BlockSpec((1, ts, 128), lambda b, c: (b, c, 0)),
                  const((4, 3 * GROUP_W)), const((1, 128)), const((1, 128)),
                  const((2, 128, GROUP_W)), const((2, 128, GROUP_W))],
        out_specs=[tile] * 7 + [pl.BlockSpec((1, ts, 128), lambda b, c: (b, c, 0))],
        compiler_params=_cparams(("parallel", "parallel")),
        name="dnprep",
    )(pq3, pq3, pq3, pcab3, cw, alog_row, dtb_row, eg, eb)


def _delta_kernel(qf, kf, vf, gcf, bef, grf, qb, kb, vb, gcb, beb, grb, bd_ref, of_ref, ob_ref,
                  s_ref, p_scr, x_scr, wq_scr, u_scr, at_scr, kd_scr, vb_scr, kg_scr, *, cb):
    j = pl.program_id(1)

    @pl.when(j == 0)
    def _():
        s_ref[...] = jnp.zeros_like(s_ref)

    C = DN_CHUNK
    bd = bd_ref[...] > 0.0
    c_idx = lax.broadcasted_iota(I32, (C, GROUP_W), 0)
    m_idx = lax.broadcasted_iota(I32, (C, GROUP_W), 1) % C
    eye = (c_idx == m_idx).astype(F32)

    def blockdiag(x):
        return jnp.where(bd, jnp.tile(x.astype(BF16), (DN_HEADS, 1)), jnp.zeros((), BF16))

    dirs = ((qf, kf, vf, gcf, bef, grf, of_ref), (qb, kb, vb, gcb, beb, grb, ob_ref))
    units = [(d, ci) for ci in range(cb) for d in range(2)]

    def chunk_of(d, ci):
        return cb - 1 - ci if d == 1 else ci

    def prepare(ui):
        d, ci = units[ui]
        q_r, k_r, v_r, gc_r, be_r, gr_r, _ = dirs[d]
        rev = d == 1
        tril = (c_idx <= m_idx) if rev else (c_idx >= m_idx)
        strict = (c_idx < m_idx) if rev else (c_idx > m_idx)
        last = 0 if rev else C - 1
        cc = chunk_of(d, ci)
        rows = slice(cc * C, (cc + 1) * C)
        q, k, v = q_r[0, rows, :], k_r[0, rows, :], v_r[0, rows, :]
        gc, be, gr = gc_r[0, rows, :], be_r[0, rows, :], gr_r[0, cc]
        eg = jnp.exp(gc)
        kbeta = k * be
        kkqk = _dot_nt(jnp.concatenate([kbeta, q], axis=0), blockdiag(k))
        decay = jnp.exp(jnp.where(tril, gc - gr, -jnp.inf))
        at_scr[ui] = kkqk[C:] * decay
        p = -jnp.where(strict, kkqk[:C] * decay, 0.0)
        p_scr[ui] = p
        x_scr[ui] = eye + p
        wq_scr[ui, C:2 * C, :] = q * eg
        kd_scr[ui] = k * jnp.exp(gc[last:last + 1] - gc)
        vb_scr[ui] = v * be
        kg_scr[ui] = kbeta * eg

    def solve(group):
        for ui in group:
            p = p_scr[ui]
            p_scr[ui] = _dot(p, blockdiag(p))
        for _ in range(4):
            for ui in group:
                p, x = p_scr[ui], x_scr[ui]
                px = _dot(jnp.concatenate([p, x], axis=0), blockdiag(p))
                p_scr[ui] = px[:C]
                x_scr[ui] = x + px[C:]
        for ui in group:
            x = x_scr[ui]
            x = x + _dot(x, blockdiag(p_scr[ui]))
            u_scr[ui] = _dot(x, blockdiag(vb_scr[ui]))
            wq_scr[ui, 0:C, :] = _dot(x, blockdiag(kg_scr[ui]))

    def recur(ui):
        d, ci = units[ui]
        gc_r, o_r = dirs[d][3], dirs[d][6]
        cc = chunk_of(d, ci)
        last = cc * C + (0 if d == 1 else C - 1)
        state = s_ref[d]
        ws_qs = _dot(wq_scr[ui], state)
        v_new = u_scr[ui] - ws_qs[:C]
        o_r[0, cc * C:(cc + 1) * C, :] = ws_qs[C:] + _dot(at_scr[ui], blockdiag(v_new))
        s_ref[d] = (state * jnp.exp(gc_r[0, last:last + 1, :])
                    + jnp.where(bd, _dot_tn(kd_scr[ui], v_new), 0.0))

    for ui in range(len(units)):
        prepare(ui)
    solve(range(len(units)))
    for ui in range(len(units)):
        recur(ui)


def _delta(qn, kn, vv, gcf, gcb, bef, beb, grf, grb, bdmask):
    B, S, _ = qn.shape
    cb = DELTA_CB
    rb = cb * DN_CHUNK
    nb = S // rb
    f = lambda b, j: (b, j, 0)
    r = lambda b, j: (b, nb - 1 - j, 0)
    tf = pl.BlockSpec((1, rb, GROUP_W), f)
    tr = pl.BlockSpec((1, rb, GROUP_W), r)
    gf = pl.BlockSpec((1, cb, 1, GROUP_W), lambda b, j: (b, j, 0, 0))
    gr = pl.BlockSpec((1, cb, 1, GROUP_W), lambda b, j: (b, nb - 1 - j, 0, 0))
    return pl.pallas_call(
        functools.partial(_delta_kernel, cb=cb),
        out_shape=[jax.ShapeDtypeStruct((B, S, GROUP_W), F32)] * 2,
        grid=(B, nb),
        in_specs=[tf, tf, tf, tf, tf, gf, tr, tr, tr, tr, tr, gr,
                  pl.BlockSpec((GROUP_W, GROUP_W), lambda b, j: (0, 0))],
        out_specs=[tf, tr],
        scratch_shapes=[pltpu.VMEM((2, GROUP_W, GROUP_W), F32)]
                       + [pltpu.VMEM((2 * cb, n * DN_CHUNK, GROUP_W), F32) for n in (1, 1, 2, 1, 1, 1, 1, 1)],
        compiler_params=_cparams(("parallel", "arbitrary")),
        name="delta",
    )(qn, kn, vv, gcf, bef, grf, qn, kn, vv, gcb, beb, grb, bdmask)


def _attn_kernel(q_ref, k_ref, v_ref, out_ref, m_ref, l_ref, o_ref, *, S):
    head0 = lax.broadcasted_iota(I32, (1, 128), 1) < ATT_HD

    def rows(start, n, dil):
        return pl.ds(start, n) if dil == 1 else pl.ds(start, n, stride=dil)

    for bi, (win, dil) in enumerate(zip(ATT_WINDOWS, ATT_DILATIONS)):
        half = win // (2 * dil)
        L = S // dil
        qb_n = min(ATT_QB, L)
        kw = min(L, qb_n + 2 * half)
        nqb = L // qb_n

        def block_stats(idx, dil=dil, half=half, L=L, qb_n=qb_n, kw=kw, nqb=nqb):
            r = idx // nqb
            m0 = (idx % nqb) * qb_n
            ks = jnp.clip(m0 - half, 0, L - kw)
            qsel = rows(r + m0 * dil, qb_n, dil)
            ksel = rows(r + ks * dil, kw, dil)
            q = q_ref[0, qsel, :]
            kk = k_ref[0, ksel, :].astype(BF16)
            vv = v_ref[0, ksel, :].astype(BF16)
            rel = (lax.broadcasted_iota(I32, (qb_n, kw), 1) - lax.broadcasted_iota(I32, (qb_n, kw), 0)
                   + (ks - m0 + half))
            valid = rel.astype(jnp.uint32) <= 2 * half
            q2 = jnp.concatenate([jnp.where(head0, q, 0.0), jnp.where(head0, 0.0, q)], axis=0)
            s = jnp.where(jnp.concatenate([valid, valid], axis=0), _dot_nt(q2, kk), NEG)
            m = jnp.max(s, axis=-1, keepdims=True)
            p = jnp.exp(s - m)
            l = jnp.sum(p, axis=-1, keepdims=True)
            o = _dot(p, vv)
            return (qsel, jnp.where(head0, m[:qb_n], m[qb_n:]), jnp.where(head0, l[:qb_n], l[qb_n:]),
                    jnp.where(head0, o[:qb_n], o[qb_n:]))

        def body(it, carry, bi=bi):
            stats = [block_stats(it * ATT_UNROLL + s) for s in range(ATT_UNROLL)]
            if bi > 0:
                olds = [(m_ref[qsel, :], l_ref[qsel, :], o_ref[qsel, :]) for qsel, _, _, _ in stats]
                merged = []
                for (qsel, m, l, o), (m_old, l_old, o_old) in zip(stats, olds):
                    m_new = jnp.maximum(m_old, m)
                    w_old, w_cur = jnp.exp(m_old - m_new), jnp.exp(m - m_new)
                    merged.append((qsel, m_new, w_old * l_old + w_cur * l, w_old * o_old + w_cur * o))
                stats = merged
            for qsel, m, l, o in stats:
                if bi == len(ATT_WINDOWS) - 1:
                    out_ref[0, qsel, :] = o / l
                else:
                    m_ref[qsel, :] = m
                    l_ref[qsel, :] = l
                    o_ref[qsel, :] = o
            return carry

        lax.fori_loop(0, dil * nqb // ATT_UNROLL, body, 0)


def _attention(pd3):
    B, S, _ = pd3.shape
    spec = lambda off: pl.BlockSpec((1, S, 128), lambda b, p: (b, 0, off + p))
    return pl.pallas_call(
        functools.partial(_attn_kernel, S=S),
        out_shape=jax.ShapeDtypeStruct((B, S, GROUP_W), F32),
        grid=(B, 2),
        in_specs=[spec(0), spec(2), spec(4)],
        out_specs=pl.BlockSpec((1, S, 128), lambda b, p: (b, 0, p)),
        scratch_shapes=[pltpu.VMEM((S, 128), F32)] * 3,
        compiler_params=_cparams(("parallel", "parallel")),
        name="attention",
    )(pd3, pd3, pd3)


def _outproj_kernel(ya_ref, yb_ref, of_ref, ob_ref, cg_ref, yd_ref, x_ref, dng_ref, mg_ref, wo_ref, g2_ref, rw_ref,
                    x2_ref, h2_ref, aff_ref):
    o = of_ref[...] + ob_ref[...]
    ms = _dot_sel(o * o, _head_sum_matrix(GROUP_W, DN_DK), 2) * (1.0 / DN_DK)
    cg = cg_ref[...]
    yc = (o * lax.rsqrt(ms + EPS) * dng_ref[...]) * (cg * _sigmoid(cg))
    acc = x_ref[...]
    for gi, y in enumerate((ya_ref[...], yb_ref[...], yc, yd_ref[...])):
        sl = slice(gi * GROUP_W, (gi + 1) * GROUP_W)
        mix = y * lax.rsqrt(jnp.mean(y * y, axis=-1, keepdims=True) + EPS) * mg_ref[:, sl]
        acc = acc + _dot(mix, wo_ref[sl, :])
    x2_ref[...] = acc
    h2 = acc * lax.rsqrt(jnp.mean(acc * acc, axis=-1, keepdims=True) + EPS) * g2_ref[...]
    h2_ref[...] = h2.astype(BF16)
    h_hi, h_lo = _split(h2, 2)
    z = jnp.dot(h_hi, rw_ref[...], preferred_element_type=F32) + jnp.dot(h_lo, rw_ref[...], preferred_element_type=F32)
    logits = z + pltpu.roll(z, 128 - N_EXPERTS, axis=1)
    logits = jnp.where(lax.broadcasted_iota(I32, logits.shape, 1) < N_EXPERTS, logits, -jnp.inf)
    ex = jnp.exp(logits - jnp.max(logits, axis=-1, keepdims=True))
    aff = ex / jnp.sum(ex, axis=-1, keepdims=True)
    aff_ref[...] = aff.T[0:N_EXPERTS, :]


def _outproj(ya, yb, of, ob, cg, yd, xt, dng, mg, wo, g2, rw):
    T = xt.shape[0]
    tm = 256
    row = lambda w: pl.BlockSpec((tm, w), lambda i: (i, 0))
    const = lambda shape: pl.BlockSpec(shape, lambda i: (0, 0))
    return pl.pallas_call(
        _outproj_kernel,
        out_shape=[jax.ShapeDtypeStruct((T, D_MODEL), F32), jax.ShapeDtypeStruct((T, D_MODEL), BF16),
                   jax.ShapeDtypeStruct((N_EXPERTS, T), F32)],
        grid=(T // tm,),
        in_specs=[row(GROUP_W)] * 6 + [row(D_MODEL), const((1, GROUP_W)), const((1, D_MODEL)),
                                       const((D_MODEL, D_MODEL)), const((1, D_MODEL)), const((D_MODEL, 128))],
        out_specs=[row(D_MODEL), row(D_MODEL), pl.BlockSpec((N_EXPERTS, tm), lambda i: (0, i))],
        compiler_params=_cparams(("parallel",)),
        name="outproj",
    )(ya, yb, of, ob, cg, yd, xt, dng, mg, wo, g2, rw)


def _strict_upper(n):
    return (lax.broadcasted_iota(I32, (n, n), 0) < lax.broadcasted_iota(I32, (n, n), 1)).astype(BF16)


def _route_kernel(aff_ref, mask_ref, pos_ref, wsel_ref, offs_ref, *, cap, nblk):
    keys = pltpu.bitcast(aff_ref[...], I32)

    def bit_body(i, thr):
        cand = thr | lax.shift_left(jnp.int32(1), 30 - i)
        cnt = jnp.sum((keys >= cand).astype(F32), axis=1, keepdims=True)
        return jnp.where(cnt >= cap, cand, thr)

    thr = lax.fori_loop(0, 31, bit_body, jnp.zeros((N_EXPERTS, 1), I32))
    need = cap - jnp.sum((keys > thr).astype(F32), axis=1, keepdims=True)
    su = _strict_upper(TOK_BLK)

    def blk_body(j, carry):
        ceq, csel = carry
        st = pl.multiple_of(j * TOK_BLK, TOK_BLK)
        kb = pltpu.bitcast(aff_ref[:, pl.ds(st, TOK_BLK)], I32)
        eqf = (kb == thr).astype(F32)
        rank = ceq + _dot(eqf, su)
        sel = ((kb > thr) | ((kb == thr) & (rank < need))).astype(F32)
        mask_ref[:, pl.ds(st, TOK_BLK)] = sel
        wsel_ref[:, pl.ds(st, TOK_BLK)] = sel * aff_ref[:, pl.ds(st, TOK_BLK)]
        pos_ref[:, pl.ds(st, TOK_BLK)] = _dot(sel, su)
        offs_ref[j] = jnp.broadcast_to(csel.astype(I32), (N_EXPERTS, 128))
        return (ceq + jnp.sum(eqf, axis=1, keepdims=True), csel + jnp.sum(sel, axis=1, keepdims=True))

    zero = jnp.zeros((N_EXPERTS, 1), F32)
    lax.fori_loop(0, nblk, blk_body, (zero, zero))


def _route(aff_t, cap):
    E, T = aff_t.shape
    nblk = T // TOK_BLK
    full = lambda shape: pl.BlockSpec(shape, lambda i: (0,) * len(shape))
    return pl.pallas_call(
        functools.partial(_route_kernel, cap=cap, nblk=nblk),
        out_shape=[jax.ShapeDtypeStruct((E, T), F32)] * 3 + [jax.ShapeDtypeStruct((nblk, E, 128), I32)],
        grid=(1,),
        in_specs=[full((E, T))],
        out_specs=[full((E, T))] * 3 + [full((nblk, E, 128))],
        compiler_params=_cparams(("arbitrary",)),
        name="route",
    )(aff_t)


def _gather_kernel(offs_ref, x_ref, m_ref, p_ref, out_ref):
    g, j = pl.program_id(0), pl.program_id(1)

    @pl.when(j == 0)
    def _():
        out_ref[...] = jnp.zeros_like(out_ref)

    W = GATHER_W
    slot0 = lax.broadcasted_iota(I32, (W, TOK_BLK), 0).astype(F32)
    for jb in range(GATHER_JB):
        blk = j * GATHER_JB + jb
        cols = slice(jb * TOK_BLK, (jb + 1) * TOK_BLK)
        targets, bases, passes = [], [], []
        for k in range(GATHER_E):
            off = offs_ref[blk * N_EXPERTS + g * GATHER_E + k]
            cnt = offs_ref[(blk + 1) * N_EXPERTS + g * GATHER_E + k] - off
            base = (off // 16) * 16
            targets.append(jnp.where(m_ref[k, :, cols] > 0.0, p_ref[k, :, cols] + (off - base).astype(F32), -1.0))
            bases.append(base)
            passes.append((off - base + cnt + W - 1) // W)
        onehot = jnp.concatenate([jnp.where(slot0 == t, 1.0, 0.0) for t in targets], axis=0).astype(BF16)
        picked = jnp.dot(onehot, x_ref[cols, :], preferred_element_type=F32).astype(BF16)
        for k in range(GATHER_E):
            out_ref[k, pl.ds(pl.multiple_of(bases[k], 16), W), :] += picked[k * W:(k + 1) * W]
        for k in range(GATHER_E):
            def more(ps, carry, k=k, cols=cols):
                hot = jnp.where(slot0 + (ps * W).astype(F32) == targets[k], 1.0, 0.0).astype(BF16)
                extra = jnp.dot(hot, x_ref[cols, :], preferred_element_type=F32).astype(BF16)
                out_ref[k, pl.ds(pl.multiple_of(bases[k] + ps * W, 16), W), :] += extra
                return carry

            lax.fori_loop(1, passes[k], more, 0)


def _gather(h2, mask3, pos3, offs, cap):
    T = h2.shape[0]
    nblk = T // TOK_BLK
    cap_x = cap + 2 * GATHER_W
    rows = GATHER_JB * TOK_BLK
    grid_spec = pltpu.PrefetchScalarGridSpec(
        num_scalar_prefetch=1,
        grid=(N_EXPERTS // GATHER_E, nblk // GATHER_JB),
        in_specs=[pl.BlockSpec((rows, D_MODEL), lambda g, j, o: (j, 0)),
                  pl.BlockSpec((GATHER_E, 1, rows), lambda g, j, o: (g, 0, j)),
                  pl.BlockSpec((GATHER_E, 1, rows), lambda g, j, o: (g, 0, j))],
        out_specs=pl.BlockSpec((GATHER_E, cap_x, D_MODEL), lambda g, j, o: (g, 0, 0),
                               pipeline_mode=pl.Buffered(1)),
    )
    return pl.pallas_call(
        _gather_kernel,
        out_shape=jax.ShapeDtypeStruct((N_EXPERTS, cap_x, D_MODEL), BF16),
        grid_spec=grid_spec,
        compiler_params=_cparams(("parallel", "arbitrary")),
        name="gather",
    )(offs, h2, mask3, pos3)


def _ffn_kernel(x_ref, wg_ref, wu_ref, wd_ref, o_ref, w_scr):
    @pl.when(pl.program_id(1) == 0)
    def _():
        for n, w_ref in enumerate((wg_ref, wu_ref, wd_ref)):
            w_scr[n] = w_ref[0, 0].astype(BF16)

    x = x_ref[0]
    g = jnp.dot(x, w_scr[0], preferred_element_type=F32)
    u = jnp.dot(x, w_scr[1], preferred_element_type=F32)
    h = (g * _sigmoid(g)) * u
    o_ref[0] = _dot(h, w_scr[2]).astype(BF16)


def _ffn(xe, wg, wu, wd, layer, cap):
    tf = min(512, cap)
    wspec = pl.BlockSpec((1, 1, D_MODEL, D_MODEL), lambda e, i: (layer, e, 0, 0))
    return pl.pallas_call(
        _ffn_kernel,
        out_shape=jax.ShapeDtypeStruct((N_EXPERTS, cap, D_MODEL), BF16),
        grid=(N_EXPERTS, cap // tf),
        in_specs=[pl.BlockSpec((1, tf, D_MODEL), lambda e, i: (e, i, 0)), wspec, wspec, wspec],
        out_specs=pl.BlockSpec((1, tf, D_MODEL), lambda e, i: (e, i, 0)),
        scratch_shapes=[pltpu.VMEM((3, D_MODEL, D_MODEL), BF16)],
        compiler_params=_cparams(("parallel", "arbitrary")),
        name="ffn",
    )(xe, wg, wu, wd)


def _slot_window_start(off, cap):
    return jnp.minimum((off // TOK_BLK) * TOK_BLK, cap - 2 * TOK_BLK)


def _scatter_kernel(offs_ref, x2_ref, w_ref, p_ref, gf_ref, *rest, cap, final):
    ye, out_ref = rest[:N_EXPERTS], rest[N_EXPERTS]
    j = pl.program_id(0)
    ps, ws = p_ref[...].T, w_ref[...].T
    offs = [offs_ref[j * N_EXPERTS + e] for e in range(N_EXPERTS)]
    cnts = [offs_ref[(j + 1) * N_EXPERTS + e] - offs[e] for e in range(N_EXPERTS)]
    rels = [offs[e] - _slot_window_start(offs[e], cap) for e in range(N_EXPERTS)]
    weights = [ws[:, e:e + 1] for e in range(N_EXPERTS)]
    SUB = 16
    most = cnts[0]
    for e in range(1, N_EXPERTS):
        most = jnp.maximum(most, cnts[e])

    def onehot(e, start, width):
        slot = lax.broadcasted_iota(I32, (TOK_BLK, width), 1).astype(F32)
        return jnp.where(slot == ps[:, e:e + 1] + (rels[e] - start).astype(F32), weights[e], 0.0).astype(BF16)

    W = SCATTER_W

    @pl.when(most <= W - SUB)
    def _():
        acc = x2_ref[...]
        for g0 in range(0, N_EXPERTS, 2):
            group = (g0, g0 + 1)
            r0 = {e: pl.multiple_of(jnp.minimum((rels[e] // SUB) * SUB, 2 * TOK_BLK - W), SUB) for e in group}
            acc = acc + jnp.dot(jnp.concatenate([onehot(e, r0[e], W) for e in group], axis=1),
                                jnp.concatenate([ye[e][0, pl.ds(r0[e], W), :] for e in group], axis=0),
                                preferred_element_type=F32)
        out_ref[...] = acc

    @pl.when(most > W - SUB)
    def _():
        out_ref[...] = x2_ref[...]
        for e in range(N_EXPERTS):
            @pl.when(cnts[e] > 0)
            def _(e=e):
                out_ref[...] += jnp.dot(onehot(e, 0, 2 * TOK_BLK), ye[e][0], preferred_element_type=F32)

    if final:
        x = out_ref[...]
        out_ref[...] = x * lax.rsqrt(jnp.mean(x * x, axis=-1, keepdims=True) + EPS) * gf_ref[...]


def _scatter(x2, wsel, pos, ye, offs, gf, cap, final):
    T = x2.shape[0]
    nblk = T // TOK_BLK
    assert cap >= 2 * TOK_BLK and cap % TOK_BLK == 0

    def ye_spec(e):
        return pl.BlockSpec((pl.Element(1), pl.Element(2 * TOK_BLK), pl.Element(D_MODEL)),
                            lambda j, o: (e, pl.multiple_of(_slot_window_start(o[j * N_EXPERTS + e], cap), TOK_BLK),
                                          0))

    row = lambda w: pl.BlockSpec((TOK_BLK, w), lambda j, o: (j, 0))
    col = pl.BlockSpec((N_EXPERTS, TOK_BLK), lambda j, o: (0, j))
    grid_spec = pltpu.PrefetchScalarGridSpec(
        num_scalar_prefetch=1,
        grid=(nblk,),
        in_specs=[row(D_MODEL), col, col, pl.BlockSpec((1, D_MODEL), lambda j, o: (0, 0))]
                 + [ye_spec(e) for e in range(N_EXPERTS)],
        out_specs=row(D_MODEL),
    )
    return pl.pallas_call(
        functools.partial(_scatter_kernel, cap=cap, final=final),
        out_shape=jax.ShapeDtypeStruct((T, D_MODEL), F32),
        grid_spec=grid_spec,
        compiler_params=_cparams(("arbitrary",)),
        name="scatter",
    )(offs, x2, wsel, pos, gf, *([ye] * N_EXPERTS))


def _block_diag(w):
    n, a, b = w.shape
    out = jnp.zeros((n * a, n * b), w.dtype)
    for i in range(n):
        out = out.at[i * a:(i + 1) * a, i * b:(i + 1) * b].set(w[i])
    return out


def _rope_tables(S):
    inv = ROPE_THETA ** (-jnp.arange(0, ROT_DIM, 2, dtype=F32) / ROT_DIM)
    ang = jnp.arange(S, dtype=F32)[:, None] * inv[None, :]
    cos, sin = jnp.cos(ang), jnp.sin(ang)
    half = ROT_DIM // 2
    ones, zeros = jnp.ones((S, ATT_HD - ROT_DIM), F32), jnp.zeros((S, ATT_HD - ROT_DIM), F32)
    zh = jnp.zeros((S, half), F32)
    c = jnp.concatenate([cos, cos, ones], axis=1)
    s1 = jnp.concatenate([-sin, zh, zeros], axis=1)
    s2 = jnp.concatenate([zh, sin, zeros], axis=1)
    return tuple(jnp.tile(t, (1, 2)) for t in (c, s1, s2))


def _expanders():
    eg = np.zeros((2, 128, GROUP_W), np.float32)
    eb = np.zeros((2, 128, GROUP_W), np.float32)
    for d in range(2):
        for h in range(DN_HEADS):
            eg[d, d * DN_HEADS + h, h * DN_DK:(h + 1) * DN_DK] = 1.0
            eb[d, 2 * DN_HEADS + d * DN_HEADS + h, h * DN_DK:(h + 1) * DN_DK] = 1.0
    hd = np.arange(GROUP_W) // DN_DK
    bd = (hd[:, None] == hd[None, :]).astype(np.float32)
    return jnp.asarray(eg), jnp.asarray(eb), jnp.asarray(bd)


def _layer_params(l, norm1_g, w_in, conv_a_w, conv_a_b, rg_wa, rg_ba, rg_wx, rg_bx, rg_lambda, pool_w, pool_scale,
                  dn_conv_w, dn_A_log, dn_dt_bias, dn_norm_g, mix_norm_g, w_out, norm2_g, router_w,
                  exp_w_gate, exp_w_up, exp_w_down):
    w = w_in[l]
    w_cat = jnp.concatenate([w[:, 0:1536], w[:, 1552:2576], w[:, 1536:1552],
                             jnp.zeros((D_MODEL, COL_END - COL_CAB - 16), F32)], axis=1).astype(BF16)
    pad8 = lambda v: jnp.concatenate([v.reshape(1, 2 * DN_HEADS), jnp.zeros((1, 120), F32)], axis=1)
    return dict(
        g1=norm1_g[l].reshape(1, D_MODEL), w_cat=w_cat,
        conv_a_w=conv_a_w[l], conv_a_b=conv_a_b[l].reshape(1, GROUP_W),
        wa=[_block_diag(rg_wa[l, d]).astype(BF16) for d in range(2)],
        wx=[_block_diag(rg_wx[l, d]).astype(BF16) for d in range(2)],
        ba=[rg_ba[l, d].reshape(1, GROUP_W) for d in range(2)],
        bx=[rg_bx[l, d].reshape(1, GROUP_W) for d in range(2)],
        lam=[rg_lambda[l, d].reshape(1, GROUP_W) for d in range(2)],
        pool_w=_block_diag(pool_w[l]).astype(BF16), pool_scale=pool_scale[l].reshape(1, GROUP_W),
        dn_conv_w=dn_conv_w[l], alog=pad8(dn_A_log[l]), dtb=pad8(dn_dt_bias[l]),
        dng=jnp.tile(dn_norm_g[l], DN_HEADS).reshape(1, GROUP_W),
        mg=mix_norm_g[l].reshape(1, D_MODEL), wo=w_out[l].astype(BF16),
        g2=norm2_g[l].reshape(1, D_MODEL),
        rw=jnp.concatenate(_split(router_w[l], 2) + [jnp.zeros((D_MODEL, 128 - 2 * N_EXPERTS), BF16)], axis=1),
        wg=exp_w_gate, wu=exp_w_up, wd=exp_w_down,
    )


def _chunk_rows(gcn, d):
    B, S, _ = gcn.shape
    n = S // DN_CHUNK
    g = gcn[:, :, d * DN_HEADS:(d + 1) * DN_HEADS].reshape(B, n, DN_CHUNK, DN_HEADS)
    return g.transpose(0, 1, 3, 2).reshape(B, n, 1, GROUP_W)


def _encoder(x, layers, final_g, consts):
    B, S, D = x.shape
    T = B * S
    cap = max(1, EC_CAPACITY * T // N_EXPERTS)
    eg, eb, bdmask = consts
    rope_c, rope_s1, rope_s2 = _rope_tables(S)
    xt = x.reshape(T, D)
    for l, p in enumerate(layers):
        pa, pb, pq, pg, pd, pcab = _inproj(xt, p["g1"], p["w_cat"], rope_c, rope_s1, rope_s2, S)
        pa3 = pa.reshape(B, S, 2 * GROUP_W)
        rg = lambda d: (p["conv_a_w"], p["conv_a_b"], p["wa"][d], p["ba"][d], p["wx"][d], p["bx"][d], p["lam"][d])
        hf = _rglru(pa3, None, *rg(0), reverse=False)
        ya = _rglru(pa3, hf, *rg(1), reverse=True)
        yb = _pool(pb.reshape(B, S, GROUP_W), p["pool_w"], p["pool_scale"])
        qn, kn, vv, gcf, gcb, bef, beb, gcn = _dnprep(pq.reshape(B, S, 3 * GROUP_W), pcab.reshape(B, S, 128),
                                                      p["dn_conv_w"], p["alog"], p["dtb"], eg, eb)
        of, ob = _delta(qn, kn, vv, gcf, gcb, bef, beb, _chunk_rows(gcn, 0), _chunk_rows(gcn, 1), bdmask)
        yd = _attention(pd.reshape(B, S, 3 * GROUP_W))
        flat = lambda a: a.reshape(T, GROUP_W)
        x2, h2, aff_t = _outproj(flat(ya), flat(yb), flat(of), flat(ob), pg, flat(yd), xt,
                                 p["dng"], p["mg"], p["wo"], p["g2"], p["rw"])
        mask, pos, wsel, offs3 = _route(aff_t, cap)
        offs = jnp.concatenate([offs3[:, :, 0].reshape(-1), jnp.full((N_EXPERTS,), cap, I32)])
        xe = _gather(h2, mask.reshape(N_EXPERTS, 1, T), pos.reshape(N_EXPERTS, 1, T), offs, cap)
        ye = _ffn(xe, p["wg"], p["wu"], p["wd"], l, cap)
        xt = _scatter(x2, wsel, pos, ye, offs, final_g, cap, final=(l == len(layers) - 1))
    return xt.reshape(B, S, D)


def kernel(x_prompt, x_sample, norm1_g, w_in, conv_a_w, conv_a_b, rg_wa, rg_ba, rg_wx, rg_bx, rg_lambda, pool_w, pool_scale, dn_conv_w, dn_A_log, dn_dt_bias, dn_norm_g, mix_norm_g, w_out, norm2_g, router_w, exp_w_gate, exp_w_up, exp_w_down, final_norm_g):
    layers = [_layer_params(l, norm1_g, w_in, conv_a_w, conv_a_b, rg_wa, rg_ba, rg_wx, rg_bx, rg_lambda, pool_w,
                            pool_scale, dn_conv_w, dn_A_log, dn_dt_bias, dn_norm_g, mix_norm_g, w_out, norm2_g,
                            router_w, exp_w_gate, exp_w_up, exp_w_down) for l in range(DEPTH)]
    consts = _expanders()
    final_g = final_norm_g.reshape(1, D_MODEL)
    return (_encoder(x_prompt, layers, final_g, consts), _encoder(x_sample, layers, final_g, consts))
```

```python
import functools
import math
import numpy as np
import jax
import jax.numpy as jnp
from jax import lax
from jax.experimental import pallas as pl
from jax.experimental.pallas import tpu as pltpu

F32, BF16, I32 = jnp.float32, jnp.bfloat16, jnp.int32

D_MODEL = 1024
DEPTH = 2
GROUP_W = 256
RG_C = 8.0
POOL_WINDOWS = (2, 4, 8, 16)
DN_HEADS = 4
DN_DK = 64
DN_CHUNK = 64
ATT_HD = 64
ROT_DIM = 16
ROPE_THETA = 500000.0
ATT_WINDOWS = (128, 512, 2048)
ATT_DILATIONS = (1, 4, 16)
N_EXPERTS = 16
EC_CAPACITY = 2
EPS = 1e-6
NEG = -1e30

COL_A, COL_B, COL_CQ, COL_CG, COL_D, COL_CAB, COL_END = 0, 512, 768, 1536, 1792, 2560, 2688

ROW_TILE = 512
HALO = 8
DELTA_CB = 8
ATT_QB = 128
ATT_UNROLL = 8
TOK_BLK = 256
GATHER_W = 64
GATHER_JB = 8
GATHER_E = 4
SCATTER_W = 128
VMEM_LIMIT = 56 * 1024 * 1024


def _cparams(sem):
    return pltpu.CompilerParams(dimension_semantics=sem, vmem_limit_bytes=VMEM_LIMIT)


def _dot(a, b):
    return jnp.dot(a.astype(BF16), b.astype(BF16), preferred_element_type=F32)


def _split(a, pieces):
    out, rem = [], a
    for i in range(pieces):
        t = rem.astype(BF16)
        out.append(t)
        if i + 1 < pieces:
            rem = rem - t.astype(F32)
    return out


def _dot_sel(a, sel, pieces=3):
    sel = sel.astype(BF16)
    return sum(jnp.dot(t, sel, preferred_element_type=F32) for t in _split(a, pieces))


def _sel_dot(sel, a, pieces=3):
    sel = sel.astype(BF16)
    return sum(jnp.dot(sel, t, preferred_element_type=F32) for t in _split(a, pieces))


def _dot_nt(a, b):
    return lax.dot_general(a.astype(BF16), b.astype(BF16), (((1,), (1,)), ((), ())), preferred_element_type=F32)


def _dot_tn(a, b):
    return lax.dot_general(a.astype(BF16), b.astype(BF16), (((0,), (0,)), ((), ())), preferred_element_type=F32)


def _sigmoid(x):
    return 1.0 / (1.0 + jnp.exp(-x))


def _softplus(x):
    return jnp.maximum(x, 0.0) + jnp.log1p(jnp.exp(-jnp.abs(x)))


def _shift_rows(e, k):
    n = e.shape[0]
    return e if k % n == 0 else pltpu.roll(e, (-k) % n, axis=0)


def _with_halo(cur_ref, prev_ref, next_ref, first, last):
    prev = jnp.where(first, 0.0, prev_ref[0])
    nxt = jnp.where(last, 0.0, next_ref[0])
    return jnp.concatenate([prev, cur_ref[0], nxt], axis=0)


def _halo_specs(ts, width, S, blk_of):
    per = ts // HALO
    last = S // HALO - 1
    cur = pl.BlockSpec((1, ts, width), lambda b, c: (b, blk_of(c), 0))
    prev = pl.BlockSpec((1, HALO, width), lambda b, c: (b, jnp.maximum(blk_of(c) * per - 1, 0), 0))
    nxt = pl.BlockSpec((1, HALO, width), lambda b, c: (b, jnp.minimum((blk_of(c) + 1) * per, last), 0))
    return cur, prev, nxt


def _inproj_kernel(x_ref, g_ref, w_ref, c_ref, s1_ref, s2_ref, pa_ref, pb_ref, pq_ref, pg_ref, pd_ref, pcab_ref):
    x = x_ref[...]
    h = x * lax.rsqrt(jnp.mean(x * x, axis=-1, keepdims=True) + EPS) * g_ref[...]
    hb = h.astype(BF16)

    def mm(lo, hi):
        return jnp.dot(hb, w_ref[:, lo:hi], preferred_element_type=F32)

    pa_ref[...] = mm(COL_A, COL_B)
    pb_ref[...] = mm(COL_B, COL_CQ)
    pq_ref[...] = mm(COL_CQ, COL_CG)
    pg_ref[...] = mm(COL_CG, COL_D)
    c, s1, s2 = c_ref[...], s1_ref[...], s2_ref[...]
    for part in range(2):
        yy = mm(COL_D + 256 * part, COL_D + 256 * part + 256)
        for half in range(2):
            y = yy[:, 128 * half:128 * half + 128]
            y = y * c + pltpu.roll(y, 128 - ROT_DIM // 2, axis=1) * s1 + pltpu.roll(y, ROT_DIM // 2, axis=1) * s2
            if part == 0:
                y = y * (ATT_HD ** -0.5)
            pd_ref[:, 256 * part + 128 * half:256 * part + 128 * half + 128] = y
    vcab = mm(COL_D + 512, COL_END)
    pd_ref[:, 512:768] = vcab[:, 0:256]
    pcab_ref[...] = vcab[:, 256:384]


def _inproj(xt, g1, w_cat, rope_c, rope_s1, rope_s2, S):
    T = xt.shape[0]
    tm = min(ROW_TILE, S)
    per_seq = S // tm
    widths = (512, 256, 768, 256, 768, 128)
    row = lambda w: pl.BlockSpec((tm, w), lambda i: (i, 0))
    rope = pl.BlockSpec((tm, 128), lambda i: (i % per_seq, 0))
    return pl.pallas_call(
        _inproj_kernel,
        out_shape=[jax.ShapeDtypeStruct((T, w), F32) for w in widths],
        grid=(T // tm,),
        in_specs=[row(D_MODEL), pl.BlockSpec((1, D_MODEL), lambda i: (0, 0)),
                  pl.BlockSpec((D_MODEL, COL_END), lambda i: (0, 0)), rope, rope, rope],
        out_specs=[row(w) for w in widths],
        compiler_params=_cparams(("parallel",)),
        name="inproj",
    )(xt, g1, w_cat, rope_c, rope_s1, rope_s2)


def _rglru_kernel(*refs, reverse, ts, nc):
    if reverse:
        (cur_ref, prev_ref, next_ref, gate_ref, hf_ref, cw_ref, cb_ref, wa_ref, ba_ref, wx_ref, bx_ref, lam_ref,
         out_ref, a_ref, b_ref, carry_ref) = refs
    else:
        (cur_ref, prev_ref, next_ref, cw_ref, cb_ref, wa_ref, ba_ref, wx_ref, bx_ref, lam_ref,
         out_ref, a_ref, b_ref, carry_ref) = refs
    c = pl.program_id(1)
    blk = (nc - 1 - c) if reverse else c
    e = _with_halo(cur_ref, prev_ref, next_ref, blk == 0, blk == nc - 1)
    cw = cw_ref[...]
    sl = slice(HALO, HALO + ts)
    u = (cw[0:1] * _shift_rows(e, -2)[sl] + cw[1:2] * _shift_rows(e, -1)[sl] + cw[2:3] * e[sl]
         + cw[3:4] * _shift_rows(e, 1)[sl]) + cb_ref[...]
    r = _sigmoid(_dot(u, wa_ref[...]) + ba_ref[...])
    i = _sigmoid(_dot(u, wx_ref[...]) + bx_ref[...])
    log_a = -RG_C * r * _softplus(-lam_ref[...])
    a_ref[...] = jnp.exp(log_a)
    b_ref[...] = jnp.sqrt(1.0 - jnp.exp(2.0 * log_a)) * (i * u)

    @pl.when(c == 0)
    def _():
        carry_ref[...] = jnp.zeros_like(carry_ref)

    row = lax.broadcasted_iota(I32, (HALO, GROUP_W), 0)
    nt = ts // HALO

    def body(it, carry):
        ti = (nt - 1 - it) if reverse else it
        st = pl.multiple_of(ti * HALO, HALO)
        a = a_ref[pl.ds(st, HALO), :]
        b = b_ref[pl.ds(st, HALO), :]
        for d in (1, 2, 4):
            k = d if reverse else -d
            valid = (row < HALO - d) if reverse else (row >= d)
            b = jnp.where(valid, a * _shift_rows(b, k) + b, b)
            a = jnp.where(valid, a * _shift_rows(a, k), a)
        h = a * carry + b
        if reverse:
            g = gate_ref[0, pl.ds(st, HALO), :]
            cdf = 0.5 * (1.0 + jnp.tanh(np.float32(np.sqrt(2.0 / np.pi)) * (g + 0.044715 * (g * g * g))))
            out_ref[0, pl.ds(st, HALO), :] = (g * cdf) * (hf_ref[0, pl.ds(st, HALO), :] + h)
            return jnp.broadcast_to(h[0:1], h.shape)
        out_ref[0, pl.ds(st, HALO), :] = h
        return jnp.broadcast_to(h[HALO - 1:HALO], h.shape)

    carry_ref[...] = lax.fori_loop(0, nt, body, carry_ref[...], unroll=4)


def _rglru(pa3, hf, cw, cb, wa, ba, wx, bx, lam, reverse):
    B, S, _ = pa3.shape
    ts = min(ROW_TILE, S)
    nc = S // ts
    blk_of = (lambda c: nc - 1 - c) if reverse else (lambda c: c)
    cur, prev, nxt = _halo_specs(ts, GROUP_W, S, blk_of)
    tile = pl.BlockSpec((1, ts, GROUP_W), lambda b, c: (b, blk_of(c), 0))
    const = lambda shape: pl.BlockSpec(shape, lambda b, c: (0,) * len(shape))
    in_specs = [cur, prev, nxt]
    args = [pa3, pa3, pa3]
    if reverse:
        in_specs += [pl.BlockSpec((1, ts, GROUP_W), lambda b, c: (b, blk_of(c), 1)), tile]
        args += [pa3, hf]
    in_specs += [const((4, GROUP_W)), const((1, GROUP_W)), const((GROUP_W, GROUP_W)), const((1, GROUP_W)),
                 const((GROUP_W, GROUP_W)), const((1, GROUP_W)), const((1, GROUP_W))]
    args += [cw, cb, wa, ba, wx, bx, lam]
    return pl.pallas_call(
        functools.partial(_rglru_kernel, reverse=reverse, ts=ts, nc=nc),
        out_shape=jax.ShapeDtypeStruct((B, S, GROUP_W), F32),
        grid=(B, nc),
        in_specs=in_specs,
        out_specs=tile,
        scratch_shapes=[pltpu.VMEM((ts, GROUP_W), F32), pltpu.VMEM((ts, GROUP_W), F32),
                        pltpu.VMEM((HALO, GROUP_W), F32)],
        compiler_params=_cparams(("parallel", "arbitrary")),
        name="rglru_bwd" if reverse else "rglru_fwd",
    )(*args)


def _pool_kernel(cur_ref, prev_ref, next_ref, w_ref, sc_ref, out_ref, *, ts, nc, S):
    c = pl.program_id(1)
    e = _with_halo(cur_ref, prev_ref, next_ref, c == 0, c == nc - 1)
    sl = slice(HALO, HALO + ts)
    a2 = e + _shift_rows(e, 1)
    a4 = a2 + _shift_rows(a2, 2)
    a8 = a4 + _shift_rows(a4, 4)
    a16 = a8 + _shift_rows(a8, 8)
    sums = [_shift_rows(a, -(w // 2))[sl] for a, w in zip((a2, a4, a8, a16), POOL_WINDOWS)]
    gi = lax.broadcasted_iota(I32, (ts, GROUP_W), 1) // (GROUP_W // len(POOL_WINDOWS))
    ssum = jnp.where(gi == 0, sums[0], jnp.where(gi == 1, sums[1], jnp.where(gi == 2, sums[2], sums[3])))
    hw = jnp.where(gi == 0, 1, jnp.where(gi == 1, 2, jnp.where(gi == 2, 4, 8)))
    t = c * ts + lax.broadcasted_iota(I32, (ts, GROUP_W), 0)
    cnt = (jnp.minimum(t + hw, S) - jnp.maximum(t - hw, 0)).astype(F32)
    p = ssum / cnt - e[sl]
    out_ref[0] = _dot(p, w_ref[...]) * sc_ref[...]


def _pool(pb3, w_bd, scale):
    B, S, _ = pb3.shape
    ts = min(ROW_TILE, S)
    nc = S // ts
    cur, prev, nxt = _halo_specs(ts, GROUP_W, S, lambda c: c)
    return pl.pallas_call(
        functools.partial(_pool_kernel, ts=ts, nc=nc, S=S),
        out_shape=jax.ShapeDtypeStruct((B, S, GROUP_W), F32),
        grid=(B, nc),
        in_specs=[cur, prev, nxt, pl.BlockSpec((GROUP_W, GROUP_W), lambda b, c: (0, 0)),
                  pl.BlockSpec((1, GROUP_W), lambda b, c: (0, 0))],
        out_specs=pl.BlockSpec((1, ts, GROUP_W), lambda b, c: (b, c, 0)),
        compiler_params=_cparams(("parallel", "parallel")),
        name="pool",
    )(pb3, pb3, pb3, w_bd, scale)


def _head_sum_matrix(n, group):
    r = lax.broadcasted_iota(I32, (n, n), 0) // group
    c = lax.broadcasted_iota(I32, (n, n), 1) // group
    return (r == c).astype(F32)


def _dnprep_kernel(cur_ref, prev_ref, next_ref, cab_ref, cw_ref, alog_ref, dtb_ref, eg_ref, eb_ref,
                   q_ref, k_ref, v_ref, gcf_ref, gcb_ref, bef_ref, beb_ref, gcn_ref, *, ts, nc):
    c = pl.program_id(1)
    e = _with_halo(cur_ref, prev_ref, next_ref, c == 0, c == nc - 1)
    cw = cw_ref[...]
    sl = slice(HALO, HALO + ts)
    y = (cw[0:1] * _shift_rows(e, -2)[sl] + cw[1:2] * _shift_rows(e, -1)[sl] + cw[2:3] * e[sl]
         + cw[3:4] * _shift_rows(e, 1)[sl])
    y = y * _sigmoid(y)
    q, k = y[:, 0:GROUP_W], y[:, GROUP_W:2 * GROUP_W]
    hs = _head_sum_matrix(GROUP_W, DN_DK)
    q_ref[0] = q * lax.rsqrt(_dot_sel(q * q, hs, 2) + EPS) * (DN_DK ** -0.5)
    k_ref[0] = k * lax.rsqrt(_dot_sel(k * k, hs, 2) + EPS)
    v_ref[0] = y[:, 2 * GROUP_W:3 * GROUP_W]
    cab = cab_ref[0]
    g = -jnp.exp(alog_ref[...]) * _softplus(cab + dtb_ref[...])
    beta = _sigmoid(cab)
    bef_ref[0] = _dot_sel(beta, eb_ref[0])
    beb_ref[0] = _dot_sel(beta, eb_ref[1])
    r = lax.broadcasted_iota(I32, (ts, ts), 0)
    cc = lax.broadcasted_iota(I32, (ts, ts), 1)
    same = (r // DN_CHUNK) == (cc // DN_CHUNK)
    lane = lax.broadcasted_iota(I32, (ts, 128), 1)
    gcn = jnp.where(lane < DN_HEADS, _sel_dot(same & (r >= cc), g), _sel_dot(same & (r <= cc), g))
    gcn_ref[0] = gcn
    gcf_ref[0] = _dot_sel(gcn, eg_ref[0])
    gcb_ref[0] = _dot_sel(gcn, eg_ref[1])


def _dnprep(pq3, pcab3, cw, alog_row, dtb_row, eg, eb):
    B, S, _ = pq3.shape
    ts = min(ROW_TILE, S)
    nc = S // ts
    cur, prev, nxt = _halo_specs(ts, 3 * GROUP_W, S, lambda c: c)
    tile = pl.BlockSpec((1, ts, GROUP_W), lambda b, c: (b, c, 0))
    const = lambda shape: pl.BlockSpec(shape, lambda b, c: (0,) * len(shape))
    return pl.pallas_call(
        functools.partial(_dnprep_kernel, ts=ts, nc=nc),
        out_shape=[jax.ShapeDtypeStruct((B, S, GROUP_W), F32)] * 7 + [jax.ShapeDtypeStruct((B, S, 128), F32)],
        grid=(B, nc),
        in_specs=[cur, prev, nxt, pl.BlockSpec((1, ts, 128), lambda b, c: (b, c, 0)),
                  const((4, 3 * GROUP_W)), const((1, 128)), const((1, 128)),
                  const((2, 128, GROUP_W)), const((2, 128, GROUP_W))],
        out_specs=[tile] * 7 + [pl.BlockSpec((1, ts, 128), lambda b, c: (b, c, 0))],
        compiler_params=_cparams(("parallel", "parallel")),
        name="dnprep",
    )(pq3, pq3, pq3, pcab3, cw, alog_row, dtb_row, eg, eb)


def _delta_kernel(qf, kf, vf, gcf, bef, grf, qb, kb, vb, gcb, beb, grb, bd_ref, of_ref, ob_ref,
                  s_ref, p_scr, x_scr, wq_scr, u_scr, at_scr, kd_scr, vb_scr, kg_scr, *, cb):
    j = pl.program_id(1)

    @pl.when(j == 0)
    def _():
        s_ref[...] = jnp.zeros_like(s_ref)

    C = DN_CHUNK
    bd = bd_ref[...] > 0.0
    c_idx = lax.broadcasted_iota(I32, (C, GROUP_W), 0)
    m_idx = lax.broadcasted_iota(I32, (C, GROUP_W), 1) % C
    eye = (c_idx == m_idx).astype(F32)

    def blockdiag(x):
        return jnp.where(bd, jnp.tile(x.astype(BF16), (DN_HEADS, 1)), jnp.zeros((), BF16))

    dirs = ((qf, kf, vf, gcf, bef, grf, of_ref), (qb, kb, vb, gcb, beb, grb, ob_ref))
    units = [(d, ci) for ci in range(cb) for d in range(2)]

    def chunk_of(d, ci):
        return cb - 1 - ci if d == 1 else ci

    def prepare(ui):
        d, ci = units[ui]
        q_r, k_r, v_r, gc_r, be_r, gr_r, _ = dirs[d]
        rev = d == 1
        tril = (c_idx <= m_idx) if rev else (c_idx >= m_idx)
        strict = (c_idx < m_idx) if rev else (c_idx > m_idx)
        last = 0 if rev else C - 1
        cc = chunk_of(d, ci)
        rows = slice(cc * C, (cc + 1) * C)
        q, k, v = q_r[0, rows, :], k_r[0, rows, :], v_r[0, rows, :]
        gc, be, gr = gc_r[0, rows, :], be_r[0, rows, :], gr_r[0, cc]
        eg = jnp.exp(gc)
        kbeta = k * be
        kkqk = _dot_nt(jnp.concatenate([kbeta, q], axis=0), blockdiag(k))
        decay = jnp.exp(jnp.where(tril, gc - gr, -jnp.inf))
        at_scr[ui] = kkqk[C:] * decay
        p = -jnp.where(strict, kkqk[:C] * decay, 0.0)
        p_scr[ui] = p
        x_scr[ui] = eye + p
        wq_scr[ui, C:2 * C, :] = q * eg
        kd_scr[ui] = k * jnp.exp(gc[last:last + 1] - gc)
        vb_scr[ui] = v * be
        kg_scr[ui] = kbeta * eg

    def solve(group):
        for ui in group:
            p = p_scr[ui]
            p_scr[ui] = _dot(p, blockdiag(p))
        for _ in range(4):
            for ui in group:
                p, x = p_scr[ui], x_scr[ui]
                px = _dot(jnp.concatenate([p, x], axis=0), blockdiag(p))
                p_scr[ui] = px[:C]
                x_scr[ui] = x + px[C:]
        for ui in group:
            x = x_scr[ui]
            x = x + _dot(x, blockdiag(p_scr[ui]))
            u_scr[ui] = _dot(x, blockdiag(vb_scr[ui]))
            wq_scr[ui, 0:C, :] = _dot(x, blockdiag(kg_scr[ui]))

    def recur(ui):
        d, ci = units[ui]
        gc_r, o_r = dirs[d][3], dirs[d][6]
        cc = chunk_of(d, ci)
        last = cc * C + (0 if d == 1 else C - 1)
        state = s_ref[d]
        ws_qs = _dot(wq_scr[ui], state)
        v_new = u_scr[ui] - ws_qs[:C]
        o_r[0, cc * C:(cc + 1) * C, :] = ws_qs[C:] + _dot(at_scr[ui], blockdiag(v_new))
        s_ref[d] = (state * jnp.exp(gc_r[0, last:last + 1, :])
                    + jnp.where(bd, _dot_tn(kd_scr[ui], v_new), 0.0))

    for ui in range(len(units)):
        prepare(ui)
    solve(range(len(units)))
    for ui in range(len(units)):
        recur(ui)


def _delta(qn, kn, vv, gcf, gcb, bef, beb, grf, grb, bdmask):
    B, S, _ = qn.shape
    cb = DELTA_CB
    rb = cb * DN_CHUNK
    nb = S // rb
    f = lambda b, j: (b, j, 0)
    r = lambda b, j: (b, nb - 1 - j, 0)
    tf = pl.BlockSpec((1, rb, GROUP_W), f)
    tr = pl.BlockSpec((1, rb, GROUP_W), r)
    gf = pl.BlockSpec((1, cb, 1, GROUP_W), lambda b, j: (b, j, 0, 0))
    gr = pl.BlockSpec((1, cb, 1, GROUP_W), lambda b, j: (b, nb - 1 - j, 0, 0))
    return pl.pallas_call(
        functools.partial(_delta_kernel, cb=cb),
        out_shape=[jax.ShapeDtypeStruct((B, S, GROUP_W), F32)] * 2,
        grid=(B, nb),
        in_specs=[tf, tf, tf, tf, tf, gf, tr, tr, tr, tr, tr, gr,
                  pl.BlockSpec((GROUP_W, GROUP_W), lambda b, j: (0, 0))],
        out_specs=[tf, tr],
        scratch_shapes=[pltpu.VMEM((2, GROUP_W, GROUP_W), F32)]
                       + [pltpu.VMEM((2 * cb, n * DN_CHUNK, GROUP_W), F32) for n in (1, 1, 2, 1, 1, 1, 1, 1)],
        compiler_params=_cparams(("parallel", "arbitrary")),
        name="delta",
    )(qn, kn, vv, gcf, bef, grf, qn, kn, vv, gcb, beb, grb, bdmask)


def _attn_kernel(q_ref, k_ref, v_ref, out_ref, m_ref, l_ref, o_ref, *, S):
    head0 = lax.broadcasted_iota(I32, (1, 128), 1) < ATT_HD

    def rows(start, n, dil):
        return pl.ds(start, n) if dil == 1 else pl.ds(start, n, stride=dil)

    for bi, (win, dil) in enumerate(zip(ATT_WINDOWS, ATT_DILATIONS)):
        half = win // (2 * dil)
        L = S // dil
        qb_n = min(ATT_QB, L)
        kw = min(L, qb_n + 2 * half)
        nqb = L // qb_n

        def block_stats(idx, dil=dil, half=half, L=L, qb_n=qb_n, kw=kw, nqb=nqb):
            r = idx // nqb
            m0 = (idx % nqb) * qb_n
            ks = jnp.clip(m0 - half, 0, L - kw)
            qsel = rows(r + m0 * dil, qb_n, dil)
            ksel = rows(r + ks * dil, kw, dil)
            q = q_ref[0, qsel, :]
            kk = k_ref[0, ksel, :].astype(BF16)
            vv = v_ref[0, ksel, :].astype(BF16)
            rel = (lax.broadcasted_iota(I32, (qb_n, kw), 1) - lax.broadcasted_iota(I32, (qb_n, kw), 0)
                   + (ks - m0 + half))
            valid = rel.astype(jnp.uint32) <= 2 * half
            q2 = jnp.concatenate([jnp.where(head0, q, 0.0), jnp.where(head0, 0.0, q)], axis=0)
            s = jnp.where(jnp.concatenate([valid, valid], axis=0), _dot_nt(q2, kk), NEG)
            m = jnp.max(s, axis=-1, keepdims=True)
            p = jnp.exp(s - m)
            l = jnp.sum(p, axis=-1, keepdims=True)
            o = _dot(p, vv)
            return (qsel, jnp.where(head0, m[:qb_n], m[qb_n:]), jnp.where(head0, l[:qb_n], l[qb_n:]),
                    jnp.where(head0, o[:qb_n], o[qb_n:]))

        unroll = math.gcd(ATT_UNROLL, dil * nqb)

        def body(it, carry, bi=bi, unroll=unroll):
            stats = [block_stats(it * unroll + s) for s in range(unroll)]
            if bi > 0:
                olds = [(m_ref[qsel, :], l_ref[qsel, :], o_ref[qsel, :]) for qsel, _, _, _ in stats]
                merged = []
                for (qsel, m, l, o), (m_old, l_old, o_old) in zip(stats, olds):
                    m_new = jnp.maximum(m_old, m)
                    w_old, w_cur = jnp.exp(m_old - m_new), jnp.exp(m - m_new)
                    merged.append((qsel, m_new, w_old * l_old + w_cur * l, w_old * o_old + w_cur * o))
                stats = merged
            for qsel, m, l, o in stats:
                if bi == len(ATT_WINDOWS) - 1:
                    out_ref[0, qsel, :] = o / l
                else:
                    m_ref[qsel, :] = m
                    l_ref[qsel, :] = l
                    o_ref[qsel, :] = o
            return carry

        lax.fori_loop(0, dil * nqb // unroll, body, 0)


def _attention(pd3):
    B, S, _ = pd3.shape
    spec = lambda off: pl.BlockSpec((1, S, 128), lambda b, p: (b, 0, off + p))
    return pl.pallas_call(
        functools.partial(_attn_kernel, S=S),
        out_shape=jax.ShapeDtypeStruct((B, S, GROUP_W), F32),
        grid=(B, 2),
        in_specs=[spec(0), spec(2), spec(4)],
        out_specs=pl.BlockSpec((1, S, 128), lambda b, p: (b, 0, p)),
        scratch_shapes=[pltpu.VMEM((S, 128), F32)] * 3,
        compiler_params=_cparams(("parallel", "parallel")),
        name="attention",
    )(pd3, pd3, pd3)


def _outproj_kernel(ya_ref, yb_ref, of_ref, ob_ref, cg_ref, yd_ref, x_ref, dng_ref, mg_ref, wo_ref, g2_ref, rw_ref,
                    x2_ref, h2_ref, aff_ref):
    o = of_ref[...] + ob_ref[...]
    ms = _dot_sel(o * o, _head_sum_matrix(GROUP_W, DN_DK), 2) * (1.0 / DN_DK)
    cg = cg_ref[...]
    yc = (o * lax.rsqrt(ms + EPS) * dng_ref[...]) * (cg * _sigmoid(cg))
    acc = x_ref[...]
    for gi, y in enumerate((ya_ref[...], yb_ref[...], yc, yd_ref[...])):
        sl = slice(gi * GROUP_W, (gi + 1) * GROUP_W)
        mix = y * lax.rsqrt(jnp.mean(y * y, axis=-1, keepdims=True) + EPS) * mg_ref[:, sl]
        acc = acc + _dot(mix, wo_ref[sl, :])
    x2_ref[...] = acc
    h2 = acc * lax.rsqrt(jnp.mean(acc * acc, axis=-1, keepdims=True) + EPS) * g2_ref[...]
    h2_ref[...] = h2.astype(BF16)
    h_hi, h_lo = _split(h2, 2)
    z = jnp.dot(h_hi, rw_ref[...], preferred_element_type=F32) + jnp.dot(h_lo, rw_ref[...], preferred_element_type=F32)
    logits = z + pltpu.roll(z, 128 - N_EXPERTS, axis=1)
    logits = jnp.where(lax.broadcasted_iota(I32, logits.shape, 1) < N_EXPERTS, logits, -jnp.inf)
    ex = jnp.exp(logits - jnp.max(logits, axis=-1, keepdims=True))
    aff = ex / jnp.sum(ex, axis=-1, keepdims=True)
    aff_ref[...] = aff.T[0:N_EXPERTS, :]


def _outproj(ya, yb, of, ob, cg, yd, xt, dng, mg, wo, g2, rw):
    T = xt.shape[0]
    tm = 256
    row = lambda w: pl.BlockSpec((tm, w), lambda i: (i, 0))
    const = lambda shape: pl.BlockSpec(shape, lambda i: (0, 0))
    return pl.pallas_call(
        _outproj_kernel,
        out_shape=[jax.ShapeDtypeStruct((T, D_MODEL), F32), jax.ShapeDtypeStruct((T, D_MODEL), BF16),
                   jax.ShapeDtypeStruct((N_EXPERTS, T), F32)],
        grid=(T // tm,),
        in_specs=[row(GROUP_W)] * 6 + [row(D_MODEL), const((1, GROUP_W)), const((1, D_MODEL)),
                                       const((D_MODEL, D_MODEL)), const((1, D_MODEL)), const((D_MODEL, 128))],
        out_specs=[row(D_MODEL), row(D_MODEL), pl.BlockSpec((N_EXPERTS, tm), lambda i: (0, i))],
        compiler_params=_cparams(("parallel",)),
        name="outproj",
    )(ya, yb, of, ob, cg, yd, xt, dng, mg, wo, g2, rw)


def _strict_upper(n):
    return (lax.broadcasted_iota(I32, (n, n), 0) < lax.broadcasted_iota(I32, (n, n), 1)).astype(BF16)


def _route_kernel(aff_ref, mask_ref, pos_ref, wsel_ref, offs_ref, *, cap, nblk):
    keys = pltpu.bitcast(aff_ref[...], I32)

    def bit_body(i, thr):
        cand = thr | lax.shift_left(jnp.int32(1), 30 - i)
        cnt = jnp.sum((keys >= cand).astype(F32), axis=1, keepdims=True)
        return jnp.where(cnt >= cap, cand, thr)

    thr = lax.fori_loop(0, 31, bit_body, jnp.zeros((N_EXPERTS, 1), I32))
    need = cap - jnp.sum((keys > thr).astype(F32), axis=1, keepdims=True)
    su = _strict_upper(TOK_BLK)

    def blk_body(j, carry):
        ceq, csel = carry
        st = pl.multiple_of(j * TOK_BLK, TOK_BLK)
        kb = pltpu.bitcast(aff_ref[:, pl.ds(st, TOK_BLK)], I32)
        eqf = (kb == thr).astype(F32)
        rank = ceq + _dot(eqf, su)
        sel = ((kb > thr) | ((kb == thr) & (rank < need))).astype(F32)
        mask_ref[:, pl.ds(st, TOK_BLK)] = sel
        wsel_ref[:, pl.ds(st, TOK_BLK)] = sel * aff_ref[:, pl.ds(st, TOK_BLK)]
        pos_ref[:, pl.ds(st, TOK_BLK)] = _dot(sel, su)
        offs_ref[j] = jnp.broadcast_to(csel.astype(I32), (N_EXPERTS, 128))
        return (ceq + jnp.sum(eqf, axis=1, keepdims=True), csel + jnp.sum(sel, axis=1, keepdims=True))

    zero = jnp.zeros((N_EXPERTS, 1), F32)
    lax.fori_loop(0, nblk, blk_body, (zero, zero))


def _route(aff_t, cap):
    E, T = aff_t.shape
    nblk = T // TOK_BLK
    full = lambda shape: pl.BlockSpec(shape, lambda i: (0,) * len(shape))
    return pl.pallas_call(
        functools.partial(_route_kernel, cap=cap, nblk=nblk),
        out_shape=[jax.ShapeDtypeStruct((E, T), F32)] * 3 + [jax.ShapeDtypeStruct((nblk, E, 128), I32)],
        grid=(1,),
        in_specs=[full((E, T))],
        out_specs=[full((E, T))] * 3 + [full((nblk, E, 128))],
        compiler_params=_cparams(("arbitrary",)),
        name="route",
    )(aff_t)


def _gather_kernel(offs_ref, x_ref, m_ref, p_ref, out_ref):
    g, j = pl.program_id(0), pl.program_id(1)

    @pl.when(j == 0)
    def _():
        out_ref[...] = jnp.zeros_like(out_ref)

    W = GATHER_W
    slot0 = lax.broadcasted_iota(I32, (W, TOK_BLK), 0).astype(F32)
    for jb in range(GATHER_JB):
        blk = j * GATHER_JB + jb
        cols = slice(jb * TOK_BLK, (jb + 1) * TOK_BLK)
        targets, bases, passes = [], [], []
        for k in range(GATHER_E):
            off = offs_ref[blk * N_EXPERTS + g * GATHER_E + k]
            cnt = offs_ref[(blk + 1) * N_EXPERTS + g * GATHER_E + k] - off
            base = (off // 16) * 16
            targets.append(jnp.where(m_ref[k, :, cols] > 0.0, p_ref[k, :, cols] + (off - base).astype(F32), -1.0))
            bases.append(base)
            passes.append((off - base + cnt + W - 1) // W)
        onehot = jnp.concatenate([jnp.where(slot0 == t, 1.0, 0.0) for t in targets], axis=0).astype(BF16)
        picked = jnp.dot(onehot, x_ref[cols, :], preferred_element_type=F32).astype(BF16)
        for k in range(GATHER_E):
            out_ref[k, pl.ds(pl.multiple_of(bases[k], 16), W), :] += picked[k * W:(k + 1) * W]
        for k in range(GATHER_E):
            def more(ps, carry, k=k, cols=cols):
                hot = jnp.where(slot0 + (ps * W).astype(F32) == targets[k], 1.0, 0.0).astype(BF16)
                extra = jnp.dot(hot, x_ref[cols, :], preferred_element_type=F32).astype(BF16)
                out_ref[k, pl.ds(pl.multiple_of(bases[k] + ps * W, 16), W), :] += extra
                return carry

            lax.fori_loop(1, passes[k], more, 0)


def _gather(h2, mask3, pos3, offs, cap):
    T = h2.shape[0]
    nblk = T // TOK_BLK
    cap_x = cap + 2 * GATHER_W
    rows = GATHER_JB * TOK_BLK
    grid_spec = pltpu.PrefetchScalarGridSpec(
        num_scalar_prefetch=1,
        grid=(N_EXPERTS // GATHER_E, nblk // GATHER_JB),
        in_specs=[pl.BlockSpec((rows, D_MODEL), lambda g, j, o: (j, 0)),
                  pl.BlockSpec((GATHER_E, 1, rows), lambda g, j, o: (g, 0, j)),
                  pl.BlockSpec((GATHER_E, 1, rows), lambda g, j, o: (g, 0, j))],
        out_specs=pl.BlockSpec((GATHER_E, cap_x, D_MODEL), lambda g, j, o: (g, 0, 0),
                               pipeline_mode=pl.Buffered(1)),
    )
    return pl.pallas_call(
        _gather_kernel,
        out_shape=jax.ShapeDtypeStruct((N_EXPERTS, cap_x, D_MODEL), BF16),
        grid_spec=grid_spec,
        compiler_params=_cparams(("parallel", "arbitrary")),
        name="gather",
    )(offs, h2, mask3, pos3)


def _ffn_kernel(x_ref, wg_ref, wu_ref, wd_ref, o_ref, w_scr):
    @pl.when(pl.program_id(1) == 0)
    def _():
        for n, w_ref in enumerate((wg_ref, wu_ref, wd_ref)):
            w_scr[n] = w_ref[0, 0].astype(BF16)

    x = x_ref[0]
    g = jnp.dot(x, w_scr[0], preferred_element_type=F32)
    u = jnp.dot(x, w_scr[1], preferred_element_type=F32)
    h = (g * _sigmoid(g)) * u
    o_ref[0] = _dot(h, w_scr[2]).astype(BF16)


def _ffn(xe, wg, wu, wd, layer, cap):
    tf = min(512, cap)
    wspec = pl.BlockSpec((1, 1, D_MODEL, D_MODEL), lambda e, i: (layer, e, 0, 0))
    return pl.pallas_call(
        _ffn_kernel,
        out_shape=jax.ShapeDtypeStruct((N_EXPERTS, cap, D_MODEL), BF16),
        grid=(N_EXPERTS, cap // tf),
        in_specs=[pl.BlockSpec((1, tf, D_MODEL), lambda e, i: (e, i, 0)), wspec, wspec, wspec],
        out_specs=pl.BlockSpec((1, tf, D_MODEL), lambda e, i: (e, i, 0)),
        scratch_shapes=[pltpu.VMEM((3, D_MODEL, D_MODEL), BF16)],
        compiler_params=_cparams(("parallel", "arbitrary")),
        name="ffn",
    )(xe, wg, wu, wd)


def _slot_window_start(off, cap):
    return jnp.minimum((off // TOK_BLK) * TOK_BLK, cap - 2 * TOK_BLK)


def _scatter_kernel(offs_ref, x2_ref, w_ref, p_ref, gf_ref, *rest, cap, final):
    ye, out_ref = rest[:N_EXPERTS], rest[N_EXPERTS]
    j = pl.program_id(0)
    ps, ws = p_ref[...].T, w_ref[...].T
    offs = [offs_ref[j * N_EXPERTS + e] for e in range(N_EXPERTS)]
    cnts = [offs_ref[(j + 1) * N_EXPERTS + e] - offs[e] for e in range(N_EXPERTS)]
    rels = [offs[e] - _slot_window_start(offs[e], cap) for e in range(N_EXPERTS)]
    weights = [ws[:, e:e + 1] for e in range(N_EXPERTS)]
    SUB = 16
    most = cnts[0]
    for e in range(1, N_EXPERTS):
        most = jnp.maximum(most, cnts[e])

    def onehot(e, start, width):
        slot = lax.broadcasted_iota(I32, (TOK_BLK, width), 1).astype(F32)
        return jnp.where(slot == ps[:, e:e + 1] + (rels[e] - start).astype(F32), weights[e], 0.0).astype(BF16)

    W = SCATTER_W

    @pl.when(most <= W - SUB)
    def _():
        acc = x2_ref[...]
        for g0 in range(0, N_EXPERTS, 2):
            group = (g0, g0 + 1)
            r0 = {e: pl.multiple_of(jnp.minimum((rels[e] // SUB) * SUB, 2 * TOK_BLK - W), SUB) for e in group}
            acc = acc + jnp.dot(jnp.concatenate([onehot(e, r0[e], W) for e in group], axis=1),
                                jnp.concatenate([ye[e][0, pl.ds(r0[e], W), :] for e in group], axis=0),
                                preferred_element_type=F32)
        out_ref[...] = acc

    @pl.when(most > W - SUB)
    def _():
        out_ref[...] = x2_ref[...]
        for e in range(N_EXPERTS):
            @pl.when(cnts[e] > 0)
            def _(e=e):
                out_ref[...] += jnp.dot(onehot(e, 0, 2 * TOK_BLK), ye[e][0], preferred_element_type=F32)

    if final:
        x = out_ref[...]
        out_ref[...] = x * lax.rsqrt(jnp.mean(x * x, axis=-1, keepdims=True) + EPS) * gf_ref[...]


def _scatter(x2, wsel, pos, ye, offs, gf, cap, final):
    T = x2.shape[0]
    nblk = T // TOK_BLK
    assert cap >= 2 * TOK_BLK and cap % TOK_BLK == 0

    def ye_spec(e):
        return pl.BlockSpec((pl.Element(1), pl.Element(2 * TOK_BLK), pl.Element(D_MODEL)),
                            lambda j, o: (e, pl.multiple_of(_slot_window_start(o[j * N_EXPERTS + e], cap), TOK_BLK),
                                          0))

    row = lambda w: pl.BlockSpec((TOK_BLK, w), lambda j, o: (j, 0))
    col = pl.BlockSpec((N_EXPERTS, TOK_BLK), lambda j, o: (0, j))
    grid_spec = pltpu.PrefetchScalarGridSpec(
        num_scalar_prefetch=1,
        grid=(nblk,),
        in_specs=[row(D_MODEL), col, col, pl.BlockSpec((1, D_MODEL), lambda j, o: (0, 0))]
                 + [ye_spec(e) for e in range(N_EXPERTS)],
        out_specs=row(D_MODEL),
    )
    return pl.pallas_call(
        functools.partial(_scatter_kernel, cap=cap, final=final),
        out_shape=jax.ShapeDtypeStruct((T, D_MODEL), F32),
        grid_spec=grid_spec,
        compiler_params=_cparams(("arbitrary",)),
        name="scatter",
    )(offs, x2, wsel, pos, gf, *([ye] * N_EXPERTS))


def _block_diag(w):
    n, a, b = w.shape
    out = jnp.zeros((n * a, n * b), w.dtype)
    for i in range(n):
        out = out.at[i * a:(i + 1) * a, i * b:(i + 1) * b].set(w[i])
    return out


def _rope_tables(S):
    inv = ROPE_THETA ** (-jnp.arange(0, ROT_DIM, 2, dtype=F32) / ROT_DIM)
    ang = jnp.arange(S, dtype=F32)[:, None] * inv[None, :]
    cos, sin = jnp.cos(ang), jnp.sin(ang)
    half = ROT_DIM // 2
    ones, zeros = jnp.ones((S, ATT_HD - ROT_DIM), F32), jnp.zeros((S, ATT_HD - ROT_DIM), F32)
    zh = jnp.zeros((S, half), F32)
    c = jnp.concatenate([cos, cos, ones], axis=1)
    s1 = jnp.concatenate([-sin, zh, zeros], axis=1)
    s2 = jnp.concatenate([zh, sin, zeros], axis=1)
    return tuple(jnp.tile(t, (1, 2)) for t in (c, s1, s2))


def _expanders():
    eg = np.zeros((2, 128, GROUP_W), np.float32)
    eb = np.zeros((2, 128, GROUP_W), np.float32)
    for d in range(2):
        for h in range(DN_HEADS):
            eg[d, d * DN_HEADS + h, h * DN_DK:(h + 1) * DN_DK] = 1.0
            eb[d, 2 * DN_HEADS + d * DN_HEADS + h, h * DN_DK:(h + 1) * DN_DK] = 1.0
    hd = np.arange(GROUP_W) // DN_DK
    bd = (hd[:, None] == hd[None, :]).astype(np.float32)
    return jnp.asarray(eg), jnp.asarray(eb), jnp.asarray(bd)


def _layer_params(l, norm1_g, w_in, conv_a_w, conv_a_b, rg_wa, rg_ba, rg_wx, rg_bx, rg_lambda, pool_w, pool_scale,
                  dn_conv_w, dn_A_log, dn_dt_bias, dn_norm_g, mix_norm_g, w_out, norm2_g, router_w,
                  exp_w_gate, exp_w_up, exp_w_down):
    w = w_in[l]
    w_cat = jnp.concatenate([w[:, 0:1536], w[:, 1552:2576], w[:, 1536:1552],
                             jnp.zeros((D_MODEL, COL_END - COL_CAB - 16), F32)], axis=1).astype(BF16)
    pad8 = lambda v: jnp.concatenate([v.reshape(1, 2 * DN_HEADS), jnp.zeros((1, 120), F32)], axis=1)
    return dict(
        g1=norm1_g[l].reshape(1, D_MODEL), w_cat=w_cat,
        conv_a_w=conv_a_w[l], conv_a_b=conv_a_b[l].reshape(1, GROUP_W),
        wa=[_block_diag(rg_wa[l, d]).astype(BF16) for d in range(2)],
        wx=[_block_diag(rg_wx[l, d]).astype(BF16) for d in range(2)],
        ba=[rg_ba[l, d].reshape(1, GROUP_W) for d in range(2)],
        bx=[rg_bx[l, d].reshape(1, GROUP_W) for d in range(2)],
        lam=[rg_lambda[l, d].reshape(1, GROUP_W) for d in range(2)],
        pool_w=_block_diag(pool_w[l]).astype(BF16), pool_scale=pool_scale[l].reshape(1, GROUP_W),
        dn_conv_w=dn_conv_w[l], alog=pad8(dn_A_log[l]), dtb=pad8(dn_dt_bias[l]),
        dng=jnp.tile(dn_norm_g[l], DN_HEADS).reshape(1, GROUP_W),
        mg=mix_norm_g[l].reshape(1, D_MODEL), wo=w_out[l].astype(BF16),
        g2=norm2_g[l].reshape(1, D_MODEL),
        rw=jnp.concatenate(_split(router_w[l], 2) + [jnp.zeros((D_MODEL, 128 - 2 * N_EXPERTS), BF16)], axis=1),
        wg=exp_w_gate, wu=exp_w_up, wd=exp_w_down,
    )


def _chunk_rows(gcn, d):
    B, S, _ = gcn.shape
    n = S // DN_CHUNK
    g = gcn[:, :, d * DN_HEADS:(d + 1) * DN_HEADS].reshape(B, n, DN_CHUNK, DN_HEADS)
    return g.transpose(0, 1, 3, 2).reshape(B, n, 1, GROUP_W)


def _encoder(x, layers, final_g, consts):
    B, S, D = x.shape
    T = B * S
    cap = max(1, EC_CAPACITY * T // N_EXPERTS)
    eg, eb, bdmask = consts
    rope_c, rope_s1, rope_s2 = _rope_tables(S)
    xt = x.reshape(T, D)
    for l, p in enumerate(layers):
        pa, pb, pq, pg, pd, pcab = _inproj(xt, p["g1"], p["w_cat"], rope_c, rope_s1, rope_s2, S)
        pa3 = pa.reshape(B, S, 2 * GROUP_W)
        rg = lambda d: (p["conv_a_w"], p["conv_a_b"], p["wa"][d], p["ba"][d], p["wx"][d], p["bx"][d], p["lam"][d])
        hf = _rglru(pa3, None, *rg(0), reverse=False)
        ya = _rglru(pa3, hf, *rg(1), reverse=True)
        yb = _pool(pb.reshape(B, S, GROUP_W), p["pool_w"], p["pool_scale"])
        qn, kn, vv, gcf, gcb, bef, beb, gcn = _dnprep(pq.reshape(B, S, 3 * GROUP_W), pcab.reshape(B, S, 128),
                                                      p["dn_conv_w"], p["alog"], p["dtb"], eg, eb)
        of, ob = _delta(qn, kn, vv, gcf, gcb, bef, beb, _chunk_rows(gcn, 0), _chunk_rows(gcn, 1), bdmask)
        yd = _attention(pd.reshape(B, S, 3 * GROUP_W))
        flat = lambda a: a.reshape(T, GROUP_W)
        x2, h2, aff_t = _outproj(flat(ya), flat(yb), flat(of), flat(ob), pg, flat(yd), xt,
                                 p["dng"], p["mg"], p["wo"], p["g2"], p["rw"])
        mask, pos, wsel, offs3 = _route(aff_t, cap)
        offs = jnp.concatenate([offs3[:, :, 0].reshape(-1), jnp.full((N_EXPERTS,), cap, I32)])
        xe = _gather(h2, mask.reshape(N_EXPERTS, 1, T), pos.reshape(N_EXPERTS, 1, T), offs, cap)
        ye = _ffn(xe, p["wg"], p["wu"], p["wd"], l, cap)
        xt = _scatter(x2, wsel, pos, ye, offs, final_g, cap, final=(l == len(layers) - 1))
    return xt.reshape(B, S, D)


def kernel(x_prompt, x_sample, norm1_g, w_in, conv_a_w, conv_a_b, rg_wa, rg_ba, rg_wx, rg_bx, rg_lambda, pool_w, pool_scale, dn_conv_w, dn_A_log, dn_dt_bias, dn_norm_g, mix_norm_g, w_out, norm2_g, router_w, exp_w_gate, exp_w_up, exp_w_down, final_norm_g):
    layers = [_layer_params(l, norm1_g, w_in, conv_a_w, conv_a_b, rg_wa, rg_ba, rg_wx, rg_bx, rg_lambda, pool_w,
                            pool_scale, dn_conv_w, dn_A_log, dn_dt_bias, dn_norm_g, mix_norm_g, w_out, norm2_g,
                            router_w, exp_w_gate, exp_w_up, exp_w_down) for l in range(DEPTH)]
    consts = _expanders()
    final_g = final_norm_g.reshape(1, D_MODEL)
    return (_encoder(x_prompt, layers, final_g, consts), _encoder(x_sample, layers, final_g, consts))
```

```python
import functools
import math
import numpy as np
import jax
import jax.numpy as jnp
from jax import lax
from jax.experimental import pallas as pl
from jax.experimental.pallas import tpu as pltpu

F32, BF16, I32 = jnp.float32, jnp.bfloat16, jnp.int32

D_MODEL = 1024
DEPTH = 2
GROUP_W = 256
RG_C = 8.0
POOL_WINDOWS = (2, 4, 8, 16)
DN_HEADS = 4
DN_DK = 64
DN_CHUNK = 64
ATT_HD = 64
ROT_DIM = 16
ROPE_THETA = 500000.0
ATT_WINDOWS = (128, 512, 2048)
ATT_DILATIONS = (1, 4, 16)
N_EXPERTS = 16
EC_CAPACITY = 2
EPS = 1e-6
NEG = -1e30

COL_A, COL_B, COL_CQ, COL_CG, COL_D, COL_CAB, COL_END = 0, 512, 768, 1536, 1792, 2560, 2688

ROW_TILE = 512
HALO = 8
DELTA_CB = 8
ATT_QB = 128
ATT_UNROLL = 8
TOK_BLK = 256
GATHER_W = 64
GATHER_JB = 8
GATHER_E = 4
SCATTER_W = 128
VMEM_LIMIT = 56 * 1024 * 1024


def _cparams(sem):
    return pltpu.CompilerParams(dimension_semantics=sem, vmem_limit_bytes=VMEM_LIMIT)


def _dot(a, b):
    return jnp.dot(a.astype(BF16), b.astype(BF16), preferred_element_type=F32)


def _split(a, pieces):
    out, rem = [], a
    for i in range(pieces):
        t = rem.astype(BF16)
        out.append(t)
        if i + 1 < pieces:
            rem = rem - t.astype(F32)
    return out


def _dot_sel(a, sel, pieces=3):
    sel = sel.astype(BF16)
    return sum(jnp.dot(t, sel, preferred_element_type=F32) for t in _split(a, pieces))


def _sel_dot(sel, a, pieces=3):
    sel = sel.astype(BF16)
    return sum(jnp.dot(sel, t, preferred_element_type=F32) for t in _split(a, pieces))


def _dot_nt(a, b):
    return lax.dot_general(a.astype(BF16), b.astype(BF16), (((1,), (1,)), ((), ())), preferred_element_type=F32)


def _dot_tn(a, b):
    return lax.dot_general(a.astype(BF16), b.astype(BF16), (((0,), (0,)), ((), ())), preferred_element_type=F32)


def _sigmoid(x):
    return 1.0 / (1.0 + jnp.exp(-x))


def _softplus(x):
    return jnp.maximum(x, 0.0) + jnp.log1p(jnp.exp(-jnp.abs(x)))


def _shift_rows(e, k):
    n = e.shape[0]
    return e if k % n == 0 else pltpu.roll(e, (-k) % n, axis=0)


def _with_halo(cur_ref, prev_ref, next_ref, first, last):
    prev = jnp.where(first, 0.0, prev_ref[0])
    nxt = jnp.where(last, 0.0, next_ref[0])
    return jnp.concatenate([prev, cur_ref[0], nxt], axis=0)


def _halo_specs(ts, width, S, blk_of):
    per = ts // HALO
    last = S // HALO - 1
    cur = pl.BlockSpec((1, ts, width), lambda b, c: (b, blk_of(c), 0))
    prev = pl.BlockSpec((1, HALO, width), lambda b, c: (b, jnp.maximum(blk_of(c) * per - 1, 0), 0))
    nxt = pl.BlockSpec((1, HALO, width), lambda b, c: (b, jnp.minimum((blk_of(c) + 1) * per, last), 0))
    return cur, prev, nxt


def _inproj_kernel(x_ref, g_ref, w_ref, c_ref, s1_ref, s2_ref, pa_ref, pb_ref, pq_ref, pg_ref, pd_ref, pcab_ref):
    x = x_ref[...]
    h = x * lax.rsqrt(jnp.mean(x * x, axis=-1, keepdims=True) + EPS) * g_ref[...]
    hb = h.astype(BF16)

    def mm(lo, hi):
        return jnp.dot(hb, w_ref[:, lo:hi], preferred_element_type=F32)

    pa_ref[...] = mm(COL_A, COL_B)
    pb_ref[...] = mm(COL_B, COL_CQ)
    pq_ref[...] = mm(COL_CQ, COL_CG)
    pg_ref[...] = mm(COL_CG, COL_D)
    c, s1, s2 = c_ref[...], s1_ref[...], s2_ref[...]
    for part in range(2):
        yy = mm(COL_D + 256 * part, COL_D + 256 * part + 256)
        for half in range(2):
            y = yy[:, 128 * half:128 * half + 128]
            y = y * c + pltpu.roll(y, 128 - ROT_DIM // 2, axis=1) * s1 + pltpu.roll(y, ROT_DIM // 2, axis=1) * s2
            if part == 0:
                y = y * (ATT_HD ** -0.5)
            pd_ref[:, 256 * part + 128 * half:256 * part + 128 * half + 128] = y
    vcab = mm(COL_D + 512, COL_END)
    pd_ref[:, 512:768] = vcab[:, 0:256]
    pcab_ref[...] = vcab[:, 256:384]


def _inproj(xt, g1, w_cat, rope_c, rope_s1, rope_s2, S):
    T = xt.shape[0]
    tm = min(ROW_TILE, S)
    per_seq = S // tm
    widths = (512, 256, 768, 256, 768, 128)
    row = lambda w: pl.BlockSpec((tm, w), lambda i: (i, 0))
    rope = pl.BlockSpec((tm, 128), lambda i: (i % per_seq, 0))
    return pl.pallas_call(
        _inproj_kernel,
        out_shape=[jax.ShapeDtypeStruct((T, w), F32) for w in widths],
        grid=(T // tm,),
        in_specs=[row(D_MODEL), pl.BlockSpec((1, D_MODEL), lambda i: (0, 0)),
                  pl.BlockSpec((D_MODEL, COL_END), lambda i: (0, 0)), rope, rope, rope],
        out_specs=[row(w) for w in widths],
        compiler_params=_cparams(("parallel",)),
        name="inproj",
    )(xt, g1, w_cat, rope_c, rope_s1, rope_s2)


def _rglru_kernel(*refs, reverse, ts, nc):
    if reverse:
        (cur_ref, prev_ref, next_ref, gate_ref, hf_ref, cw_ref, cb_ref, wa_ref, ba_ref, wx_ref, bx_ref, lam_ref,
         out_ref, a_ref, b_ref, carry_ref) = refs
    else:
        (cur_ref, prev_ref, next_ref, cw_ref, cb_ref, wa_ref, ba_ref, wx_ref, bx_ref, lam_ref,
         out_ref, a_ref, b_ref, carry_ref) = refs
    c = pl.program_id(1)
    blk = (nc - 1 - c) if reverse else c
    e = _with_halo(cur_ref, prev_ref, next_ref, blk == 0, blk == nc - 1)
    cw = cw_ref[...]
    sl = slice(HALO, HALO + ts)
    u = (cw[0:1] * _shift_rows(e, -2)[sl] + cw[1:2] * _shift_rows(e, -1)[sl] + cw[2:3] * e[sl]
         + cw[3:4] * _shift_rows(e, 1)[sl]) + cb_ref[...]
    r = _sigmoid(_dot(u, wa_ref[...]) + ba_ref[...])
    i = _sigmoid(_dot(u, wx_ref[...]) + bx_ref[...])
    log_a = -RG_C * r * _softplus(-lam_ref[...])
    a_ref[...] = jnp.exp(log_a)
    b_ref[...] = jnp.sqrt(1.0 - jnp.exp(2.0 * log_a)) * (i * u)

    @pl.when(c == 0)
    def _():
        carry_ref[...] = jnp.zeros_like(carry_ref)

    row = lax.broadcasted_iota(I32, (HALO, GROUP_W), 0)
    nt = ts // HALO

    def body(it, carry):
        ti = (nt - 1 - it) if reverse else it
        st = pl.multiple_of(ti * HALO, HALO)
        a = a_ref[pl.ds(st, HALO), :]
        b = b_ref[pl.ds(st, HALO), :]
        for d in (1, 2, 4):
            k = d if reverse else -d
            valid = (row < HALO - d) if reverse else (row >= d)
            b = jnp.where(valid, a * _shift_rows(b, k) + b, b)
            a = jnp.where(valid, a * _shift_rows(a, k), a)
        h = a * carry + b
        if reverse:
            g = gate_ref[0, pl.ds(st, HALO), :]
            cdf = 0.5 * (1.0 + jnp.tanh(np.float32(np.sqrt(2.0 / np.pi)) * (g + 0.044715 * (g * g * g))))
            out_ref[0, pl.ds(st, HALO), :] = (g * cdf) * (hf_ref[0, pl.ds(st, HALO), :] + h)
            return jnp.broadcast_to(h[0:1], h.shape)
        out_ref[0, pl.ds(st, HALO), :] = h
        return jnp.broadcast_to(h[HALO - 1:HALO], h.shape)

    carry_ref[...] = lax.fori_loop(0, nt, body, carry_ref[...], unroll=4)


def _rglru(pa3, hf, cw, cb, wa, ba, wx, bx, lam, reverse):
    B, S, _ = pa3.shape
    ts = min(ROW_TILE, S)
    nc = S // ts
    blk_of = (lambda c: nc - 1 - c) if reverse else (lambda c: c)
    cur, prev, nxt = _halo_specs(ts, GROUP_W, S, blk_of)
    tile = pl.BlockSpec((1, ts, GROUP_W), lambda b, c: (b, blk_of(c), 0))
    const = lambda shape: pl.BlockSpec(shape, lambda b, c: (0,) * len(shape))
    in_specs = [cur, prev, nxt]
    args = [pa3, pa3, pa3]
    if reverse:
        in_specs += [pl.BlockSpec((1, ts, GROUP_W), lambda b, c: (b, blk_of(c), 1)), tile]
        args += [pa3, hf]
    in_specs += [const((4, GROUP_W)), const((1, GROUP_W)), const((GROUP_W, GROUP_W)), const((1, GROUP_W)),
                 const((GROUP_W, GROUP_W)), const((1, GROUP_W)), const((1, GROUP_W))]
    args += [cw, cb, wa, ba, wx, bx, lam]
    return pl.pallas_call(
        functools.partial(_rglru_kernel, reverse=reverse, ts=ts, nc=nc),
        out_shape=jax.ShapeDtypeStruct((B, S, GROUP_W), F32),
        grid=(B, nc),
        in_specs=in_specs,
        out_specs=tile,
        scratch_shapes=[pltpu.VMEM((ts, GROUP_W), F32), pltpu.VMEM((ts, GROUP_W), F32),
                        pltpu.VMEM((HALO, GROUP_W), F32)],
        compiler_params=_cparams(("parallel", "arbitrary")),
        name="rglru_bwd" if reverse else "rglru_fwd",
    )(*args)


def _pool_kernel(cur_ref, prev_ref, next_ref, w_ref, sc_ref, out_ref, *, ts, nc, S):
    c = pl.program_id(1)
    e = _with_halo(cur_ref, prev_ref, next_ref, c == 0, c == nc - 1)
    sl = slice(HALO, HALO + ts)
    a2 = e + _shift_rows(e, 1)
    a4 = a2 + _shift_rows(a2, 2)
    a8 = a4 + _shift_rows(a4, 4)
    a16 = a8 + _shift_rows(a8, 8)
    sums = [_shift_rows(a, -(w // 2))[sl] for a, w in zip((a2, a4, a8, a16), POOL_WINDOWS)]
    gi = lax.broadcasted_iota(I32, (ts, GROUP_W), 1) // (GROUP_W // len(POOL_WINDOWS))
    ssum = jnp.where(gi == 0, sums[0], jnp.where(gi == 1, sums[1], jnp.where(gi == 2, sums[2], sums[3])))
    hw = jnp.where(gi == 0, 1, jnp.where(gi == 1, 2, jnp.where(gi == 2, 4, 8)))
    t = c * ts + lax.broadcasted_iota(I32, (ts, GROUP_W), 0)
    cnt = (jnp.minimum(t + hw, S) - jnp.maximum(t - hw, 0)).astype(F32)
    p = ssum / cnt - e[sl]
    out_ref[0] = _dot(p, w_ref[...]) * sc_ref[...]


def _pool(pb3, w_bd, scale):
    B, S, _ = pb3.shape
    ts = min(ROW_TILE, S)
    nc = S // ts
    cur, prev, nxt = _halo_specs(ts, GROUP_W, S, lambda c: c)
    return pl.pallas_call(
        functools.partial(_pool_kernel, ts=ts, nc=nc, S=S),
        out_shape=jax.ShapeDtypeStruct((B, S, GROUP_W), F32),
        grid=(B, nc),
        in_specs=[cur, prev, nxt, pl.BlockSpec((GROUP_W, GROUP_W), lambda b, c: (0, 0)),
                  pl.BlockSpec((1, GROUP_W), lambda b, c: (0, 0))],
        out_specs=pl.BlockSpec((1, ts, GROUP_W), lambda b, c: (b, c, 0)),
        compiler_params=_cparams(("parallel", "parallel")),
        name="pool",
    )(pb3, pb3, pb3, w_bd, scale)


def _head_sum_matrix(n, group):
    r = lax.broadcasted_iota(I32, (n, n), 0) // group
    c = lax.broadcasted_iota(I32, (n, n), 1) // group
    return (r == c).astype(F32)


def _dnprep_kernel(cur_ref, prev_ref, next_ref, cab_ref, cw_ref, alog_ref, dtb_ref, eg_ref, eb_ref,
                   q_ref, k_ref, v_ref, gcf_ref, gcb_ref, bef_ref, beb_ref, gcn_ref, *, ts, nc):
    c = pl.program_id(1)
    e = _with_halo(cur_ref, prev_ref, next_ref, c == 0, c == nc - 1)
    cw = cw_ref[...]
    sl = slice(HALO, HALO + ts)
    y = (cw[0:1] * _shift_rows(e, -2)[sl] + cw[1:2] * _shift_rows(e, -1)[sl] + cw[2:3] * e[sl]
         + cw[3:4] * _shift_rows(e, 1)[sl])
    y = y * _sigmoid(y)
    q, k = y[:, 0:GROUP_W], y[:, GROUP_W:2 * GROUP_W]
    hs = _head_sum_matrix(GROUP_W, DN_DK)
    q_ref[0] = q * lax.rsqrt(_dot_sel(q * q, hs, 2) + EPS) * (DN_DK ** -0.5)
    k_ref[0] = k * lax.rsqrt(_dot_sel(k * k, hs, 2) + EPS)
    v_ref[0] = y[:, 2 * GROUP_W:3 * GROUP_W]
    cab = cab_ref[0]
    g = -jnp.exp(alog_ref[...]) * _softplus(cab + dtb_ref[...])
    beta = _sigmoid(cab)
    bef_ref[0] = _dot_sel(beta, eb_ref[0])
    beb_ref[0] = _dot_sel(beta, eb_ref[1])
    r = lax.broadcasted_iota(I32, (ts, ts), 0)
    cc = lax.broadcasted_iota(I32, (ts, ts), 1)
    same = (r // DN_CHUNK) == (cc // DN_CHUNK)
    lane = lax.broadcasted_iota(I32, (ts, 128), 1)
    gcn = jnp.where(lane < DN_HEADS, _sel_dot(same & (r >= cc), g), _sel_dot(same & (r <= cc), g))
    gcn_ref[0] = gcn
    gcf_ref[0] = _dot_sel(gcn, eg_ref[0])
    gcb_ref[0] = _dot_sel(gcn, eg_ref[1])


def _dnprep(pq3, pcab3, cw, alog_row, dtb_row, eg, eb):
    B, S, _ = pq3.shape
    ts = min(ROW_TILE, S)
    nc = S // ts
    cur, prev, nxt = _halo_specs(ts, 3 * GROUP_W, S, lambda c: c)
    tile = pl.BlockSpec((1, ts, GROUP_W), lambda b, c: (b, c, 0))
    const = lambda shape: pl.BlockSpec(shape, lambda b, c: (0,) * len(shape))
    return pl.pallas_call(
        functools.partial(_dnprep_kernel, ts=ts, nc=nc),
        out_shape=[jax.ShapeDtypeStruct((B, S, GROUP_W), F32)] * 7 + [jax.ShapeDtypeStruct((B, S, 128), F32)],
        grid=(B, nc),
        in_specs=[cur, prev, nxt, pl.BlockSpec((1, ts, 128), lambda b, c: (b, c, 0)),
                  const((4, 3 * GROUP_W)), const((1, 128)), const((1, 128)),
                  const((2, 128, GROUP_W)), const((2, 128, GROUP_W))],
        out_specs=[tile] * 7 + [pl.BlockSpec((1, ts, 128), lambda b, c: (b, c, 0))],
        compiler_params=_cparams(("parallel", "parallel")),
        name="dnprep",
    )(pq3, pq3, pq3, pcab3, cw, alog_row, dtb_row, eg, eb)


def _delta_kernel(qf, kf, vf, gcf, bef, grf, qb, kb, vb, gcb, beb, grb, bd_ref, of_ref, ob_ref,
                  s_ref, p_scr, x_scr, wq_scr, u_scr, at_scr, kd_scr, vb_scr, kg_scr, *, cb):
    j = pl.program_id(1)

    @pl.when(j == 0)
    def _():
        s_ref[...] = jnp.zeros_like(s_ref)

    C = DN_CHUNK
    bd = bd_ref[...] > 0.0
    c_idx = lax.broadcasted_iota(I32, (C, GROUP_W), 0)
    m_idx = lax.broadcasted_iota(I32, (C, GROUP_W), 1) % C
    eye = (c_idx == m_idx).astype(F32)

    def blockdiag(x):
        return jnp.where(bd, jnp.tile(x.astype(BF16), (DN_HEADS, 1)), jnp.zeros((), BF16))

    dirs = ((qf, kf, vf, gcf, bef, grf, of_ref), (qb, kb, vb, gcb, beb, grb, ob_ref))
    units = [(d, ci) for ci in range(cb) for d in range(2)]

    def chunk_of(d, ci):
        return cb - 1 - ci if d == 1 else ci

    def prepare(ui):
        d, ci = units[ui]
        q_r, k_r, v_r, gc_r, be_r, gr_r, _ = dirs[d]
        rev = d == 1
        tril = (c_idx <= m_idx) if rev else (c_idx >= m_idx)
        strict = (c_idx < m_idx) if rev else (c_idx > m_idx)
        last = 0 if rev else C - 1
        cc = chunk_of(d, ci)
        rows = slice(cc * C, (cc + 1) * C)
        q, k, v = q_r[0, rows, :], k_r[0, rows, :], v_r[0, rows, :]
        gc, be, gr = gc_r[0, rows, :], be_r[0, rows, :], gr_r[0, cc]
        eg = jnp.exp(gc)
        kbeta = k * be
        kkqk = _dot_nt(jnp.concatenate([kbeta, q], axis=0), blockdiag(k))
        decay = jnp.exp(jnp.where(tril, gc - gr, -jnp.inf))
        at_scr[ui] = (kkqk[C:] * decay).astype(BF16)
        p = -jnp.where(strict, kkqk[:C] * decay, 0.0)
        p_scr[ui] = p.astype(BF16)
        x_scr[ui] = eye + p
        wq_scr[ui, C:2 * C, :] = (q * eg).astype(BF16)
        kd_scr[ui] = (k * jnp.exp(gc[last:last + 1] - gc)).astype(BF16)
        vb_scr[ui] = (v * be).astype(BF16)
        kg_scr[ui] = (kbeta * eg).astype(BF16)

    def solve(group):
        for ui in group:
            p = p_scr[ui]
            p_scr[ui] = _dot(p, blockdiag(p)).astype(BF16)
        for _ in range(4):
            for ui in group:
                p, x = p_scr[ui], x_scr[ui]
                px = _dot(jnp.concatenate([p, x.astype(BF16)], axis=0), blockdiag(p))
                p_scr[ui] = px[:C].astype(BF16)
                x_scr[ui] = x + px[C:]
        for ui in group:
            x = x_scr[ui]
            x = x + _dot(x, blockdiag(p_scr[ui]))
            u_scr[ui] = _dot(x, blockdiag(vb_scr[ui]))
            wq_scr[ui, 0:C, :] = _dot(x, blockdiag(kg_scr[ui])).astype(BF16)

    def recur(ui):
        d, ci = units[ui]
        gc_r, o_r = dirs[d][3], dirs[d][6]
        cc = chunk_of(d, ci)
        last = cc * C + (0 if d == 1 else C - 1)
        state = s_ref[d]
        ws_qs = _dot(wq_scr[ui], state)
        v_new = u_scr[ui] - ws_qs[:C]
        o_r[0, cc * C:(cc + 1) * C, :] = ws_qs[C:] + _dot(at_scr[ui], blockdiag(v_new))
        s_ref[d] = (state * jnp.exp(gc_r[0, last:last + 1, :])
                    + jnp.where(bd, _dot_tn(kd_scr[ui], v_new), 0.0))

    for ui in range(len(units)):
        prepare(ui)
    solve(range(len(units)))
    for ui in range(len(units)):
        recur(ui)


def _delta(qn, kn, vv, gcf, gcb, bef, beb, grf, grb, bdmask):
    B, S, _ = qn.shape
    cb = DELTA_CB
    rb = cb * DN_CHUNK
    nb = S // rb
    f = lambda b, j: (b, j, 0)
    r = lambda b, j: (b, nb - 1 - j, 0)
    tf = pl.BlockSpec((1, rb, GROUP_W), f)
    tr = pl.BlockSpec((1, rb, GROUP_W), r)
    gf = pl.BlockSpec((1, cb, 1, GROUP_W), lambda b, j: (b, j, 0, 0))
    gr = pl.BlockSpec((1, cb, 1, GROUP_W), lambda b, j: (b, nb - 1 - j, 0, 0))
    return pl.pallas_call(
        functools.partial(_delta_kernel, cb=cb),
        out_shape=[jax.ShapeDtypeStruct((B, S, GROUP_W), F32)] * 2,
        grid=(B, nb),
        in_specs=[tf, tf, tf, tf, tf, gf, tr, tr, tr, tr, tr, gr,
                  pl.BlockSpec((GROUP_W, GROUP_W), lambda b, j: (0, 0))],
        out_specs=[tf, tr],
        scratch_shapes=[pltpu.VMEM((2, GROUP_W, GROUP_W), F32)]
                       + [pltpu.VMEM((2 * cb, n * DN_CHUNK, GROUP_W), dt) for n, dt in
                          ((1, BF16), (1, F32), (2, BF16), (1, F32), (1, BF16), (1, BF16), (1, BF16), (1, BF16))],
        compiler_params=_cparams(("parallel", "arbitrary")),
        name="delta",
    )(qn, kn, vv, gcf, bef, grf, qn, kn, vv, gcb, beb, grb, bdmask)


def _attn_kernel(q_ref, k_ref, v_ref, out_ref, m_ref, l_ref, o_ref, *, S):
    head0 = lax.broadcasted_iota(I32, (1, 128), 1) < ATT_HD

    def rows(start, n, dil):
        return pl.ds(start, n) if dil == 1 else pl.ds(start, n, stride=dil)

    for bi, (win, dil) in enumerate(zip(ATT_WINDOWS, ATT_DILATIONS)):
        half = win // (2 * dil)
        L = S // dil
        qb_n = min(ATT_QB, L)
        kw = min(L, qb_n + 2 * half)
        nqb = L // qb_n

        def block_stats(idx, dil=dil, half=half, L=L, qb_n=qb_n, kw=kw, nqb=nqb):
            r = idx // nqb
            m0 = (idx % nqb) * qb_n
            ks = jnp.clip(m0 - half, 0, L - kw)
            qsel = rows(r + m0 * dil, qb_n, dil)
            ksel = rows(r + ks * dil, kw, dil)
            q = q_ref[0, qsel, :]
            kk = k_ref[0, ksel, :].astype(BF16)
            vv = v_ref[0, ksel, :].astype(BF16)
            rel = (lax.broadcasted_iota(I32, (qb_n, kw), 1) - lax.broadcasted_iota(I32, (qb_n, kw), 0)
                   + (ks - m0 + half))
            valid = rel.astype(jnp.uint32) <= 2 * half
            q2 = jnp.concatenate([jnp.where(head0, q, 0.0), jnp.where(head0, 0.0, q)], axis=0)
            s = jnp.where(jnp.concatenate([valid, valid], axis=0), _dot_nt(q2, kk), NEG)
            m = jnp.max(s, axis=-1, keepdims=True)
            p = jnp.exp(s - m)
            l = jnp.sum(p, axis=-1, keepdims=True)
            o = _dot(p, vv)
            return (qsel, jnp.where(head0, m[:qb_n], m[qb_n:]), jnp.where(head0, l[:qb_n], l[qb_n:]),
                    jnp.where(head0, o[:qb_n], o[qb_n:]))

        unroll = math.gcd(ATT_UNROLL, dil * nqb)

        def body(it, carry, bi=bi, unroll=unroll):
            stats = [block_stats(it * unroll + s) for s in range(unroll)]
            if bi > 0:
                olds = [(m_ref[qsel, :], l_ref[qsel, :], o_ref[qsel, :]) for qsel, _, _, _ in stats]
                merged = []
                for (qsel, m, l, o), (m_old, l_old, o_old) in zip(stats, olds):
                    m_new = jnp.maximum(m_old, m)
                    w_old, w_cur = jnp.exp(m_old - m_new), jnp.exp(m - m_new)
                    merged.append((qsel, m_new, w_old * l_old + w_cur * l, w_old * o_old + w_cur * o))
                stats = merged
            for qsel, m, l, o in stats:
                if bi == len(ATT_WINDOWS) - 1:
                    out_ref[0, qsel, :] = o / l
                else:
                    m_ref[qsel, :] = m
                    l_ref[qsel, :] = l
                    o_ref[qsel, :] = o
            return carry

        lax.fori_loop(0, dil * nqb // unroll, body, 0)


def _attention(pd3):
    B, S, _ = pd3.shape
    spec = lambda off: pl.BlockSpec((1, S, 128), lambda b, p: (b, 0, off + p))
    return pl.pallas_call(
        functools.partial(_attn_kernel, S=S),
        out_shape=jax.ShapeDtypeStruct((B, S, GROUP_W), F32),
        grid=(B, 2),
        in_specs=[spec(0), spec(2), spec(4)],
        out_specs=pl.BlockSpec((1, S, 128), lambda b, p: (b, 0, p)),
        scratch_shapes=[pltpu.VMEM((S, 128), F32)] * 3,
        compiler_params=_cparams(("parallel", "parallel")),
        name="attention",
    )(pd3, pd3, pd3)


def _outproj_kernel(ya_ref, yb_ref, of_ref, ob_ref, cg_ref, yd_ref, x_ref, dng_ref, mg_ref, wo_ref, g2_ref, rw_ref,
                    x2_ref, h2_ref, aff_ref):
    o = of_ref[...] + ob_ref[...]
    ms = _dot_sel(o * o, _head_sum_matrix(GROUP_W, DN_DK), 2) * (1.0 / DN_DK)
    cg = cg_ref[...]
    yc = (o * lax.rsqrt(ms + EPS) * dng_ref[...]) * (cg * _sigmoid(cg))
    acc = x_ref[...]
    for gi, y in enumerate((ya_ref[...], yb_ref[...], yc, yd_ref[...])):
        sl = slice(gi * GROUP_W, (gi + 1) * GROUP_W)
        mix = y * lax.rsqrt(jnp.mean(y * y, axis=-1, keepdims=True) + EPS) * mg_ref[:, sl]
        acc = acc + _dot(mix, wo_ref[sl, :])
    x2_ref[...] = acc
    h2 = acc * lax.rsqrt(jnp.mean(acc * acc, axis=-1, keepdims=True) + EPS) * g2_ref[...]
    h2_ref[...] = h2.astype(BF16)
    h_hi, h_lo = _split(h2, 2)
    z = _dot_nt(rw_ref[...], h_hi) + _dot_nt(rw_ref[...], h_lo)
    logits = z[0:N_EXPERTS] + z[N_EXPERTS:2 * N_EXPERTS]
    ex = jnp.exp(logits - jnp.max(logits, axis=0, keepdims=True))
    aff_ref[...] = ex / jnp.sum(ex, axis=0, keepdims=True)


def _outproj(ya, yb, of, ob, cg, yd, xt, dng, mg, wo, g2, rw):
    T = xt.shape[0]
    tm = ROW_TILE
    row = lambda w: pl.BlockSpec((tm, w), lambda i: (i, 0))
    const = lambda shape: pl.BlockSpec(shape, lambda i: (0, 0))
    return pl.pallas_call(
        _outproj_kernel,
        out_shape=[jax.ShapeDtypeStruct((T, D_MODEL), F32), jax.ShapeDtypeStruct((T, D_MODEL), BF16),
                   jax.ShapeDtypeStruct((N_EXPERTS, T), F32)],
        grid=(T // tm,),
        in_specs=[row(GROUP_W)] * 6 + [row(D_MODEL), const((1, GROUP_W)), const((1, D_MODEL)),
                                       const((D_MODEL, D_MODEL)), const((1, D_MODEL)),
                                       const((2 * N_EXPERTS, D_MODEL))],
        out_specs=[row(D_MODEL), row(D_MODEL), pl.BlockSpec((N_EXPERTS, tm), lambda i: (0, i))],
        compiler_params=_cparams(("parallel",)),
        name="outproj",
    )(ya, yb, of, ob, cg, yd, xt, dng, mg, wo, g2, rw)


def _strict_upper(n):
    return (lax.broadcasted_iota(I32, (n, n), 0) < lax.broadcasted_iota(I32, (n, n), 1)).astype(BF16)


def _route_kernel(aff_ref, mask_ref, pos_ref, wsel_ref, offs_ref, *, cap, nblk):
    keys = pltpu.bitcast(aff_ref[...], I32)

    def bit_body(i, thr):
        cand = thr | lax.shift_left(jnp.int32(1), 30 - i)
        cnt = jnp.sum((keys >= cand).astype(F32), axis=1, keepdims=True)
        return jnp.where(cnt >= cap, cand, thr)

    thr = lax.fori_loop(0, 31, bit_body, jnp.zeros((N_EXPERTS, 1), I32))
    need = cap - jnp.sum((keys > thr).astype(F32), axis=1, keepdims=True)
    su = _strict_upper(TOK_BLK)

    def blk_body(j, carry):
        ceq, csel = carry
        st = pl.multiple_of(j * TOK_BLK, TOK_BLK)
        kb = pltpu.bitcast(aff_ref[:, pl.ds(st, TOK_BLK)], I32)
        eqf = (kb == thr).astype(F32)
        rank = ceq + _dot(eqf, su)
        sel = ((kb > thr) | ((kb == thr) & (rank < need))).astype(F32)
        mask_ref[:, pl.ds(st, TOK_BLK)] = sel
        wsel_ref[:, pl.ds(st, TOK_BLK)] = sel * aff_ref[:, pl.ds(st, TOK_BLK)]
        pos_ref[:, pl.ds(st, TOK_BLK)] = _dot(sel, su)
        offs_ref[j] = jnp.broadcast_to(csel.astype(I32), (N_EXPERTS, 128))
        return (ceq + jnp.sum(eqf, axis=1, keepdims=True), csel + jnp.sum(sel, axis=1, keepdims=True))

    zero = jnp.zeros((N_EXPERTS, 1), F32)
    lax.fori_loop(0, nblk, blk_body, (zero, zero))


def _route(aff_t, cap):
    E, T = aff_t.shape
    nblk = T // TOK_BLK
    full = lambda shape: pl.BlockSpec(shape, lambda i: (0,) * len(shape))
    return pl.pallas_call(
        functools.partial(_route_kernel, cap=cap, nblk=nblk),
        out_shape=[jax.ShapeDtypeStruct((E, T), F32)] * 3 + [jax.ShapeDtypeStruct((nblk, E, 128), I32)],
        grid=(1,),
        in_specs=[full((E, T))],
        out_specs=[full((E, T))] * 3 + [full((nblk, E, 128))],
        compiler_params=_cparams(("arbitrary",)),
        name="route",
    )(aff_t)


def _gather_kernel(offs_ref, x_ref, m_ref, p_ref, out_ref):
    g, j = pl.program_id(0), pl.program_id(1)

    @pl.when(j == 0)
    def _():
        out_ref[...] = jnp.zeros_like(out_ref)

    W = GATHER_W
    slot0 = lax.broadcasted_iota(I32, (W, TOK_BLK), 0).astype(F32)
    for jb in range(GATHER_JB):
        blk = j * GATHER_JB + jb
        cols = slice(jb * TOK_BLK, (jb + 1) * TOK_BLK)
        targets, bases, passes = [], [], []
        for k in range(GATHER_E):
            off = offs_ref[blk * N_EXPERTS + g * GATHER_E + k]
            cnt = offs_ref[(blk + 1) * N_EXPERTS + g * GATHER_E + k] - off
            base = (off // 16) * 16
            targets.append(jnp.where(m_ref[k, :, cols] > 0.0, p_ref[k, :, cols] + (off - base).astype(F32), -1.0))
            bases.append(base)
            passes.append((off - base + cnt + W - 1) // W)
        onehot = jnp.concatenate([jnp.where(slot0 == t, 1.0, 0.0) for t in targets], axis=0).astype(BF16)
        picked = jnp.dot(onehot, x_ref[cols, :], preferred_element_type=F32).astype(BF16)
        for k in range(GATHER_E):
            out_ref[k, pl.ds(pl.multiple_of(bases[k], 16), W), :] += picked[k * W:(k + 1) * W]
        for k in range(GATHER_E):
            def more(ps, carry, k=k, cols=cols):
                hot = jnp.where(slot0 + (ps * W).astype(F32) == targets[k], 1.0, 0.0).astype(BF16)
                extra = jnp.dot(hot, x_ref[cols, :], preferred_element_type=F32).astype(BF16)
                out_ref[k, pl.ds(pl.multiple_of(bases[k] + ps * W, 16), W), :] += extra
                return carry

            lax.fori_loop(1, passes[k], more, 0)


def _gather(h2, mask3, pos3, offs, cap):
    T = h2.shape[0]
    nblk = T // TOK_BLK
    cap_x = cap + 2 * GATHER_W
    rows = GATHER_JB * TOK_BLK
    grid_spec = pltpu.PrefetchScalarGridSpec(
        num_scalar_prefetch=1,
        grid=(N_EXPERTS // GATHER_E, nblk // GATHER_JB),
        in_specs=[pl.BlockSpec((rows, D_MODEL), lambda g, j, o: (j, 0)),
                  pl.BlockSpec((GATHER_E, 1, rows), lambda g, j, o: (g, 0, j)),
                  pl.BlockSpec((GATHER_E, 1, rows), lambda g, j, o: (g, 0, j))],
        out_specs=pl.BlockSpec((GATHER_E, cap_x, D_MODEL), lambda g, j, o: (g, 0, 0),
                               pipeline_mode=pl.Buffered(1)),
    )
    return pl.pallas_call(
        _gather_kernel,
        out_shape=jax.ShapeDtypeStruct((N_EXPERTS, cap_x, D_MODEL), BF16),
        grid_spec=grid_spec,
        compiler_params=_cparams(("parallel", "arbitrary")),
        name="gather",
    )(offs, h2, mask3, pos3)


def _ffn_kernel(x_ref, wg_ref, wu_ref, wd_ref, o_ref, w_scr):
    @pl.when(pl.program_id(1) == 0)
    def _():
        for n, w_ref in enumerate((wg_ref, wu_ref, wd_ref)):
            w_scr[n] = w_ref[0, 0].astype(BF16)

    x = x_ref[0]
    g = jnp.dot(x, w_scr[0], preferred_element_type=F32)
    u = jnp.dot(x, w_scr[1], preferred_element_type=F32)
    h = (g * _sigmoid(g)) * u
    o_ref[0] = _dot(h, w_scr[2]).astype(BF16)


def _ffn(xe, wg, wu, wd, layer, cap):
    tf = min(512, cap)
    wspec = pl.BlockSpec((1, 1, D_MODEL, D_MODEL), lambda e, i: (layer, e, 0, 0))
    return pl.pallas_call(
        _ffn_kernel,
        out_shape=jax.ShapeDtypeStruct((N_EXPERTS, cap, D_MODEL), BF16),
        grid=(N_EXPERTS, cap // tf),
        in_specs=[pl.BlockSpec((1, tf, D_MODEL), lambda e, i: (e, i, 0)), wspec, wspec, wspec],
        out_specs=pl.BlockSpec((1, tf, D_MODEL), lambda e, i: (e, i, 0)),
        scratch_shapes=[pltpu.VMEM((3, D_MODEL, D_MODEL), BF16)],
        compiler_params=_cparams(("parallel", "arbitrary")),
        name="ffn",
    )(xe, wg, wu, wd)


def _slot_window_start(off, cap):
    return jnp.minimum((off // TOK_BLK) * TOK_BLK, cap - 2 * TOK_BLK)


def _scatter_kernel(offs_ref, x2_ref, w_ref, p_ref, gf_ref, *rest, cap, final):
    ye, out_ref = rest[:N_EXPERTS], rest[N_EXPERTS]
    j = pl.program_id(0)
    ps, ws = p_ref[...].T, w_ref[...].T
    offs = [offs_ref[j * N_EXPERTS + e] for e in range(N_EXPERTS)]
    cnts = [offs_ref[(j + 1) * N_EXPERTS + e] - offs[e] for e in range(N_EXPERTS)]
    rels = [offs[e] - _slot_window_start(offs[e], cap) for e in range(N_EXPERTS)]
    weights = [ws[:, e:e + 1] for e in range(N_EXPERTS)]
    SUB = 16
    most = cnts[0]
    for e in range(1, N_EXPERTS):
        most = jnp.maximum(most, cnts[e])

    def onehot(e, start, width):
        slot = lax.broadcasted_iota(I32, (TOK_BLK, width), 1).astype(F32)
        return jnp.where(slot == ps[:, e:e + 1] + (rels[e] - start).astype(F32), weights[e], 0.0).astype(BF16)

    W = SCATTER_W

    @pl.when(most <= W - SUB)
    def _():
        acc = x2_ref[...]
        for g0 in range(0, N_EXPERTS, 2):
            group = (g0, g0 + 1)
            r0 = {e: pl.multiple_of(jnp.minimum((rels[e] // SUB) * SUB, 2 * TOK_BLK - W), SUB) for e in group}
            acc = acc + jnp.dot(jnp.concatenate([onehot(e, r0[e], W) for e in group], axis=1),
                                jnp.concatenate([ye[e][0, pl.ds(r0[e], W), :] for e in group], axis=0),
                                preferred_element_type=F32)
        out_ref[...] = acc

    @pl.when(most > W - SUB)
    def _():
        out_ref[...] = x2_ref[...]
        for e in range(N_EXPERTS):
            @pl.when(cnts[e] > 0)
            def _(e=e):
                out_ref[...] += jnp.dot(onehot(e, 0, 2 * TOK_BLK), ye[e][0], preferred_element_type=F32)

    if final:
        x = out_ref[...]
        out_ref[...] = x * lax.rsqrt(jnp.mean(x * x, axis=-1, keepdims=True) + EPS) * gf_ref[...]


def _scatter(x2, wsel, pos, ye, offs, gf, cap, final):
    T = x2.shape[0]
    nblk = T // TOK_BLK
    assert cap >= 2 * TOK_BLK and cap % TOK_BLK == 0

    def ye_spec(e):
        return pl.BlockSpec((pl.Element(1), pl.Element(2 * TOK_BLK), pl.Element(D_MODEL)),
                            lambda j, o: (e, pl.multiple_of(_slot_window_start(o[j * N_EXPERTS + e], cap), TOK_BLK),
                                          0))

    row = lambda w: pl.BlockSpec((TOK_BLK, w), lambda j, o: (j, 0))
    col = pl.BlockSpec((N_EXPERTS, TOK_BLK), lambda j, o: (0, j))
    grid_spec = pltpu.PrefetchScalarGridSpec(
        num_scalar_prefetch=1,
        grid=(nblk,),
        in_specs=[row(D_MODEL), col, col, pl.BlockSpec((1, D_MODEL), lambda j, o: (0, 0))]
                 + [ye_spec(e) for e in range(N_EXPERTS)],
        out_specs=row(D_MODEL),
    )
    return pl.pallas_call(
        functools.partial(_scatter_kernel, cap=cap, final=final),
        out_shape=jax.ShapeDtypeStruct((T, D_MODEL), F32),
        grid_spec=grid_spec,
        compiler_params=_cparams(("arbitrary",)),
        name="scatter",
    )(offs, x2, wsel, pos, gf, *([ye] * N_EXPERTS))


def _block_diag(w):
    n, a, b = w.shape
    out = jnp.zeros((n * a, n * b), w.dtype)
    for i in range(n):
        out = out.at[i * a:(i + 1) * a, i * b:(i + 1) * b].set(w[i])
    return out


def _rope_tables(S):
    inv = ROPE_THETA ** (-jnp.arange(0, ROT_DIM, 2, dtype=F32) / ROT_DIM)
    ang = jnp.arange(S, dtype=F32)[:, None] * inv[None, :]
    cos, sin = jnp.cos(ang), jnp.sin(ang)
    half = ROT_DIM // 2
    ones, zeros = jnp.ones((S, ATT_HD - ROT_DIM), F32), jnp.zeros((S, ATT_HD - ROT_DIM), F32)
    zh = jnp.zeros((S, half), F32)
    c = jnp.concatenate([cos, cos, ones], axis=1)
    s1 = jnp.concatenate([-sin, zh, zeros], axis=1)
    s2 = jnp.concatenate([zh, sin, zeros], axis=1)
    return tuple(jnp.tile(t, (1, 2)) for t in (c, s1, s2))


def _expanders():
    eg = np.zeros((2, 128, GROUP_W), np.float32)
    eb = np.zeros((2, 128, GROUP_W), np.float32)
    for d in range(2):
        for h in range(DN_HEADS):
            eg[d, d * DN_HEADS + h, h * DN_DK:(h + 1) * DN_DK] = 1.0
            eb[d, 2 * DN_HEADS + d * DN_HEADS + h, h * DN_DK:(h + 1) * DN_DK] = 1.0
    hd = np.arange(GROUP_W) // DN_DK
    bd = (hd[:, None] == hd[None, :]).astype(np.float32)
    return jnp.asarray(eg), jnp.asarray(eb), jnp.asarray(bd)


def _layer_params(l, norm1_g, w_in, conv_a_w, conv_a_b, rg_wa, rg_ba, rg_wx, rg_bx, rg_lambda, pool_w, pool_scale,
                  dn_conv_w, dn_A_log, dn_dt_bias, dn_norm_g, mix_norm_g, w_out, norm2_g, router_w,
                  exp_w_gate, exp_w_up, exp_w_down):
    w = w_in[l]
    w_cat = jnp.concatenate([w[:, 0:1536], w[:, 1552:2576], w[:, 1536:1552],
                             jnp.zeros((D_MODEL, COL_END - COL_CAB - 16), F32)], axis=1).astype(BF16)
    pad8 = lambda v: jnp.concatenate([v.reshape(1, 2 * DN_HEADS), jnp.zeros((1, 120), F32)], axis=1)
    return dict(
        g1=norm1_g[l].reshape(1, D_MODEL), w_cat=w_cat,
        conv_a_w=conv_a_w[l], conv_a_b=conv_a_b[l].reshape(1, GROUP_W),
        wa=[_block_diag(rg_wa[l, d]).astype(BF16) for d in range(2)],
        wx=[_block_diag(rg_wx[l, d]).astype(BF16) for d in range(2)],
        ba=[rg_ba[l, d].reshape(1, GROUP_W) for d in range(2)],
        bx=[rg_bx[l, d].reshape(1, GROUP_W) for d in range(2)],
        lam=[rg_lambda[l, d].reshape(1, GROUP_W) for d in range(2)],
        pool_w=_block_diag(pool_w[l]).astype(BF16), pool_scale=pool_scale[l].reshape(1, GROUP_W),
        dn_conv_w=dn_conv_w[l], alog=pad8(dn_A_log[l]), dtb=pad8(dn_dt_bias[l]),
        dng=jnp.tile(dn_norm_g[l], DN_HEADS).reshape(1, GROUP_W),
        mg=mix_norm_g[l].reshape(1, D_MODEL), wo=w_out[l].astype(BF16),
        g2=norm2_g[l].reshape(1, D_MODEL),
        rw=jnp.concatenate(_split(router_w[l].T, 2), axis=0),
        wg=exp_w_gate, wu=exp_w_up, wd=exp_w_down,
    )


def _chunk_rows(gcn, d):
    B, S, _ = gcn.shape
    n = S // DN_CHUNK
    g = gcn[:, :, d * DN_HEADS:(d + 1) * DN_HEADS].reshape(B, n, DN_CHUNK, DN_HEADS)
    return g.transpose(0, 1, 3, 2).reshape(B, n, 1, GROUP_W)


def _encoder(x, layers, final_g, consts):
    B, S, D = x.shape
    T = B * S
    cap = max(1, EC_CAPACITY * T // N_EXPERTS)
    eg, eb, bdmask = consts
    rope_c, rope_s1, rope_s2 = _rope_tables(S)
    xt = x.reshape(T, D)
    for l, p in enumerate(layers):
        pa, pb, pq, pg, pd, pcab = _inproj(xt, p["g1"], p["w_cat"], rope_c, rope_s1, rope_s2, S)
        pa3 = pa.reshape(B, S, 2 * GROUP_W)
        rg = lambda d: (p["conv_a_w"], p["conv_a_b"], p["wa"][d], p["ba"][d], p["wx"][d], p["bx"][d], p["lam"][d])
        hf = _rglru(pa3, None, *rg(0), reverse=False)
        ya = _rglru(pa3, hf, *rg(1), reverse=True)
        yb = _pool(pb.reshape(B, S, GROUP_W), p["pool_w"], p["pool_scale"])
        qn, kn, vv, gcf, gcb, bef, beb, gcn = _dnprep(pq.reshape(B, S, 3 * GROUP_W), pcab.reshape(B, S, 128),
                                                      p["dn_conv_w"], p["alog"], p["dtb"], eg, eb)
        of, ob = _delta(qn, kn, vv, gcf, gcb, bef, beb, _chunk_rows(gcn, 0), _chunk_rows(gcn, 1), bdmask)
        yd = _attention(pd.reshape(B, S, 3 * GROUP_W))
        flat = lambda a: a.reshape(T, GROUP_W)
        x2, h2, aff_t = _outproj(flat(ya), flat(yb), flat(of), flat(ob), pg, flat(yd), xt,
                                 p["dng"], p["mg"], p["wo"], p["g2"], p["rw"])
        mask, pos, wsel, offs3 = _route(aff_t, cap)
        offs = jnp.concatenate([offs3[:, :, 0].reshape(-1), jnp.full((N_EXPERTS,), cap, I32)])
        xe = _gather(h2, mask.reshape(N_EXPERTS, 1, T), pos.reshape(N_EXPERTS, 1, T), offs, cap)
        ye = _ffn(xe, p["wg"], p["wu"], p["wd"], l, cap)
        xt = _scatter(x2, wsel, pos, ye, offs, final_g, cap, final=(l == len(layers) - 1))
    return xt.reshape(B, S, D)


def kernel(x_prompt, x_sample, norm1_g, w_in, conv_a_w, conv_a_b, rg_wa, rg_ba, rg_wx, rg_bx, rg_lambda, pool_w, pool_scale, dn_conv_w, dn_A_log, dn_dt_bias, dn_norm_g, mix_norm_g, w_out, norm2_g, router_w, exp_w_gate, exp_w_up, exp_w_down, final_norm_g):
    layers = [_layer_params(l, norm1_g, w_in, conv_a_w, conv_a_b, rg_wa, rg_ba, rg_wx, rg_bx, rg_lambda, pool_w,
                            pool_scale, dn_conv_w, dn_A_log, dn_dt_bias, dn_norm_g, mix_norm_g, w_out, norm2_g,
                            router_w, exp_w_gate, exp_w_up, exp_w_down) for l in range(DEPTH)]
    consts = _expanders()
    final_g = final_norm_g.reshape(1, D_MODEL)
    return (_encoder(x_prompt, layers, final_g, consts), _encoder(x_sample, layers, final_g, consts))
```

```python
import functools
import math
import numpy as np
import jax
import jax.numpy as jnp
from jax import lax
from jax.experimental import pallas as pl
from jax.experimental.pallas import tpu as pltpu

F32, BF16, I32 = jnp.float32, jnp.bfloat16, jnp.int32

D_MODEL = 1024
DEPTH = 2
GROUP_W = 256
RG_C = 8.0
POOL_WINDOWS = (2, 4, 8, 16)
DN_HEADS = 4
DN_DK = 64
DN_CHUNK = 64
ATT_HD = 64
ROT_DIM = 16
ROPE_THETA = 500000.0
ATT_WINDOWS = (128, 512, 2048)
ATT_DILATIONS = (1, 4, 16)
N_EXPERTS = 16
EC_CAPACITY = 2
EPS = 1e-6
NEG = -1e30

COL_A, COL_B, COL_CQ, COL_CG, COL_D, COL_CAB, COL_END = 0, 512, 768, 1536, 1792, 2560, 2688

ROW_TILE = 512
HALO = 8
DELTA_CB = 8
ATT_QB = 128
ATT_UNROLL = 8
TOK_BLK = 256
GATHER_W = 64
GATHER_JB = 8
GATHER_E = 4
SCATTER_W = 128
VMEM_LIMIT = 56 * 1024 * 1024


def _cparams(sem):
    return pltpu.CompilerParams(dimension_semantics=sem, vmem_limit_bytes=VMEM_LIMIT)


def _dot(a, b):
    return jnp.dot(a.astype(BF16), b.astype(BF16), preferred_element_type=F32)


def _split(a, pieces):
    out, rem = [], a
    for i in range(pieces):
        t = rem.astype(BF16)
        out.append(t)
        if i + 1 < pieces:
            rem = rem - t.astype(F32)
    return out


def _dot_sel(a, sel, pieces=3):
    sel = sel.astype(BF16)
    return sum(jnp.dot(t, sel, preferred_element_type=F32) for t in _split(a, pieces))


def _dot_nt(a, b):
    return lax.dot_general(a.astype(BF16), b.astype(BF16), (((1,), (1,)), ((), ())), preferred_element_type=F32)


def _dot_tn(a, b):
    return lax.dot_general(a.astype(BF16), b.astype(BF16), (((0,), (0,)), ((), ())), preferred_element_type=F32)


def _sigmoid(x):
    return 1.0 / (1.0 + jnp.exp(-x))


def _softplus(x):
    return jnp.maximum(x, 0.0) + jnp.log1p(jnp.exp(-jnp.abs(x)))


def _shift_rows(e, k):
    n = e.shape[0]
    return e if k % n == 0 else pltpu.roll(e, (-k) % n, axis=0)


def _with_halo(cur_ref, prev_ref, next_ref, first, last):
    prev = jnp.where(first, 0.0, prev_ref[0])
    nxt = jnp.where(last, 0.0, next_ref[0])
    return jnp.concatenate([prev, cur_ref[0], nxt], axis=0)


def _halo_specs(ts, width, S, blk_of):
    per = ts // HALO
    last = S // HALO - 1
    cur = pl.BlockSpec((1, ts, width), lambda b, c: (b, blk_of(c), 0))
    prev = pl.BlockSpec((1, HALO, width), lambda b, c: (b, jnp.maximum(blk_of(c) * per - 1, 0), 0))
    nxt = pl.BlockSpec((1, HALO, width), lambda b, c: (b, jnp.minimum((blk_of(c) + 1) * per, last), 0))
    return cur, prev, nxt


def _inproj_kernel(x_ref, g_ref, w_ref, c_ref, s1_ref, s2_ref, pa_ref, pb_ref, pq_ref, pg_ref, pd_ref, pcab_ref):
    x = x_ref[...]
    h = x * lax.rsqrt(jnp.mean(x * x, axis=-1, keepdims=True) + EPS) * g_ref[...]
    hb = h.astype(BF16)

    def mm(lo, hi):
        return jnp.dot(hb, w_ref[:, lo:hi], preferred_element_type=F32)

    pa_ref[...] = mm(COL_A, COL_B)
    pb_ref[...] = mm(COL_B, COL_CQ)
    pq_ref[...] = mm(COL_CQ, COL_CG)
    pg_ref[...] = mm(COL_CG, COL_D)
    c, s1, s2 = c_ref[...], s1_ref[...], s2_ref[...]
    for part in range(2):
        yy = mm(COL_D + 256 * part, COL_D + 256 * part + 256)
        for half in range(2):
            y = yy[:, 128 * half:128 * half + 128]
            y = y * c + pltpu.roll(y, 128 - ROT_DIM // 2, axis=1) * s1 + pltpu.roll(y, ROT_DIM // 2, axis=1) * s2
            if part == 0:
                y = y * (ATT_HD ** -0.5)
            pd_ref[:, 256 * part + 128 * half:256 * part + 128 * half + 128] = y
    vcab = mm(COL_D + 512, COL_END)
    pd_ref[:, 512:768] = vcab[:, 0:256]
    pcab_ref[...] = vcab[:, 256:384]


def _inproj(xt, g1, w_cat, rope_c, rope_s1, rope_s2, S):
    T = xt.shape[0]
    tm = min(ROW_TILE, S)
    per_seq = S // tm
    widths = (512, 256, 768, 256, 768, 128)
    row = lambda w: pl.BlockSpec((tm, w), lambda i: (i, 0))
    rope = pl.BlockSpec((tm, 128), lambda i: (i % per_seq, 0))
    return pl.pallas_call(
        _inproj_kernel,
        out_shape=[jax.ShapeDtypeStruct((T, w), F32) for w in widths],
        grid=(T // tm,),
        in_specs=[row(D_MODEL), pl.BlockSpec((1, D_MODEL), lambda i: (0, 0)),
                  pl.BlockSpec((D_MODEL, COL_END), lambda i: (0, 0)), rope, rope, rope],
        out_specs=[row(w) for w in widths],
        compiler_params=_cparams(("parallel",)),
        name="inproj",
    )(xt, g1, w_cat, rope_c, rope_s1, rope_s2)


def _rglru_kernel(*refs, reverse, ts, nc):
    if reverse:
        (cur_ref, prev_ref, next_ref, gate_ref, hf_ref, cw_ref, cb_ref, wa_ref, ba_ref, wx_ref, bx_ref, lam_ref,
         out_ref, a_ref, b_ref, carry_ref) = refs
    else:
        (cur_ref, prev_ref, next_ref, cw_ref, cb_ref, wa_ref, ba_ref, wx_ref, bx_ref, lam_ref,
         out_ref, a_ref, b_ref, carry_ref) = refs
    c = pl.program_id(1)
    blk = (nc - 1 - c) if reverse else c
    e = _with_halo(cur_ref, prev_ref, next_ref, blk == 0, blk == nc - 1)
    cw = cw_ref[...]
    sl = slice(HALO, HALO + ts)
    u = (cw[0:1] * _shift_rows(e, -2)[sl] + cw[1:2] * _shift_rows(e, -1)[sl] + cw[2:3] * e[sl]
         + cw[3:4] * _shift_rows(e, 1)[sl]) + cb_ref[...]
    r = _sigmoid(_dot(u, wa_ref[...]) + ba_ref[...])
    i = _sigmoid(_dot(u, wx_ref[...]) + bx_ref[...])
    log_a = -RG_C * r * _softplus(-lam_ref[...])
    a_ref[...] = jnp.exp(log_a)
    b_ref[...] = jnp.sqrt(1.0 - jnp.exp(2.0 * log_a)) * (i * u)

    @pl.when(c == 0)
    def _():
        carry_ref[...] = jnp.zeros_like(carry_ref)

    row = lax.broadcasted_iota(I32, (HALO, GROUP_W), 0)
    nt = ts // HALO

    def body(it, carry):
        ti = (nt - 1 - it) if reverse else it
        st = pl.multiple_of(ti * HALO, HALO)
        a = a_ref[pl.ds(st, HALO), :]
        b = b_ref[pl.ds(st, HALO), :]
        for d in (1, 2, 4):
            k = d if reverse else -d
            valid = (row < HALO - d) if reverse else (row >= d)
            b = jnp.where(valid, a * _shift_rows(b, k) + b, b)
            a = jnp.where(valid, a * _shift_rows(a, k), a)
        h = a * carry + b
        if reverse:
            g = gate_ref[0, pl.ds(st, HALO), :]
            cdf = 0.5 * (1.0 + jnp.tanh(np.float32(np.sqrt(2.0 / np.pi)) * (g + 0.044715 * (g * g * g))))
            out_ref[0, pl.ds(st, HALO), :] = (g * cdf) * (hf_ref[0, pl.ds(st, HALO), :] + h)
            return jnp.broadcast_to(h[0:1], h.shape)
        out_ref[0, pl.ds(st, HALO), :] = h
        return jnp.broadcast_to(h[HALO - 1:HALO], h.shape)

    carry_ref[...] = lax.fori_loop(0, nt, body, carry_ref[...], unroll=4)


def _rglru(pa3, hf, cw, cb, wa, ba, wx, bx, lam, reverse):
    B, S, _ = pa3.shape
    ts = min(ROW_TILE, S)
    nc = S // ts
    blk_of = (lambda c: nc - 1 - c) if reverse else (lambda c: c)
    cur, prev, nxt = _halo_specs(ts, GROUP_W, S, blk_of)
    tile = pl.BlockSpec((1, ts, GROUP_W), lambda b, c: (b, blk_of(c), 0))
    const = lambda shape: pl.BlockSpec(shape, lambda b, c: (0,) * len(shape))
    in_specs = [cur, prev, nxt]
    args = [pa3, pa3, pa3]
    if reverse:
        in_specs += [pl.BlockSpec((1, ts, GROUP_W), lambda b, c: (b, blk_of(c), 1)), tile]
        args += [pa3, hf]
    in_specs += [const((4, GROUP_W)), const((1, GROUP_W)), const((GROUP_W, GROUP_W)), const((1, GROUP_W)),
                 const((GROUP_W, GROUP_W)), const((1, GROUP_W)), const((1, GROUP_W))]
    args += [cw, cb, wa, ba, wx, bx, lam]
    return pl.pallas_call(
        functools.partial(_rglru_kernel, reverse=reverse, ts=ts, nc=nc),
        out_shape=jax.ShapeDtypeStruct((B, S, GROUP_W), F32),
        grid=(B, nc),
        in_specs=in_specs,
        out_specs=tile,
        scratch_shapes=[pltpu.VMEM((ts, GROUP_W), F32), pltpu.VMEM((ts, GROUP_W), F32),
                        pltpu.VMEM((HALO, GROUP_W), F32)],
        compiler_params=_cparams(("parallel", "arbitrary")),
        name="rglru_bwd" if reverse else "rglru_fwd",
    )(*args)


def _pool_kernel(cur_ref, prev_ref, next_ref, w_ref, sc_ref, out_ref, *, ts, nc, S):
    c = pl.program_id(1)
    e = _with_halo(cur_ref, prev_ref, next_ref, c == 0, c == nc - 1)
    sl = slice(HALO, HALO + ts)
    a2 = e + _shift_rows(e, 1)
    a4 = a2 + _shift_rows(a2, 2)
    a8 = a4 + _shift_rows(a4, 4)
    a16 = a8 + _shift_rows(a8, 8)
    sums = [_shift_rows(a, -(w // 2))[sl] for a, w in zip((a2, a4, a8, a16), POOL_WINDOWS)]
    gi = lax.broadcasted_iota(I32, (ts, GROUP_W), 1) // (GROUP_W // len(POOL_WINDOWS))
    ssum = jnp.where(gi == 0, sums[0], jnp.where(gi == 1, sums[1], jnp.where(gi == 2, sums[2], sums[3])))
    hw = jnp.where(gi == 0, 1, jnp.where(gi == 1, 2, jnp.where(gi == 2, 4, 8)))
    t = c * ts + lax.broadcasted_iota(I32, (ts, GROUP_W), 0)
    cnt = (jnp.minimum(t + hw, S) - jnp.maximum(t - hw, 0)).astype(F32)
    p = ssum / cnt - e[sl]
    out_ref[0] = _dot(p, w_ref[...]) * sc_ref[...]


def _pool(pb3, w_bd, scale):
    B, S, _ = pb3.shape
    ts = min(ROW_TILE, S)
    nc = S // ts
    cur, prev, nxt = _halo_specs(ts, GROUP_W, S, lambda c: c)
    return pl.pallas_call(
        functools.partial(_pool_kernel, ts=ts, nc=nc, S=S),
        out_shape=jax.ShapeDtypeStruct((B, S, GROUP_W), F32),
        grid=(B, nc),
        in_specs=[cur, prev, nxt, pl.BlockSpec((GROUP_W, GROUP_W), lambda b, c: (0, 0)),
                  pl.BlockSpec((1, GROUP_W), lambda b, c: (0, 0))],
        out_specs=pl.BlockSpec((1, ts, GROUP_W), lambda b, c: (b, c, 0)),
        compiler_params=_cparams(("parallel", "parallel")),
        name="pool",
    )(pb3, pb3, pb3, w_bd, scale)


def _head_sum_matrix(n, group):
    r = lax.broadcasted_iota(I32, (n, n), 0) // group
    c = lax.broadcasted_iota(I32, (n, n), 1) // group
    return (r == c).astype(F32)


def _dnprep_kernel(cur_ref, prev_ref, next_ref, cab_ref, cw_ref, alog_ref, dtb_ref, eg_ref, eb_ref,
                   q_ref, k_ref, v_ref, gcf_ref, gcb_ref, bef_ref, beb_ref, gcn_ref, *, ts, nc):
    c = pl.program_id(1)
    e = _with_halo(cur_ref, prev_ref, next_ref, c == 0, c == nc - 1)
    cw = cw_ref[...]
    sl = slice(HALO, HALO + ts)
    y = (cw[0:1] * _shift_rows(e, -2)[sl] + cw[1:2] * _shift_rows(e, -1)[sl] + cw[2:3] * e[sl]
         + cw[3:4] * _shift_rows(e, 1)[sl])
    y = y * _sigmoid(y)
    q, k = y[:, 0:GROUP_W], y[:, GROUP_W:2 * GROUP_W]
    hs = _head_sum_matrix(GROUP_W, DN_DK)
    q_ref[0] = q * lax.rsqrt(_dot_sel(q * q, hs, 2) + EPS) * (DN_DK ** -0.5)
    k_ref[0] = k * lax.rsqrt(_dot_sel(k * k, hs, 2) + EPS)
    v_ref[0] = y[:, 2 * GROUP_W:3 * GROUP_W]
    cab = cab_ref[0]
    g = -jnp.exp(alog_ref[...]) * _softplus(cab + dtb_ref[...])
    beta = _sigmoid(cab)
    bef_ref[0] = _dot_sel(beta, eb_ref[0], 2)
    beb_ref[0] = _dot_sel(beta, eb_ref[1], 2)
    pos = lax.broadcasted_iota(I32, (ts, 128), 0) % DN_CHUNK
    fwd = rev = g
    d = 1
    while d < DN_CHUNK:
        fwd = fwd + jnp.where(pos >= d, _shift_rows(fwd, -d), 0.0)
        rev = rev + jnp.where(pos < DN_CHUNK - d, _shift_rows(rev, d), 0.0)
        d *= 2
    gcn = jnp.where(lax.broadcasted_iota(I32, (ts, 128), 1) < DN_HEADS, fwd, rev)
    gcn_ref[0] = gcn
    gcf_ref[0] = _dot_sel(gcn, eg_ref[0])
    gcb_ref[0] = _dot_sel(gcn, eg_ref[1])


def _dnprep(pq3, pcab3, cw, alog_row, dtb_row, eg, eb):
    B, S, _ = pq3.shape
    ts = min(ROW_TILE, S)
    nc = S // ts
    cur, prev, nxt = _halo_specs(ts, 3 * GROUP_W, S, lambda c: c)
    tile = pl.BlockSpec((1, ts, GROUP_W), lambda b, c: (b, c, 0))
    const = lambda shape: pl.BlockSpec(shape, lambda b, c: (0,) * len(shape))
    return pl.pallas_call(
        functools.partial(_dnprep_kernel, ts=ts, nc=nc),
        out_shape=[jax.ShapeDtypeStruct((B, S, GROUP_W), F32)] * 7 + [jax.ShapeDtypeStruct((B, S, 128), F32)],
        grid=(B, nc),
        in_specs=[cur, prev, nxt, pl.BlockSpec((1, ts, 128), lambda b, c: (b, c, 0)),
                  const((4, 3 * GROUP_W)), const((1, 128)), const((1, 128)),
                  const((2, 128, GROUP_W)), const((2, 128, GROUP_W))],
        out_specs=[tile] * 7 + [pl.BlockSpec((1, ts, 128), lambda b, c: (b, c, 0))],
        compiler_params=_cparams(("parallel", "parallel")),
        name="dnprep",
    )(pq3, pq3, pq3, pcab3, cw, alog_row, dtb_row, eg, eb)


def _delta_kernel(qf, kf, vf, gcf, bef, grf, qb, kb, vb, gcb, beb, grb, bd_ref, of_ref, ob_ref,
                  s_ref, p_scr, x_scr, wq_scr, u_scr, at_scr, kd_scr, vb_scr, kg_scr, *, cb):
    j = pl.program_id(1)

    @pl.when(j == 0)
    def _():
        s_ref[...] = jnp.zeros_like(s_ref)

    C = DN_CHUNK
    bd = bd_ref[...] > 0.0
    c_idx = lax.broadcasted_iota(I32, (C, GROUP_W), 0)
    m_idx = lax.broadcasted_iota(I32, (C, GROUP_W), 1) % C
    eye = (c_idx == m_idx).astype(F32)

    def blockdiag(x):
        return jnp.where(bd, jnp.tile(x.astype(BF16), (DN_HEADS, 1)), jnp.zeros((), BF16))

    dirs = ((qf, kf, vf, gcf, bef, grf, of_ref), (qb, kb, vb, gcb, beb, grb, ob_ref))
    units = [(d, ci) for ci in range(cb) for d in range(2)]

    def chunk_of(d, ci):
        return cb - 1 - ci if d == 1 else ci

    def prepare(ui):
        d, ci = units[ui]
        q_r, k_r, v_r, gc_r, be_r, gr_r, _ = dirs[d]
        rev = d == 1
        tril = (c_idx <= m_idx) if rev else (c_idx >= m_idx)
        strict = (c_idx < m_idx) if rev else (c_idx > m_idx)
        last = 0 if rev else C - 1
        cc = chunk_of(d, ci)
        rows = slice(cc * C, (cc + 1) * C)
        q, k, v = q_r[0, rows, :], k_r[0, rows, :], v_r[0, rows, :]
        gc, be, gr = gc_r[0, rows, :], be_r[0, rows, :], gr_r[0, cc]
        eg = jnp.exp(gc)
        kbeta = k * be
        kkqk = _dot_nt(jnp.concatenate([kbeta, q], axis=0), blockdiag(k))
        decay = jnp.exp(jnp.where(tril, gc - gr, -jnp.inf))
        at_scr[ui] = (kkqk[C:] * decay).astype(BF16)
        p = -jnp.where(strict, kkqk[:C] * decay, 0.0)
        p_scr[ui] = p.astype(BF16)
        x_scr[ui] = eye + p
        wq_scr[ui, C:2 * C, :] = (q * eg).astype(BF16)
        kd_scr[ui] = (k * jnp.exp(gc[last:last + 1] - gc)).astype(BF16)
        vb_scr[ui] = (v * be).astype(BF16)
        kg_scr[ui] = (kbeta * eg).astype(BF16)

    def solve(group):
        for ui in group:
            p = p_scr[ui]
            p_scr[ui] = _dot(p, blockdiag(p)).astype(BF16)
        for _ in range(4):
            for ui in group:
                p, x = p_scr[ui], x_scr[ui]
                px = _dot(jnp.concatenate([p, x.astype(BF16)], axis=0), blockdiag(p))
                p_scr[ui] = px[:C].astype(BF16)
                x_scr[ui] = x + px[C:]
        for ui in group:
            x = x_scr[ui]
            x = x + _dot(x, blockdiag(p_scr[ui]))
            u_scr[ui] = _dot(x, blockdiag(vb_scr[ui]))
            wq_scr[ui, 0:C, :] = _dot(x, blockdiag(kg_scr[ui])).astype(BF16)

    def recur(ui):
        d, ci = units[ui]
        gc_r, o_r = dirs[d][3], dirs[d][6]
        cc = chunk_of(d, ci)
        last = cc * C + (0 if d == 1 else C - 1)
        state = s_ref[d]
        ws_qs = _dot(wq_scr[ui], state)
        v_new = u_scr[ui] - ws_qs[:C]
        o_r[0, cc * C:(cc + 1) * C, :] = ws_qs[C:] + _dot(at_scr[ui], blockdiag(v_new))
        s_ref[d] = (state * jnp.exp(gc_r[0, last:last + 1, :])
                    + jnp.where(bd, _dot_tn(kd_scr[ui], v_new), 0.0))

    for ui in range(len(units)):
        prepare(ui)
    solve(range(len(units)))
    for ui in range(len(units)):
        recur(ui)


def _delta(qn, kn, vv, gcf, gcb, bef, beb, grf, grb, bdmask):
    B, S, _ = qn.shape
    cb = DELTA_CB
    rb = cb * DN_CHUNK
    nb = S // rb
    f = lambda b, j: (b, j, 0)
    r = lambda b, j: (b, nb - 1 - j, 0)
    tf = pl.BlockSpec((1, rb, GROUP_W), f)
    tr = pl.BlockSpec((1, rb, GROUP_W), r)
    gf = pl.BlockSpec((1, cb, 1, GROUP_W), lambda b, j: (b, j, 0, 0))
    gr = pl.BlockSpec((1, cb, 1, GROUP_W), lambda b, j: (b, nb - 1 - j, 0, 0))
    return pl.pallas_call(
        functools.partial(_delta_kernel, cb=cb),
        out_shape=[jax.ShapeDtypeStruct((B, S, GROUP_W), F32)] * 2,
        grid=(B, nb),
        in_specs=[tf, tf, tf, tf, tf, gf, tr, tr, tr, tr, tr, gr,
                  pl.BlockSpec((GROUP_W, GROUP_W), lambda b, j: (0, 0))],
        out_specs=[tf, tr],
        scratch_shapes=[pltpu.VMEM((2, GROUP_W, GROUP_W), F32)]
                       + [pltpu.VMEM((2 * cb, n * DN_CHUNK, GROUP_W), dt) for n, dt in
                          ((1, BF16), (1, F32), (2, BF16), (1, F32), (1, BF16), (1, BF16), (1, BF16), (1, BF16))],
        compiler_params=_cparams(("parallel", "arbitrary")),
        name="delta",
    )(qn, kn, vv, gcf, bef, grf, qn, kn, vv, gcb, beb, grb, bdmask)


def _attn_kernel(q_ref, k_ref, v_ref, out_ref, m_ref, l_ref, o_ref, *, S):
    head0 = lax.broadcasted_iota(I32, (1, 128), 1) < ATT_HD

    def rows(start, n, dil):
        return pl.ds(start, n) if dil == 1 else pl.ds(start, n, stride=dil)

    for bi, (win, dil) in enumerate(zip(ATT_WINDOWS, ATT_DILATIONS)):
        half = win // (2 * dil)
        L = S // dil
        qb_n = min(ATT_QB, L)
        kw = min(L, qb_n + 2 * half)
        nqb = L // qb_n

        def block_stats(idx, dil=dil, half=half, L=L, qb_n=qb_n, kw=kw, nqb=nqb):
            r = idx // nqb
            m0 = (idx % nqb) * qb_n
            ks = jnp.clip(m0 - half, 0, L - kw)
            qsel = rows(r + m0 * dil, qb_n, dil)
            ksel = rows(r + ks * dil, kw, dil)
            q = q_ref[0, qsel, :]
            kk = k_ref[0, ksel, :].astype(BF16)
            vv = v_ref[0, ksel, :].astype(BF16)
            rel = (lax.broadcasted_iota(I32, (qb_n, kw), 1) - lax.broadcasted_iota(I32, (qb_n, kw), 0)
                   + (ks - m0 + half))
            valid = rel.astype(jnp.uint32) <= 2 * half
            q2 = jnp.concatenate([jnp.where(head0, q, 0.0), jnp.where(head0, 0.0, q)], axis=0)
            s = jnp.where(jnp.concatenate([valid, valid], axis=0), _dot_nt(q2, kk), NEG)
            m = jnp.max(s, axis=-1, keepdims=True)
            p = jnp.exp(s - m)
            l = jnp.sum(p, axis=-1, keepdims=True)
            o = _dot(p, vv)
            return (qsel, jnp.where(head0, m[:qb_n], m[qb_n:]), jnp.where(head0, l[:qb_n], l[qb_n:]),
                    jnp.where(head0, o[:qb_n], o[qb_n:]))

        unroll = math.gcd(ATT_UNROLL, dil * nqb)

        def body(it, carry, bi=bi, unroll=unroll):
            stats = [block_stats(it * unroll + s) for s in range(unroll)]
            if bi > 0:
                olds = [(m_ref[qsel, :], l_ref[qsel, :], o_ref[qsel, :]) for qsel, _, _, _ in stats]
                merged = []
                for (qsel, m, l, o), (m_old, l_old, o_old) in zip(stats, olds):
                    m_new = jnp.maximum(m_old, m)
                    w_old, w_cur = jnp.exp(m_old - m_new), jnp.exp(m - m_new)
                    merged.append((qsel, m_new, w_old * l_old + w_cur * l, w_old * o_old + w_cur * o))
                stats = merged
            for qsel, m, l, o in stats:
                if bi == len(ATT_WINDOWS) - 1:
                    out_ref[0, qsel, :] = o / l
                else:
                    m_ref[qsel, :] = m
                    l_ref[qsel, :] = l
                    o_ref[qsel, :] = o
            return carry

        lax.fori_loop(0, dil * nqb // unroll, body, 0)


def _attention(pd3):
    B, S, _ = pd3.shape
    spec = lambda off: pl.BlockSpec((1, S, 128), lambda b, p: (b, 0, off + p))
    return pl.pallas_call(
        functools.partial(_attn_kernel, S=S),
        out_shape=jax.ShapeDtypeStruct((B, S, GROUP_W), F32),
        grid=(B, 2),
        in_specs=[spec(0), spec(2), spec(4)],
        out_specs=pl.BlockSpec((1, S, 128), lambda b, p: (b, 0, p)),
        scratch_shapes=[pltpu.VMEM((S, 128), F32)] * 3,
        compiler_params=_cparams(("parallel", "parallel")),
        name="attention",
    )(pd3, pd3, pd3)


def _outproj_kernel(ya_ref, yb_ref, of_ref, ob_ref, cg_ref, yd_ref, x_ref, dng_ref, mg_ref, wo_ref, g2_ref, rw_ref,
                    x2_ref, h2_ref, aff_ref):
    o = of_ref[...] + ob_ref[...]
    ms = _dot_sel(o * o, _head_sum_matrix(GROUP_W, DN_DK), 2) * (1.0 / DN_DK)
    cg = cg_ref[...]
    yc = (o * lax.rsqrt(ms + EPS) * dng_ref[...]) * (cg * _sigmoid(cg))
    acc = x_ref[...]
    for gi, y in enumerate((ya_ref[...], yb_ref[...], yc, yd_ref[...])):
        sl = slice(gi * GROUP_W, (gi + 1) * GROUP_W)
        mix = y * lax.rsqrt(jnp.mean(y * y, axis=-1, keepdims=True) + EPS) * mg_ref[:, sl]
        acc = acc + _dot(mix, wo_ref[sl, :])
    x2_ref[...] = acc
    h2 = acc * lax.rsqrt(jnp.mean(acc * acc, axis=-1, keepdims=True) + EPS) * g2_ref[...]
    h2_ref[...] = h2.astype(BF16)
    h_hi, h_lo = _split(h2, 2)
    z = _dot_nt(rw_ref[...], h_hi) + _dot_nt(rw_ref[...], h_lo)
    logits = z[0:N_EXPERTS] + z[N_EXPERTS:2 * N_EXPERTS]
    ex = jnp.exp(logits - jnp.max(logits, axis=0, keepdims=True))
    aff_ref[...] = ex / jnp.sum(ex, axis=0, keepdims=True)


def _outproj(ya, yb, of, ob, cg, yd, xt, dng, mg, wo, g2, rw):
    T = xt.shape[0]
    tm = ROW_TILE
    row = lambda w: pl.BlockSpec((tm, w), lambda i: (i, 0))
    const = lambda shape: pl.BlockSpec(shape, lambda i: (0, 0))
    return pl.pallas_call(
        _outproj_kernel,
        out_shape=[jax.ShapeDtypeStruct((T, D_MODEL), F32), jax.ShapeDtypeStruct((T, D_MODEL), BF16),
                   jax.ShapeDtypeStruct((N_EXPERTS, T), F32)],
        grid=(T // tm,),
        in_specs=[row(GROUP_W)] * 6 + [row(D_MODEL), const((1, GROUP_W)), const((1, D_MODEL)),
                                       const((D_MODEL, D_MODEL)), const((1, D_MODEL)),
                                       const((2 * N_EXPERTS, D_MODEL))],
        out_specs=[row(D_MODEL), row(D_MODEL), pl.BlockSpec((N_EXPERTS, tm), lambda i: (0, i))],
        compiler_params=_cparams(("parallel",)),
        name="outproj",
    )(ya, yb, of, ob, cg, yd, xt, dng, mg, wo, g2, rw)


def _strict_upper(n):
    return (lax.broadcasted_iota(I32, (n, n), 0) < lax.broadcasted_iota(I32, (n, n), 1)).astype(BF16)


def _route_kernel(aff_ref, mask_ref, pos_ref, wsel_ref, offs_ref, *, cap, nblk):
    keys = pltpu.bitcast(aff_ref[...], I32)

    def bit_body(i, thr):
        cand = thr | lax.shift_left(jnp.int32(1), 30 - i)
        cnt = jnp.sum((keys >= cand).astype(F32), axis=1, keepdims=True)
        return jnp.where(cnt >= cap, cand, thr)

    thr = lax.fori_loop(0, 31, bit_body, jnp.zeros((N_EXPERTS, 1), I32))
    need = cap - jnp.sum((keys > thr).astype(F32), axis=1, keepdims=True)
    su = _strict_upper(TOK_BLK)

    def blk_body(j, carry):
        ceq, csel = carry
        st = pl.multiple_of(j * TOK_BLK, TOK_BLK)
        kb = pltpu.bitcast(aff_ref[:, pl.ds(st, TOK_BLK)], I32)
        eqf = (kb == thr).astype(F32)
        rank = ceq + _dot(eqf, su)
        sel = ((kb > thr) | ((kb == thr) & (rank < need))).astype(F32)
        mask_ref[:, pl.ds(st, TOK_BLK)] = sel
        wsel_ref[:, pl.ds(st, TOK_BLK)] = sel * aff_ref[:, pl.ds(st, TOK_BLK)]
        pos_ref[:, pl.ds(st, TOK_BLK)] = _dot(sel, su)
        offs_ref[j] = jnp.broadcast_to(csel.astype(I32), (N_EXPERTS, 128))
        return (ceq + jnp.sum(eqf, axis=1, keepdims=True), csel + jnp.sum(sel, axis=1, keepdims=True))

    zero = jnp.zeros((N_EXPERTS, 1), F32)
    lax.fori_loop(0, nblk, blk_body, (zero, zero))


def _route(aff_t, cap):
    E, T = aff_t.shape
    nblk = T // TOK_BLK
    full = lambda shape: pl.BlockSpec(shape, lambda i: (0,) * len(shape))
    return pl.pallas_call(
        functools.partial(_route_kernel, cap=cap, nblk=nblk),
        out_shape=[jax.ShapeDtypeStruct((E, T), F32)] * 3 + [jax.ShapeDtypeStruct((nblk, E, 128), I32)],
        grid=(1,),
        in_specs=[full((E, T))],
        out_specs=[full((E, T))] * 3 + [full((nblk, E, 128))],
        compiler_params=_cparams(("arbitrary",)),
        name="route",
    )(aff_t)


def _gather_kernel(offs_ref, x_ref, m_ref, p_ref, out_ref):
    g, j = pl.program_id(0), pl.program_id(1)

    @pl.when(j == 0)
    def _():
        out_ref[...] = jnp.zeros_like(out_ref)

    W = GATHER_W
    slot0 = lax.broadcasted_iota(I32, (W, TOK_BLK), 0).astype(F32)
    for jb in range(GATHER_JB):
        blk = j * GATHER_JB + jb
        cols = slice(jb * TOK_BLK, (jb + 1) * TOK_BLK)
        targets, bases, passes = [], [], []
        for k in range(GATHER_E):
            off = offs_ref[blk * N_EXPERTS + g * GATHER_E + k]
            cnt = offs_ref[(blk + 1) * N_EXPERTS + g * GATHER_E + k] - off
            base = (off // 16) * 16
            targets.append(jnp.where(m_ref[k, :, cols] > 0.0, p_ref[k, :, cols] + (off - base).astype(F32), -1.0))
            bases.append(base)
            passes.append((off - base + cnt + W - 1) // W)
        onehot = jnp.concatenate([jnp.where(slot0 == t, 1.0, 0.0) for t in targets], axis=0).astype(BF16)
        picked = jnp.dot(onehot, x_ref[cols, :], preferred_element_type=F32).astype(BF16)
        for k in range(GATHER_E):
            out_ref[k, pl.ds(pl.multiple_of(bases[k], 16), W), :] += picked[k * W:(k + 1) * W]
        for k in range(GATHER_E):
            def more(ps, carry, k=k, cols=cols):
                hot = jnp.where(slot0 + (ps * W).astype(F32) == targets[k], 1.0, 0.0).astype(BF16)
                extra = jnp.dot(hot, x_ref[cols, :], preferred_element_type=F32).astype(BF16)
                out_ref[k, pl.ds(pl.multiple_of(bases[k] + ps * W, 16), W), :] += extra
                return carry

            lax.fori_loop(1, passes[k], more, 0)


def _gather(h2, mask3, pos3, offs, cap):
    T = h2.shape[0]
    nblk = T // TOK_BLK
    cap_x = cap + 2 * GATHER_W
    rows = GATHER_JB * TOK_BLK
    grid_spec = pltpu.PrefetchScalarGridSpec(
        num_scalar_prefetch=1,
        grid=(N_EXPERTS // GATHER_E, nblk // GATHER_JB),
        in_specs=[pl.BlockSpec((rows, D_MODEL), lambda g, j, o: (j, 0)),
                  pl.BlockSpec((GATHER_E, 1, rows), lambda g, j, o: (g, 0, j)),
                  pl.BlockSpec((GATHER_E, 1, rows), lambda g, j, o: (g, 0, j))],
        out_specs=pl.BlockSpec((GATHER_E, cap_x, D_MODEL), lambda g, j, o: (g, 0, 0),
                               pipeline_mode=pl.Buffered(1)),
    )
    return pl.pallas_call(
        _gather_kernel,
        out_shape=jax.ShapeDtypeStruct((N_EXPERTS, cap_x, D_MODEL), BF16),
        grid_spec=grid_spec,
        compiler_params=_cparams(("parallel", "arbitrary")),
        name="gather",
    )(offs, h2, mask3, pos3)


def _ffn_kernel(x_ref, wg_ref, wu_ref, wd_ref, o_ref, w_scr):
    @pl.when(pl.program_id(1) == 0)
    def _():
        for n, w_ref in enumerate((wg_ref, wu_ref, wd_ref)):
            w_scr[n] = w_ref[0, 0].astype(BF16)

    x = x_ref[0]
    g = jnp.dot(x, w_scr[0], preferred_element_type=F32)
    u = jnp.dot(x, w_scr[1], preferred_element_type=F32)
    h = (g * _sigmoid(g)) * u
    o_ref[0] = _dot(h, w_scr[2]).astype(BF16)


def _ffn(xe, wg, wu, wd, layer, cap):
    tf = min(512, cap)
    wspec = pl.BlockSpec((1, 1, D_MODEL, D_MODEL), lambda e, i: (layer, e, 0, 0))
    return pl.pallas_call(
        _ffn_kernel,
        out_shape=jax.ShapeDtypeStruct((N_EXPERTS, cap, D_MODEL), BF16),
        grid=(N_EXPERTS, cap // tf),
        in_specs=[pl.BlockSpec((1, tf, D_MODEL), lambda e, i: (e, i, 0)), wspec, wspec, wspec],
        out_specs=pl.BlockSpec((1, tf, D_MODEL), lambda e, i: (e, i, 0)),
        scratch_shapes=[pltpu.VMEM((3, D_MODEL, D_MODEL), BF16)],
        compiler_params=_cparams(("parallel", "arbitrary")),
        name="ffn",
    )(xe, wg, wu, wd)


def _slot_window_start(off, cap):
    return jnp.minimum((off // TOK_BLK) * TOK_BLK, cap - 2 * TOK_BLK)


def _scatter_kernel(offs_ref, x2_ref, w_ref, p_ref, gf_ref, *rest, cap, final):
    ye, out_ref = rest[:N_EXPERTS], rest[N_EXPERTS]
    j = pl.program_id(0)
    ps, ws = p_ref[...].T, w_ref[...].T
    offs = [offs_ref[j * N_EXPERTS + e] for e in range(N_EXPERTS)]
    cnts = [offs_ref[(j + 1) * N_EXPERTS + e] - offs[e] for e in range(N_EXPERTS)]
    rels = [offs[e] - _slot_window_start(offs[e], cap) for e in range(N_EXPERTS)]
    weights = [ws[:, e:e + 1] for e in range(N_EXPERTS)]
    SUB = 16
    most = cnts[0]
    for e in range(1, N_EXPERTS):
        most = jnp.maximum(most, cnts[e])

    def onehot(e, start, width):
        slot = lax.broadcasted_iota(I32, (TOK_BLK, width), 1).astype(F32)
        return jnp.where(slot == ps[:, e:e + 1] + (rels[e] - start).astype(F32), weights[e], 0.0).astype(BF16)

    W = SCATTER_W

    @pl.when(most <= W - SUB)
    def _():
        acc = x2_ref[...]
        for g0 in range(0, N_EXPERTS, 2):
            group = (g0, g0 + 1)
            r0 = {e: pl.multiple_of(jnp.minimum((rels[e] // SUB) * SUB, 2 * TOK_BLK - W), SUB) for e in group}
            acc = acc + jnp.dot(jnp.concatenate([onehot(e, r0[e], W) for e in group], axis=1),
                                jnp.concatenate([ye[e][0, pl.ds(r0[e], W), :] for e in group], axis=0),
                                preferred_element_type=F32)
        out_ref[...] = acc

    @pl.when(most > W - SUB)
    def _():
        out_ref[...] = x2_ref[...]
        for e in range(N_EXPERTS):
            @pl.when(cnts[e] > 0)
            def _(e=e):
                out_ref[...] += jnp.dot(onehot(e, 0, 2 * TOK_BLK), ye[e][0], preferred_element_type=F32)

    if final:
        x = out_ref[...]
        out_ref[...] = x * lax.rsqrt(jnp.mean(x * x, axis=-1, keepdims=True) + EPS) * gf_ref[...]


def _scatter(x2, wsel, pos, ye, offs, gf, cap, final):
    T = x2.shape[0]
    nblk = T // TOK_BLK
    assert cap >= 2 * TOK_BLK and cap % TOK_BLK == 0

    def ye_spec(e):
        return pl.BlockSpec((pl.Element(1), pl.Element(2 * TOK_BLK), pl.Element(D_MODEL)),
                            lambda j, o: (e, pl.multiple_of(_slot_window_start(o[j * N_EXPERTS + e], cap), TOK_BLK),
                                          0))

    row = lambda w: pl.BlockSpec((TOK_BLK, w), lambda j, o: (j, 0))
    col = pl.BlockSpec((N_EXPERTS, TOK_BLK), lambda j, o: (0, j))
    grid_spec = pltpu.PrefetchScalarGridSpec(
        num_scalar_prefetch=1,
        grid=(nblk,),
        in_specs=[row(D_MODEL), col, col, pl.BlockSpec((1, D_MODEL), lambda j, o: (0, 0))]
                 + [ye_spec(e) for e in range(N_EXPERTS)],
        out_specs=row(D_MODEL),
    )
    return pl.pallas_call(
        functools.partial(_scatter_kernel, cap=cap, final=final),
        out_shape=jax.ShapeDtypeStruct((T, D_MODEL), F32),
        grid_spec=grid_spec,
        compiler_params=_cparams(("arbitrary",)),
        name="scatter",
    )(offs, x2, wsel, pos, gf, *([ye] * N_EXPERTS))


def _block_diag(w):
    n, a, b = w.shape
    out = jnp.zeros((n * a, n * b), w.dtype)
    for i in range(n):
        out = out.at[i * a:(i + 1) * a, i * b:(i + 1) * b].set(w[i])
    return out


def _rope_tables(S):
    inv = ROPE_THETA ** (-jnp.arange(0, ROT_DIM, 2, dtype=F32) / ROT_DIM)
    ang = jnp.arange(S, dtype=F32)[:, None] * inv[None, :]
    cos, sin = jnp.cos(ang), jnp.sin(ang)
    half = ROT_DIM // 2
    ones, zeros = jnp.ones((S, ATT_HD - ROT_DIM), F32), jnp.zeros((S, ATT_HD - ROT_DIM), F32)
    zh = jnp.zeros((S, half), F32)
    c = jnp.concatenate([cos, cos, ones], axis=1)
    s1 = jnp.concatenate([-sin, zh, zeros], axis=1)
    s2 = jnp.concatenate([zh, sin, zeros], axis=1)
    return tuple(jnp.tile(t, (1, 2)) for t in (c, s1, s2))


def _expanders():
    eg = np.zeros((2, 128, GROUP_W), np.float32)
    eb = np.zeros((2, 128, GROUP_W), np.float32)
    for d in range(2):
        for h in range(DN_HEADS):
            eg[d, d * DN_HEADS + h, h * DN_DK:(h + 1) * DN_DK] = 1.0
            eb[d, 2 * DN_HEADS + d * DN_HEADS + h, h * DN_DK:(h + 1) * DN_DK] = 1.0
    hd = np.arange(GROUP_W) // DN_DK
    bd = (hd[:, None] == hd[None, :]).astype(np.float32)
    return jnp.asarray(eg), jnp.asarray(eb), jnp.asarray(bd)


def _layer_params(l, norm1_g, w_in, conv_a_w, conv_a_b, rg_wa, rg_ba, rg_wx, rg_bx, rg_lambda, pool_w, pool_scale,
                  dn_conv_w, dn_A_log, dn_dt_bias, dn_norm_g, mix_norm_g, w_out, norm2_g, router_w,
                  exp_w_gate, exp_w_up, exp_w_down):
    w = w_in[l]
    w_cat = jnp.concatenate([w[:, 0:1536], w[:, 1552:2576], w[:, 1536:1552],
                             jnp.zeros((D_MODEL, COL_END - COL_CAB - 16), F32)], axis=1).astype(BF16)
    pad8 = lambda v: jnp.concatenate([v.reshape(1, 2 * DN_HEADS), jnp.zeros((1, 120), F32)], axis=1)
    return dict(
        g1=norm1_g[l].reshape(1, D_MODEL), w_cat=w_cat,
        conv_a_w=conv_a_w[l], conv_a_b=conv_a_b[l].reshape(1, GROUP_W),
        wa=[_block_diag(rg_wa[l, d]).astype(BF16) for d in range(2)],
        wx=[_block_diag(rg_wx[l, d]).astype(BF16) for d in range(2)],
        ba=[rg_ba[l, d].reshape(1, GROUP_W) for d in range(2)],
        bx=[rg_bx[l, d].reshape(1, GROUP_W) for d in range(2)],
        lam=[rg_lambda[l, d].reshape(1, GROUP_W) for d in range(2)],
        pool_w=_block_diag(pool_w[l]).astype(BF16), pool_scale=pool_scale[l].reshape(1, GROUP_W),
        dn_conv_w=dn_conv_w[l], alog=pad8(dn_A_log[l]), dtb=pad8(dn_dt_bias[l]),
        dng=jnp.tile(dn_norm_g[l], DN_HEADS).reshape(1, GROUP_W),
        mg=mix_norm_g[l].reshape(1, D_MODEL), wo=w_out[l].astype(BF16),
        g2=norm2_g[l].reshape(1, D_MODEL),
        rw=jnp.concatenate(_split(router_w[l].T, 2), axis=0),
        wg=exp_w_gate, wu=exp_w_up, wd=exp_w_down,
    )


def _chunk_rows(gcn, d):
    B, S, _ = gcn.shape
    n = S // DN_CHUNK
    g = gcn[:, :, d * DN_HEADS:(d + 1) * DN_HEADS].reshape(B, n, DN_CHUNK, DN_HEADS)
    return g.transpose(0, 1, 3, 2).reshape(B, n, 1, GROUP_W)


def _encoder(x, layers, final_g, consts):
    B, S, D = x.shape
    T = B * S
    cap = max(1, EC_CAPACITY * T // N_EXPERTS)
    eg, eb, bdmask = consts
    rope_c, rope_s1, rope_s2 = _rope_tables(S)
    xt = x.reshape(T, D)
    for l, p in enumerate(layers):
        pa, pb, pq, pg, pd, pcab = _inproj(xt, p["g1"], p["w_cat"], rope_c, rope_s1, rope_s2, S)
        pa3 = pa.reshape(B, S, 2 * GROUP_W)
        rg = lambda d: (p["conv_a_w"], p["conv_a_b"], p["wa"][d], p["ba"][d], p["wx"][d], p["bx"][d], p["lam"][d])
        hf = _rglru(pa3, None, *rg(0), reverse=False)
        ya = _rglru(pa3, hf, *rg(1), reverse=True)
        yb = _pool(pb.reshape(B, S, GROUP_W), p["pool_w"], p["pool_scale"])
        qn, kn, vv, gcf, gcb, bef, beb, gcn = _dnprep(pq.reshape(B, S, 3 * GROUP_W), pcab.reshape(B, S, 128),
                                                      p["dn_conv_w"], p["alog"], p["dtb"], eg, eb)
        of, ob = _delta(qn, kn, vv, gcf, gcb, bef, beb, _chunk_rows(gcn, 0), _chunk_rows(gcn, 1), bdmask)
        yd = _attention(pd.reshape(B, S, 3 * GROUP_W))
        flat = lambda a: a.reshape(T, GROUP_W)
        x2, h2, aff_t = _outproj(flat(ya), flat(yb), flat(of), flat(ob), pg, flat(yd), xt,
                                 p["dng"], p["mg"], p["wo"], p["g2"], p["rw"])
        mask, pos, wsel, offs3 = _route(aff_t, cap)
        offs = jnp.concatenate([offs3[:, :, 0].reshape(-1), jnp.full((N_EXPERTS,), cap, I32)])
        xe = _gather(h2, mask.reshape(N_EXPERTS, 1, T), pos.reshape(N_EXPERTS, 1, T), offs, cap)
        ye = _ffn(xe, p["wg"], p["wu"], p["wd"], l, cap)
        xt = _scatter(x2, wsel, pos, ye, offs, final_g, cap, final=(l == len(layers) - 1))
    return xt.reshape(B, S, D)


def kernel(x_prompt, x_sample, norm1_g, w_in, conv_a_w, conv_a_b, rg_wa, rg_ba, rg_wx, rg_bx, rg_lambda, pool_w, pool_scale, dn_conv_w, dn_A_log, dn_dt_bias, dn_norm_g, mix_norm_g, w_out, norm2_g, router_w, exp_w_gate, exp_w_up, exp_w_down, final_norm_g):
    layers = [_layer_params(l, norm1_g, w_in, conv_a_w, conv_a_b, rg_wa, rg_ba, rg_wx, rg_bx, rg_lambda, pool_w,
                            pool_scale, dn_conv_w, dn_A_log, dn_dt_bias, dn_norm_g, mix_norm_g, w_out, norm2_g,
                            router_w, exp_w_gate, exp_w_up, exp_w_down) for l in range(DEPTH)]
    consts = _expanders()
    final_g = final_norm_g.reshape(1, D_MODEL)
    return (_encoder(x_prompt, layers, final_g, consts), _encoder(x_sample, layers, final_g, consts))
```

```python
import functools
import math
import numpy as np
import jax
import jax.numpy as jnp
from jax import lax
from jax.experimental import pallas as pl
from jax.experimental.pallas import tpu as pltpu

F32, BF16, I32 = jnp.float32, jnp.bfloat16, jnp.int32

D_MODEL = 1024
DEPTH = 2
GROUP_W = 256
RG_C = 8.0
POOL_WINDOWS = (2, 4, 8, 16)
DN_HEADS = 4
DN_DK = 64
DN_CHUNK = 64
ATT_HD = 64
ROT_DIM = 16
ROPE_THETA = 500000.0
ATT_WINDOWS = (128, 512, 2048)
ATT_DILATIONS = (1, 4, 16)
N_EXPERTS = 16
EC_CAPACITY = 2
EPS = 1e-6
NEG = -1e30

COL_A, COL_B, COL_CQ, COL_CG, COL_D, COL_CAB, COL_END = 0, 512, 768, 1536, 1792, 2560, 2688

ROW_TILE = 512
HALO = 8
DELTA_CB = 8
ATT_QB = 128
ATT_UNROLL = 8
TOK_BLK = 256
GATHER_W = 64
GATHER_JB = 8
GATHER_E = 4
SCATTER_W = 128
VMEM_LIMIT = 56 * 1024 * 1024


def _cparams(sem):
    return pltpu.CompilerParams(dimension_semantics=sem, vmem_limit_bytes=VMEM_LIMIT)


def _dot(a, b):
    return jnp.dot(a.astype(BF16), b.astype(BF16), preferred_element_type=F32)


def _split(a, pieces):
    out, rem = [], a
    for i in range(pieces):
        t = rem.astype(BF16)
        out.append(t)
        if i + 1 < pieces:
            rem = rem - t.astype(F32)
    return out


def _dot_sel(a, sel, pieces=3):
    sel = sel.astype(BF16)
    return sum(jnp.dot(t, sel, preferred_element_type=F32) for t in _split(a, pieces))


def _dot_nt(a, b):
    return lax.dot_general(a.astype(BF16), b.astype(BF16), (((1,), (1,)), ((), ())), preferred_element_type=F32)


def _dot_tn(a, b):
    return lax.dot_general(a.astype(BF16), b.astype(BF16), (((0,), (0,)), ((), ())), preferred_element_type=F32)


def _sigmoid(x):
    return 1.0 / (1.0 + jnp.exp(-x))


def _softplus(x):
    return jnp.maximum(x, 0.0) + jnp.log1p(jnp.exp(-jnp.abs(x)))


def _shift_rows(e, k):
    n = e.shape[0]
    return e if k % n == 0 else pltpu.roll(e, (-k) % n, axis=0)


def _with_halo(cur_ref, prev_ref, next_ref, first, last):
    prev = jnp.where(first, 0.0, prev_ref[0])
    nxt = jnp.where(last, 0.0, next_ref[0])
    return jnp.concatenate([prev, cur_ref[0], nxt], axis=0)


def _halo_specs(ts, width, S, blk_of):
    per = ts // HALO
    last = S // HALO - 1
    cur = pl.BlockSpec((1, ts, width), lambda b, c: (b, blk_of(c), 0))
    prev = pl.BlockSpec((1, HALO, width), lambda b, c: (b, jnp.maximum(blk_of(c) * per - 1, 0), 0))
    nxt = pl.BlockSpec((1, HALO, width), lambda b, c: (b, jnp.minimum((blk_of(c) + 1) * per, last), 0))
    return cur, prev, nxt


def _inproj_kernel(x_ref, g_ref, w_ref, c_ref, s1_ref, s2_ref, pa_ref, pb_ref, pq_ref, pg_ref, pd_ref, pcab_ref):
    x = x_ref[...]
    h = x * lax.rsqrt(jnp.mean(x * x, axis=-1, keepdims=True) + EPS) * g_ref[...]
    hb = h.astype(BF16)

    def mm(lo, hi):
        return jnp.dot(hb, w_ref[:, lo:hi], preferred_element_type=F32)

    pa_ref[...] = mm(COL_A, COL_B)
    pb_ref[...] = mm(COL_B, COL_CQ)
    pq_ref[...] = mm(COL_CQ, COL_CG)
    pg_ref[...] = mm(COL_CG, COL_D)
    c, s1, s2 = c_ref[...], s1_ref[...], s2_ref[...]
    for part in range(2):
        yy = mm(COL_D + 256 * part, COL_D + 256 * part + 256)
        for half in range(2):
            y = yy[:, 128 * half:128 * half + 128]
            y = y * c + pltpu.roll(y, 128 - ROT_DIM // 2, axis=1) * s1 + pltpu.roll(y, ROT_DIM // 2, axis=1) * s2
            if part == 0:
                y = y * (ATT_HD ** -0.5)
            pd_ref[:, 256 * part + 128 * half:256 * part + 128 * half + 128] = y
    vcab = mm(COL_D + 512, COL_END)
    pd_ref[:, 512:768] = vcab[:, 0:256]
    pcab_ref[...] = vcab[:, 256:384]


def _inproj(xt, g1, w_cat, rope_c, rope_s1, rope_s2, S):
    T = xt.shape[0]
    tm = min(ROW_TILE, S)
    per_seq = S // tm
    widths = (512, 256, 768, 256, 768, 128)
    row = lambda w: pl.BlockSpec((tm, w), lambda i: (i, 0))
    rope = pl.BlockSpec((tm, 128), lambda i: (i % per_seq, 0))
    return pl.pallas_call(
        _inproj_kernel,
        out_shape=[jax.ShapeDtypeStruct((T, w), F32) for w in widths],
        grid=(T // tm,),
        in_specs=[row(D_MODEL), pl.BlockSpec((1, D_MODEL), lambda i: (0, 0)),
                  pl.BlockSpec((D_MODEL, COL_END), lambda i: (0, 0)), rope, rope, rope],
        out_specs=[row(w) for w in widths],
        compiler_params=_cparams(("parallel",)),
        name="inproj",
    )(xt, g1, w_cat, rope_c, rope_s1, rope_s2)


def _rglru_kernel(*refs, reverse, ts, nc):
    if reverse:
        (cur_ref, prev_ref, next_ref, gate_ref, hf_ref, cw_ref, cb_ref, wa_ref, ba_ref, wx_ref, bx_ref, lam_ref,
         out_ref, a_ref, b_ref, carry_ref) = refs
    else:
        (cur_ref, prev_ref, next_ref, cw_ref, cb_ref, wa_ref, ba_ref, wx_ref, bx_ref, lam_ref,
         out_ref, a_ref, b_ref, carry_ref) = refs
    c = pl.program_id(1)
    blk = (nc - 1 - c) if reverse else c
    e = _with_halo(cur_ref, prev_ref, next_ref, blk == 0, blk == nc - 1)
    cw = cw_ref[...]
    sl = slice(HALO, HALO + ts)
    u = (cw[0:1] * _shift_rows(e, -2)[sl] + cw[1:2] * _shift_rows(e, -1)[sl] + cw[2:3] * e[sl]
         + cw[3:4] * _shift_rows(e, 1)[sl]) + cb_ref[...]
    r = _sigmoid(_dot(u, wa_ref[...]) + ba_ref[...])
    i = _sigmoid(_dot(u, wx_ref[...]) + bx_ref[...])
    log_a = -RG_C * r * _softplus(-lam_ref[...])
    a_ref[...] = jnp.exp(log_a)
    b_ref[...] = jnp.sqrt(1.0 - jnp.exp(2.0 * log_a)) * (i * u)

    @pl.when(c == 0)
    def _():
        carry_ref[...] = jnp.zeros_like(carry_ref)

    row = lax.broadcasted_iota(I32, (HALO, GROUP_W), 0)
    nt = ts // HALO

    def body(it, carry):
        ti = (nt - 1 - it) if reverse else it
        st = pl.multiple_of(ti * HALO, HALO)
        a = a_ref[pl.ds(st, HALO), :]
        b = b_ref[pl.ds(st, HALO), :]
        for d in (1, 2, 4):
            k = d if reverse else -d
            valid = (row < HALO - d) if reverse else (row >= d)
            b = jnp.where(valid, a * _shift_rows(b, k) + b, b)
            a = jnp.where(valid, a * _shift_rows(a, k), a)
        h = a * carry + b
        if reverse:
            g = gate_ref[0, pl.ds(st, HALO), :]
            cdf = 0.5 * (1.0 + jnp.tanh(np.float32(np.sqrt(2.0 / np.pi)) * (g + 0.044715 * (g * g * g))))
            out_ref[0, pl.ds(st, HALO), :] = (g * cdf) * (hf_ref[0, pl.ds(st, HALO), :] + h)
            return jnp.broadcast_to(h[0:1], h.shape)
        out_ref[0, pl.ds(st, HALO), :] = h
        return jnp.broadcast_to(h[HALO - 1:HALO], h.shape)

    carry_ref[...] = lax.fori_loop(0, nt, body, carry_ref[...], unroll=4)


def _rglru(pa3, hf, cw, cb, wa, ba, wx, bx, lam, reverse):
    B, S, _ = pa3.shape
    ts = min(ROW_TILE, S)
    nc = S // ts
    blk_of = (lambda c: nc - 1 - c) if reverse else (lambda c: c)
    cur, prev, nxt = _halo_specs(ts, GROUP_W, S, blk_of)
    tile = pl.BlockSpec((1, ts, GROUP_W), lambda b, c: (b, blk_of(c), 0))
    const = lambda shape: pl.BlockSpec(shape, lambda b, c: (0,) * len(shape))
    in_specs = [cur, prev, nxt]
    args = [pa3, pa3, pa3]
    if reverse:
        in_specs += [pl.BlockSpec((1, ts, GROUP_W), lambda b, c: (b, blk_of(c), 1)), tile]
        args += [pa3, hf]
    in_specs += [const((4, GROUP_W)), const((1, GROUP_W)), const((GROUP_W, GROUP_W)), const((1, GROUP_W)),
                 const((GROUP_W, GROUP_W)), const((1, GROUP_W)), const((1, GROUP_W))]
    args += [cw, cb, wa, ba, wx, bx, lam]
    return pl.pallas_call(
        functools.partial(_rglru_kernel, reverse=reverse, ts=ts, nc=nc),
        out_shape=jax.ShapeDtypeStruct((B, S, GROUP_W), F32),
        grid=(B, nc),
        in_specs=in_specs,
        out_specs=tile,
        scratch_shapes=[pltpu.VMEM((ts, GROUP_W), F32), pltpu.VMEM((ts, GROUP_W), F32),
                        pltpu.VMEM((HALO, GROUP_W), F32)],
        compiler_params=_cparams(("parallel", "arbitrary")),
        name="rglru_bwd" if reverse else "rglru_fwd",
    )(*args)


def _pool_kernel(cur_ref, prev_ref, next_ref, w_ref, sc_ref, out_ref, *, ts, nc, S):
    c = pl.program_id(1)
    e = _with_halo(cur_ref, prev_ref, next_ref, c == 0, c == nc - 1)
    sl = slice(HALO, HALO + ts)
    a2 = e + _shift_rows(e, 1)
    a4 = a2 + _shift_rows(a2, 2)
    a8 = a4 + _shift_rows(a4, 4)
    a16 = a8 + _shift_rows(a8, 8)
    sums = [_shift_rows(a, -(w // 2))[sl] for a, w in zip((a2, a4, a8, a16), POOL_WINDOWS)]
    gi = lax.broadcasted_iota(I32, (ts, GROUP_W), 1) // (GROUP_W // len(POOL_WINDOWS))
    ssum = jnp.where(gi == 0, sums[0], jnp.where(gi == 1, sums[1], jnp.where(gi == 2, sums[2], sums[3])))
    hw = jnp.where(gi == 0, 1, jnp.where(gi == 1, 2, jnp.where(gi == 2, 4, 8)))
    t = c * ts + lax.broadcasted_iota(I32, (ts, GROUP_W), 0)
    cnt = (jnp.minimum(t + hw, S) - jnp.maximum(t - hw, 0)).astype(F32)
    p = ssum / cnt - e[sl]
    out_ref[0] = _dot(p, w_ref[...]) * sc_ref[...]


def _pool(pb3, w_bd, scale):
    B, S, _ = pb3.shape
    ts = min(ROW_TILE, S)
    nc = S // ts
    cur, prev, nxt = _halo_specs(ts, GROUP_W, S, lambda c: c)
    return pl.pallas_call(
        functools.partial(_pool_kernel, ts=ts, nc=nc, S=S),
        out_shape=jax.ShapeDtypeStruct((B, S, GROUP_W), F32),
        grid=(B, nc),
        in_specs=[cur, prev, nxt, pl.BlockSpec((GROUP_W, GROUP_W), lambda b, c: (0, 0)),
                  pl.BlockSpec((1, GROUP_W), lambda b, c: (0, 0))],
        out_specs=pl.BlockSpec((1, ts, GROUP_W), lambda b, c: (b, c, 0)),
        compiler_params=_cparams(("parallel", "parallel")),
        name="pool",
    )(pb3, pb3, pb3, w_bd, scale)


def _head_sum_matrix(n, group):
    r = lax.broadcasted_iota(I32, (n, n), 0) // group
    c = lax.broadcasted_iota(I32, (n, n), 1) // group
    return (r == c).astype(F32)


def _dnprep_kernel(cur_ref, prev_ref, next_ref, cab_ref, cw_ref, alog_ref, dtb_ref, eg_ref, eb_ref,
                   q_ref, k_ref, v_ref, gcf_ref, gcb_ref, bef_ref, beb_ref, gcn_ref, *, ts, nc):
    c = pl.program_id(1)
    e = _with_halo(cur_ref, prev_ref, next_ref, c == 0, c == nc - 1)
    cw = cw_ref[...]
    sl = slice(HALO, HALO + ts)
    y = (cw[0:1] * _shift_rows(e, -2)[sl] + cw[1:2] * _shift_rows(e, -1)[sl] + cw[2:3] * e[sl]
         + cw[3:4] * _shift_rows(e, 1)[sl])
    y = y * _sigmoid(y)
    q, k = y[:, 0:GROUP_W], y[:, GROUP_W:2 * GROUP_W]
    hs = _head_sum_matrix(GROUP_W, DN_DK)
    q_ref[0] = q * lax.rsqrt(_dot_sel(q * q, hs, 2) + EPS) * (DN_DK ** -0.5)
    k_ref[0] = k * lax.rsqrt(_dot_sel(k * k, hs, 2) + EPS)
    v_ref[0] = y[:, 2 * GROUP_W:3 * GROUP_W]
    cab = cab_ref[0]
    g = -jnp.exp(alog_ref[...]) * _softplus(cab + dtb_ref[...])
    beta = _sigmoid(cab)
    bef_ref[0] = _dot_sel(beta, eb_ref[0], 2)
    beb_ref[0] = _dot_sel(beta, eb_ref[1], 2)
    pos = lax.broadcasted_iota(I32, (ts, 128), 0) % DN_CHUNK
    fwd = rev = g
    d = 1
    while d < DN_CHUNK:
        fwd = fwd + jnp.where(pos >= d, _shift_rows(fwd, -d), 0.0)
        rev = rev + jnp.where(pos < DN_CHUNK - d, _shift_rows(rev, d), 0.0)
        d *= 2
    gcn = jnp.where(lax.broadcasted_iota(I32, (ts, 128), 1) < DN_HEADS, fwd, rev)
    gcn_ref[0] = gcn
    gcf_ref[0] = _dot_sel(gcn, eg_ref[0])
    gcb_ref[0] = _dot_sel(gcn, eg_ref[1])


def _dnprep(pq3, pcab3, cw, alog_row, dtb_row, eg, eb):
    B, S, _ = pq3.shape
    ts = min(ROW_TILE, S)
    nc = S // ts
    cur, prev, nxt = _halo_specs(ts, 3 * GROUP_W, S, lambda c: c)
    tile = pl.BlockSpec((1, ts, GROUP_W), lambda b, c: (b, c, 0))
    const = lambda shape: pl.BlockSpec(shape, lambda b, c: (0,) * len(shape))
    return pl.pallas_call(
        functools.partial(_dnprep_kernel, ts=ts, nc=nc),
        out_shape=[jax.ShapeDtypeStruct((B, S, GROUP_W), F32)] * 7 + [jax.ShapeDtypeStruct((B, S, 128), F32)],
        grid=(B, nc),
        in_specs=[cur, prev, nxt, pl.BlockSpec((1, ts, 128), lambda b, c: (b, c, 0)),
                  const((4, 3 * GROUP_W)), const((1, 128)), const((1, 128)),
                  const((2, 128, GROUP_W)), const((2, 128, GROUP_W))],
        out_specs=[tile] * 7 + [pl.BlockSpec((1, ts, 128), lambda b, c: (b, c, 0))],
        compiler_params=_cparams(("parallel", "parallel")),
        name="dnprep",
    )(pq3, pq3, pq3, pcab3, cw, alog_row, dtb_row, eg, eb)


def _delta_kernel(qf, kf, vf, gcf, bef, grf, qb, kb, vb, gcb, beb, grb, bd_ref, of_ref, ob_ref,
                  s_ref, p_scr, x_scr, wq_scr, u_scr, at_scr, kd_scr, vb_scr, kg_scr, *, cb):
    j = pl.program_id(1)

    @pl.when(j == 0)
    def _():
        s_ref[...] = jnp.zeros_like(s_ref)

    C = DN_CHUNK
    bd = bd_ref[...] > 0.0
    c_idx = lax.broadcasted_iota(I32, (C, GROUP_W), 0)
    m_idx = lax.broadcasted_iota(I32, (C, GROUP_W), 1) % C
    eye = (c_idx == m_idx).astype(F32)

    def blockdiag(x):
        return jnp.where(bd, jnp.tile(x.astype(BF16), (DN_HEADS, 1)), jnp.zeros((), BF16))

    dirs = ((qf, kf, vf, gcf, bef, grf, of_ref), (qb, kb, vb, gcb, beb, grb, ob_ref))
    units = [(d, ci) for ci in range(cb) for d in range(2)]

    def chunk_of(d, ci):
        return cb - 1 - ci if d == 1 else ci

    def prepare(ui):
        d, ci = units[ui]
        q_r, k_r, v_r, gc_r, be_r, gr_r, _ = dirs[d]
        rev = d == 1
        tril = (c_idx <= m_idx) if rev else (c_idx >= m_idx)
        strict = (c_idx < m_idx) if rev else (c_idx > m_idx)
        last = 0 if rev else C - 1
        cc = chunk_of(d, ci)
        rows = slice(cc * C, (cc + 1) * C)
        q, k, v = q_r[0, rows, :], k_r[0, rows, :], v_r[0, rows, :]
        gc, be, gr = gc_r[0, rows, :], be_r[0, rows, :], gr_r[0, cc]
        eg = jnp.exp(gc)
        kbeta = k * be
        kkqk = _dot_nt(jnp.concatenate([kbeta, q], axis=0), blockdiag(k))
        decay = jnp.exp(jnp.where(tril, gc - gr, -jnp.inf))
        at_scr[ui] = (kkqk[C:] * decay).astype(BF16)
        p = -jnp.where(strict, kkqk[:C] * decay, 0.0)
        p_scr[ui] = p.astype(BF16)
        x_scr[ui] = eye + p
        wq_scr[ui, C:2 * C, :] = (q * eg).astype(BF16)
        kd_scr[ui] = (k * jnp.exp(gc[last:last + 1] - gc)).astype(BF16)
        vb_scr[ui] = (v * be).astype(BF16)
        kg_scr[ui] = (kbeta * eg).astype(BF16)

    def solve(group):
        for ui in group:
            p = p_scr[ui]
            p_scr[ui] = _dot(p, blockdiag(p)).astype(BF16)
        for _ in range(4):
            for ui in group:
                p, x = p_scr[ui], x_scr[ui]
                px = _dot(jnp.concatenate([p, x.astype(BF16)], axis=0), blockdiag(p))
                p_scr[ui] = px[:C].astype(BF16)
                x_scr[ui] = x + px[C:]
        for ui in group:
            x = x_scr[ui]
            x = x + _dot(x, blockdiag(p_scr[ui]))
            u_scr[ui] = _dot(x, blockdiag(vb_scr[ui]))
            wq_scr[ui, 0:C, :] = _dot(x, blockdiag(kg_scr[ui])).astype(BF16)

    def recur(ui):
        d, ci = units[ui]
        gc_r, o_r = dirs[d][3], dirs[d][6]
        cc = chunk_of(d, ci)
        last = cc * C + (0 if d == 1 else C - 1)
        state = s_ref[d]
        ws_qs = _dot(wq_scr[ui], state)
        v_new = u_scr[ui] - ws_qs[:C]
        o_r[0, cc * C:(cc + 1) * C, :] = ws_qs[C:] + _dot(at_scr[ui], blockdiag(v_new))
        s_ref[d] = (state * jnp.exp(gc_r[0, last:last + 1, :])
                    + jnp.where(bd, _dot_tn(kd_scr[ui], v_new), 0.0))

    for ui in range(len(units)):
        prepare(ui)
    solve(range(len(units)))
    for ui in range(len(units)):
        recur(ui)


def _delta(qn, kn, vv, gcf, gcb, bef, beb, grf, grb, bdmask):
    B, S, _ = qn.shape
    cb = DELTA_CB
    rb = cb * DN_CHUNK
    nb = S // rb
    f = lambda b, j: (b, j, 0)
    r = lambda b, j: (b, nb - 1 - j, 0)
    tf = pl.BlockSpec((1, rb, GROUP_W), f)
    tr = pl.BlockSpec((1, rb, GROUP_W), r)
    gf = pl.BlockSpec((1, cb, 1, GROUP_W), lambda b, j: (b, j, 0, 0))
    gr = pl.BlockSpec((1, cb, 1, GROUP_W), lambda b, j: (b, nb - 1 - j, 0, 0))
    return pl.pallas_call(
        functools.partial(_delta_kernel, cb=cb),
        out_shape=[jax.ShapeDtypeStruct((B, S, GROUP_W), F32)] * 2,
        grid=(B, nb),
        in_specs=[tf, tf, tf, tf, tf, gf, tr, tr, tr, tr, tr, gr,
                  pl.BlockSpec((GROUP_W, GROUP_W), lambda b, j: (0, 0))],
        out_specs=[tf, tr],
        scratch_shapes=[pltpu.VMEM((2, GROUP_W, GROUP_W), F32)]
                       + [pltpu.VMEM((2 * cb, n * DN_CHUNK, GROUP_W), dt) for n, dt in
                          ((1, BF16), (1, F32), (2, BF16), (1, F32), (1, BF16), (1, BF16), (1, BF16), (1, BF16))],
        compiler_params=_cparams(("parallel", "arbitrary")),
        name="delta",
    )(qn, kn, vv, gcf, bef, grf, qn, kn, vv, gcb, beb, grb, bdmask)


def _attn_kernel(q_ref, k_ref, v_ref, out_ref, m_ref, l_ref, o_ref, *, S):
    head0 = lax.broadcasted_iota(I32, (1, 128), 1) < ATT_HD

    def rows(start, n, dil):
        return pl.ds(start, n) if dil == 1 else pl.ds(start, n, stride=dil)

    for bi, (win, dil) in enumerate(zip(ATT_WINDOWS, ATT_DILATIONS)):
        half = win // (2 * dil)
        L = S // dil
        qb_n = min(ATT_QB, L)
        kw = min(L, qb_n + 2 * half)
        nqb = L // qb_n

        def block_stats(idx, dil=dil, half=half, L=L, qb_n=qb_n, kw=kw, nqb=nqb):
            r = idx // nqb
            m0 = (idx % nqb) * qb_n
            ks = jnp.clip(m0 - half, 0, L - kw)
            qsel = rows(r + m0 * dil, qb_n, dil)
            ksel = rows(r + ks * dil, kw, dil)
            q = q_ref[0, qsel, :]
            kk = k_ref[0, ksel, :].astype(BF16)
            vv = v_ref[0, ksel, :].astype(BF16)
            rel = (lax.broadcasted_iota(I32, (qb_n, kw), 1) - lax.broadcasted_iota(I32, (qb_n, kw), 0)
                   + (ks - m0 + half))
            valid = rel.astype(jnp.uint32) <= 2 * half
            q2 = jnp.concatenate([jnp.where(head0, q, 0.0), jnp.where(head0, 0.0, q)], axis=0)
            s = jnp.where(jnp.concatenate([valid, valid], axis=0), _dot_nt(q2, kk), NEG)
            m = jnp.max(s, axis=-1, keepdims=True)
            p = jnp.exp(s - m)
            l = jnp.sum(p, axis=-1, keepdims=True)
            o = _dot(p, vv)
            return (qsel, jnp.where(head0, m[:qb_n], m[qb_n:]), jnp.where(head0, l[:qb_n], l[qb_n:]),
                    jnp.where(head0, o[:qb_n], o[qb_n:]))

        unroll = math.gcd(ATT_UNROLL, dil * nqb)

        def body(it, carry, bi=bi, unroll=unroll):
            stats = [block_stats(it * unroll + s) for s in range(unroll)]
            if bi > 0:
                olds = [(m_ref[qsel, :], l_ref[qsel, :], o_ref[qsel, :]) for qsel, _, _, _ in stats]
                merged = []
                for (qsel, m, l, o), (m_old, l_old, o_old) in zip(stats, olds):
                    m_new = jnp.maximum(m_old, m)
                    w_old, w_cur = jnp.exp(m_old - m_new), jnp.exp(m - m_new)
                    merged.append((qsel, m_new, w_old * l_old + w_cur * l, w_old * o_old + w_cur * o))
                stats = merged
            for qsel, m, l, o in stats:
                if bi == len(ATT_WINDOWS) - 1:
                    out_ref[0, qsel, :] = o / l
                else:
                    m_ref[qsel, :] = m
                    l_ref[qsel, :] = l
                    o_ref[qsel, :] = o
            return carry

        lax.fori_loop(0, dil * nqb // unroll, body, 0)


def _attention(pd3):
    B, S, _ = pd3.shape
    spec = lambda off: pl.BlockSpec((1, S, 128), lambda b, p: (b, 0, off + p))
    return pl.pallas_call(
        functools.partial(_attn_kernel, S=S),
        out_shape=jax.ShapeDtypeStruct((B, S, GROUP_W), F32),
        grid=(B, 2),
        in_specs=[spec(0), spec(2), spec(4)],
        out_specs=pl.BlockSpec((1, S, 128), lambda b, p: (b, 0, p)),
        scratch_shapes=[pltpu.VMEM((S, 128), F32)] * 3,
        compiler_params=_cparams(("parallel", "parallel")),
        name="attention",
    )(pd3, pd3, pd3)


def _outproj_kernel(ya_ref, yb_ref, of_ref, ob_ref, cg_ref, yd_ref, x_ref, dng_ref, mg_ref, wo_ref, g2_ref, rw_ref,
                    x2_ref, h2_ref, aff_ref):
    o = of_ref[...] + ob_ref[...]
    ms = _dot_sel(o * o, _head_sum_matrix(GROUP_W, DN_DK), 2) * (1.0 / DN_DK)
    cg = cg_ref[...]
    yc = (o * lax.rsqrt(ms + EPS) * dng_ref[...]) * (cg * _sigmoid(cg))
    acc = x_ref[...]
    for gi, y in enumerate((ya_ref[...], yb_ref[...], yc, yd_ref[...])):
        sl = slice(gi * GROUP_W, (gi + 1) * GROUP_W)
        mix = y * lax.rsqrt(jnp.mean(y * y, axis=-1, keepdims=True) + EPS) * mg_ref[:, sl]
        acc = acc + _dot(mix, wo_ref[sl, :])
    x2_ref[...] = acc
    h2 = acc * lax.rsqrt(jnp.mean(acc * acc, axis=-1, keepdims=True) + EPS) * g2_ref[...]
    h2_ref[...] = h2.astype(BF16)
    h_hi, h_lo = _split(h2, 2)
    z = _dot_nt(rw_ref[...], h_hi) + _dot_nt(rw_ref[...], h_lo)
    logits = z[0:N_EXPERTS] + z[N_EXPERTS:2 * N_EXPERTS]
    ex = jnp.exp(logits - jnp.max(logits, axis=0, keepdims=True))
    aff_ref[...] = ex / jnp.sum(ex, axis=0, keepdims=True)


def _outproj(ya, yb, of, ob, cg, yd, xt, dng, mg, wo, g2, rw):
    T = xt.shape[0]
    tm = ROW_TILE
    row = lambda w: pl.BlockSpec((tm, w), lambda i: (i, 0))
    const = lambda shape: pl.BlockSpec(shape, lambda i: (0, 0))
    return pl.pallas_call(
        _outproj_kernel,
        out_shape=[jax.ShapeDtypeStruct((T, D_MODEL), F32), jax.ShapeDtypeStruct((T, D_MODEL), BF16),
                   jax.ShapeDtypeStruct((N_EXPERTS, T), F32)],
        grid=(T // tm,),
        in_specs=[row(GROUP_W)] * 6 + [row(D_MODEL), const((1, GROUP_W)), const((1, D_MODEL)),
                                       const((D_MODEL, D_MODEL)), const((1, D_MODEL)),
                                       const((2 * N_EXPERTS, D_MODEL))],
        out_specs=[row(D_MODEL), row(D_MODEL), pl.BlockSpec((N_EXPERTS, tm), lambda i: (0, i))],
        compiler_params=_cparams(("parallel",)),
        name="outproj",
    )(ya, yb, of, ob, cg, yd, xt, dng, mg, wo, g2, rw)


def _strict_upper(n):
    return (lax.broadcasted_iota(I32, (n, n), 0) < lax.broadcasted_iota(I32, (n, n), 1)).astype(BF16)


def _route_kernel(aff_ref, mask_ref, pos_ref, wsel_ref, offs_ref, *, cap, nblk):
    keys = pltpu.bitcast(aff_ref[...], I32)

    def bit_body(i, thr):
        cand = thr | lax.shift_left(jnp.int32(1), 30 - i)
        cnt = jnp.sum((keys >= cand).astype(F32), axis=1, keepdims=True)
        return jnp.where(cnt >= cap, cand, thr)

    thr = lax.fori_loop(0, 31, bit_body, jnp.zeros((N_EXPERTS, 1), I32))
    need = cap - jnp.sum((keys > thr).astype(F32), axis=1, keepdims=True)
    su = _strict_upper(TOK_BLK)

    def blk_body(j, carry):
        ceq, csel = carry
        st = pl.multiple_of(j * TOK_BLK, TOK_BLK)
        kb = pltpu.bitcast(aff_ref[:, pl.ds(st, TOK_BLK)], I32)
        eqf = (kb == thr).astype(F32)
        rank = ceq + _dot(eqf, su)
        sel = ((kb > thr) | ((kb == thr) & (rank < need))).astype(F32)
        mask_ref[:, pl.ds(st, TOK_BLK)] = sel
        wsel_ref[:, pl.ds(st, TOK_BLK)] = sel * aff_ref[:, pl.ds(st, TOK_BLK)]
        pos_ref[:, pl.ds(st, TOK_BLK)] = _dot(sel, su)
        offs_ref[j] = jnp.broadcast_to(csel.astype(I32), (N_EXPERTS, 128))
        return (ceq + jnp.sum(eqf, axis=1, keepdims=True), csel + jnp.sum(sel, axis=1, keepdims=True))

    zero = jnp.zeros((N_EXPERTS, 1), F32)
    lax.fori_loop(0, nblk, blk_body, (zero, zero))


def _route(aff_t, cap):
    E, T = aff_t.shape
    nblk = T // TOK_BLK
    full = lambda shape: pl.BlockSpec(shape, lambda i: (0,) * len(shape))
    return pl.pallas_call(
        functools.partial(_route_kernel, cap=cap, nblk=nblk),
        out_shape=[jax.ShapeDtypeStruct((E, T), F32)] * 3 + [jax.ShapeDtypeStruct((nblk, E, 128), I32)],
        grid=(1,),
        in_specs=[full((E, T))],
        out_specs=[full((E, T))] * 3 + [full((nblk, E, 128))],
        compiler_params=_cparams(("arbitrary",)),
        name="route",
    )(aff_t)


def _gather_kernel(offs_ref, x_ref, m_ref, p_ref, out_ref):
    g, j = pl.program_id(0), pl.program_id(1)

    @pl.when(j == 0)
    def _():
        out_ref[...] = jnp.zeros_like(out_ref)

    W = GATHER_W
    slot0 = lax.broadcasted_iota(I32, (W, TOK_BLK), 0).astype(F32)
    for jb in range(GATHER_JB):
        blk = j * GATHER_JB + jb
        cols = slice(jb * TOK_BLK, (jb + 1) * TOK_BLK)
        targets, bases, passes = [], [], []
        for k in range(GATHER_E):
            off = offs_ref[blk * N_EXPERTS + g * GATHER_E + k]
            cnt = offs_ref[(blk + 1) * N_EXPERTS + g * GATHER_E + k] - off
            base = (off // 16) * 16
            targets.append(jnp.where(m_ref[k, :, cols] > 0.0, p_ref[k, :, cols] + (off - base).astype(F32), -1.0))
            bases.append(base)
            passes.append((off - base + cnt + W - 1) // W)
        onehot = jnp.concatenate([jnp.where(slot0 == t, 1.0, 0.0) for t in targets], axis=0).astype(BF16)
        picked = jnp.dot(onehot, x_ref[cols, :], preferred_element_type=F32).astype(BF16)
        for k in range(GATHER_E):
            out_ref[k, pl.ds(pl.multiple_of(bases[k], 16), W), :] += picked[k * W:(k + 1) * W]
        for k in range(GATHER_E):
            def more(ps, carry, k=k, cols=cols):
                hot = jnp.where(slot0 + (ps * W).astype(F32) == targets[k], 1.0, 0.0).astype(BF16)
                extra = jnp.dot(hot, x_ref[cols, :], preferred_element_type=F32).astype(BF16)
                out_ref[k, pl.ds(pl.multiple_of(bases[k] + ps * W, 16), W), :] += extra
                return carry

            lax.fori_loop(1, passes[k], more, 0)


def _gather(h2, mask3, pos3, offs, cap):
    T = h2.shape[0]
    nblk = T // TOK_BLK
    cap_x = cap + 2 * GATHER_W
    rows = GATHER_JB * TOK_BLK
    grid_spec = pltpu.PrefetchScalarGridSpec(
        num_scalar_prefetch=1,
        grid=(N_EXPERTS // GATHER_E, nblk // GATHER_JB),
        in_specs=[pl.BlockSpec((rows, D_MODEL), lambda g, j, o: (j, 0)),
                  pl.BlockSpec((GATHER_E, 1, rows), lambda g, j, o: (g, 0, j)),
                  pl.BlockSpec((GATHER_E, 1, rows), lambda g, j, o: (g, 0, j))],
        out_specs=pl.BlockSpec((GATHER_E, cap_x, D_MODEL), lambda g, j, o: (g, 0, 0),
                               pipeline_mode=pl.Buffered(1)),
    )
    return pl.pallas_call(
        _gather_kernel,
        out_shape=jax.ShapeDtypeStruct((N_EXPERTS, cap_x, D_MODEL), BF16),
        grid_spec=grid_spec,
        compiler_params=_cparams(("parallel", "arbitrary")),
        name="gather",
    )(offs, h2, mask3, pos3)


def _ffn_kernel(*refs, bounds):
    n = len(bounds) - 1
    xs, w_refs, outs, w_scr = refs[:n], refs[n:n + 3], refs[n + 3:2 * n + 3], refs[2 * n + 3]
    i = pl.program_id(1)

    @pl.when(i == 0)
    def _():
        for k, w_ref in enumerate(w_refs):
            w_scr[k] = w_ref[0, 0].astype(BF16)

    for b in range(n):
        @pl.when(jnp.logical_and(i >= bounds[b], i < bounds[b + 1]))
        def _(b=b):
            x = xs[b][0]
            g = jnp.dot(x, w_scr[0], preferred_element_type=F32)
            u = jnp.dot(x, w_scr[1], preferred_element_type=F32)
            h = (g * _sigmoid(g)) * u
            outs[b][0] = _dot(h, w_scr[2]).astype(BF16)


def _ffn(xes, wg, wu, wd, layer, caps):
    tf = min(512, *caps)
    bounds = [0]
    for cap in caps:
        bounds.append(bounds[-1] + cap // tf)

    def tile_spec(b):
        return pl.BlockSpec((1, tf, D_MODEL),
                            lambda e, i: (e, jnp.clip(i - bounds[b], 0, caps[b] // tf - 1), 0))

    wspec = pl.BlockSpec((1, 1, D_MODEL, D_MODEL), lambda e, i: (layer, e, 0, 0))
    return pl.pallas_call(
        functools.partial(_ffn_kernel, bounds=tuple(bounds)),
        out_shape=[jax.ShapeDtypeStruct((N_EXPERTS, cap, D_MODEL), BF16) for cap in caps],
        grid=(N_EXPERTS, bounds[-1]),
        in_specs=[tile_spec(b) for b in range(len(caps))] + [wspec] * 3,
        out_specs=[tile_spec(b) for b in range(len(caps))],
        scratch_shapes=[pltpu.VMEM((3, D_MODEL, D_MODEL), BF16)],
        compiler_params=_cparams(("parallel", "arbitrary")),
        name="ffn",
    )(*xes, wg, wu, wd)


def _slot_window_start(off, cap):
    return jnp.minimum((off // TOK_BLK) * TOK_BLK, cap - 2 * TOK_BLK)


def _scatter_kernel(offs_ref, x2_ref, w_ref, p_ref, gf_ref, *rest, cap, final):
    ye, out_ref = rest[:N_EXPERTS], rest[N_EXPERTS]
    j = pl.program_id(0)
    ps, ws = p_ref[...].T, w_ref[...].T
    offs = [offs_ref[j * N_EXPERTS + e] for e in range(N_EXPERTS)]
    cnts = [offs_ref[(j + 1) * N_EXPERTS + e] - offs[e] for e in range(N_EXPERTS)]
    rels = [offs[e] - _slot_window_start(offs[e], cap) for e in range(N_EXPERTS)]
    weights = [ws[:, e:e + 1] for e in range(N_EXPERTS)]
    SUB = 16
    most = cnts[0]
    for e in range(1, N_EXPERTS):
        most = jnp.maximum(most, cnts[e])

    def onehot(e, start, width):
        slot = lax.broadcasted_iota(I32, (TOK_BLK, width), 1).astype(F32)
        return jnp.where(slot == ps[:, e:e + 1] + (rels[e] - start).astype(F32), weights[e], 0.0).astype(BF16)

    W = SCATTER_W

    @pl.when(most <= W - SUB)
    def _():
        acc = x2_ref[...]
        for g0 in range(0, N_EXPERTS, 2):
            group = (g0, g0 + 1)
            r0 = {e: pl.multiple_of(jnp.minimum((rels[e] // SUB) * SUB, 2 * TOK_BLK - W), SUB) for e in group}
            acc = acc + jnp.dot(jnp.concatenate([onehot(e, r0[e], W) for e in group], axis=1),
                                jnp.concatenate([ye[e][0, pl.ds(r0[e], W), :] for e in group], axis=0),
                                preferred_element_type=F32)
        out_ref[...] = acc

    @pl.when(most > W - SUB)
    def _():
        out_ref[...] = x2_ref[...]
        for e in range(N_EXPERTS):
            @pl.when(cnts[e] > 0)
            def _(e=e):
                out_ref[...] += jnp.dot(onehot(e, 0, 2 * TOK_BLK), ye[e][0], preferred_element_type=F32)

    if final:
        x = out_ref[...]
        out_ref[...] = x * lax.rsqrt(jnp.mean(x * x, axis=-1, keepdims=True) + EPS) * gf_ref[...]


def _scatter(x2, wsel, pos, ye, offs, gf, cap, final):
    T = x2.shape[0]
    nblk = T // TOK_BLK
    assert cap >= 2 * TOK_BLK and cap % TOK_BLK == 0

    def ye_spec(e):
        return pl.BlockSpec((pl.Element(1), pl.Element(2 * TOK_BLK), pl.Element(D_MODEL)),
                            lambda j, o: (e, pl.multiple_of(_slot_window_start(o[j * N_EXPERTS + e], cap), TOK_BLK),
                                          0))

    row = lambda w: pl.BlockSpec((TOK_BLK, w), lambda j, o: (j, 0))
    col = pl.BlockSpec((N_EXPERTS, TOK_BLK), lambda j, o: (0, j))
    grid_spec = pltpu.PrefetchScalarGridSpec(
        num_scalar_prefetch=1,
        grid=(nblk,),
        in_specs=[row(D_MODEL), col, col, pl.BlockSpec((1, D_MODEL), lambda j, o: (0, 0))]
                 + [ye_spec(e) for e in range(N_EXPERTS)],
        out_specs=row(D_MODEL),
    )
    return pl.pallas_call(
        functools.partial(_scatter_kernel, cap=cap, final=final),
        out_shape=jax.ShapeDtypeStruct((T, D_MODEL), F32),
        grid_spec=grid_spec,
        compiler_params=_cparams(("arbitrary",)),
        name="scatter",
    )(offs, x2, wsel, pos, gf, *([ye] * N_EXPERTS))


def _block_diag(w):
    n, a, b = w.shape
    out = jnp.zeros((n * a, n * b), w.dtype)
    for i in range(n):
        out = out.at[i * a:(i + 1) * a, i * b:(i + 1) * b].set(w[i])
    return out


def _rope_tables(S):
    inv = ROPE_THETA ** (-jnp.arange(0, ROT_DIM, 2, dtype=F32) / ROT_DIM)
    ang = jnp.arange(S, dtype=F32)[:, None] * inv[None, :]
    cos, sin = jnp.cos(ang), jnp.sin(ang)
    half = ROT_DIM // 2
    ones, zeros = jnp.ones((S, ATT_HD - ROT_DIM), F32), jnp.zeros((S, ATT_HD - ROT_DIM), F32)
    zh = jnp.zeros((S, half), F32)
    c = jnp.concatenate([cos, cos, ones], axis=1)
    s1 = jnp.concatenate([-sin, zh, zeros], axis=1)
    s2 = jnp.concatenate([zh, sin, zeros], axis=1)
    return tuple(jnp.tile(t, (1, 2)) for t in (c, s1, s2))


def _expanders():
    eg = np.zeros((2, 128, GROUP_W), np.float32)
    eb = np.zeros((2, 128, GROUP_W), np.float32)
    for d in range(2):
        for h in range(DN_HEADS):
            eg[d, d * DN_HEADS + h, h * DN_DK:(h + 1) * DN_DK] = 1.0
            eb[d, 2 * DN_HEADS + d * DN_HEADS + h, h * DN_DK:(h + 1) * DN_DK] = 1.0
    hd = np.arange(GROUP_W) // DN_DK
    bd = (hd[:, None] == hd[None, :]).astype(np.float32)
    return jnp.asarray(eg), jnp.asarray(eb), jnp.asarray(bd)


def _layer_params(l, norm1_g, w_in, conv_a_w, conv_a_b, rg_wa, rg_ba, rg_wx, rg_bx, rg_lambda, pool_w, pool_scale,
                  dn_conv_w, dn_A_log, dn_dt_bias, dn_norm_g, mix_norm_g, w_out, norm2_g, router_w,
                  exp_w_gate, exp_w_up, exp_w_down):
    w = w_in[l]
    w_cat = jnp.concatenate([w[:, 0:1536], w[:, 1552:2576], w[:, 1536:1552],
                             jnp.zeros((D_MODEL, COL_END - COL_CAB - 16), F32)], axis=1).astype(BF16)
    pad8 = lambda v: jnp.concatenate([v.reshape(1, 2 * DN_HEADS), jnp.zeros((1, 120), F32)], axis=1)
    return dict(
        g1=norm1_g[l].reshape(1, D_MODEL), w_cat=w_cat,
        conv_a_w=conv_a_w[l], conv_a_b=conv_a_b[l].reshape(1, GROUP_W),
        wa=[_block_diag(rg_wa[l, d]).astype(BF16) for d in range(2)],
        wx=[_block_diag(rg_wx[l, d]).astype(BF16) for d in range(2)],
        ba=[rg_ba[l, d].reshape(1, GROUP_W) for d in range(2)],
        bx=[rg_bx[l, d].reshape(1, GROUP_W) for d in range(2)],
        lam=[rg_lambda[l, d].reshape(1, GROUP_W) for d in range(2)],
        pool_w=_block_diag(pool_w[l]).astype(BF16), pool_scale=pool_scale[l].reshape(1, GROUP_W),
        dn_conv_w=dn_conv_w[l], alog=pad8(dn_A_log[l]), dtb=pad8(dn_dt_bias[l]),
        dng=jnp.tile(dn_norm_g[l], DN_HEADS).reshape(1, GROUP_W),
        mg=mix_norm_g[l].reshape(1, D_MODEL), wo=w_out[l].astype(BF16),
        g2=norm2_g[l].reshape(1, D_MODEL),
        rw=jnp.concatenate(_split(router_w[l].T, 2), axis=0),
        wg=exp_w_gate, wu=exp_w_up, wd=exp_w_down,
    )


def _chunk_rows(gcn, d):
    B, S, _ = gcn.shape
    n = S // DN_CHUNK
    g = gcn[:, :, d * DN_HEADS:(d + 1) * DN_HEADS].reshape(B, n, DN_CHUNK, DN_HEADS)
    return g.transpose(0, 1, 3, 2).reshape(B, n, 1, GROUP_W)


def _capacity(T):
    return max(1, EC_CAPACITY * T // N_EXPERTS)


def _mix_and_route(xt, B, S, p, consts, ropes):
    T = B * S
    cap = _capacity(T)
    eg, eb, bdmask = consts
    pa, pb, pq, pg, pd, pcab = _inproj(xt, p["g1"], p["w_cat"], *ropes, S)
    pa3 = pa.reshape(B, S, 2 * GROUP_W)
    rg = lambda d: (p["conv_a_w"], p["conv_a_b"], p["wa"][d], p["ba"][d], p["wx"][d], p["bx"][d], p["lam"][d])
    hf = _rglru(pa3, None, *rg(0), reverse=False)
    ya = _rglru(pa3, hf, *rg(1), reverse=True)
    yb = _pool(pb.reshape(B, S, GROUP_W), p["pool_w"], p["pool_scale"])
    qn, kn, vv, gcf, gcb, bef, beb, gcn = _dnprep(pq.reshape(B, S, 3 * GROUP_W), pcab.reshape(B, S, 128),
                                                  p["dn_conv_w"], p["alog"], p["dtb"], eg, eb)
    of, ob = _delta(qn, kn, vv, gcf, gcb, bef, beb, _chunk_rows(gcn, 0), _chunk_rows(gcn, 1), bdmask)
    yd = _attention(pd.reshape(B, S, 3 * GROUP_W))
    flat = lambda a: a.reshape(T, GROUP_W)
    x2, h2, aff_t = _outproj(flat(ya), flat(yb), flat(of), flat(ob), pg, flat(yd), xt,
                             p["dng"], p["mg"], p["wo"], p["g2"], p["rw"])
    mask, pos, wsel, offs3 = _route(aff_t, cap)
    offs = jnp.concatenate([offs3[:, :, 0].reshape(-1), jnp.full((N_EXPERTS,), cap, I32)])
    xe = _gather(h2, mask.reshape(N_EXPERTS, 1, T), pos.reshape(N_EXPERTS, 1, T), offs, cap)
    return dict(x2=x2, wsel=wsel, pos=pos, offs=offs, xe=xe, cap=cap)


def kernel(x_prompt, x_sample, norm1_g, w_in, conv_a_w, conv_a_b, rg_wa, rg_ba, rg_wx, rg_bx, rg_lambda, pool_w, pool_scale, dn_conv_w, dn_A_log, dn_dt_bias, dn_norm_g, mix_norm_g, w_out, norm2_g, router_w, exp_w_gate, exp_w_up, exp_w_down, final_norm_g):
    layers = [_layer_params(l, norm1_g, w_in, conv_a_w, conv_a_b, rg_wa, rg_ba, rg_wx, rg_bx, rg_lambda, pool_w,
                            pool_scale, dn_conv_w, dn_A_log, dn_dt_bias, dn_norm_g, mix_norm_g, w_out, norm2_g,
                            router_w, exp_w_gate, exp_w_up, exp_w_down) for l in range(DEPTH)]
    consts = _expanders()
    final_g = final_norm_g.reshape(1, D_MODEL)
    batches = (x_prompt, x_sample)
    shapes = [x.shape for x in batches]
    ropes = [_rope_tables(S) for _, S, _ in shapes]
    xts = [x.reshape(B * S, D) for x, (B, S, D) in zip(batches, shapes)]
    for l, p in enumerate(layers):
        st = [_mix_and_route(xt, B, S, p, consts, rp) for xt, (B, S, _), rp in zip(xts, shapes, ropes)]
        yes = _ffn([s["xe"] for s in st], p["wg"], p["wu"], p["wd"], l, [s["cap"] for s in st])
        xts = [_scatter(s["x2"], s["wsel"], s["pos"], ye, s["offs"], final_g, s["cap"], final=(l == DEPTH - 1))
               for s, ye in zip(st, yes)]
    return tuple(xt.reshape(shape) for xt, shape in zip(xts, shapes))
```

```python
import functools
import math
import numpy as np
import jax
import jax.numpy as jnp
from jax import lax
from jax.experimental import pallas as pl
from jax.experimental.pallas import tpu as pltpu

F32, BF16, I32 = jnp.float32, jnp.bfloat16, jnp.int32

D_MODEL = 1024
DEPTH = 2
GROUP_W = 256
RG_C = 8.0
POOL_WINDOWS = (2, 4, 8, 16)
DN_HEADS = 4
DN_DK = 64
DN_CHUNK = 64
ATT_HD = 64
ROT_DIM = 16
ROPE_THETA = 500000.0
ATT_WINDOWS = (128, 512, 2048)
ATT_DILATIONS = (1, 4, 16)
N_EXPERTS = 16
EC_CAPACITY = 2
EPS = 1e-6
NEG = -1e30

COL_A, COL_B, COL_CQ, COL_CG, COL_D, COL_CAB, COL_END = 0, 512, 768, 1536, 1792, 2560, 2688

ROW_TILE = 1024
HALO = 8
DELTA_CB = 8
ATT_QB = 128
ATT_UNROLL = 8
TOK_BLK = 256
GATHER_W = 64
GATHER_JB = 8
GATHER_E = 4
SCATTER_W = 128
VMEM_LIMIT = 56 * 1024 * 1024


def _cparams(sem):
    return pltpu.CompilerParams(dimension_semantics=sem, vmem_limit_bytes=VMEM_LIMIT)


def _dot(a, b):
    return jnp.dot(a.astype(BF16), b.astype(BF16), preferred_element_type=F32)


def _split(a, pieces):
    out, rem = [], a
    for i in range(pieces):
        t = rem.astype(BF16)
        out.append(t)
        if i + 1 < pieces:
            rem = rem - t.astype(F32)
    return out


def _dot_sel(a, sel, pieces=3):
    sel = sel.astype(BF16)
    return sum(jnp.dot(t, sel, preferred_element_type=F32) for t in _split(a, pieces))


def _dot_nt(a, b):
    return lax.dot_general(a.astype(BF16), b.astype(BF16), (((1,), (1,)), ((), ())), preferred_element_type=F32)


def _dot_tn(a, b):
    return lax.dot_general(a.astype(BF16), b.astype(BF16), (((0,), (0,)), ((), ())), preferred_element_type=F32)


def _sigmoid(x):
    return 1.0 / (1.0 + jnp.exp(-x))


def _softplus(x):
    return jnp.maximum(x, 0.0) + jnp.log1p(jnp.exp(-jnp.abs(x)))


def _shift_rows(e, k):
    n = e.shape[0]
    return e if k % n == 0 else pltpu.roll(e, (-k) % n, axis=0)


def _with_halo(cur_ref, prev_ref, next_ref, first, last):
    prev = jnp.where(first, 0.0, prev_ref[0])
    nxt = jnp.where(last, 0.0, next_ref[0])
    return jnp.concatenate([prev, cur_ref[0], nxt], axis=0)


def _halo_specs(ts, width, S, blk_of):
    per = ts // HALO
    last = S // HALO - 1
    cur = pl.BlockSpec((1, ts, width), lambda b, c: (b, blk_of(c), 0))
    prev = pl.BlockSpec((1, HALO, width), lambda b, c: (b, jnp.maximum(blk_of(c) * per - 1, 0), 0))
    nxt = pl.BlockSpec((1, HALO, width), lambda b, c: (b, jnp.minimum((blk_of(c) + 1) * per, last), 0))
    return cur, prev, nxt


def _inproj_kernel(x_ref, g_ref, w_ref, c_ref, s1_ref, s2_ref, pa_ref, pb_ref, pq_ref, pg_ref, pd_ref, pcab_ref):
    x = x_ref[...]
    h = x * lax.rsqrt(jnp.mean(x * x, axis=-1, keepdims=True) + EPS) * g_ref[...]
    hb = h.astype(BF16)

    def mm(lo, hi):
        return jnp.dot(hb, w_ref[:, lo:hi], preferred_element_type=F32)

    pa_ref[...] = mm(COL_A, COL_B)
    pb_ref[...] = mm(COL_B, COL_CQ)
    pq_ref[...] = mm(COL_CQ, COL_CG)
    pg_ref[...] = mm(COL_CG, COL_D)
    c, s1, s2 = c_ref[...], s1_ref[...], s2_ref[...]
    for part in range(2):
        yy = mm(COL_D + 256 * part, COL_D + 256 * part + 256)
        for half in range(2):
            y = yy[:, 128 * half:128 * half + 128]
            y = y * c + pltpu.roll(y, 128 - ROT_DIM // 2, axis=1) * s1 + pltpu.roll(y, ROT_DIM // 2, axis=1) * s2
            if part == 0:
                y = y * (ATT_HD ** -0.5)
            pd_ref[:, 256 * part + 128 * half:256 * part + 128 * half + 128] = y
    vcab = mm(COL_D + 512, COL_END)
    pd_ref[:, 512:768] = vcab[:, 0:256]
    pcab_ref[...] = vcab[:, 256:384]


def _inproj(xt, g1, w_cat, rope_c, rope_s1, rope_s2, S):
    T = xt.shape[0]
    tm = min(ROW_TILE, S)
    per_seq = S // tm
    widths = (512, 256, 768, 256, 768, 128)
    row = lambda w: pl.BlockSpec((tm, w), lambda i: (i, 0))
    rope = pl.BlockSpec((tm, 128), lambda i: (i % per_seq, 0))
    return pl.pallas_call(
        _inproj_kernel,
        out_shape=[jax.ShapeDtypeStruct((T, w), F32) for w in widths],
        grid=(T // tm,),
        in_specs=[row(D_MODEL), pl.BlockSpec((1, D_MODEL), lambda i: (0, 0)),
                  pl.BlockSpec((D_MODEL, COL_END), lambda i: (0, 0)), rope, rope, rope],
        out_specs=[row(w) for w in widths],
        compiler_params=_cparams(("parallel",)),
        name="inproj",
    )(xt, g1, w_cat, rope_c, rope_s1, rope_s2)


def _rglru_kernel(*refs, reverse, ts, nc):
    if reverse:
        (cur_ref, prev_ref, next_ref, gate_ref, hf_ref, cw_ref, cb_ref, wa_ref, ba_ref, wx_ref, bx_ref, lam_ref,
         out_ref, a_ref, b_ref, carry_ref) = refs
    else:
        (cur_ref, prev_ref, next_ref, cw_ref, cb_ref, wa_ref, ba_ref, wx_ref, bx_ref, lam_ref,
         out_ref, a_ref, b_ref, carry_ref) = refs
    c = pl.program_id(1)
    blk = (nc - 1 - c) if reverse else c
    e = _with_halo(cur_ref, prev_ref, next_ref, blk == 0, blk == nc - 1)
    cw = cw_ref[...]
    sl = slice(HALO, HALO + ts)
    u = (cw[0:1] * _shift_rows(e, -2)[sl] + cw[1:2] * _shift_rows(e, -1)[sl] + cw[2:3] * e[sl]
         + cw[3:4] * _shift_rows(e, 1)[sl]) + cb_ref[...]
    r = _sigmoid(_dot(u, wa_ref[...]) + ba_ref[...])
    i = _sigmoid(_dot(u, wx_ref[...]) + bx_ref[...])
    log_a = -RG_C * r * _softplus(-lam_ref[...])
    a_ref[...] = jnp.exp(log_a)
    b_ref[...] = jnp.sqrt(1.0 - jnp.exp(2.0 * log_a)) * (i * u)

    @pl.when(c == 0)
    def _():
        carry_ref[...] = jnp.zeros_like(carry_ref)

    row = lax.broadcasted_iota(I32, (HALO, GROUP_W), 0)
    nt = ts // HALO

    def body(it, carry):
        ti = (nt - 1 - it) if reverse else it
        st = pl.multiple_of(ti * HALO, HALO)
        a = a_ref[pl.ds(st, HALO), :]
        b = b_ref[pl.ds(st, HALO), :]
        for d in (1, 2, 4):
            k = d if reverse else -d
            valid = (row < HALO - d) if reverse else (row >= d)
            b = jnp.where(valid, a * _shift_rows(b, k) + b, b)
            a = jnp.where(valid, a * _shift_rows(a, k), a)
        h = a * carry + b
        if reverse:
            g = gate_ref[0, pl.ds(st, HALO), :]
            cdf = 0.5 * (1.0 + jnp.tanh(np.float32(np.sqrt(2.0 / np.pi)) * (g + 0.044715 * (g * g * g))))
            out_ref[0, pl.ds(st, HALO), :] = (g * cdf) * (hf_ref[0, pl.ds(st, HALO), :] + h)
            return jnp.broadcast_to(h[0:1], h.shape)
        out_ref[0, pl.ds(st, HALO), :] = h
        return jnp.broadcast_to(h[HALO - 1:HALO], h.shape)

    carry_ref[...] = lax.fori_loop(0, nt, body, carry_ref[...], unroll=4)


def _rglru(pa3, hf, cw, cb, wa, ba, wx, bx, lam, reverse):
    B, S, _ = pa3.shape
    ts = min(ROW_TILE, S)
    nc = S // ts
    blk_of = (lambda c: nc - 1 - c) if reverse else (lambda c: c)
    cur, prev, nxt = _halo_specs(ts, GROUP_W, S, blk_of)
    tile = pl.BlockSpec((1, ts, GROUP_W), lambda b, c: (b, blk_of(c), 0))
    const = lambda shape: pl.BlockSpec(shape, lambda b, c: (0,) * len(shape))
    in_specs = [cur, prev, nxt]
    args = [pa3, pa3, pa3]
    if reverse:
        in_specs += [pl.BlockSpec((1, ts, GROUP_W), lambda b, c: (b, blk_of(c), 1)), tile]
        args += [pa3, hf]
    in_specs += [const((4, GROUP_W)), const((1, GROUP_W)), const((GROUP_W, GROUP_W)), const((1, GROUP_W)),
                 const((GROUP_W, GROUP_W)), const((1, GROUP_W)), const((1, GROUP_W))]
    args += [cw, cb, wa, ba, wx, bx, lam]
    return pl.pallas_call(
        functools.partial(_rglru_kernel, reverse=reverse, ts=ts, nc=nc),
        out_shape=jax.ShapeDtypeStruct((B, S, GROUP_W), F32),
        grid=(B, nc),
        in_specs=in_specs,
        out_specs=tile,
        scratch_shapes=[pltpu.VMEM((ts, GROUP_W), F32), pltpu.VMEM((ts, GROUP_W), F32),
                        pltpu.VMEM((HALO, GROUP_W), F32)],
        compiler_params=_cparams(("parallel", "arbitrary")),
        name="rglru_bwd" if reverse else "rglru_fwd",
    )(*args)


def _pool_kernel(cur_ref, prev_ref, next_ref, w_ref, sc_ref, out_ref, *, ts, nc, S):
    c = pl.program_id(1)
    e = _with_halo(cur_ref, prev_ref, next_ref, c == 0, c == nc - 1)
    sl = slice(HALO, HALO + ts)
    a2 = e + _shift_rows(e, 1)
    a4 = a2 + _shift_rows(a2, 2)
    a8 = a4 + _shift_rows(a4, 4)
    a16 = a8 + _shift_rows(a8, 8)
    sums = [_shift_rows(a, -(w // 2))[sl] for a, w in zip((a2, a4, a8, a16), POOL_WINDOWS)]
    gi = lax.broadcasted_iota(I32, (ts, GROUP_W), 1) // (GROUP_W // len(POOL_WINDOWS))
    ssum = jnp.where(gi == 0, sums[0], jnp.where(gi == 1, sums[1], jnp.where(gi == 2, sums[2], sums[3])))
    hw = jnp.where(gi == 0, 1, jnp.where(gi == 1, 2, jnp.where(gi == 2, 4, 8)))
    t = c * ts + lax.broadcasted_iota(I32, (ts, GROUP_W), 0)
    cnt = (jnp.minimum(t + hw, S) - jnp.maximum(t - hw, 0)).astype(F32)
    p = ssum / cnt - e[sl]
    out_ref[0] = _dot(p, w_ref[...]) * sc_ref[...]


def _pool(pb3, w_bd, scale):
    B, S, _ = pb3.shape
    ts = min(ROW_TILE, S)
    nc = S // ts
    cur, prev, nxt = _halo_specs(ts, GROUP_W, S, lambda c: c)
    return pl.pallas_call(
        functools.partial(_pool_kernel, ts=ts, nc=nc, S=S),
        out_shape=jax.ShapeDtypeStruct((B, S, GROUP_W), F32),
        grid=(B, nc),
        in_specs=[cur, prev, nxt, pl.BlockSpec((GROUP_W, GROUP_W), lambda b, c: (0, 0)),
                  pl.BlockSpec((1, GROUP_W), lambda b, c: (0, 0))],
        out_specs=pl.BlockSpec((1, ts, GROUP_W), lambda b, c: (b, c, 0)),
        compiler_params=_cparams(("parallel", "parallel")),
        name="pool",
    )(pb3, pb3, pb3, w_bd, scale)


def _head_sum_matrix(n, group):
    r = lax.broadcasted_iota(I32, (n, n), 0) // group
    c = lax.broadcasted_iota(I32, (n, n), 1) // group
    return (r == c).astype(F32)


def _dnprep_kernel(cur_ref, prev_ref, next_ref, cab_ref, cw_ref, alog_ref, dtb_ref, eg_ref, eb_ref,
                   q_ref, k_ref, v_ref, gcf_ref, gcb_ref, bef_ref, beb_ref, gcn_ref, *, ts, nc):
    c = pl.program_id(1)
    e = _with_halo(cur_ref, prev_ref, next_ref, c == 0, c == nc - 1)
    cw = cw_ref[...]
    sl = slice(HALO, HALO + ts)
    y = (cw[0:1] * _shift_rows(e, -2)[sl] + cw[1:2] * _shift_rows(e, -1)[sl] + cw[2:3] * e[sl]
         + cw[3:4] * _shift_rows(e, 1)[sl])
    y = y * _sigmoid(y)
    q, k = y[:, 0:GROUP_W], y[:, GROUP_W:2 * GROUP_W]
    hs = _head_sum_matrix(GROUP_W, DN_DK)
    q_ref[0] = q * lax.rsqrt(_dot_sel(q * q, hs, 2) + EPS) * (DN_DK ** -0.5)
    k_ref[0] = k * lax.rsqrt(_dot_sel(k * k, hs, 2) + EPS)
    v_ref[0] = y[:, 2 * GROUP_W:3 * GROUP_W]
    cab = cab_ref[0]
    g = -jnp.exp(alog_ref[...]) * _softplus(cab + dtb_ref[...])
    beta = _sigmoid(cab)
    bef_ref[0] = _dot_sel(beta, eb_ref[0], 2)
    beb_ref[0] = _dot_sel(beta, eb_ref[1], 2)
    pos = lax.broadcasted_iota(I32, (ts, 128), 0) % DN_CHUNK
    fwd = rev = g
    d = 1
    while d < DN_CHUNK:
        fwd = fwd + jnp.where(pos >= d, _shift_rows(fwd, -d), 0.0)
        rev = rev + jnp.where(pos < DN_CHUNK - d, _shift_rows(rev, d), 0.0)
        d *= 2
    gcn = jnp.where(lax.broadcasted_iota(I32, (ts, 128), 1) < DN_HEADS, fwd, rev)
    gcn_ref[0] = gcn
    gcf_ref[0] = _dot_sel(gcn, eg_ref[0])
    gcb_ref[0] = _dot_sel(gcn, eg_ref[1])


def _dnprep(pq3, pcab3, cw, alog_row, dtb_row, eg, eb):
    B, S, _ = pq3.shape
    ts = min(ROW_TILE, S)
    nc = S // ts
    cur, prev, nxt = _halo_specs(ts, 3 * GROUP_W, S, lambda c: c)
    tile = pl.BlockSpec((1, ts, GROUP_W), lambda b, c: (b, c, 0))
    const = lambda shape: pl.BlockSpec(shape, lambda b, c: (0,) * len(shape))
    return pl.pallas_call(
        functools.partial(_dnprep_kernel, ts=ts, nc=nc),
        out_shape=[jax.ShapeDtypeStruct((B, S, GROUP_W), F32)] * 7 + [jax.ShapeDtypeStruct((B, S, 128), F32)],
        grid=(B, nc),
        in_specs=[cur, prev, nxt, pl.BlockSpec((1, ts, 128), lambda b, c: (b, c, 0)),
                  const((4, 3 * GROUP_W)), const((1, 128)), const((1, 128)),
                  const((2, 128, GROUP_W)), const((2, 128, GROUP_W))],
        out_specs=[tile] * 7 + [pl.BlockSpec((1, ts, 128), lambda b, c: (b, c, 0))],
        compiler_params=_cparams(("parallel", "parallel")),
        name="dnprep",
    )(pq3, pq3, pq3, pcab3, cw, alog_row, dtb_row, eg, eb)


def _delta_kernel(qf, kf, vf, gcf, bef, grf, qb, kb, vb, gcb, beb, grb, bd_ref, of_ref, ob_ref,
                  s_ref, p_scr, x_scr, wq_scr, u_scr, at_scr, kd_scr, vb_scr, kg_scr, *, cb):
    j = pl.program_id(1)

    @pl.when(j == 0)
    def _():
        s_ref[...] = jnp.zeros_like(s_ref)

    C = DN_CHUNK
    bd = bd_ref[...] > 0.0
    c_idx = lax.broadcasted_iota(I32, (C, GROUP_W), 0)
    m_idx = lax.broadcasted_iota(I32, (C, GROUP_W), 1) % C
    eye = (c_idx == m_idx).astype(F32)

    def blockdiag(x):
        return jnp.where(bd, jnp.tile(x.astype(BF16), (DN_HEADS, 1)), jnp.zeros((), BF16))

    dirs = ((qf, kf, vf, gcf, bef, grf, of_ref), (qb, kb, vb, gcb, beb, grb, ob_ref))
    units = [(d, ci) for ci in range(cb) for d in range(2)]

    def chunk_of(d, ci):
        return cb - 1 - ci if d == 1 else ci

    def prepare(ui):
        d, ci = units[ui]
        q_r, k_r, v_r, gc_r, be_r, gr_r, _ = dirs[d]
        rev = d == 1
        tril = (c_idx <= m_idx) if rev else (c_idx >= m_idx)
        strict = (c_idx < m_idx) if rev else (c_idx > m_idx)
        last = 0 if rev else C - 1
        cc = chunk_of(d, ci)
        rows = slice(cc * C, (cc + 1) * C)
        q, k, v = q_r[0, rows, :], k_r[0, rows, :], v_r[0, rows, :]
        gc, be, gr = gc_r[0, rows, :], be_r[0, rows, :], gr_r[0, cc]
        eg = jnp.exp(gc)
        kbeta = k * be
        kkqk = _dot_nt(jnp.concatenate([kbeta, q], axis=0), blockdiag(k))
        decay = jnp.exp(jnp.where(tril, gc - gr, -jnp.inf))
        at_scr[ui] = (kkqk[C:] * decay).astype(BF16)
        p = -jnp.where(strict, kkqk[:C] * decay, 0.0)
        p_scr[ui] = p.astype(BF16)
        x_scr[ui] = eye + p
        wq_scr[ui, C:2 * C, :] = (q * eg).astype(BF16)
        kd_scr[ui] = (k * jnp.exp(gc[last:last + 1] - gc)).astype(BF16)
        vb_scr[ui] = (v * be).astype(BF16)
        kg_scr[ui] = (kbeta * eg).astype(BF16)

    def solve(group):
        for ui in group:
            p = p_scr[ui]
            p_scr[ui] = _dot(p, blockdiag(p)).astype(BF16)
        for _ in range(4):
            for ui in group:
                p, x = p_scr[ui], x_scr[ui]
                px = _dot(jnp.concatenate([p, x.astype(BF16)], axis=0), blockdiag(p))
                p_scr[ui] = px[:C].astype(BF16)
                x_scr[ui] = x + px[C:]
        for ui in group:
            x = x_scr[ui]
            x = x + _dot(x, blockdiag(p_scr[ui]))
            u_scr[ui] = _dot(x, blockdiag(vb_scr[ui]))
            wq_scr[ui, 0:C, :] = _dot(x, blockdiag(kg_scr[ui])).astype(BF16)

    def recur(ui):
        d, ci = units[ui]
        gc_r, o_r = dirs[d][3], dirs[d][6]
        cc = chunk_of(d, ci)
        last = cc * C + (0 if d == 1 else C - 1)
        state = s_ref[d]
        ws_qs = _dot(wq_scr[ui], state)
        v_new = u_scr[ui] - ws_qs[:C]
        o_r[0, cc * C:(cc + 1) * C, :] = ws_qs[C:] + _dot(at_scr[ui], blockdiag(v_new))
        s_ref[d] = (state * jnp.exp(gc_r[0, last:last + 1, :])
                    + jnp.where(bd, _dot_tn(kd_scr[ui], v_new), 0.0))

    for ui in range(len(units)):
        prepare(ui)
    solve(range(len(units)))
    for ui in range(len(units)):
        recur(ui)


def _delta(qn, kn, vv, gcf, gcb, bef, beb, grf, grb, bdmask):
    B, S, _ = qn.shape
    cb = DELTA_CB
    rb = cb * DN_CHUNK
    nb = S // rb
    f = lambda b, j: (b, j, 0)
    r = lambda b, j: (b, nb - 1 - j, 0)
    tf = pl.BlockSpec((1, rb, GROUP_W), f)
    tr = pl.BlockSpec((1, rb, GROUP_W), r)
    gf = pl.BlockSpec((1, cb, 1, GROUP_W), lambda b, j: (b, j, 0, 0))
    gr = pl.BlockSpec((1, cb, 1, GROUP_W), lambda b, j: (b, nb - 1 - j, 0, 0))
    return pl.pallas_call(
        functools.partial(_delta_kernel, cb=cb),
        out_shape=[jax.ShapeDtypeStruct((B, S, GROUP_W), F32)] * 2,
        grid=(B, nb),
        in_specs=[tf, tf, tf, tf, tf, gf, tr, tr, tr, tr, tr, gr,
                  pl.BlockSpec((GROUP_W, GROUP_W), lambda b, j: (0, 0))],
        out_specs=[tf, tr],
        scratch_shapes=[pltpu.VMEM((2, GROUP_W, GROUP_W), F32)]
                       + [pltpu.VMEM((2 * cb, n * DN_CHUNK, GROUP_W), dt) for n, dt in
                          ((1, BF16), (1, F32), (2, BF16), (1, F32), (1, BF16), (1, BF16), (1, BF16), (1, BF16))],
        compiler_params=_cparams(("parallel", "arbitrary")),
        name="delta",
    )(qn, kn, vv, gcf, bef, grf, qn, kn, vv, gcb, beb, grb, bdmask)


def _attn_kernel(q_ref, k_ref, v_ref, out_ref, m_ref, l_ref, o_ref, *, S):
    head0 = lax.broadcasted_iota(I32, (1, 128), 1) < ATT_HD

    def rows(start, n, dil):
        return pl.ds(start, n) if dil == 1 else pl.ds(start, n, stride=dil)

    for bi, (win, dil) in enumerate(zip(ATT_WINDOWS, ATT_DILATIONS)):
        half = win // (2 * dil)
        L = S // dil
        qb_n = min(ATT_QB, L)
        kw = min(L, qb_n + 2 * half)
        nqb = L // qb_n

        def block_stats(idx, dil=dil, half=half, L=L, qb_n=qb_n, kw=kw, nqb=nqb):
            r = idx // nqb
            m0 = (idx % nqb) * qb_n
            ks = jnp.clip(m0 - half, 0, L - kw)
            qsel = rows(r + m0 * dil, qb_n, dil)
            ksel = rows(r + ks * dil, kw, dil)
            q = q_ref[0, qsel, :]
            kk = k_ref[0, ksel, :].astype(BF16)
            vv = v_ref[0, ksel, :].astype(BF16)
            rel = (lax.broadcasted_iota(I32, (qb_n, kw), 1) - lax.broadcasted_iota(I32, (qb_n, kw), 0)
                   + (ks - m0 + half))
            valid = rel.astype(jnp.uint32) <= 2 * half
            q2 = jnp.concatenate([jnp.where(head0, q, 0.0), jnp.where(head0, 0.0, q)], axis=0)
            s = jnp.where(jnp.concatenate([valid, valid], axis=0), _dot_nt(q2, kk), NEG)
            m = jnp.max(s, axis=-1, keepdims=True)
            p = jnp.exp(s - m)
            l = jnp.sum(p, axis=-1, keepdims=True)
            o = _dot(p, vv)
            return (qsel, jnp.where(head0, m[:qb_n], m[qb_n:]), jnp.where(head0, l[:qb_n], l[qb_n:]),
                    jnp.where(head0, o[:qb_n], o[qb_n:]))

        unroll = math.gcd(ATT_UNROLL, dil * nqb)

        def body(it, carry, bi=bi, unroll=unroll):
            stats = [block_stats(it * unroll + s) for s in range(unroll)]
            if bi > 0:
                olds = [(m_ref[qsel, :], l_ref[qsel, :], o_ref[qsel, :]) for qsel, _, _, _ in stats]
                merged = []
                for (qsel, m, l, o), (m_old, l_old, o_old) in zip(stats, olds):
                    m_new = jnp.maximum(m_old, m)
                    w_old, w_cur = jnp.exp(m_old - m_new), jnp.exp(m - m_new)
                    merged.append((qsel, m_new, w_old * l_old + w_cur * l, w_old * o_old + w_cur * o))
                stats = merged
            for qsel, m, l, o in stats:
                if bi == len(ATT_WINDOWS) - 1:
                    out_ref[0, qsel, :] = o / l
                else:
                    m_ref[qsel, :] = m
                    l_ref[qsel, :] = l
                    o_ref[qsel, :] = o
            return carry

        lax.fori_loop(0, dil * nqb // unroll, body, 0)


def _attention(pd3):
    B, S, _ = pd3.shape
    spec = lambda off: pl.BlockSpec((1, S, 128), lambda b, p: (b, 0, off + p))
    return pl.pallas_call(
        functools.partial(_attn_kernel, S=S),
        out_shape=jax.ShapeDtypeStruct((B, S, GROUP_W), F32),
        grid=(B, 2),
        in_specs=[spec(0), spec(2), spec(4)],
        out_specs=pl.BlockSpec((1, S, 128), lambda b, p: (b, 0, p)),
        scratch_shapes=[pltpu.VMEM((S, 128), F32)] * 3,
        compiler_params=_cparams(("parallel", "parallel")),
        name="attention",
    )(pd3, pd3, pd3)


def _outproj_kernel(ya_ref, yb_ref, of_ref, ob_ref, cg_ref, yd_ref, x_ref, dng_ref, mg_ref, wo_ref, g2_ref, rw_ref,
                    x2_ref, h2_ref, aff_ref):
    o = of_ref[...] + ob_ref[...]
    ms = _dot_sel(o * o, _head_sum_matrix(GROUP_W, DN_DK), 2) * (1.0 / DN_DK)
    cg = cg_ref[...]
    yc = (o * lax.rsqrt(ms + EPS) * dng_ref[...]) * (cg * _sigmoid(cg))
    acc = x_ref[...]
    for gi, y in enumerate((ya_ref[...], yb_ref[...], yc, yd_ref[...])):
        sl = slice(gi * GROUP_W, (gi + 1) * GROUP_W)
        mix = y * lax.rsqrt(jnp.mean(y * y, axis=-1, keepdims=True) + EPS) * mg_ref[:, sl]
        acc = acc + _dot(mix, wo_ref[sl, :])
    x2_ref[...] = acc
    h2 = acc * lax.rsqrt(jnp.mean(acc * acc, axis=-1, keepdims=True) + EPS) * g2_ref[...]
    h2_ref[...] = h2.astype(BF16)
    h_hi, h_lo = _split(h2, 2)
    z = _dot_nt(rw_ref[...], h_hi) + _dot_nt(rw_ref[...], h_lo)
    logits = z[0:N_EXPERTS] + z[N_EXPERTS:2 * N_EXPERTS]
    ex = jnp.exp(logits - jnp.max(logits, axis=0, keepdims=True))
    aff_ref[...] = ex / jnp.sum(ex, axis=0, keepdims=True)


def _outproj(ya, yb, of, ob, cg, yd, xt, dng, mg, wo, g2, rw):
    T = xt.shape[0]
    tm = ROW_TILE
    row = lambda w: pl.BlockSpec((tm, w), lambda i: (i, 0))
    const = lambda shape: pl.BlockSpec(shape, lambda i: (0, 0))
    return pl.pallas_call(
        _outproj_kernel,
        out_shape=[jax.ShapeDtypeStruct((T, D_MODEL), F32), jax.ShapeDtypeStruct((T, D_MODEL), BF16),
                   jax.ShapeDtypeStruct((N_EXPERTS, T), F32)],
        grid=(T // tm,),
        in_specs=[row(GROUP_W)] * 6 + [row(D_MODEL), const((1, GROUP_W)), const((1, D_MODEL)),
                                       const((D_MODEL, D_MODEL)), const((1, D_MODEL)),
                                       const((2 * N_EXPERTS, D_MODEL))],
        out_specs=[row(D_MODEL), row(D_MODEL), pl.BlockSpec((N_EXPERTS, tm), lambda i: (0, i))],
        compiler_params=_cparams(("parallel",)),
        name="outproj",
    )(ya, yb, of, ob, cg, yd, xt, dng, mg, wo, g2, rw)


def _strict_upper(n):
    return (lax.broadcasted_iota(I32, (n, n), 0) < lax.broadcasted_iota(I32, (n, n), 1)).astype(BF16)


def _route_kernel(aff_ref, mask_ref, pos_ref, wsel_ref, offs_ref, *, cap, nblk):
    keys = pltpu.bitcast(aff_ref[...], I32)

    def bit_body(i, thr):
        cand = thr | lax.shift_left(jnp.int32(1), 30 - i)
        cnt = jnp.sum((keys >= cand).astype(F32), axis=1, keepdims=True)
        return jnp.where(cnt >= cap, cand, thr)

    thr = lax.fori_loop(0, 31, bit_body, jnp.zeros((N_EXPERTS, 1), I32))
    need = cap - jnp.sum((keys > thr).astype(F32), axis=1, keepdims=True)
    su = _strict_upper(TOK_BLK)

    def blk_body(j, carry):
        ceq, csel = carry
        st = pl.multiple_of(j * TOK_BLK, TOK_BLK)
        kb = pltpu.bitcast(aff_ref[:, pl.ds(st, TOK_BLK)], I32)
        eqf = (kb == thr).astype(F32)
        rank = ceq + _dot(eqf, su)
        sel = ((kb > thr) | ((kb == thr) & (rank < need))).astype(F32)
        mask_ref[:, pl.ds(st, TOK_BLK)] = sel
        wsel_ref[:, pl.ds(st, TOK_BLK)] = sel * aff_ref[:, pl.ds(st, TOK_BLK)]
        pos_ref[:, pl.ds(st, TOK_BLK)] = _dot(sel, su)
        offs_ref[j] = jnp.broadcast_to(csel.astype(I32), (N_EXPERTS, 128))
        return (ceq + jnp.sum(eqf, axis=1, keepdims=True), csel + jnp.sum(sel, axis=1, keepdims=True))

    zero = jnp.zeros((N_EXPERTS, 1), F32)
    lax.fori_loop(0, nblk, blk_body, (zero, zero))


def _route(aff_t, cap):
    E, T = aff_t.shape
    nblk = T // TOK_BLK
    full = lambda shape: pl.BlockSpec(shape, lambda i: (0,) * len(shape))
    return pl.pallas_call(
        functools.partial(_route_kernel, cap=cap, nblk=nblk),
        out_shape=[jax.ShapeDtypeStruct((E, T), F32)] * 3 + [jax.ShapeDtypeStruct((nblk, E, 128), I32)],
        grid=(1,),
        in_specs=[full((E, T))],
        out_specs=[full((E, T))] * 3 + [full((nblk, E, 128))],
        compiler_params=_cparams(("arbitrary",)),
        name="route",
    )(aff_t)


def _gather_kernel(offs_ref, x_ref, m_ref, p_ref, out_ref):
    g, j = pl.program_id(0), pl.program_id(1)

    @pl.when(j == 0)
    def _():
        out_ref[...] = jnp.zeros_like(out_ref)

    W = GATHER_W
    slot0 = lax.broadcasted_iota(I32, (W, TOK_BLK), 0).astype(F32)
    for jb in range(GATHER_JB):
        blk = j * GATHER_JB + jb
        cols = slice(jb * TOK_BLK, (jb + 1) * TOK_BLK)
        targets, bases, passes = [], [], []
        for k in range(GATHER_E):
            off = offs_ref[blk * N_EXPERTS + g * GATHER_E + k]
            cnt = offs_ref[(blk + 1) * N_EXPERTS + g * GATHER_E + k] - off
            base = (off // 16) * 16
            targets.append(jnp.where(m_ref[k, :, cols] > 0.0, p_ref[k, :, cols] + (off - base).astype(F32), -1.0))
            bases.append(base)
            passes.append((off - base + cnt + W - 1) // W)
        onehot = jnp.concatenate([jnp.where(slot0 == t, 1.0, 0.0) for t in targets], axis=0).astype(BF16)
        picked = jnp.dot(onehot, x_ref[cols, :], preferred_element_type=F32).astype(BF16)
        for k in range(GATHER_E):
            out_ref[k, pl.ds(pl.multiple_of(bases[k], 16), W), :] += picked[k * W:(k + 1) * W]
        for k in range(GATHER_E):
            def more(ps, carry, k=k, cols=cols):
                hot = jnp.where(slot0 + (ps * W).astype(F32) == targets[k], 1.0, 0.0).astype(BF16)
                extra = jnp.dot(hot, x_ref[cols, :], preferred_element_type=F32).astype(BF16)
                out_ref[k, pl.ds(pl.multiple_of(bases[k] + ps * W, 16), W), :] += extra
                return carry

            lax.fori_loop(1, passes[k], more, 0)


def _gather(h2, mask3, pos3, offs, cap):
    T = h2.shape[0]
    nblk = T // TOK_BLK
    cap_x = cap + 2 * GATHER_W
    rows = GATHER_JB * TOK_BLK
    grid_spec = pltpu.PrefetchScalarGridSpec(
        num_scalar_prefetch=1,
        grid=(N_EXPERTS // GATHER_E, nblk // GATHER_JB),
        in_specs=[pl.BlockSpec((rows, D_MODEL), lambda g, j, o: (j, 0)),
                  pl.BlockSpec((GATHER_E, 1, rows), lambda g, j, o: (g, 0, j)),
                  pl.BlockSpec((GATHER_E, 1, rows), lambda g, j, o: (g, 0, j))],
        out_specs=pl.BlockSpec((GATHER_E, cap_x, D_MODEL), lambda g, j, o: (g, 0, 0),
                               pipeline_mode=pl.Buffered(1)),
    )
    return pl.pallas_call(
        _gather_kernel,
        out_shape=jax.ShapeDtypeStruct((N_EXPERTS, cap_x, D_MODEL), BF16),
        grid_spec=grid_spec,
        compiler_params=_cparams(("parallel", "arbitrary")),
        name="gather",
    )(offs, h2, mask3, pos3)


def _ffn_kernel(*refs, bounds):
    n = len(bounds) - 1
    xs, w_refs, outs, w_scr = refs[:n], refs[n:n + 3], refs[n + 3:2 * n + 3], refs[2 * n + 3]
    i = pl.program_id(1)

    @pl.when(i == 0)
    def _():
        for k, w_ref in enumerate(w_refs):
            w_scr[k] = w_ref[0, 0].astype(BF16)

    for b in range(n):
        @pl.when(jnp.logical_and(i >= bounds[b], i < bounds[b + 1]))
        def _(b=b):
            x = xs[b][0]
            g = jnp.dot(x, w_scr[0], preferred_element_type=F32)
            u = jnp.dot(x, w_scr[1], preferred_element_type=F32)
            h = (g * _sigmoid(g)) * u
            outs[b][0] = _dot(h, w_scr[2]).astype(BF16)


def _ffn(xes, wg, wu, wd, layer, caps):
    tf = min(512, *caps)
    bounds = [0]
    for cap in caps:
        bounds.append(bounds[-1] + cap // tf)

    def tile_spec(b):
        return pl.BlockSpec((1, tf, D_MODEL),
                            lambda e, i: (e, jnp.clip(i - bounds[b], 0, caps[b] // tf - 1), 0))

    wspec = pl.BlockSpec((1, 1, D_MODEL, D_MODEL), lambda e, i: (layer, e, 0, 0))
    return pl.pallas_call(
        functools.partial(_ffn_kernel, bounds=tuple(bounds)),
        out_shape=[jax.ShapeDtypeStruct((N_EXPERTS, cap, D_MODEL), BF16) for cap in caps],
        grid=(N_EXPERTS, bounds[-1]),
        in_specs=[tile_spec(b) for b in range(len(caps))] + [wspec] * 3,
        out_specs=[tile_spec(b) for b in range(len(caps))],
        scratch_shapes=[pltpu.VMEM((3, D_MODEL, D_MODEL), BF16)],
        compiler_params=_cparams(("parallel", "arbitrary")),
        name="ffn",
    )(*xes, wg, wu, wd)


def _slot_window_start(off, cap):
    return jnp.minimum((off // TOK_BLK) * TOK_BLK, cap - 2 * TOK_BLK)


def _scatter_kernel(offs_ref, x2_ref, w_ref, p_ref, gf_ref, *rest, cap, final):
    ye, out_ref = rest[:N_EXPERTS], rest[N_EXPERTS]
    j = pl.program_id(0)
    ps, ws = p_ref[...].T, w_ref[...].T
    offs = [offs_ref[j * N_EXPERTS + e] for e in range(N_EXPERTS)]
    cnts = [offs_ref[(j + 1) * N_EXPERTS + e] - offs[e] for e in range(N_EXPERTS)]
    rels = [offs[e] - _slot_window_start(offs[e], cap) for e in range(N_EXPERTS)]
    weights = [ws[:, e:e + 1] for e in range(N_EXPERTS)]
    SUB = 16
    most = cnts[0]
    for e in range(1, N_EXPERTS):
        most = jnp.maximum(most, cnts[e])

    def onehot(e, start, width):
        slot = lax.broadcasted_iota(I32, (TOK_BLK, width), 1).astype(F32)
        return jnp.where(slot == ps[:, e:e + 1] + (rels[e] - start).astype(F32), weights[e], 0.0).astype(BF16)

    W = SCATTER_W

    @pl.when(most <= W - SUB)
    def _():
        acc = x2_ref[...]
        for g0 in range(0, N_EXPERTS, 2):
            group = (g0, g0 + 1)
            r0 = {e: pl.multiple_of(jnp.minimum((rels[e] // SUB) * SUB, 2 * TOK_BLK - W), SUB) for e in group}
            acc = acc + jnp.dot(jnp.concatenate([onehot(e, r0[e], W) for e in group], axis=1),
                                jnp.concatenate([ye[e][0, pl.ds(r0[e], W), :] for e in group], axis=0),
                                preferred_element_type=F32)
        out_ref[...] = acc

    @pl.when(most > W - SUB)
    def _():
        out_ref[...] = x2_ref[...]
        for e in range(N_EXPERTS):
            @pl.when(cnts[e] > 0)
            def _(e=e):
                out_ref[...] += jnp.dot(onehot(e, 0, 2 * TOK_BLK), ye[e][0], preferred_element_type=F32)

    if final:
        x = out_ref[...]
        out_ref[...] = x * lax.rsqrt(jnp.mean(x * x, axis=-1, keepdims=True) + EPS) * gf_ref[...]


def _scatter(x2, wsel, pos, ye, offs, gf, cap, final):
    T = x2.shape[0]
    nblk = T // TOK_BLK
    assert cap >= 2 * TOK_BLK and cap % TOK_BLK == 0

    def ye_spec(e):
        return pl.BlockSpec((pl.Element(1), pl.Element(2 * TOK_BLK), pl.Element(D_MODEL)),
                            lambda j, o: (e, pl.multiple_of(_slot_window_start(o[j * N_EXPERTS + e], cap), TOK_BLK),
                                          0))

    row = lambda w: pl.BlockSpec((TOK_BLK, w), lambda j, o: (j, 0))
    col = pl.BlockSpec((N_EXPERTS, TOK_BLK), lambda j, o: (0, j))
    grid_spec = pltpu.PrefetchScalarGridSpec(
        num_scalar_prefetch=1,
        grid=(nblk,),
        in_specs=[row(D_MODEL), col, col, pl.BlockSpec((1, D_MODEL), lambda j, o: (0, 0))]
                 + [ye_spec(e) for e in range(N_EXPERTS)],
        out_specs=row(D_MODEL),
    )
    return pl.pallas_call(
        functools.partial(_scatter_kernel, cap=cap, final=final),
        out_shape=jax.ShapeDtypeStruct((T, D_MODEL), F32),
        grid_spec=grid_spec,
        compiler_params=_cparams(("arbitrary",)),
        name="scatter",
    )(offs, x2, wsel, pos, gf, *([ye] * N_EXPERTS))


def _block_diag(w):
    n, a, b = w.shape
    out = jnp.zeros((n * a, n * b), w.dtype)
    for i in range(n):
        out = out.at[i * a:(i + 1) * a, i * b:(i + 1) * b].set(w[i])
    return out


def _rope_tables(S):
    inv = ROPE_THETA ** (-jnp.arange(0, ROT_DIM, 2, dtype=F32) / ROT_DIM)
    ang = jnp.arange(S, dtype=F32)[:, None] * inv[None, :]
    cos, sin = jnp.cos(ang), jnp.sin(ang)
    half = ROT_DIM // 2
    ones, zeros = jnp.ones((S, ATT_HD - ROT_DIM), F32), jnp.zeros((S, ATT_HD - ROT_DIM), F32)
    zh = jnp.zeros((S, half), F32)
    c = jnp.concatenate([cos, cos, ones], axis=1)
    s1 = jnp.concatenate([-sin, zh, zeros], axis=1)
    s2 = jnp.concatenate([zh, sin, zeros], axis=1)
    return tuple(jnp.tile(t, (1, 2)) for t in (c, s1, s2))


def _expanders():
    eg = np.zeros((2, 128, GROUP_W), np.float32)
    eb = np.zeros((2, 128, GROUP_W), np.float32)
    for d in range(2):
        for h in range(DN_HEADS):
            eg[d, d * DN_HEADS + h, h * DN_DK:(h + 1) * DN_DK] = 1.0
            eb[d, 2 * DN_HEADS + d * DN_HEADS + h, h * DN_DK:(h + 1) * DN_DK] = 1.0
    hd = np.arange(GROUP_W) // DN_DK
    bd = (hd[:, None] == hd[None, :]).astype(np.float32)
    return jnp.asarray(eg), jnp.asarray(eb), jnp.asarray(bd)


def _layer_params(l, norm1_g, w_in, conv_a_w, conv_a_b, rg_wa, rg_ba, rg_wx, rg_bx, rg_lambda, pool_w, pool_scale,
                  dn_conv_w, dn_A_log, dn_dt_bias, dn_norm_g, mix_norm_g, w_out, norm2_g, router_w,
                  exp_w_gate, exp_w_up, exp_w_down):
    w = w_in[l]
    w_cat = jnp.concatenate([w[:, 0:1536], w[:, 1552:2576], w[:, 1536:1552],
                             jnp.zeros((D_MODEL, COL_END - COL_CAB - 16), F32)], axis=1).astype(BF16)
    pad8 = lambda v: jnp.concatenate([v.reshape(1, 2 * DN_HEADS), jnp.zeros((1, 120), F32)], axis=1)
    return dict(
        g1=norm1_g[l].reshape(1, D_MODEL), w_cat=w_cat,
        conv_a_w=conv_a_w[l], conv_a_b=conv_a_b[l].reshape(1, GROUP_W),
        wa=[_block_diag(rg_wa[l, d]).astype(BF16) for d in range(2)],
        wx=[_block_diag(rg_wx[l, d]).astype(BF16) for d in range(2)],
        ba=[rg_ba[l, d].reshape(1, GROUP_W) for d in range(2)],
        bx=[rg_bx[l, d].reshape(1, GROUP_W) for d in range(2)],
        lam=[rg_lambda[l, d].reshape(1, GROUP_W) for d in range(2)],
        pool_w=_block_diag(pool_w[l]).astype(BF16), pool_scale=pool_scale[l].reshape(1, GROUP_W),
        dn_conv_w=dn_conv_w[l], alog=pad8(dn_A_log[l]), dtb=pad8(dn_dt_bias[l]),
        dng=jnp.tile(dn_norm_g[l], DN_HEADS).reshape(1, GROUP_W),
        mg=mix_norm_g[l].reshape(1, D_MODEL), wo=w_out[l].astype(BF16),
        g2=norm2_g[l].reshape(1, D_MODEL),
        rw=jnp.concatenate(_split(router_w[l].T, 2), axis=0),
        wg=exp_w_gate, wu=exp_w_up, wd=exp_w_down,
    )


def _chunk_rows(gcn, d):
    B, S, _ = gcn.shape
    n = S // DN_CHUNK
    g = gcn[:, :, d * DN_HEADS:(d + 1) * DN_HEADS].reshape(B, n, DN_CHUNK, DN_HEADS)
    return g.transpose(0, 1, 3, 2).reshape(B, n, 1, GROUP_W)


def _capacity(T):
    return max(1, EC_CAPACITY * T // N_EXPERTS)


def _mix_and_route(xt, B, S, p, consts, ropes):
    T = B * S
    cap = _capacity(T)
    eg, eb, bdmask = consts
    pa, pb, pq, pg, pd, pcab = _inproj(xt, p["g1"], p["w_cat"], *ropes, S)
    pa3 = pa.reshape(B, S, 2 * GROUP_W)
    rg = lambda d: (p["conv_a_w"], p["conv_a_b"], p["wa"][d], p["ba"][d], p["wx"][d], p["bx"][d], p["lam"][d])
    hf = _rglru(pa3, None, *rg(0), reverse=False)
    ya = _rglru(pa3, hf, *rg(1), reverse=True)
    yb = _pool(pb.reshape(B, S, GROUP_W), p["pool_w"], p["pool_scale"])
    qn, kn, vv, gcf, gcb, bef, beb, gcn = _dnprep(pq.reshape(B, S, 3 * GROUP_W), pcab.reshape(B, S, 128),
                                                  p["dn_conv_w"], p["alog"], p["dtb"], eg, eb)
    of, ob = _delta(qn, kn, vv, gcf, gcb, bef, beb, _chunk_rows(gcn, 0), _chunk_rows(gcn, 1), bdmask)
    yd = _attention(pd.reshape(B, S, 3 * GROUP_W))
    flat = lambda a: a.reshape(T, GROUP_W)
    x2, h2, aff_t = _outproj(flat(ya), flat(yb), flat(of), flat(ob), pg, flat(yd), xt,
                             p["dng"], p["mg"], p["wo"], p["g2"], p["rw"])
    mask, pos, wsel, offs3 = _route(aff_t, cap)
    offs = jnp.concatenate([offs3[:, :, 0].reshape(-1), jnp.full((N_EXPERTS,), cap, I32)])
    xe = _gather(h2, mask.reshape(N_EXPERTS, 1, T), pos.reshape(N_EXPERTS, 1, T), offs, cap)
    return dict(x2=x2, wsel=wsel, pos=pos, offs=offs, xe=xe, cap=cap)


def kernel(x_prompt, x_sample, norm1_g, w_in, conv_a_w, conv_a_b, rg_wa, rg_ba, rg_wx, rg_bx, rg_lambda, pool_w, pool_scale, dn_conv_w, dn_A_log, dn_dt_bias, dn_norm_g, mix_norm_g, w_out, norm2_g, router_w, exp_w_gate, exp_w_up, exp_w_down, final_norm_g):
    layers = [_layer_params(l, norm1_g, w_in, conv_a_w, conv_a_b, rg_wa, rg_ba, rg_wx, rg_bx, rg_lambda, pool_w,
                            pool_scale, dn_conv_w, dn_A_log, dn_dt_bias, dn_norm_g, mix_norm_g, w_out, norm2_g,
                            router_w, exp_w_gate, exp_w_up, exp_w_down) for l in range(DEPTH)]
    consts = _expanders()
    final_g = final_norm_g.reshape(1, D_MODEL)
    batches = (x_prompt, x_sample)
    shapes = [x.shape for x in batches]
    ropes = [_rope_tables(S) for _, S, _ in shapes]
    xts = [x.reshape(B * S, D) for x, (B, S, D) in zip(batches, shapes)]
    for l, p in enumerate(layers):
        st = [_mix_and_route(xt, B, S, p, consts, rp) for xt, (B, S, _), rp in zip(xts, shapes, ropes)]
        yes = _ffn([s["xe"] for s in st], p["wg"], p["wu"], p["wd"], l, [s["cap"] for s in st])
        xts = [_scatter(s["x2"], s["wsel"], s["pos"], ye, s["offs"], final_g, s["cap"], final=(l == DEPTH - 1))
               for s, ye in zip(st, yes)]
    return tuple(xt.reshape(shape) for xt, shape in zip(xts, shapes))
```

```python
import functools
import math
import numpy as np
import jax
import jax.numpy as jnp
from jax import lax
from jax.experimental import pallas as pl
from jax.experimental.pallas import tpu as pltpu

F32, BF16, I32 = jnp.float32, jnp.bfloat16, jnp.int32

D_MODEL = 1024
DEPTH = 2
GROUP_W = 256
RG_C = 8.0
POOL_WINDOWS = (2, 4, 8, 16)
DN_HEADS = 4
DN_DK = 64
DN_CHUNK = 64
ATT_HD = 64
ROT_DIM = 16
ROPE_THETA = 500000.0
ATT_WINDOWS = (128, 512, 2048)
ATT_DILATIONS = (1, 4, 16)
N_EXPERTS = 16
EC_CAPACITY = 2
EPS = 1e-6
NEG = -1e30

COL_A, COL_B, COL_CQ, COL_CG, COL_D, COL_CAB, COL_END = 0, 512, 768, 1536, 1792, 2560, 2688

ROW_TILE = 1024
HALO = 8
DELTA_CB = 16
ATT_QB = 128
ATT_UNROLL = 16
TOK_BLK = 256
GATHER_W = 64
GATHER_JB = 8
GATHER_E = 4
SCATTER_W = 128
VMEM_LIMIT = 56 * 1024 * 1024


def _cparams(sem):
    return pltpu.CompilerParams(dimension_semantics=sem, vmem_limit_bytes=VMEM_LIMIT)


def _dot(a, b):
    return jnp.dot(a.astype(BF16), b.astype(BF16), preferred_element_type=F32)


def _split(a, pieces):
    out, rem = [], a
    for i in range(pieces):
        t = rem.astype(BF16)
        out.append(t)
        if i + 1 < pieces:
            rem = rem - t.astype(F32)
    return out


def _dot_sel(a, sel, pieces=3):
    sel = sel.astype(BF16)
    return sum(jnp.dot(t, sel, preferred_element_type=F32) for t in _split(a, pieces))


def _dot_nt(a, b):
    return lax.dot_general(a.astype(BF16), b.astype(BF16), (((1,), (1,)), ((), ())), preferred_element_type=F32)


def _dot_tn(a, b):
    return lax.dot_general(a.astype(BF16), b.astype(BF16), (((0,), (0,)), ((), ())), preferred_element_type=F32)


def _sigmoid(x):
    return 1.0 / (1.0 + jnp.exp(-x))


def _softplus(x):
    return jnp.maximum(x, 0.0) + jnp.log1p(jnp.exp(-jnp.abs(x)))


def _shift_rows(e, k):
    n = e.shape[0]
    return e if k % n == 0 else pltpu.roll(e, (-k) % n, axis=0)


def _with_halo(cur_ref, prev_ref, next_ref, first, last):
    prev = jnp.where(first, 0.0, prev_ref[0])
    nxt = jnp.where(last, 0.0, next_ref[0])
    return jnp.concatenate([prev, cur_ref[0], nxt], axis=0)


def _halo_specs(ts, width, S, blk_of):
    per = ts // HALO
    last = S // HALO - 1
    cur = pl.BlockSpec((1, ts, width), lambda b, c: (b, blk_of(c), 0))
    prev = pl.BlockSpec((1, HALO, width), lambda b, c: (b, jnp.maximum(blk_of(c) * per - 1, 0), 0))
    nxt = pl.BlockSpec((1, HALO, width), lambda b, c: (b, jnp.minimum((blk_of(c) + 1) * per, last), 0))
    return cur, prev, nxt


def _inproj_kernel(x_ref, g_ref, w_ref, c_ref, s1_ref, s2_ref, pa_ref, pb_ref, pq_ref, pg_ref, pd_ref, pcab_ref):
    x = x_ref[...]
    h = x * lax.rsqrt(jnp.mean(x * x, axis=-1, keepdims=True) + EPS) * g_ref[...]
    hb = h.astype(BF16)

    def mm(lo, hi):
        return jnp.dot(hb, w_ref[:, lo:hi], preferred_element_type=F32)

    pa_ref[...] = mm(COL_A, COL_B)
    pb_ref[...] = mm(COL_B, COL_CQ)
    pq_ref[...] = mm(COL_CQ, COL_CG)
    pg_ref[...] = mm(COL_CG, COL_D)
    c, s1, s2 = c_ref[...], s1_ref[...], s2_ref[...]
    for part in range(2):
        yy = mm(COL_D + 256 * part, COL_D + 256 * part + 256)
        for half in range(2):
            y = yy[:, 128 * half:128 * half + 128]
            y = y * c + pltpu.roll(y, 128 - ROT_DIM // 2, axis=1) * s1 + pltpu.roll(y, ROT_DIM // 2, axis=1) * s2
            if part == 0:
                y = y * (ATT_HD ** -0.5)
            pd_ref[:, 256 * part + 128 * half:256 * part + 128 * half + 128] = y
    vcab = mm(COL_D + 512, COL_END)
    pd_ref[:, 512:768] = vcab[:, 0:256]
    pcab_ref[...] = vcab[:, 256:384]


def _inproj(xt, g1, w_cat, rope_c, rope_s1, rope_s2, S):
    T = xt.shape[0]
    tm = min(ROW_TILE, S)
    per_seq = S // tm
    widths = (512, 256, 768, 256, 768, 128)
    row = lambda w: pl.BlockSpec((tm, w), lambda i: (i, 0))
    rope = pl.BlockSpec((tm, 128), lambda i: (i % per_seq, 0))
    return pl.pallas_call(
        _inproj_kernel,
        out_shape=[jax.ShapeDtypeStruct((T, w), F32) for w in widths],
        grid=(T // tm,),
        in_specs=[row(D_MODEL), pl.BlockSpec((1, D_MODEL), lambda i: (0, 0)),
                  pl.BlockSpec((D_MODEL, COL_END), lambda i: (0, 0)), rope, rope, rope],
        out_specs=[row(w) for w in widths],
        compiler_params=_cparams(("parallel",)),
        name="inproj",
    )(xt, g1, w_cat, rope_c, rope_s1, rope_s2)


def _rglru_kernel(*refs, reverse, ts, nc):
    if reverse:
        (cur_ref, prev_ref, next_ref, gate_ref, hf_ref, cw_ref, cb_ref, wa_ref, ba_ref, wx_ref, bx_ref, lam_ref,
         out_ref, a_ref, b_ref, carry_ref) = refs
    else:
        (cur_ref, prev_ref, next_ref, cw_ref, cb_ref, wa_ref, ba_ref, wx_ref, bx_ref, lam_ref,
         out_ref, a_ref, b_ref, carry_ref) = refs
    c = pl.program_id(1)
    blk = (nc - 1 - c) if reverse else c
    e = _with_halo(cur_ref, prev_ref, next_ref, blk == 0, blk == nc - 1)
    cw = cw_ref[...]
    sl = slice(HALO, HALO + ts)
    u = (cw[0:1] * _shift_rows(e, -2)[sl] + cw[1:2] * _shift_rows(e, -1)[sl] + cw[2:3] * e[sl]
         + cw[3:4] * _shift_rows(e, 1)[sl]) + cb_ref[...]
    r = _sigmoid(_dot(u, wa_ref[...]) + ba_ref[...])
    i = _sigmoid(_dot(u, wx_ref[...]) + bx_ref[...])
    log_a = -RG_C * r * _softplus(-lam_ref[...])
    a_ref[...] = jnp.exp(log_a)
    b_ref[...] = jnp.sqrt(1.0 - jnp.exp(2.0 * log_a)) * (i * u)

    @pl.when(c == 0)
    def _():
        carry_ref[...] = jnp.zeros_like(carry_ref)

    row = lax.broadcasted_iota(I32, (HALO, GROUP_W), 0)
    nt = ts // HALO

    def body(it, carry):
        ti = (nt - 1 - it) if reverse else it
        st = pl.multiple_of(ti * HALO, HALO)
        a = a_ref[pl.ds(st, HALO), :]
        b = b_ref[pl.ds(st, HALO), :]
        for d in (1, 2, 4):
            k = d if reverse else -d
            valid = (row < HALO - d) if reverse else (row >= d)
            b = jnp.where(valid, a * _shift_rows(b, k) + b, b)
            a = jnp.where(valid, a * _shift_rows(a, k), a)
        h = a * carry + b
        if reverse:
            g = gate_ref[0, pl.ds(st, HALO), :]
            cdf = 0.5 * (1.0 + jnp.tanh(np.float32(np.sqrt(2.0 / np.pi)) * (g + 0.044715 * (g * g * g))))
            out_ref[0, pl.ds(st, HALO), :] = (g * cdf) * (hf_ref[0, pl.ds(st, HALO), :] + h)
            return jnp.broadcast_to(h[0:1], h.shape)
        out_ref[0, pl.ds(st, HALO), :] = h
        return jnp.broadcast_to(h[HALO - 1:HALO], h.shape)

    carry_ref[...] = lax.fori_loop(0, nt, body, carry_ref[...], unroll=4)


def _rglru(pa3, hf, cw, cb, wa, ba, wx, bx, lam, reverse):
    B, S, _ = pa3.shape
    ts = min(ROW_TILE, S)
    nc = S // ts
    blk_of = (lambda c: nc - 1 - c) if reverse else (lambda c: c)
    cur, prev, nxt = _halo_specs(ts, GROUP_W, S, blk_of)
    tile = pl.BlockSpec((1, ts, GROUP_W), lambda b, c: (b, blk_of(c), 0))
    const = lambda shape: pl.BlockSpec(shape, lambda b, c: (0,) * len(shape))
    in_specs = [cur, prev, nxt]
    args = [pa3, pa3, pa3]
    if reverse:
        in_specs += [pl.BlockSpec((1, ts, GROUP_W), lambda b, c: (b, blk_of(c), 1)), tile]
        args += [pa3, hf]
    in_specs += [const((4, GROUP_W)), const((1, GROUP_W)), const((GROUP_W, GROUP_W)), const((1, GROUP_W)),
                 const((GROUP_W, GROUP_W)), const((1, GROUP_W)), const((1, GROUP_W))]
    args += [cw, cb, wa, ba, wx, bx, lam]
    return pl.pallas_call(
        functools.partial(_rglru_kernel, reverse=reverse, ts=ts, nc=nc),
        out_shape=jax.ShapeDtypeStruct((B, S, GROUP_W), F32),
        grid=(B, nc),
        in_specs=in_specs,
        out_specs=tile,
        scratch_shapes=[pltpu.VMEM((ts, GROUP_W), F32), pltpu.VMEM((ts, GROUP_W), F32),
                        pltpu.VMEM((HALO, GROUP_W), F32)],
        compiler_params=_cparams(("parallel", "arbitrary")),
        name="rglru_bwd" if reverse else "rglru_fwd",
    )(*args)


def _pool_kernel(cur_ref, prev_ref, next_ref, w_ref, sc_ref, out_ref, *, ts, nc, S):
    c = pl.program_id(1)
    e = _with_halo(cur_ref, prev_ref, next_ref, c == 0, c == nc - 1)
    sl = slice(HALO, HALO + ts)
    a2 = e + _shift_rows(e, 1)
    a4 = a2 + _shift_rows(a2, 2)
    a8 = a4 + _shift_rows(a4, 4)
    a16 = a8 + _shift_rows(a8, 8)
    sums = [_shift_rows(a, -(w // 2))[sl] for a, w in zip((a2, a4, a8, a16), POOL_WINDOWS)]
    gi = lax.broadcasted_iota(I32, (ts, GROUP_W), 1) // (GROUP_W // len(POOL_WINDOWS))
    ssum = jnp.where(gi == 0, sums[0], jnp.where(gi == 1, sums[1], jnp.where(gi == 2, sums[2], sums[3])))
    hw = jnp.where(gi == 0, 1, jnp.where(gi == 1, 2, jnp.where(gi == 2, 4, 8)))
    t = c * ts + lax.broadcasted_iota(I32, (ts, GROUP_W), 0)
    cnt = (jnp.minimum(t + hw, S) - jnp.maximum(t - hw, 0)).astype(F32)
    p = ssum / cnt - e[sl]
    out_ref[0] = _dot(p, w_ref[...]) * sc_ref[...]


def _pool(pb3, w_bd, scale):
    B, S, _ = pb3.shape
    ts = min(ROW_TILE, S)
    nc = S // ts
    cur, prev, nxt = _halo_specs(ts, GROUP_W, S, lambda c: c)
    return pl.pallas_call(
        functools.partial(_pool_kernel, ts=ts, nc=nc, S=S),
        out_shape=jax.ShapeDtypeStruct((B, S, GROUP_W), F32),
        grid=(B, nc),
        in_specs=[cur, prev, nxt, pl.BlockSpec((GROUP_W, GROUP_W), lambda b, c: (0, 0)),
                  pl.BlockSpec((1, GROUP_W), lambda b, c: (0, 0))],
        out_specs=pl.BlockSpec((1, ts, GROUP_W), lambda b, c: (b, c, 0)),
        compiler_params=_cparams(("parallel", "parallel")),
        name="pool",
    )(pb3, pb3, pb3, w_bd, scale)


def _head_sum_matrix(n, group):
    r = lax.broadcasted_iota(I32, (n, n), 0) // group
    c = lax.broadcasted_iota(I32, (n, n), 1) // group
    return (r == c).astype(F32)


def _dnprep_kernel(cur_ref, prev_ref, next_ref, cab_ref, cw_ref, alog_ref, dtb_ref, eg_ref, eb_ref,
                   q_ref, k_ref, v_ref, gcf_ref, gcb_ref, bef_ref, beb_ref, gcn_ref, *, ts, nc):
    c = pl.program_id(1)
    e = _with_halo(cur_ref, prev_ref, next_ref, c == 0, c == nc - 1)
    cw = cw_ref[...]
    sl = slice(HALO, HALO + ts)
    y = (cw[0:1] * _shift_rows(e, -2)[sl] + cw[1:2] * _shift_rows(e, -1)[sl] + cw[2:3] * e[sl]
         + cw[3:4] * _shift_rows(e, 1)[sl])
    y = y * _sigmoid(y)
    q, k = y[:, 0:GROUP_W], y[:, GROUP_W:2 * GROUP_W]
    hs = _head_sum_matrix(GROUP_W, DN_DK)
    q_ref[0] = q * lax.rsqrt(_dot_sel(q * q, hs, 2) + EPS) * (DN_DK ** -0.5)
    k_ref[0] = k * lax.rsqrt(_dot_sel(k * k, hs, 2) + EPS)
    v_ref[0] = y[:, 2 * GROUP_W:3 * GROUP_W]
    cab = cab_ref[0]
    g = -jnp.exp(alog_ref[...]) * _softplus(cab + dtb_ref[...])
    beta = _sigmoid(cab)
    bef_ref[0] = _dot_sel(beta, eb_ref[0], 2)
    beb_ref[0] = _dot_sel(beta, eb_ref[1], 2)
    pos = lax.broadcasted_iota(I32, (ts, 128), 0) % DN_CHUNK
    fwd = rev = g
    d = 1
    while d < DN_CHUNK:
        fwd = fwd + jnp.where(pos >= d, _shift_rows(fwd, -d), 0.0)
        rev = rev + jnp.where(pos < DN_CHUNK - d, _shift_rows(rev, d), 0.0)
        d *= 2
    gcn = jnp.where(lax.broadcasted_iota(I32, (ts, 128), 1) < DN_HEADS, fwd, rev)
    gcn_ref[0] = gcn
    gcf_ref[0] = _dot_sel(gcn, eg_ref[0])
    gcb_ref[0] = _dot_sel(gcn, eg_ref[1])


def _dnprep(pq3, pcab3, cw, alog_row, dtb_row, eg, eb):
    B, S, _ = pq3.shape
    ts = min(ROW_TILE, S)
    nc = S // ts
    cur, prev, nxt = _halo_specs(ts, 3 * GROUP_W, S, lambda c: c)
    tile = pl.BlockSpec((1, ts, GROUP_W), lambda b, c: (b, c, 0))
    const = lambda shape: pl.BlockSpec(shape, lambda b, c: (0,) * len(shape))
    return pl.pallas_call(
        functools.partial(_dnprep_kernel, ts=ts, nc=nc),
        out_shape=[jax.ShapeDtypeStruct((B, S, GROUP_W), F32)] * 7 + [jax.ShapeDtypeStruct((B, S, 128), F32)],
        grid=(B, nc),
        in_specs=[cur, prev, nxt, pl.BlockSpec((1, ts, 128), lambda b, c: (b, c, 0)),
                  const((4, 3 * GROUP_W)), const((1, 128)), const((1, 128)),
                  const((2, 128, GROUP_W)), const((2, 128, GROUP_W))],
        out_specs=[tile] * 7 + [pl.BlockSpec((1, ts, 128), lambda b, c: (b, c, 0))],
        compiler_params=_cparams(("parallel", "parallel")),
        name="dnprep",
    )(pq3, pq3, pq3, pcab3, cw, alog_row, dtb_row, eg, eb)


def _delta_kernel(qf, kf, vf, gcf, bef, grf, qb, kb, vb, gcb, beb, grb, bd_ref, of_ref, ob_ref,
                  s_ref, p_scr, x_scr, wq_scr, u_scr, at_scr, kd_scr, vb_scr, kg_scr, *, cb):
    j = pl.program_id(1)

    @pl.when(j == 0)
    def _():
        s_ref[...] = jnp.zeros_like(s_ref)

    C = DN_CHUNK
    bd = bd_ref[...] > 0.0
    c_idx = lax.broadcasted_iota(I32, (C, GROUP_W), 0)
    m_idx = lax.broadcasted_iota(I32, (C, GROUP_W), 1) % C
    eye = (c_idx == m_idx).astype(F32)

    def blockdiag(x):
        return jnp.where(bd, jnp.tile(x.astype(BF16), (DN_HEADS, 1)), jnp.zeros((), BF16))

    dirs = ((qf, kf, vf, gcf, bef, grf, of_ref), (qb, kb, vb, gcb, beb, grb, ob_ref))
    units = [(d, ci) for ci in range(cb) for d in range(2)]

    def chunk_of(d, ci):
        return cb - 1 - ci if d == 1 else ci

    def prepare(ui):
        d, ci = units[ui]
        q_r, k_r, v_r, gc_r, be_r, gr_r, _ = dirs[d]
        rev = d == 1
        tril = (c_idx <= m_idx) if rev else (c_idx >= m_idx)
        strict = (c_idx < m_idx) if rev else (c_idx > m_idx)
        last = 0 if rev else C - 1
        cc = chunk_of(d, ci)
        rows = slice(cc * C, (cc + 1) * C)
        q, k, v = q_r[0, rows, :], k_r[0, rows, :], v_r[0, rows, :]
        gc, be, gr = gc_r[0, rows, :], be_r[0, rows, :], gr_r[0, cc]
        eg = jnp.exp(gc)
        kbeta = k * be
        kkqk = _dot_nt(jnp.concatenate([kbeta, q], axis=0), blockdiag(k))
        decay = jnp.exp(jnp.where(tril, gc - gr, -jnp.inf))
        at_scr[ui] = (kkqk[C:] * decay).astype(BF16)
        p = -jnp.where(strict, kkqk[:C] * decay, 0.0)
        p_scr[ui] = p.astype(BF16)
        x_scr[ui] = eye + p
        wq_scr[ui, C:2 * C, :] = (q * eg).astype(BF16)
        kd_scr[ui] = (k * jnp.exp(gc[last:last + 1] - gc)).astype(BF16)
        vb_scr[ui] = (v * be).astype(BF16)
        kg_scr[ui] = (kbeta * eg).astype(BF16)

    def solve(group):
        for ui in group:
            p = p_scr[ui]
            p_scr[ui] = _dot(p, blockdiag(p)).astype(BF16)
        for _ in range(4):
            for ui in group:
                p, x = p_scr[ui], x_scr[ui]
                px = _dot(jnp.concatenate([p, x.astype(BF16)], axis=0), blockdiag(p))
                p_scr[ui] = px[:C].astype(BF16)
                x_scr[ui] = x + px[C:]
        for ui in group:
            x = x_scr[ui]
            x = x + _dot(x, blockdiag(p_scr[ui]))
            u_scr[ui] = _dot(x, blockdiag(vb_scr[ui]))
            wq_scr[ui, 0:C, :] = _dot(x, blockdiag(kg_scr[ui])).astype(BF16)

    def recur(ui):
        d, ci = units[ui]
        gc_r, o_r = dirs[d][3], dirs[d][6]
        cc = chunk_of(d, ci)
        last = cc * C + (0 if d == 1 else C - 1)
        state = s_ref[d]
        ws_qs = _dot(wq_scr[ui], state)
        v_new = u_scr[ui] - ws_qs[:C]
        o_r[0, cc * C:(cc + 1) * C, :] = ws_qs[C:] + _dot(at_scr[ui], blockdiag(v_new))
        s_ref[d] = (state * jnp.exp(gc_r[0, last:last + 1, :])
                    + jnp.where(bd, _dot_tn(kd_scr[ui], v_new), 0.0))

    for ui in range(len(units)):
        prepare(ui)
    solve(range(len(units)))
    for ui in range(len(units)):
        recur(ui)


def _delta(qn, kn, vv, gcf, gcb, bef, beb, grf, grb, bdmask):
    B, S, _ = qn.shape
    cb = DELTA_CB
    rb = cb * DN_CHUNK
    nb = S // rb
    f = lambda b, j: (b, j, 0)
    r = lambda b, j: (b, nb - 1 - j, 0)
    tf = pl.BlockSpec((1, rb, GROUP_W), f)
    tr = pl.BlockSpec((1, rb, GROUP_W), r)
    gf = pl.BlockSpec((1, cb, 1, GROUP_W), lambda b, j: (b, j, 0, 0))
    gr = pl.BlockSpec((1, cb, 1, GROUP_W), lambda b, j: (b, nb - 1 - j, 0, 0))
    return pl.pallas_call(
        functools.partial(_delta_kernel, cb=cb),
        out_shape=[jax.ShapeDtypeStruct((B, S, GROUP_W), F32)] * 2,
        grid=(B, nb),
        in_specs=[tf, tf, tf, tf, tf, gf, tr, tr, tr, tr, tr, gr,
                  pl.BlockSpec((GROUP_W, GROUP_W), lambda b, j: (0, 0))],
        out_specs=[tf, tr],
        scratch_shapes=[pltpu.VMEM((2, GROUP_W, GROUP_W), F32)]
                       + [pltpu.VMEM((2 * cb, n * DN_CHUNK, GROUP_W), dt) for n, dt in
                          ((1, BF16), (1, F32), (2, BF16), (1, F32), (1, BF16), (1, BF16), (1, BF16), (1, BF16))],
        compiler_params=_cparams(("parallel", "arbitrary")),
        name="delta",
    )(qn, kn, vv, gcf, bef, grf, qn, kn, vv, gcb, beb, grb, bdmask)


def _attn_kernel(q_ref, k_ref, v_ref, out_ref, m_ref, l_ref, o_ref, *, S):
    head0 = lax.broadcasted_iota(I32, (1, 128), 1) < ATT_HD

    def rows(start, n, dil):
        return pl.ds(start, n) if dil == 1 else pl.ds(start, n, stride=dil)

    for bi, (win, dil) in enumerate(zip(ATT_WINDOWS, ATT_DILATIONS)):
        half = win // (2 * dil)
        L = S // dil
        qb_n = min(ATT_QB, L)
        kw = min(L, qb_n + 2 * half)
        nqb = L // qb_n

        def block_stats(idx, dil=dil, half=half, L=L, qb_n=qb_n, kw=kw, nqb=nqb):
            r = idx // nqb
            m0 = (idx % nqb) * qb_n
            ks = jnp.clip(m0 - half, 0, L - kw)
            qsel = rows(r + m0 * dil, qb_n, dil)
            ksel = rows(r + ks * dil, kw, dil)
            q = q_ref[0, qsel, :]
            kk = k_ref[0, ksel, :].astype(BF16)
            vv = v_ref[0, ksel, :].astype(BF16)
            rel = (lax.broadcasted_iota(I32, (qb_n, kw), 1) - lax.broadcasted_iota(I32, (qb_n, kw), 0)
                   + (ks - m0 + half))
            valid = rel.astype(jnp.uint32) <= 2 * half
            q2 = jnp.concatenate([jnp.where(head0, q, 0.0), jnp.where(head0, 0.0, q)], axis=0)
            s = jnp.where(jnp.concatenate([valid, valid], axis=0), _dot_nt(q2, kk), NEG)
            m = jnp.max(s, axis=-1, keepdims=True)
            p = jnp.exp(s - m)
            l = jnp.sum(p, axis=-1, keepdims=True)
            o = _dot(p, vv)
            return (qsel, jnp.where(head0, m[:qb_n], m[qb_n:]), jnp.where(head0, l[:qb_n], l[qb_n:]),
                    jnp.where(head0, o[:qb_n], o[qb_n:]))

        unroll = math.gcd(ATT_UNROLL, dil * nqb)

        def body(it, carry, bi=bi, unroll=unroll):
            stats = [block_stats(it * unroll + s) for s in range(unroll)]
            if bi > 0:
                olds = [(m_ref[qsel, :], l_ref[qsel, :], o_ref[qsel, :]) for qsel, _, _, _ in stats]
                merged = []
                for (qsel, m, l, o), (m_old, l_old, o_old) in zip(stats, olds):
                    m_new = jnp.maximum(m_old, m)
                    w_old, w_cur = jnp.exp(m_old - m_new), jnp.exp(m - m_new)
                    merged.append((qsel, m_new, w_old * l_old + w_cur * l, w_old * o_old + w_cur * o))
                stats = merged
            for qsel, m, l, o in stats:
                if bi == len(ATT_WINDOWS) - 1:
                    out_ref[0, qsel, :] = o / l
                else:
                    m_ref[qsel, :] = m
                    l_ref[qsel, :] = l
                    o_ref[qsel, :] = o
            return carry

        lax.fori_loop(0, dil * nqb // unroll, body, 0)


def _attention(pd3):
    B, S, _ = pd3.shape
    spec = lambda off: pl.BlockSpec((1, S, 128), lambda b, p: (b, 0, off + p))
    return pl.pallas_call(
        functools.partial(_attn_kernel, S=S),
        out_shape=jax.ShapeDtypeStruct((B, S, GROUP_W), F32),
        grid=(B, 2),
        in_specs=[spec(0), spec(2), spec(4)],
        out_specs=pl.BlockSpec((1, S, 128), lambda b, p: (b, 0, p)),
        scratch_shapes=[pltpu.VMEM((S, 128), F32)] * 3,
        compiler_params=_cparams(("parallel", "parallel")),
        name="attention",
    )(pd3, pd3, pd3)


def _outproj_kernel(ya_ref, yb_ref, of_ref, ob_ref, cg_ref, yd_ref, x_ref, dng_ref, mg_ref, wo_ref, g2_ref, rw_ref,
                    x2_ref, h2_ref, aff_ref):
    o = of_ref[...] + ob_ref[...]
    ms = _dot_sel(o * o, _head_sum_matrix(GROUP_W, DN_DK), 2) * (1.0 / DN_DK)
    cg = cg_ref[...]
    yc = (o * lax.rsqrt(ms + EPS) * dng_ref[...]) * (cg * _sigmoid(cg))
    acc = x_ref[...]
    for gi, y in enumerate((ya_ref[...], yb_ref[...], yc, yd_ref[...])):
        sl = slice(gi * GROUP_W, (gi + 1) * GROUP_W)
        mix = y * lax.rsqrt(jnp.mean(y * y, axis=-1, keepdims=True) + EPS) * mg_ref[:, sl]
        acc = acc + _dot(mix, wo_ref[sl, :])
    x2_ref[...] = acc
    h2 = acc * lax.rsqrt(jnp.mean(acc * acc, axis=-1, keepdims=True) + EPS) * g2_ref[...]
    h2_ref[...] = h2.astype(BF16)
    h_hi, h_lo = _split(h2, 2)
    z = _dot_nt(rw_ref[...], h_hi) + _dot_nt(rw_ref[...], h_lo)
    logits = z[0:N_EXPERTS] + z[N_EXPERTS:2 * N_EXPERTS]
    ex = jnp.exp(logits - jnp.max(logits, axis=0, keepdims=True))
    aff_ref[...] = ex / jnp.sum(ex, axis=0, keepdims=True)


def _outproj(ya, yb, of, ob, cg, yd, xt, dng, mg, wo, g2, rw):
    T = xt.shape[0]
    tm = ROW_TILE
    row = lambda w: pl.BlockSpec((tm, w), lambda i: (i, 0))
    const = lambda shape: pl.BlockSpec(shape, lambda i: (0, 0))
    return pl.pallas_call(
        _outproj_kernel,
        out_shape=[jax.ShapeDtypeStruct((T, D_MODEL), F32), jax.ShapeDtypeStruct((T, D_MODEL), BF16),
                   jax.ShapeDtypeStruct((N_EXPERTS, T), F32)],
        grid=(T // tm,),
        in_specs=[row(GROUP_W)] * 6 + [row(D_MODEL), const((1, GROUP_W)), const((1, D_MODEL)),
                                       const((D_MODEL, D_MODEL)), const((1, D_MODEL)),
                                       const((2 * N_EXPERTS, D_MODEL))],
        out_specs=[row(D_MODEL), row(D_MODEL), pl.BlockSpec((N_EXPERTS, tm), lambda i: (0, i))],
        compiler_params=_cparams(("parallel",)),
        name="outproj",
    )(ya, yb, of, ob, cg, yd, xt, dng, mg, wo, g2, rw)


def _strict_upper(n):
    return (lax.broadcasted_iota(I32, (n, n), 0) < lax.broadcasted_iota(I32, (n, n), 1)).astype(BF16)


def _route_kernel(aff_ref, mask_ref, pos_ref, wsel_ref, offs_ref, *, cap, nblk):
    keys = pltpu.bitcast(aff_ref[...], I32)

    def bit_body(i, thr):
        cand = thr | lax.shift_left(jnp.int32(1), 30 - i)
        cnt = jnp.sum((keys >= cand).astype(F32), axis=1, keepdims=True)
        return jnp.where(cnt >= cap, cand, thr)

    thr = lax.fori_loop(0, 31, bit_body, jnp.zeros((N_EXPERTS, 1), I32))
    need = cap - jnp.sum((keys > thr).astype(F32), axis=1, keepdims=True)
    su = _strict_upper(TOK_BLK)

    def blk_body(j, carry):
        ceq, csel = carry
        st = pl.multiple_of(j * TOK_BLK, TOK_BLK)
        kb = pltpu.bitcast(aff_ref[:, pl.ds(st, TOK_BLK)], I32)
        eqf = (kb == thr).astype(F32)
        rank = ceq + _dot(eqf, su)
        sel = ((kb > thr) | ((kb == thr) & (rank < need))).astype(F32)
        mask_ref[:, pl.ds(st, TOK_BLK)] = sel
        wsel_ref[:, pl.ds(st, TOK_BLK)] = sel * aff_ref[:, pl.ds(st, TOK_BLK)]
        pos_ref[:, pl.ds(st, TOK_BLK)] = _dot(sel, su)
        offs_ref[j] = jnp.broadcast_to(csel.astype(I32), (N_EXPERTS, 128))
        return (ceq + jnp.sum(eqf, axis=1, keepdims=True), csel + jnp.sum(sel, axis=1, keepdims=True))

    zero = jnp.zeros((N_EXPERTS, 1), F32)
    lax.fori_loop(0, nblk, blk_body, (zero, zero))


def _route(aff_t, cap):
    E, T = aff_t.shape
    nblk = T // TOK_BLK
    full = lambda shape: pl.BlockSpec(shape, lambda i: (0,) * len(shape))
    return pl.pallas_call(
        functools.partial(_route_kernel, cap=cap, nblk=nblk),
        out_shape=[jax.ShapeDtypeStruct((E, T), F32)] * 3 + [jax.ShapeDtypeStruct((nblk, E, 128), I32)],
        grid=(1,),
        in_specs=[full((E, T))],
        out_specs=[full((E, T))] * 3 + [full((nblk, E, 128))],
        compiler_params=_cparams(("arbitrary",)),
        name="route",
    )(aff_t)


def _gather_kernel(offs_ref, x_ref, m_ref, p_ref, out_ref):
    g, j = pl.program_id(0), pl.program_id(1)

    @pl.when(j == 0)
    def _():
        out_ref[...] = jnp.zeros_like(out_ref)

    W = GATHER_W
    slot0 = lax.broadcasted_iota(I32, (W, TOK_BLK), 0).astype(F32)
    for jb in range(GATHER_JB):
        blk = j * GATHER_JB + jb
        cols = slice(jb * TOK_BLK, (jb + 1) * TOK_BLK)
        targets, bases, passes = [], [], []
        for k in range(GATHER_E):
            off = offs_ref[blk * N_EXPERTS + g * GATHER_E + k]
            cnt = offs_ref[(blk + 1) * N_EXPERTS + g * GATHER_E + k] - off
            base = (off // 16) * 16
            targets.append(jnp.where(m_ref[k, :, cols] > 0.0, p_ref[k, :, cols] + (off - base).astype(F32), -1.0))
            bases.append(base)
            passes.append((off - base + cnt + W - 1) // W)
        onehot = jnp.concatenate([jnp.where(slot0 == t, 1.0, 0.0) for t in targets], axis=0).astype(BF16)
        picked = jnp.dot(onehot, x_ref[cols, :], preferred_element_type=F32).astype(BF16)
        for k in range(GATHER_E):
            out_ref[k, pl.ds(pl.multiple_of(bases[k], 16), W), :] += picked[k * W:(k + 1) * W]
        for k in range(GATHER_E):
            def more(ps, carry, k=k, cols=cols):
                hot = jnp.where(slot0 + (ps * W).astype(F32) == targets[k], 1.0, 0.0).astype(BF16)
                extra = jnp.dot(hot, x_ref[cols, :], preferred_element_type=F32).astype(BF16)
                out_ref[k, pl.ds(pl.multiple_of(bases[k] + ps * W, 16), W), :] += extra
                return carry

            lax.fori_loop(1, passes[k], more, 0)


def _gather(h2, mask3, pos3, offs, cap):
    T = h2.shape[0]
    nblk = T // TOK_BLK
    cap_x = cap + 2 * GATHER_W
    rows = GATHER_JB * TOK_BLK
    grid_spec = pltpu.PrefetchScalarGridSpec(
        num_scalar_prefetch=1,
        grid=(N_EXPERTS // GATHER_E, nblk // GATHER_JB),
        in_specs=[pl.BlockSpec((rows, D_MODEL), lambda g, j, o: (j, 0)),
                  pl.BlockSpec((GATHER_E, 1, rows), lambda g, j, o: (g, 0, j)),
                  pl.BlockSpec((GATHER_E, 1, rows), lambda g, j, o: (g, 0, j))],
        out_specs=pl.BlockSpec((GATHER_E, cap_x, D_MODEL), lambda g, j, o: (g, 0, 0),
                               pipeline_mode=pl.Buffered(1)),
    )
    return pl.pallas_call(
        _gather_kernel,
        out_shape=jax.ShapeDtypeStruct((N_EXPERTS, cap_x, D_MODEL), BF16),
        grid_spec=grid_spec,
        compiler_params=_cparams(("parallel", "arbitrary")),
        name="gather",
    )(offs, h2, mask3, pos3)


def _ffn_kernel(*refs, bounds):
    n = len(bounds) - 1
    xs, w_refs, outs, w_scr = refs[:n], refs[n:n + 3], refs[n + 3:2 * n + 3], refs[2 * n + 3]
    i = pl.program_id(1)

    @pl.when(i == 0)
    def _():
        for k, w_ref in enumerate(w_refs):
            w_scr[k] = w_ref[0, 0].astype(BF16)

    for b in range(n):
        @pl.when(jnp.logical_and(i >= bounds[b], i < bounds[b + 1]))
        def _(b=b):
            x = xs[b][0]
            g = jnp.dot(x, w_scr[0], preferred_element_type=F32)
            u = jnp.dot(x, w_scr[1], preferred_element_type=F32)
            h = (g * _sigmoid(g)) * u
            outs[b][0] = _dot(h, w_scr[2]).astype(BF16)


def _ffn(xes, wg, wu, wd, layer, caps):
    tf = min(512, *caps)
    bounds = [0]
    for cap in caps:
        bounds.append(bounds[-1] + cap // tf)

    def tile_spec(b):
        return pl.BlockSpec((1, tf, D_MODEL),
                            lambda e, i: (e, jnp.clip(i - bounds[b], 0, caps[b] // tf - 1), 0))

    wspec = pl.BlockSpec((1, 1, D_MODEL, D_MODEL), lambda e, i: (layer, e, 0, 0))
    return pl.pallas_call(
        functools.partial(_ffn_kernel, bounds=tuple(bounds)),
        out_shape=[jax.ShapeDtypeStruct((N_EXPERTS, cap, D_MODEL), BF16) for cap in caps],
        grid=(N_EXPERTS, bounds[-1]),
        in_specs=[tile_spec(b) for b in range(len(caps))] + [wspec] * 3,
        out_specs=[tile_spec(b) for b in range(len(caps))],
        scratch_shapes=[pltpu.VMEM((3, D_MODEL, D_MODEL), BF16)],
        compiler_params=_cparams(("parallel", "arbitrary")),
        name="ffn",
    )(*xes, wg, wu, wd)


def _slot_window_start(off, cap):
    return jnp.minimum((off // TOK_BLK) * TOK_BLK, cap - 2 * TOK_BLK)


def _scatter_kernel(offs_ref, x2_ref, w_ref, p_ref, gf_ref, *rest, cap, final):
    ye, out_ref = rest[:N_EXPERTS], rest[N_EXPERTS]
    j = pl.program_id(0)
    ps, ws = p_ref[...].T, w_ref[...].T
    offs = [offs_ref[j * N_EXPERTS + e] for e in range(N_EXPERTS)]
    cnts = [offs_ref[(j + 1) * N_EXPERTS + e] - offs[e] for e in range(N_EXPERTS)]
    rels = [offs[e] - _slot_window_start(offs[e], cap) for e in range(N_EXPERTS)]
    weights = [ws[:, e:e + 1] for e in range(N_EXPERTS)]
    SUB = 16
    most = cnts[0]
    for e in range(1, N_EXPERTS):
        most = jnp.maximum(most, cnts[e])

    def onehot(e, start, width):
        slot = lax.broadcasted_iota(I32, (TOK_BLK, width), 1).astype(F32)
        return jnp.where(slot == ps[:, e:e + 1] + (rels[e] - start).astype(F32), weights[e], 0.0).astype(BF16)

    W = SCATTER_W

    @pl.when(most <= W - SUB)
    def _():
        acc = x2_ref[...]
        for g0 in range(0, N_EXPERTS, 2):
            group = (g0, g0 + 1)
            r0 = {e: pl.multiple_of(jnp.minimum((rels[e] // SUB) * SUB, 2 * TOK_BLK - W), SUB) for e in group}
            acc = acc + jnp.dot(jnp.concatenate([onehot(e, r0[e], W) for e in group], axis=1),
                                jnp.concatenate([ye[e][0, pl.ds(r0[e], W), :] for e in group], axis=0),
                                preferred_element_type=F32)
        out_ref[...] = acc

    @pl.when(most > W - SUB)
    def _():
        out_ref[...] = x2_ref[...]
        for e in range(N_EXPERTS):
            @pl.when(cnts[e] > 0)
            def _(e=e):
                out_ref[...] += jnp.dot(onehot(e, 0, 2 * TOK_BLK), ye[e][0], preferred_element_type=F32)

    if final:
        x = out_ref[...]
        out_ref[...] = x * lax.rsqrt(jnp.mean(x * x, axis=-1, keepdims=True) + EPS) * gf_ref[...]


def _scatter(x2, wsel, pos, ye, offs, gf, cap, final):
    T = x2.shape[0]
    nblk = T // TOK_BLK
    assert cap >= 2 * TOK_BLK and cap % TOK_BLK == 0

    def ye_spec(e):
        return pl.BlockSpec((pl.Element(1), pl.Element(2 * TOK_BLK), pl.Element(D_MODEL)),
                            lambda j, o: (e, pl.multiple_of(_slot_window_start(o[j * N_EXPERTS + e], cap), TOK_BLK),
                                          0))

    row = lambda w: pl.BlockSpec((TOK_BLK, w), lambda j, o: (j, 0))
    col = pl.BlockSpec((N_EXPERTS, TOK_BLK), lambda j, o: (0, j))
    grid_spec = pltpu.PrefetchScalarGridSpec(
        num_scalar_prefetch=1,
        grid=(nblk,),
        in_specs=[row(D_MODEL), col, col, pl.BlockSpec((1, D_MODEL), lambda j, o: (0, 0))]
                 + [ye_spec(e) for e in range(N_EXPERTS)],
        out_specs=row(D_MODEL),
    )
    return pl.pallas_call(
        functools.partial(_scatter_kernel, cap=cap, final=final),
        out_shape=jax.ShapeDtypeStruct((T, D_MODEL), F32),
        grid_spec=grid_spec,
        compiler_params=_cparams(("arbitrary",)),
        name="scatter",
    )(offs, x2, wsel, pos, gf, *([ye] * N_EXPERTS))


def _block_diag(w):
    n, a, b = w.shape
    out = jnp.zeros((n * a, n * b), w.dtype)
    for i in range(n):
        out = out.at[i * a:(i + 1) * a, i * b:(i + 1) * b].set(w[i])
    return out


def _rope_tables(S):
    inv = ROPE_THETA ** (-jnp.arange(0, ROT_DIM, 2, dtype=F32) / ROT_DIM)
    ang = jnp.arange(S, dtype=F32)[:, None] * inv[None, :]
    cos, sin = jnp.cos(ang), jnp.sin(ang)
    half = ROT_DIM // 2
    ones, zeros = jnp.ones((S, ATT_HD - ROT_DIM), F32), jnp.zeros((S, ATT_HD - ROT_DIM), F32)
    zh = jnp.zeros((S, half), F32)
    c = jnp.concatenate([cos, cos, ones], axis=1)
    s1 = jnp.concatenate([-sin, zh, zeros], axis=1)
    s2 = jnp.concatenate([zh, sin, zeros], axis=1)
    return tuple(jnp.tile(t, (1, 2)) for t in (c, s1, s2))


def _expanders():
    eg = np.zeros((2, 128, GROUP_W), np.float32)
    eb = np.zeros((2, 128, GROUP_W), np.float32)
    for d in range(2):
        for h in range(DN_HEADS):
            eg[d, d * DN_HEADS + h, h * DN_DK:(h + 1) * DN_DK] = 1.0
            eb[d, 2 * DN_HEADS + d * DN_HEADS + h, h * DN_DK:(h + 1) * DN_DK] = 1.0
    hd = np.arange(GROUP_W) // DN_DK
    bd = (hd[:, None] == hd[None, :]).astype(np.float32)
    return jnp.asarray(eg), jnp.asarray(eb), jnp.asarray(bd)


def _layer_params(l, norm1_g, w_in, conv_a_w, conv_a_b, rg_wa, rg_ba, rg_wx, rg_bx, rg_lambda, pool_w, pool_scale,
                  dn_conv_w, dn_A_log, dn_dt_bias, dn_norm_g, mix_norm_g, w_out, norm2_g, router_w,
                  exp_w_gate, exp_w_up, exp_w_down):
    w = w_in[l]
    w_cat = jnp.concatenate([w[:, 0:1536], w[:, 1552:2576], w[:, 1536:1552],
                             jnp.zeros((D_MODEL, COL_END - COL_CAB - 16), F32)], axis=1).astype(BF16)
    pad8 = lambda v: jnp.concatenate([v.reshape(1, 2 * DN_HEADS), jnp.zeros((1, 120), F32)], axis=1)
    return dict(
        g1=norm1_g[l].reshape(1, D_MODEL), w_cat=w_cat,
        conv_a_w=conv_a_w[l], conv_a_b=conv_a_b[l].reshape(1, GROUP_W),
        wa=[_block_diag(rg_wa[l, d]).astype(BF16) for d in range(2)],
        wx=[_block_diag(rg_wx[l, d]).astype(BF16) for d in range(2)],
        ba=[rg_ba[l, d].reshape(1, GROUP_W) for d in range(2)],
        bx=[rg_bx[l, d].reshape(1, GROUP_W) for d in range(2)],
        lam=[rg_lambda[l, d].reshape(1, GROUP_W) for d in range(2)],
        pool_w=_block_diag(pool_w[l]).astype(BF16), pool_scale=pool_scale[l].reshape(1, GROUP_W),
        dn_conv_w=dn_conv_w[l], alog=pad8(dn_A_log[l]), dtb=pad8(dn_dt_bias[l]),
        dng=jnp.tile(dn_norm_g[l], DN_HEADS).reshape(1, GROUP_W),
        mg=mix_norm_g[l].reshape(1, D_MODEL), wo=w_out[l].astype(BF16),
        g2=norm2_g[l].reshape(1, D_MODEL),
        rw=jnp.concatenate(_split(router_w[l].T, 2), axis=0),
        wg=exp_w_gate, wu=exp_w_up, wd=exp_w_down,
    )


def _chunk_rows(gcn, d):
    B, S, _ = gcn.shape
    n = S // DN_CHUNK
    g = gcn[:, :, d * DN_HEADS:(d + 1) * DN_HEADS].reshape(B, n, DN_CHUNK, DN_HEADS)
    return g.transpose(0, 1, 3, 2).reshape(B, n, 1, GROUP_W)


def _capacity(T):
    return max(1, EC_CAPACITY * T // N_EXPERTS)


def _mix_and_route(xt, B, S, p, consts, ropes):
    T = B * S
    cap = _capacity(T)
    eg, eb, bdmask = consts
    pa, pb, pq, pg, pd, pcab = _inproj(xt, p["g1"], p["w_cat"], *ropes, S)
    pa3 = pa.reshape(B, S, 2 * GROUP_W)
    rg = lambda d: (p["conv_a_w"], p["conv_a_b"], p["wa"][d], p["ba"][d], p["wx"][d], p["bx"][d], p["lam"][d])
    hf = _rglru(pa3, None, *rg(0), reverse=False)
    ya = _rglru(pa3, hf, *rg(1), reverse=True)
    yb = _pool(pb.reshape(B, S, GROUP_W), p["pool_w"], p["pool_scale"])
    qn, kn, vv, gcf, gcb, bef, beb, gcn = _dnprep(pq.reshape(B, S, 3 * GROUP_W), pcab.reshape(B, S, 128),
                                                  p["dn_conv_w"], p["alog"], p["dtb"], eg, eb)
    of, ob = _delta(qn, kn, vv, gcf, gcb, bef, beb, _chunk_rows(gcn, 0), _chunk_rows(gcn, 1), bdmask)
    yd = _attention(pd.reshape(B, S, 3 * GROUP_W))
    flat = lambda a: a.reshape(T, GROUP_W)
    x2, h2, aff_t = _outproj(flat(ya), flat(yb), flat(of), flat(ob), pg, flat(yd), xt,
                             p["dng"], p["mg"], p["wo"], p["g2"], p["rw"])
    mask, pos, wsel, offs3 = _route(aff_t, cap)
    offs = jnp.concatenate([offs3[:, :, 0].reshape(-1), jnp.full((N_EXPERTS,), cap, I32)])
    xe = _gather(h2, mask.reshape(N_EXPERTS, 1, T), pos.reshape(N_EXPERTS, 1, T), offs, cap)
    return dict(x2=x2, wsel=wsel, pos=pos, offs=offs, xe=xe, cap=cap)


def kernel(x_prompt, x_sample, norm1_g, w_in, conv_a_w, conv_a_b, rg_wa, rg_ba, rg_wx, rg_bx, rg_lambda, pool_w, pool_scale, dn_conv_w, dn_A_log, dn_dt_bias, dn_norm_g, mix_norm_g, w_out, norm2_g, router_w, exp_w_gate, exp_w_up, exp_w_down, final_norm_g):
    layers = [_layer_params(l, norm1_g, w_in, conv_a_w, conv_a_b, rg_wa, rg_ba, rg_wx, rg_bx, rg_lambda, pool_w,
                            pool_scale, dn_conv_w, dn_A_log, dn_dt_bias, dn_norm_g, mix_norm_g, w_out, norm2_g,
                            router_w, exp_w_gate, exp_w_up, exp_w_down) for l in range(DEPTH)]
    consts = _expanders()
    final_g = final_norm_g.reshape(1, D_MODEL)
    batches = (x_prompt, x_sample)
    shapes = [x.shape for x in batches]
    ropes = [_rope_tables(S) for _, S, _ in shapes]
    xts = [x.reshape(B * S, D) for x, (B, S, D) in zip(batches, shapes)]
    for l, p in enumerate(layers):
        st = [_mix_and_route(xt, B, S, p, consts, rp) for xt, (B, S, _), rp in zip(xts, shapes, ropes)]
        yes = _ffn([s["xe"] for s in st], p["wg"], p["wu"], p["wd"], l, [s["cap"] for s in st])
        xts = [_scatter(s["x2"], s["wsel"], s["pos"], ye, s["offs"], final_g, s["cap"], final=(l == DEPTH - 1))
               for s, ye in zip(st, yes)]
    return tuple(xt.reshape(shape) for xt, shape in zip(xts, shapes))
```

```python
import functools
import math
import numpy as np
import jax
import jax.numpy as jnp
from jax import lax
from jax.experimental import pallas as pl
from jax.experimental.pallas import tpu as pltpu

F32, BF16, I32 = jnp.float32, jnp.bfloat16, jnp.int32

D_MODEL = 1024
DEPTH = 2
GROUP_W = 256
RG_C = 8.0
POOL_WINDOWS = (2, 4, 8, 16)
DN_HEADS = 4
DN_DK = 64
DN_CHUNK = 64
ATT_HD = 64
ROT_DIM = 16
ROPE_THETA = 500000.0
ATT_WINDOWS = (128, 512, 2048)
ATT_DILATIONS = (1, 4, 16)
N_EXPERTS = 16
EC_CAPACITY = 2
EPS = 1e-6
NEG = -1e30

COL_A, COL_B, COL_CQ, COL_CG, COL_D, COL_CAB, COL_END = 0, 512, 768, 1536, 1792, 2560, 2688

ROW_TILE = 1024
MIX_TILE = 2048
HALO = 8
DELTA_CB = 16
ATT_QB = 128
ATT_UNROLL = 16
TOK_BLK = 256
GATHER_W = 64
GATHER_JB = 8
GATHER_E = 4
SCATTER_W = 128
VMEM_LIMIT = 56 * 1024 * 1024


def _cparams(sem):
    return pltpu.CompilerParams(dimension_semantics=sem, vmem_limit_bytes=VMEM_LIMIT)


def _dot(a, b):
    return jnp.dot(a.astype(BF16), b.astype(BF16), preferred_element_type=F32)


def _split(a, pieces):
    out, rem = [], a
    for i in range(pieces):
        t = rem.astype(BF16)
        out.append(t)
        if i + 1 < pieces:
            rem = rem - t.astype(F32)
    return out


def _dot_sel(a, sel, pieces=3):
    sel = sel.astype(BF16)
    return sum(jnp.dot(t, sel, preferred_element_type=F32) for t in _split(a, pieces))


def _dot_nt(a, b):
    return lax.dot_general(a.astype(BF16), b.astype(BF16), (((1,), (1,)), ((), ())), preferred_element_type=F32)


def _dot_tn(a, b):
    return lax.dot_general(a.astype(BF16), b.astype(BF16), (((0,), (0,)), ((), ())), preferred_element_type=F32)


def _sigmoid(x):
    return 1.0 / (1.0 + jnp.exp(-x))


def _softplus(x):
    return jnp.maximum(x, 0.0) + jnp.log1p(jnp.exp(-jnp.abs(x)))


def _shift_rows(e, k):
    n = e.shape[0]
    return e if k % n == 0 else pltpu.roll(e, (-k) % n, axis=0)


def _with_halo(cur_ref, prev_ref, next_ref, first, last):
    prev = jnp.where(first, 0.0, prev_ref[0])
    nxt = jnp.where(last, 0.0, next_ref[0])
    return jnp.concatenate([prev, cur_ref[0], nxt], axis=0)


def _halo_specs(ts, width, S, blk_of):
    per = ts // HALO
    last = S // HALO - 1
    cur = pl.BlockSpec((1, ts, width), lambda b, c: (b, blk_of(c), 0))
    prev = pl.BlockSpec((1, HALO, width), lambda b, c: (b, jnp.maximum(blk_of(c) * per - 1, 0), 0))
    nxt = pl.BlockSpec((1, HALO, width), lambda b, c: (b, jnp.minimum((blk_of(c) + 1) * per, last), 0))
    return cur, prev, nxt


def _inproj_kernel(x_ref, g_ref, w_ref, c_ref, s1_ref, s2_ref, pa_ref, pb_ref, pq_ref, pg_ref, pd_ref, pcab_ref):
    x = x_ref[...]
    h = x * lax.rsqrt(jnp.mean(x * x, axis=-1, keepdims=True) + EPS) * g_ref[...]
    hb = h.astype(BF16)

    def mm(lo, hi):
        return jnp.dot(hb, w_ref[:, lo:hi], preferred_element_type=F32)

    pa_ref[...] = mm(COL_A, COL_B)
    pb_ref[...] = mm(COL_B, COL_CQ)
    pq_ref[...] = mm(COL_CQ, COL_CG)
    pg_ref[...] = mm(COL_CG, COL_D)
    c, s1, s2 = c_ref[...], s1_ref[...], s2_ref[...]
    for part in range(2):
        yy = mm(COL_D + 256 * part, COL_D + 256 * part + 256)
        for half in range(2):
            y = yy[:, 128 * half:128 * half + 128]
            y = y * c + pltpu.roll(y, 128 - ROT_DIM // 2, axis=1) * s1 + pltpu.roll(y, ROT_DIM // 2, axis=1) * s2
            if part == 0:
                y = y * (ATT_HD ** -0.5)
            pd_ref[:, 256 * part + 128 * half:256 * part + 128 * half + 128] = y
    vcab = mm(COL_D + 512, COL_END)
    pd_ref[:, 512:768] = vcab[:, 0:256]
    pcab_ref[...] = vcab[:, 256:384]


def _inproj(xt, g1, w_cat, rope_c, rope_s1, rope_s2, S):
    T = xt.shape[0]
    tm = min(ROW_TILE, S)
    per_seq = S // tm
    widths = (512, 256, 768, 256, 768, 128)
    row = lambda w: pl.BlockSpec((tm, w), lambda i: (i, 0))
    rope = pl.BlockSpec((tm, 128), lambda i: (i % per_seq, 0))
    return pl.pallas_call(
        _inproj_kernel,
        out_shape=[jax.ShapeDtypeStruct((T, w), F32) for w in widths],
        grid=(T // tm,),
        in_specs=[row(D_MODEL), pl.BlockSpec((1, D_MODEL), lambda i: (0, 0)),
                  pl.BlockSpec((D_MODEL, COL_END), lambda i: (0, 0)), rope, rope, rope],
        out_specs=[row(w) for w in widths],
        compiler_params=_cparams(("parallel",)),
        name="inproj",
    )(xt, g1, w_cat, rope_c, rope_s1, rope_s2)


def _rglru_kernel(*refs, reverse, ts, nc):
    if reverse:
        (cur_ref, prev_ref, next_ref, gate_ref, hf_ref, cw_ref, cb_ref, wa_ref, ba_ref, wx_ref, bx_ref, lam_ref,
         out_ref, a_ref, b_ref, carry_ref) = refs
    else:
        (cur_ref, prev_ref, next_ref, cw_ref, cb_ref, wa_ref, ba_ref, wx_ref, bx_ref, lam_ref,
         out_ref, a_ref, b_ref, carry_ref) = refs
    c = pl.program_id(1)
    blk = (nc - 1 - c) if reverse else c
    e = _with_halo(cur_ref, prev_ref, next_ref, blk == 0, blk == nc - 1)
    cw = cw_ref[...]
    sl = slice(HALO, HALO + ts)
    u = (cw[0:1] * _shift_rows(e, -2)[sl] + cw[1:2] * _shift_rows(e, -1)[sl] + cw[2:3] * e[sl]
         + cw[3:4] * _shift_rows(e, 1)[sl]) + cb_ref[...]
    r = _sigmoid(_dot(u, wa_ref[...]) + ba_ref[...])
    i = _sigmoid(_dot(u, wx_ref[...]) + bx_ref[...])
    log_a = -RG_C * r * _softplus(-lam_ref[...])
    a_ref[...] = jnp.exp(log_a)
    b_ref[...] = jnp.sqrt(1.0 - jnp.exp(2.0 * log_a)) * (i * u)

    @pl.when(c == 0)
    def _():
        carry_ref[...] = jnp.zeros_like(carry_ref)

    row = lax.broadcasted_iota(I32, (HALO, GROUP_W), 0)
    nt = ts // HALO

    def body(it, carry):
        ti = (nt - 1 - it) if reverse else it
        st = pl.multiple_of(ti * HALO, HALO)
        a = a_ref[pl.ds(st, HALO), :]
        b = b_ref[pl.ds(st, HALO), :]
        for d in (1, 2, 4):
            k = d if reverse else -d
            valid = (row < HALO - d) if reverse else (row >= d)
            b = jnp.where(valid, a * _shift_rows(b, k) + b, b)
            a = jnp.where(valid, a * _shift_rows(a, k), a)
        h = a * carry + b
        if reverse:
            g = gate_ref[0, pl.ds(st, HALO), :]
            cdf = 0.5 * (1.0 + jnp.tanh(np.float32(np.sqrt(2.0 / np.pi)) * (g + 0.044715 * (g * g * g))))
            out_ref[0, pl.ds(st, HALO), :] = (g * cdf) * (hf_ref[0, pl.ds(st, HALO), :] + h)
            return jnp.broadcast_to(h[0:1], h.shape)
        out_ref[0, pl.ds(st, HALO), :] = h
        return jnp.broadcast_to(h[HALO - 1:HALO], h.shape)

    carry_ref[...] = lax.fori_loop(0, nt, body, carry_ref[...], unroll=4)


def _rglru(pa3, hf, cw, cb, wa, ba, wx, bx, lam, reverse):
    B, S, _ = pa3.shape
    ts = min(MIX_TILE, S)
    nc = S // ts
    blk_of = (lambda c: nc - 1 - c) if reverse else (lambda c: c)
    cur, prev, nxt = _halo_specs(ts, GROUP_W, S, blk_of)
    tile = pl.BlockSpec((1, ts, GROUP_W), lambda b, c: (b, blk_of(c), 0))
    const = lambda shape: pl.BlockSpec(shape, lambda b, c: (0,) * len(shape))
    in_specs = [cur, prev, nxt]
    args = [pa3, pa3, pa3]
    if reverse:
        in_specs += [pl.BlockSpec((1, ts, GROUP_W), lambda b, c: (b, blk_of(c), 1)), tile]
        args += [pa3, hf]
    in_specs += [const((4, GROUP_W)), const((1, GROUP_W)), const((GROUP_W, GROUP_W)), const((1, GROUP_W)),
                 const((GROUP_W, GROUP_W)), const((1, GROUP_W)), const((1, GROUP_W))]
    args += [cw, cb, wa, ba, wx, bx, lam]
    return pl.pallas_call(
        functools.partial(_rglru_kernel, reverse=reverse, ts=ts, nc=nc),
        out_shape=jax.ShapeDtypeStruct((B, S, GROUP_W), F32),
        grid=(B, nc),
        in_specs=in_specs,
        out_specs=tile,
        scratch_shapes=[pltpu.VMEM((ts, GROUP_W), F32), pltpu.VMEM((ts, GROUP_W), F32),
                        pltpu.VMEM((HALO, GROUP_W), F32)],
        compiler_params=_cparams(("parallel", "arbitrary")),
        name="rglru_bwd" if reverse else "rglru_fwd",
    )(*args)


def _pool_kernel(cur_ref, prev_ref, next_ref, w_ref, sc_ref, out_ref, *, ts, nc, S):
    c = pl.program_id(1)
    e = _with_halo(cur_ref, prev_ref, next_ref, c == 0, c == nc - 1)
    sl = slice(HALO, HALO + ts)
    a2 = e + _shift_rows(e, 1)
    a4 = a2 + _shift_rows(a2, 2)
    a8 = a4 + _shift_rows(a4, 4)
    a16 = a8 + _shift_rows(a8, 8)
    sums = [_shift_rows(a, -(w // 2))[sl] for a, w in zip((a2, a4, a8, a16), POOL_WINDOWS)]
    gi = lax.broadcasted_iota(I32, (ts, GROUP_W), 1) // (GROUP_W // len(POOL_WINDOWS))
    ssum = jnp.where(gi == 0, sums[0], jnp.where(gi == 1, sums[1], jnp.where(gi == 2, sums[2], sums[3])))
    hw = jnp.where(gi == 0, 1, jnp.where(gi == 1, 2, jnp.where(gi == 2, 4, 8)))
    t = c * ts + lax.broadcasted_iota(I32, (ts, GROUP_W), 0)
    cnt = (jnp.minimum(t + hw, S) - jnp.maximum(t - hw, 0)).astype(F32)
    p = ssum / cnt - e[sl]
    out_ref[0] = _dot(p, w_ref[...]) * sc_ref[...]


def _pool(pb3, w_bd, scale):
    B, S, _ = pb3.shape
    ts = min(MIX_TILE, S)
    nc = S // ts
    cur, prev, nxt = _halo_specs(ts, GROUP_W, S, lambda c: c)
    return pl.pallas_call(
        functools.partial(_pool_kernel, ts=ts, nc=nc, S=S),
        out_shape=jax.ShapeDtypeStruct((B, S, GROUP_W), F32),
        grid=(B, nc),
        in_specs=[cur, prev, nxt, pl.BlockSpec((GROUP_W, GROUP_W), lambda b, c: (0, 0)),
                  pl.BlockSpec((1, GROUP_W), lambda b, c: (0, 0))],
        out_specs=pl.BlockSpec((1, ts, GROUP_W), lambda b, c: (b, c, 0)),
        compiler_params=_cparams(("parallel", "parallel")),
        name="pool",
    )(pb3, pb3, pb3, w_bd, scale)


def _head_sum_matrix(n, group):
    r = lax.broadcasted_iota(I32, (n, n), 0) // group
    c = lax.broadcasted_iota(I32, (n, n), 1) // group
    return (r == c).astype(F32)


def _dnprep_kernel(cur_ref, prev_ref, next_ref, cab_ref, cw_ref, alog_ref, dtb_ref, eg_ref, eb_ref,
                   q_ref, k_ref, v_ref, gcf_ref, gcb_ref, bef_ref, beb_ref, gcn_ref, *, ts, nc):
    c = pl.program_id(1)
    e = _with_halo(cur_ref, prev_ref, next_ref, c == 0, c == nc - 1)
    cw = cw_ref[...]
    sl = slice(HALO, HALO + ts)
    y = (cw[0:1] * _shift_rows(e, -2)[sl] + cw[1:2] * _shift_rows(e, -1)[sl] + cw[2:3] * e[sl]
         + cw[3:4] * _shift_rows(e, 1)[sl])
    y = y * _sigmoid(y)
    q, k = y[:, 0:GROUP_W], y[:, GROUP_W:2 * GROUP_W]
    hs = _head_sum_matrix(GROUP_W, DN_DK)
    q_ref[0] = q * lax.rsqrt(_dot_sel(q * q, hs, 2) + EPS) * (DN_DK ** -0.5)
    k_ref[0] = k * lax.rsqrt(_dot_sel(k * k, hs, 2) + EPS)
    v_ref[0] = y[:, 2 * GROUP_W:3 * GROUP_W]
    cab = cab_ref[0]
    g = -jnp.exp(alog_ref[...]) * _softplus(cab + dtb_ref[...])
    beta = _sigmoid(cab)
    bef_ref[0] = _dot_sel(beta, eb_ref[0], 2)
    beb_ref[0] = _dot_sel(beta, eb_ref[1], 2)
    pos = lax.broadcasted_iota(I32, (ts, 128), 0) % DN_CHUNK
    fwd = rev = g
    d = 1
    while d < DN_CHUNK:
        fwd = fwd + jnp.where(pos >= d, _shift_rows(fwd, -d), 0.0)
        rev = rev + jnp.where(pos < DN_CHUNK - d, _shift_rows(rev, d), 0.0)
        d *= 2
    gcn = jnp.where(lax.broadcasted_iota(I32, (ts, 128), 1) < DN_HEADS, fwd, rev)
    gcn_ref[0] = gcn
    gcf_ref[0] = _dot_sel(gcn, eg_ref[0])
    gcb_ref[0] = _dot_sel(gcn, eg_ref[1])


def _dnprep(pq3, pcab3, cw, alog_row, dtb_row, eg, eb):
    B, S, _ = pq3.shape
    ts = min(ROW_TILE, S)
    nc = S // ts
    cur, prev, nxt = _halo_specs(ts, 3 * GROUP_W, S, lambda c: c)
    tile = pl.BlockSpec((1, ts, GROUP_W), lambda b, c: (b, c, 0))
    const = lambda shape: pl.BlockSpec(shape, lambda b, c: (0,) * len(shape))
    return pl.pallas_call(
        functools.partial(_dnprep_kernel, ts=ts, nc=nc),
        out_shape=[jax.ShapeDtypeStruct((B, S, GROUP_W), F32)] * 7 + [jax.ShapeDtypeStruct((B, S, 128), F32)],
        grid=(B, nc),
        in_specs=[cur, prev, nxt, pl.BlockSpec((1, ts, 128), lambda b, c: (b, c, 0)),
                  const((4, 3 * GROUP_W)), const((1, 128)), const((1, 128)),
                  const((2, 128, GROUP_W)), const((2, 128, GROUP_W))],
        out_specs=[tile] * 7 + [pl.BlockSpec((1, ts, 128), lambda b, c: (b, c, 0))],
        compiler_params=_cparams(("parallel", "parallel")),
        name="dnprep",
    )(pq3, pq3, pq3, pcab3, cw, alog_row, dtb_row, eg, eb)


def _delta_kernel(qf, kf, vf, gcf, bef, grf, qb, kb, vb, gcb, beb, grb, bd_ref, of_ref, ob_ref,
                  s_ref, p_scr, x_scr, wq_scr, u_scr, at_scr, kd_scr, vb_scr, kg_scr, *, cb):
    j = pl.program_id(1)

    @pl.when(j == 0)
    def _():
        s_ref[...] = jnp.zeros_like(s_ref)

    C = DN_CHUNK
    bd = bd_ref[...] > 0.0
    c_idx = lax.broadcasted_iota(I32, (C, GROUP_W), 0)
    m_idx = lax.broadcasted_iota(I32, (C, GROUP_W), 1) % C
    eye = (c_idx == m_idx).astype(F32)

    def blockdiag(x):
        return jnp.where(bd, jnp.tile(x.astype(BF16), (DN_HEADS, 1)), jnp.zeros((), BF16))

    dirs = ((qf, kf, vf, gcf, bef, grf, of_ref), (qb, kb, vb, gcb, beb, grb, ob_ref))
    units = [(d, ci) for ci in range(cb) for d in range(2)]

    def chunk_of(d, ci):
        return cb - 1 - ci if d == 1 else ci

    def prepare(ui):
        d, ci = units[ui]
        q_r, k_r, v_r, gc_r, be_r, gr_r, _ = dirs[d]
        rev = d == 1
        tril = (c_idx <= m_idx) if rev else (c_idx >= m_idx)
        strict = (c_idx < m_idx) if rev else (c_idx > m_idx)
        last = 0 if rev else C - 1
        cc = chunk_of(d, ci)
        rows = slice(cc * C, (cc + 1) * C)
        q, k, v = q_r[0, rows, :], k_r[0, rows, :], v_r[0, rows, :]
        gc, be, gr = gc_r[0, rows, :], be_r[0, rows, :], gr_r[0, cc]
        eg = jnp.exp(gc)
        kbeta = k * be
        kkqk = _dot_nt(jnp.concatenate([kbeta, q], axis=0), blockdiag(k))
        decay = jnp.exp(jnp.where(tril, gc - gr, -jnp.inf))
        at_scr[ui] = (kkqk[C:] * decay).astype(BF16)
        p = -jnp.where(strict, kkqk[:C] * decay, 0.0)
        p_scr[ui] = p.astype(BF16)
        x_scr[ui] = eye + p
        wq_scr[ui, C:2 * C, :] = (q * eg).astype(BF16)
        kd_scr[ui] = (k * jnp.exp(gc[last:last + 1] - gc)).astype(BF16)
        vb_scr[ui] = (v * be).astype(BF16)
        kg_scr[ui] = (kbeta * eg).astype(BF16)

    def solve(group):
        for ui in group:
            p = p_scr[ui]
            p_scr[ui] = _dot(p, blockdiag(p)).astype(BF16)
        for _ in range(4):
            for ui in group:
                p, x = p_scr[ui], x_scr[ui]
                px = _dot(jnp.concatenate([p, x.astype(BF16)], axis=0), blockdiag(p))
                p_scr[ui] = px[:C].astype(BF16)
                x_scr[ui] = x + px[C:]
        for ui in group:
            x = x_scr[ui]
            x = x + _dot(x, blockdiag(p_scr[ui]))
            u_scr[ui] = _dot(x, blockdiag(vb_scr[ui]))
            wq_scr[ui, 0:C, :] = _dot(x, blockdiag(kg_scr[ui])).astype(BF16)

    def recur(ui):
        d, ci = units[ui]
        gc_r, o_r = dirs[d][3], dirs[d][6]
        cc = chunk_of(d, ci)
        last = cc * C + (0 if d == 1 else C - 1)
        state = s_ref[d]
        ws_qs = _dot(wq_scr[ui], state)
        v_new = u_scr[ui] - ws_qs[:C]
        o_r[0, cc * C:(cc + 1) * C, :] = ws_qs[C:] + _dot(at_scr[ui], blockdiag(v_new))
        s_ref[d] = (state * jnp.exp(gc_r[0, last:last + 1, :])
                    + jnp.where(bd, _dot_tn(kd_scr[ui], v_new), 0.0))

    for ui in range(len(units)):
        prepare(ui)
    solve(range(len(units)))
    for ui in range(len(units)):
        recur(ui)


def _delta(qn, kn, vv, gcf, gcb, bef, beb, grf, grb, bdmask):
    B, S, _ = qn.shape
    cb = DELTA_CB
    rb = cb * DN_CHUNK
    nb = S // rb
    f = lambda b, j: (b, j, 0)
    r = lambda b, j: (b, nb - 1 - j, 0)
    tf = pl.BlockSpec((1, rb, GROUP_W), f)
    tr = pl.BlockSpec((1, rb, GROUP_W), r)
    gf = pl.BlockSpec((1, cb, 1, GROUP_W), lambda b, j: (b, j, 0, 0))
    gr = pl.BlockSpec((1, cb, 1, GROUP_W), lambda b, j: (b, nb - 1 - j, 0, 0))
    return pl.pallas_call(
        functools.partial(_delta_kernel, cb=cb),
        out_shape=[jax.ShapeDtypeStruct((B, S, GROUP_W), F32)] * 2,
        grid=(B, nb),
        in_specs=[tf, tf, tf, tf, tf, gf, tr, tr, tr, tr, tr, gr,
                  pl.BlockSpec((GROUP_W, GROUP_W), lambda b, j: (0, 0))],
        out_specs=[tf, tr],
        scratch_shapes=[pltpu.VMEM((2, GROUP_W, GROUP_W), F32)]
                       + [pltpu.VMEM((2 * cb, n * DN_CHUNK, GROUP_W), dt) for n, dt in
                          ((1, BF16), (1, F32), (2, BF16), (1, F32), (1, BF16), (1, BF16), (1, BF16), (1, BF16))],
        compiler_params=_cparams(("parallel", "arbitrary")),
        name="delta",
    )(qn, kn, vv, gcf, bef, grf, qn, kn, vv, gcb, beb, grb, bdmask)


def _attn_kernel(q_ref, k_ref, v_ref, out_ref, m_ref, l_ref, o_ref, *, S):
    head0 = lax.broadcasted_iota(I32, (1, 128), 1) < ATT_HD

    def rows(start, n, dil):
        return pl.ds(start, n) if dil == 1 else pl.ds(start, n, stride=dil)

    for bi, (win, dil) in enumerate(zip(ATT_WINDOWS, ATT_DILATIONS)):
        half = win // (2 * dil)
        L = S // dil
        qb_n = min(ATT_QB, L)
        kw = min(L, qb_n + 2 * half)
        nqb = L // qb_n

        def block_stats(idx, dil=dil, half=half, L=L, qb_n=qb_n, kw=kw, nqb=nqb):
            r = idx // nqb
            m0 = (idx % nqb) * qb_n
            ks = jnp.clip(m0 - half, 0, L - kw)
            qsel = rows(r + m0 * dil, qb_n, dil)
            ksel = rows(r + ks * dil, kw, dil)
            q = q_ref[0, qsel, :]
            kk = k_ref[0, ksel, :].astype(BF16)
            vv = v_ref[0, ksel, :].astype(BF16)
            rel = (lax.broadcasted_iota(I32, (qb_n, kw), 1) - lax.broadcasted_iota(I32, (qb_n, kw), 0)
                   + (ks - m0 + half))
            valid = rel.astype(jnp.uint32) <= 2 * half
            q2 = jnp.concatenate([jnp.where(head0, q, 0.0), jnp.where(head0, 0.0, q)], axis=0)
            s = jnp.where(jnp.concatenate([valid, valid], axis=0), _dot_nt(q2, kk), NEG)
            m = jnp.max(s, axis=-1, keepdims=True)
            p = jnp.exp(s - m)
            l = jnp.sum(p, axis=-1, keepdims=True)
            o = _dot(p, vv)
            return (qsel, jnp.where(head0, m[:qb_n], m[qb_n:]), jnp.where(head0, l[:qb_n], l[qb_n:]),
                    jnp.where(head0, o[:qb_n], o[qb_n:]))

        unroll = math.gcd(ATT_UNROLL, dil * nqb)

        def body(it, carry, bi=bi, unroll=unroll):
            stats = [block_stats(it * unroll + s) for s in range(unroll)]
            if bi > 0:
                olds = [(m_ref[qsel, :], l_ref[qsel, :], o_ref[qsel, :]) for qsel, _, _, _ in stats]
                merged = []
                for (qsel, m, l, o), (m_old, l_old, o_old) in zip(stats, olds):
                    m_new = jnp.maximum(m_old, m)
                    w_old, w_cur = jnp.exp(m_old - m_new), jnp.exp(m - m_new)
                    merged.append((qsel, m_new, w_old * l_old + w_cur * l, w_old * o_old + w_cur * o))
                stats = merged
            for qsel, m, l, o in stats:
                if bi == len(ATT_WINDOWS) - 1:
                    out_ref[0, qsel, :] = o / l
                else:
                    m_ref[qsel, :] = m
                    l_ref[qsel, :] = l
                    o_ref[qsel, :] = o
            return carry

        lax.fori_loop(0, dil * nqb // unroll, body, 0)


def _attention(pd3):
    B, S, _ = pd3.shape
    spec = lambda off: pl.BlockSpec((1, S, 128), lambda b, p: (b, 0, off + p))
    return pl.pallas_call(
        functools.partial(_attn_kernel, S=S),
        out_shape=jax.ShapeDtypeStruct((B, S, GROUP_W), F32),
        grid=(B, 2),
        in_specs=[spec(0), spec(2), spec(4)],
        out_specs=pl.BlockSpec((1, S, 128), lambda b, p: (b, 0, p)),
        scratch_shapes=[pltpu.VMEM((S, 128), F32)] * 3,
        compiler_params=_cparams(("parallel", "parallel")),
        name="attention",
    )(pd3, pd3, pd3)


def _outproj_kernel(ya_ref, yb_ref, of_ref, ob_ref, cg_ref, yd_ref, x_ref, dng_ref, mg_ref, wo_ref, g2_ref, rw_ref,
                    x2_ref, h2_ref, aff_ref):
    o = of_ref[...] + ob_ref[...]
    ms = _dot_sel(o * o, _head_sum_matrix(GROUP_W, DN_DK), 2) * (1.0 / DN_DK)
    cg = cg_ref[...]
    yc = (o * lax.rsqrt(ms + EPS) * dng_ref[...]) * (cg * _sigmoid(cg))
    acc = x_ref[...]
    for gi, y in enumerate((ya_ref[...], yb_ref[...], yc, yd_ref[...])):
        sl = slice(gi * GROUP_W, (gi + 1) * GROUP_W)
        mix = y * lax.rsqrt(jnp.mean(y * y, axis=-1, keepdims=True) + EPS) * mg_ref[:, sl]
        acc = acc + _dot(mix, wo_ref[sl, :])
    x2_ref[...] = acc
    h2 = acc * lax.rsqrt(jnp.mean(acc * acc, axis=-1, keepdims=True) + EPS) * g2_ref[...]
    h2_ref[...] = h2.astype(BF16)
    h_hi, h_lo = _split(h2, 2)
    z = _dot_nt(rw_ref[...], h_hi) + _dot_nt(rw_ref[...], h_lo)
    logits = z[0:N_EXPERTS] + z[N_EXPERTS:2 * N_EXPERTS]
    ex = jnp.exp(logits - jnp.max(logits, axis=0, keepdims=True))
    aff_ref[...] = ex / jnp.sum(ex, axis=0, keepdims=True)


def _outproj(ya, yb, of, ob, cg, yd, xt, dng, mg, wo, g2, rw):
    T = xt.shape[0]
    tm = ROW_TILE
    row = lambda w: pl.BlockSpec((tm, w), lambda i: (i, 0))
    const = lambda shape: pl.BlockSpec(shape, lambda i: (0, 0))
    return pl.pallas_call(
        _outproj_kernel,
        out_shape=[jax.ShapeDtypeStruct((T, D_MODEL), F32), jax.ShapeDtypeStruct((T, D_MODEL), BF16),
                   jax.ShapeDtypeStruct((N_EXPERTS, T), F32)],
        grid=(T // tm,),
        in_specs=[row(GROUP_W)] * 6 + [row(D_MODEL), const((1, GROUP_W)), const((1, D_MODEL)),
                                       const((D_MODEL, D_MODEL)), const((1, D_MODEL)),
                                       const((2 * N_EXPERTS, D_MODEL))],
        out_specs=[row(D_MODEL), row(D_MODEL), pl.BlockSpec((N_EXPERTS, tm), lambda i: (0, i))],
        compiler_params=_cparams(("parallel",)),
        name="outproj",
    )(ya, yb, of, ob, cg, yd, xt, dng, mg, wo, g2, rw)


def _strict_upper(n):
    return (lax.broadcasted_iota(I32, (n, n), 0) < lax.broadcasted_iota(I32, (n, n), 1)).astype(BF16)


def _route_kernel(aff_ref, mask_ref, pos_ref, wsel_ref, offs_ref, *, cap, nblk):
    keys = pltpu.bitcast(aff_ref[...], I32)

    def bit_body(i, thr):
        cand = thr | lax.shift_left(jnp.int32(1), 30 - i)
        cnt = jnp.sum((keys >= cand).astype(F32), axis=1, keepdims=True)
        return jnp.where(cnt >= cap, cand, thr)

    thr = lax.fori_loop(0, 31, bit_body, jnp.zeros((N_EXPERTS, 1), I32))
    need = cap - jnp.sum((keys > thr).astype(F32), axis=1, keepdims=True)
    su = _strict_upper(TOK_BLK)

    def blk_body(j, carry):
        ceq, csel = carry
        st = pl.multiple_of(j * TOK_BLK, TOK_BLK)
        kb = pltpu.bitcast(aff_ref[:, pl.ds(st, TOK_BLK)], I32)
        eqf = (kb == thr).astype(F32)
        rank = ceq + _dot(eqf, su)
        sel = ((kb > thr) | ((kb == thr) & (rank < need))).astype(F32)
        mask_ref[:, pl.ds(st, TOK_BLK)] = sel
        wsel_ref[:, pl.ds(st, TOK_BLK)] = sel * aff_ref[:, pl.ds(st, TOK_BLK)]
        pos_ref[:, pl.ds(st, TOK_BLK)] = _dot(sel, su)
        offs_ref[j] = jnp.broadcast_to(csel.astype(I32), (N_EXPERTS, 128))
        return (ceq + jnp.sum(eqf, axis=1, keepdims=True), csel + jnp.sum(sel, axis=1, keepdims=True))

    zero = jnp.zeros((N_EXPERTS, 1), F32)
    lax.fori_loop(0, nblk, blk_body, (zero, zero))


def _route(aff_t, cap):
    E, T = aff_t.shape
    nblk = T // TOK_BLK
    full = lambda shape: pl.BlockSpec(shape, lambda i: (0,) * len(shape))
    return pl.pallas_call(
        functools.partial(_route_kernel, cap=cap, nblk=nblk),
        out_shape=[jax.ShapeDtypeStruct((E, T), F32)] * 3 + [jax.ShapeDtypeStruct((nblk, E, 128), I32)],
        grid=(1,),
        in_specs=[full((E, T))],
        out_specs=[full((E, T))] * 3 + [full((nblk, E, 128))],
        compiler_params=_cparams(("arbitrary",)),
        name="route",
    )(aff_t)


def _gather_kernel(offs_ref, x_ref, m_ref, p_ref, out_ref):
    g, j = pl.program_id(0), pl.program_id(1)

    @pl.when(j == 0)
    def _():
        out_ref[...] = jnp.zeros_like(out_ref)

    W = GATHER_W
    slot0 = lax.broadcasted_iota(I32, (W, TOK_BLK), 0).astype(F32)
    for jb in range(GATHER_JB):
        blk = j * GATHER_JB + jb
        cols = slice(jb * TOK_BLK, (jb + 1) * TOK_BLK)
        targets, bases, passes = [], [], []
        for k in range(GATHER_E):
            off = offs_ref[blk * N_EXPERTS + g * GATHER_E + k]
            cnt = offs_ref[(blk + 1) * N_EXPERTS + g * GATHER_E + k] - off
            base = (off // 16) * 16
            targets.append(jnp.where(m_ref[k, :, cols] > 0.0, p_ref[k, :, cols] + (off - base).astype(F32), -1.0))
            bases.append(base)
            passes.append((off - base + cnt + W - 1) // W)
        onehot = jnp.concatenate([jnp.where(slot0 == t, 1.0, 0.0) for t in targets], axis=0).astype(BF16)
        picked = jnp.dot(onehot, x_ref[cols, :], preferred_element_type=F32).astype(BF16)
        for k in range(GATHER_E):
            out_ref[k, pl.ds(pl.multiple_of(bases[k], 16), W), :] += picked[k * W:(k + 1) * W]
        for k in range(GATHER_E):
            def more(ps, carry, k=k, cols=cols):
                hot = jnp.where(slot0 + (ps * W).astype(F32) == targets[k], 1.0, 0.0).astype(BF16)
                extra = jnp.dot(hot, x_ref[cols, :], preferred_element_type=F32).astype(BF16)
                out_ref[k, pl.ds(pl.multiple_of(bases[k] + ps * W, 16), W), :] += extra
                return carry

            lax.fori_loop(1, passes[k], more, 0)


def _gather(h2, mask3, pos3, offs, cap):
    T = h2.shape[0]
    nblk = T // TOK_BLK
    cap_x = cap + 2 * GATHER_W
    rows = GATHER_JB * TOK_BLK
    grid_spec = pltpu.PrefetchScalarGridSpec(
        num_scalar_prefetch=1,
        grid=(N_EXPERTS // GATHER_E, nblk // GATHER_JB),
        in_specs=[pl.BlockSpec((rows, D_MODEL), lambda g, j, o: (j, 0)),
                  pl.BlockSpec((GATHER_E, 1, rows), lambda g, j, o: (g, 0, j)),
                  pl.BlockSpec((GATHER_E, 1, rows), lambda g, j, o: (g, 0, j))],
        out_specs=pl.BlockSpec((GATHER_E, cap_x, D_MODEL), lambda g, j, o: (g, 0, 0),
                               pipeline_mode=pl.Buffered(1)),
    )
    return pl.pallas_call(
        _gather_kernel,
        out_shape=jax.ShapeDtypeStruct((N_EXPERTS, cap_x, D_MODEL), BF16),
        grid_spec=grid_spec,
        compiler_params=_cparams(("parallel", "arbitrary")),
        name="gather",
    )(offs, h2, mask3, pos3)


def _ffn_kernel(*refs, bounds):
    n = len(bounds) - 1
    xs, w_refs, outs, w_scr = refs[:n], refs[n:n + 3], refs[n + 3:2 * n + 3], refs[2 * n + 3]
    i = pl.program_id(1)

    @pl.when(i == 0)
    def _():
        for k, w_ref in enumerate(w_refs):
            w_scr[k] = w_ref[0, 0].astype(BF16)

    for b in range(n):
        @pl.when(jnp.logical_and(i >= bounds[b], i < bounds[b + 1]))
        def _(b=b):
            x = xs[b][0]
            g = jnp.dot(x, w_scr[0], preferred_element_type=F32)
            u = jnp.dot(x, w_scr[1], preferred_element_type=F32)
            h = (g * _sigmoid(g)) * u
            outs[b][0] = _dot(h, w_scr[2]).astype(BF16)


def _ffn(xes, wg, wu, wd, layer, caps):
    tf = min(512, *caps)
    bounds = [0]
    for cap in caps:
        bounds.append(bounds[-1] + cap // tf)

    def tile_spec(b):
        return pl.BlockSpec((1, tf, D_MODEL),
                            lambda e, i: (e, jnp.clip(i - bounds[b], 0, caps[b] // tf - 1), 0))

    wspec = pl.BlockSpec((1, 1, D_MODEL, D_MODEL), lambda e, i: (layer, e, 0, 0))
    return pl.pallas_call(
        functools.partial(_ffn_kernel, bounds=tuple(bounds)),
        out_shape=[jax.ShapeDtypeStruct((N_EXPERTS, cap, D_MODEL), BF16) for cap in caps],
        grid=(N_EXPERTS, bounds[-1]),
        in_specs=[tile_spec(b) for b in range(len(caps))] + [wspec] * 3,
        out_specs=[tile_spec(b) for b in range(len(caps))],
        scratch_shapes=[pltpu.VMEM((3, D_MODEL, D_MODEL), BF16)],
        compiler_params=_cparams(("parallel", "arbitrary")),
        name="ffn",
    )(*xes, wg, wu, wd)


def _slot_window_start(off, cap):
    return jnp.minimum((off // TOK_BLK) * TOK_BLK, cap - 2 * TOK_BLK)


def _scatter_kernel(offs_ref, x2_ref, w_ref, p_ref, gf_ref, *rest, cap, final):
    ye, out_ref = rest[:N_EXPERTS], rest[N_EXPERTS]
    j = pl.program_id(0)
    ps, ws = p_ref[...].T, w_ref[...].T
    offs = [offs_ref[j * N_EXPERTS + e] for e in range(N_EXPERTS)]
    cnts = [offs_ref[(j + 1) * N_EXPERTS + e] - offs[e] for e in range(N_EXPERTS)]
    rels = [offs[e] - _slot_window_start(offs[e], cap) for e in range(N_EXPERTS)]
    weights = [ws[:, e:e + 1] for e in range(N_EXPERTS)]
    SUB = 16
    most = cnts[0]
    for e in range(1, N_EXPERTS):
        most = jnp.maximum(most, cnts[e])

    def onehot(e, start, width):
        slot = lax.broadcasted_iota(I32, (TOK_BLK, width), 1).astype(F32)
        return jnp.where(slot == ps[:, e:e + 1] + (rels[e] - start).astype(F32), weights[e], 0.0).astype(BF16)

    W = SCATTER_W

    @pl.when(most <= W - SUB)
    def _():
        acc = x2_ref[...]
        for g0 in range(0, N_EXPERTS, 2):
            group = (g0, g0 + 1)
            r0 = {e: pl.multiple_of(jnp.minimum((rels[e] // SUB) * SUB, 2 * TOK_BLK - W), SUB) for e in group}
            acc = acc + jnp.dot(jnp.concatenate([onehot(e, r0[e], W) for e in group], axis=1),
                                jnp.concatenate([ye[e][0, pl.ds(r0[e], W), :] for e in group], axis=0),
                                preferred_element_type=F32)
        out_ref[...] = acc

    @pl.when(most > W - SUB)
    def _():
        out_ref[...] = x2_ref[...]
        for e in range(N_EXPERTS):
            @pl.when(cnts[e] > 0)
            def _(e=e):
                out_ref[...] += jnp.dot(onehot(e, 0, 2 * TOK_BLK), ye[e][0], preferred_element_type=F32)

    if final:
        x = out_ref[...]
        out_ref[...] = x * lax.rsqrt(jnp.mean(x * x, axis=-1, keepdims=True) + EPS) * gf_ref[...]


def _scatter(x2, wsel, pos, ye, offs, gf, cap, final):
    T = x2.shape[0]
    nblk = T // TOK_BLK
    assert cap >= 2 * TOK_BLK and cap % TOK_BLK == 0

    def ye_spec(e):
        return pl.BlockSpec((pl.Element(1), pl.Element(2 * TOK_BLK), pl.Element(D_MODEL)),
                            lambda j, o: (e, pl.multiple_of(_slot_window_start(o[j * N_EXPERTS + e], cap), TOK_BLK),
                                          0))

    row = lambda w: pl.BlockSpec((TOK_BLK, w), lambda j, o: (j, 0))
    col = pl.BlockSpec((N_EXPERTS, TOK_BLK), lambda j, o: (0, j))
    grid_spec = pltpu.PrefetchScalarGridSpec(
        num_scalar_prefetch=1,
        grid=(nblk,),
        in_specs=[row(D_MODEL), col, col, pl.BlockSpec((1, D_MODEL), lambda j, o: (0, 0))]
                 + [ye_spec(e) for e in range(N_EXPERTS)],
        out_specs=row(D_MODEL),
    )
    return pl.pallas_call(
        functools.partial(_scatter_kernel, cap=cap, final=final),
        out_shape=jax.ShapeDtypeStruct((T, D_MODEL), F32),
        grid_spec=grid_spec,
        compiler_params=_cparams(("arbitrary",)),
        name="scatter",
    )(offs, x2, wsel, pos, gf, *([ye] * N_EXPERTS))


def _block_diag(w):
    n, a, b = w.shape
    out = jnp.zeros((n * a, n * b), w.dtype)
    for i in range(n):
        out = out.at[i * a:(i + 1) * a, i * b:(i + 1) * b].set(w[i])
    return out


def _rope_tables(S):
    inv = ROPE_THETA ** (-jnp.arange(0, ROT_DIM, 2, dtype=F32) / ROT_DIM)
    ang = jnp.arange(S, dtype=F32)[:, None] * inv[None, :]
    cos, sin = jnp.cos(ang), jnp.sin(ang)
    half = ROT_DIM // 2
    ones, zeros = jnp.ones((S, ATT_HD - ROT_DIM), F32), jnp.zeros((S, ATT_HD - ROT_DIM), F32)
    zh = jnp.zeros((S, half), F32)
    c = jnp.concatenate([cos, cos, ones], axis=1)
    s1 = jnp.concatenate([-sin, zh, zeros], axis=1)
    s2 = jnp.concatenate([zh, sin, zeros], axis=1)
    return tuple(jnp.tile(t, (1, 2)) for t in (c, s1, s2))


def _expanders():
    eg = np.zeros((2, 128, GROUP_W), np.float32)
    eb = np.zeros((2, 128, GROUP_W), np.float32)
    for d in range(2):
        for h in range(DN_HEADS):
            eg[d, d * DN_HEADS + h, h * DN_DK:(h + 1) * DN_DK] = 1.0
            eb[d, 2 * DN_HEADS + d * DN_HEADS + h, h * DN_DK:(h + 1) * DN_DK] = 1.0
    hd = np.arange(GROUP_W) // DN_DK
    bd = (hd[:, None] == hd[None, :]).astype(np.float32)
    return jnp.asarray(eg), jnp.asarray(eb), jnp.asarray(bd)


def _layer_params(l, norm1_g, w_in, conv_a_w, conv_a_b, rg_wa, rg_ba, rg_wx, rg_bx, rg_lambda, pool_w, pool_scale,
                  dn_conv_w, dn_A_log, dn_dt_bias, dn_norm_g, mix_norm_g, w_out, norm2_g, router_w,
                  exp_w_gate, exp_w_up, exp_w_down):
    w = w_in[l]
    w_cat = jnp.concatenate([w[:, 0:1536], w[:, 1552:2576], w[:, 1536:1552],
                             jnp.zeros((D_MODEL, COL_END - COL_CAB - 16), F32)], axis=1).astype(BF16)
    pad8 = lambda v: jnp.concatenate([v.reshape(1, 2 * DN_HEADS), jnp.zeros((1, 120), F32)], axis=1)
    return dict(
        g1=norm1_g[l].reshape(1, D_MODEL), w_cat=w_cat,
        conv_a_w=conv_a_w[l], conv_a_b=conv_a_b[l].reshape(1, GROUP_W),
        wa=[_block_diag(rg_wa[l, d]).astype(BF16) for d in range(2)],
        wx=[_block_diag(rg_wx[l, d]).astype(BF16) for d in range(2)],
        ba=[rg_ba[l, d].reshape(1, GROUP_W) for d in range(2)],
        bx=[rg_bx[l, d].reshape(1, GROUP_W) for d in range(2)],
        lam=[rg_lambda[l, d].reshape(1, GROUP_W) for d in range(2)],
        pool_w=_block_diag(pool_w[l]).astype(BF16), pool_scale=pool_scale[l].reshape(1, GROUP_W),
        dn_conv_w=dn_conv_w[l], alog=pad8(dn_A_log[l]), dtb=pad8(dn_dt_bias[l]),
        dng=jnp.tile(dn_norm_g[l], DN_HEADS).reshape(1, GROUP_W),
        mg=mix_norm_g[l].reshape(1, D_MODEL), wo=w_out[l].astype(BF16),
        g2=norm2_g[l].reshape(1, D_MODEL),
        rw=jnp.concatenate(_split(router_w[l].T, 2), axis=0),
        wg=exp_w_gate, wu=exp_w_up, wd=exp_w_down,
    )


def _chunk_rows(gcn, d):
    B, S, _ = gcn.shape
    n = S // DN_CHUNK
    g = gcn[:, :, d * DN_HEADS:(d + 1) * DN_HEADS].reshape(B, n, DN_CHUNK, DN_HEADS)
    return g.transpose(0, 1, 3, 2).reshape(B, n, 1, GROUP_W)


def _capacity(T):
    return max(1, EC_CAPACITY * T // N_EXPERTS)


def _mix_and_route(xt, B, S, p, consts, ropes):
    T = B * S
    cap = _capacity(T)
    eg, eb, bdmask = consts
    pa, pb, pq, pg, pd, pcab = _inproj(xt, p["g1"], p["w_cat"], *ropes, S)
    pa3 = pa.reshape(B, S, 2 * GROUP_W)
    rg = lambda d: (p["conv_a_w"], p["conv_a_b"], p["wa"][d], p["ba"][d], p["wx"][d], p["bx"][d], p["lam"][d])
    hf = _rglru(pa3, None, *rg(0), reverse=False)
    ya = _rglru(pa3, hf, *rg(1), reverse=True)
    yb = _pool(pb.reshape(B, S, GROUP_W), p["pool_w"], p["pool_scale"])
    qn, kn, vv, gcf, gcb, bef, beb, gcn = _dnprep(pq.reshape(B, S, 3 * GROUP_W), pcab.reshape(B, S, 128),
                                                  p["dn_conv_w"], p["alog"], p["dtb"], eg, eb)
    of, ob = _delta(qn, kn, vv, gcf, gcb, bef, beb, _chunk_rows(gcn, 0), _chunk_rows(gcn, 1), bdmask)
    yd = _attention(pd.reshape(B, S, 3 * GROUP_W))
    flat = lambda a: a.reshape(T, GROUP_W)
    x2, h2, aff_t = _outproj(flat(ya), flat(yb), flat(of), flat(ob), pg, flat(yd), xt,
                             p["dng"], p["mg"], p["wo"], p["g2"], p["rw"])
    mask, pos, wsel, offs3 = _route(aff_t, cap)
    offs = jnp.concatenate([offs3[:, :, 0].reshape(-1), jnp.full((N_EXPERTS,), cap, I32)])
    xe = _gather(h2, mask.reshape(N_EXPERTS, 1, T), pos.reshape(N_EXPERTS, 1, T), offs, cap)
    return dict(x2=x2, wsel=wsel, pos=pos, offs=offs, xe=xe, cap=cap)


def kernel(x_prompt, x_sample, norm1_g, w_in, conv_a_w, conv_a_b, rg_wa, rg_ba, rg_wx, rg_bx, rg_lambda, pool_w, pool_scale, dn_conv_w, dn_A_log, dn_dt_bias, dn_norm_g, mix_norm_g, w_out, norm2_g, router_w, exp_w_gate, exp_w_up, exp_w_down, final_norm_g):
    layers = [_layer_params(l, norm1_g, w_in, conv_a_w, conv_a_b, rg_wa, rg_ba, rg_wx, rg_bx, rg_lambda, pool_w,
                            pool_scale, dn_conv_w, dn_A_log, dn_dt_bias, dn_norm_g, mix_norm_g, w_out, norm2_g,
                            router_w, exp_w_gate, exp_w_up, exp_w_down) for l in range(DEPTH)]
    consts = _expanders()
    final_g = final_norm_g.reshape(1, D_MODEL)
    batches = (x_prompt, x_sample)
    shapes = [x.shape for x in batches]
    ropes = [_rope_tables(S) for _, S, _ in shapes]
    xts = [x.reshape(B * S, D) for x, (B, S, D) in zip(batches, shapes)]
    for l, p in enumerate(layers):
        st = [_mix_and_route(xt, B, S, p, consts, rp) for xt, (B, S, _), rp in zip(xts, shapes, ropes)]
        yes = _ffn([s["xe"] for s in st], p["wg"], p["wu"], p["wd"], l, [s["cap"] for s in st])
        xts = [_scatter(s["x2"], s["wsel"], s["pos"], ye, s["offs"], final_g, s["cap"], final=(l == DEPTH - 1))
               for s, ye in zip(st, yes)]
    return tuple(xt.reshape(shape) for xt, shape in zip(xts, shapes))
```

```python
import functools
import math
import numpy as np
import jax
import jax.numpy as jnp
from jax import lax
from jax.experimental import pallas as pl
from jax.experimental.pallas import tpu as pltpu

F32, BF16, I32 = jnp.float32, jnp.bfloat16, jnp.int32

D_MODEL = 1024
DEPTH = 2
GROUP_W = 256
RG_C = 8.0
POOL_WINDOWS = (2, 4, 8, 16)
DN_HEADS = 4
DN_DK = 64
DN_CHUNK = 64
ATT_HD = 64
ROT_DIM = 16
ROPE_THETA = 500000.0
ATT_WINDOWS = (128, 512, 2048)
ATT_DILATIONS = (1, 4, 16)
N_EXPERTS = 16
EC_CAPACITY = 2
EPS = 1e-6
NEG = -1e30

COL_A, COL_B, COL_CQ, COL_CG, COL_D, COL_CAB, COL_END = 0, 512, 768, 1536, 1792, 2560, 2688

ROW_TILE = 1024
MIX_TILE = 2048
HALO = 8
DELTA_CB = 16
ATT_QB = 128
ATT_UNROLL = 16
TOK_BLK = 256
GATHER_W = 64
GATHER_JB = 16
GATHER_E = 4
SCATTER_W = 128
VMEM_LIMIT = 56 * 1024 * 1024


def _cparams(sem):
    return pltpu.CompilerParams(dimension_semantics=sem, vmem_limit_bytes=VMEM_LIMIT)


def _dot(a, b):
    return jnp.dot(a.astype(BF16), b.astype(BF16), preferred_element_type=F32)


def _split(a, pieces):
    out, rem = [], a
    for i in range(pieces):
        t = rem.astype(BF16)
        out.append(t)
        if i + 1 < pieces:
            rem = rem - t.astype(F32)
    return out


def _dot_sel(a, sel, pieces=3):
    sel = sel.astype(BF16)
    return sum(jnp.dot(t, sel, preferred_element_type=F32) for t in _split(a, pieces))


def _dot_nt(a, b):
    return lax.dot_general(a.astype(BF16), b.astype(BF16), (((1,), (1,)), ((), ())), preferred_element_type=F32)


def _dot_tn(a, b):
    return lax.dot_general(a.astype(BF16), b.astype(BF16), (((0,), (0,)), ((), ())), preferred_element_type=F32)


def _sigmoid(x):
    return 1.0 / (1.0 + jnp.exp(-x))


def _softplus(x):
    return jnp.maximum(x, 0.0) + jnp.log1p(jnp.exp(-jnp.abs(x)))


def _shift_rows(e, k):
    n = e.shape[0]
    return e if k % n == 0 else pltpu.roll(e, (-k) % n, axis=0)


def _with_halo(cur_ref, prev_ref, next_ref, first, last):
    prev = jnp.where(first, 0.0, prev_ref[0])
    nxt = jnp.where(last, 0.0, next_ref[0])
    return jnp.concatenate([prev, cur_ref[0], nxt], axis=0)


def _halo_specs(ts, width, S, blk_of):
    per = ts // HALO
    last = S // HALO - 1
    cur = pl.BlockSpec((1, ts, width), lambda b, c: (b, blk_of(c), 0))
    prev = pl.BlockSpec((1, HALO, width), lambda b, c: (b, jnp.maximum(blk_of(c) * per - 1, 0), 0))
    nxt = pl.BlockSpec((1, HALO, width), lambda b, c: (b, jnp.minimum((blk_of(c) + 1) * per, last), 0))
    return cur, prev, nxt


def _inproj_kernel(x_ref, g_ref, w_ref, c_ref, s1_ref, s2_ref, pa_ref, pb_ref, pq_ref, pg_ref, pd_ref, pcab_ref):
    x = x_ref[...]
    h = x * lax.rsqrt(jnp.mean(x * x, axis=-1, keepdims=True) + EPS) * g_ref[...]
    hb = h.astype(BF16)

    def mm(lo, hi):
        return jnp.dot(hb, w_ref[:, lo:hi], preferred_element_type=F32)

    pa_ref[...] = mm(COL_A, COL_B)
    pb_ref[...] = mm(COL_B, COL_CQ)
    pq_ref[...] = mm(COL_CQ, COL_CG)
    pg_ref[...] = mm(COL_CG, COL_D)
    c, s1, s2 = c_ref[...], s1_ref[...], s2_ref[...]
    for part in range(2):
        yy = mm(COL_D + 256 * part, COL_D + 256 * part + 256)
        for half in range(2):
            y = yy[:, 128 * half:128 * half + 128]
            y = y * c + pltpu.roll(y, 128 - ROT_DIM // 2, axis=1) * s1 + pltpu.roll(y, ROT_DIM // 2, axis=1) * s2
            if part == 0:
                y = y * (ATT_HD ** -0.5)
            pd_ref[:, 256 * part + 128 * half:256 * part + 128 * half + 128] = y
    vcab = mm(COL_D + 512, COL_END)
    pd_ref[:, 512:768] = vcab[:, 0:256]
    pcab_ref[...] = vcab[:, 256:384]


def _inproj(xt, g1, w_cat, rope_c, rope_s1, rope_s2, S):
    T = xt.shape[0]
    tm = min(ROW_TILE, S)
    per_seq = S // tm
    widths = (512, 256, 768, 256, 768, 128)
    row = lambda w: pl.BlockSpec((tm, w), lambda i: (i, 0))
    rope = pl.BlockSpec((tm, 128), lambda i: (i % per_seq, 0))
    return pl.pallas_call(
        _inproj_kernel,
        out_shape=[jax.ShapeDtypeStruct((T, w), F32) for w in widths],
        grid=(T // tm,),
        in_specs=[row(D_MODEL), pl.BlockSpec((1, D_MODEL), lambda i: (0, 0)),
                  pl.BlockSpec((D_MODEL, COL_END), lambda i: (0, 0)), rope, rope, rope],
        out_specs=[row(w) for w in widths],
        compiler_params=_cparams(("parallel",)),
        name="inproj",
    )(xt, g1, w_cat, rope_c, rope_s1, rope_s2)


def _rglru_kernel(*refs, reverse, ts, nc):
    if reverse:
        (cur_ref, prev_ref, next_ref, gate_ref, hf_ref, cw_ref, cb_ref, wa_ref, ba_ref, wx_ref, bx_ref, lam_ref,
         out_ref, a_ref, b_ref, carry_ref) = refs
    else:
        (cur_ref, prev_ref, next_ref, cw_ref, cb_ref, wa_ref, ba_ref, wx_ref, bx_ref, lam_ref,
         out_ref, a_ref, b_ref, carry_ref) = refs
    c = pl.program_id(1)
    blk = (nc - 1 - c) if reverse else c
    e = _with_halo(cur_ref, prev_ref, next_ref, blk == 0, blk == nc - 1)
    cw = cw_ref[...]
    sl = slice(HALO, HALO + ts)
    u = (cw[0:1] * _shift_rows(e, -2)[sl] + cw[1:2] * _shift_rows(e, -1)[sl] + cw[2:3] * e[sl]
         + cw[3:4] * _shift_rows(e, 1)[sl]) + cb_ref[...]
    r = _sigmoid(_dot(u, wa_ref[...]) + ba_ref[...])
    i = _sigmoid(_dot(u, wx_ref[...]) + bx_ref[...])
    log_a = -RG_C * r * _softplus(-lam_ref[...])
    a_ref[...] = jnp.exp(log_a)
    b_ref[...] = jnp.sqrt(1.0 - jnp.exp(2.0 * log_a)) * (i * u)

    @pl.when(c == 0)
    def _():
        carry_ref[...] = jnp.zeros_like(carry_ref)

    row = lax.broadcasted_iota(I32, (HALO, GROUP_W), 0)
    nt = ts // HALO

    def body(it, carry):
        ti = (nt - 1 - it) if reverse else it
        st = pl.multiple_of(ti * HALO, HALO)
        a = a_ref[pl.ds(st, HALO), :]
        b = b_ref[pl.ds(st, HALO), :]
        for d in (1, 2, 4):
            k = d if reverse else -d
            valid = (row < HALO - d) if reverse else (row >= d)
            b = jnp.where(valid, a * _shift_rows(b, k) + b, b)
            a = jnp.where(valid, a * _shift_rows(a, k), a)
        h = a * carry + b
        if reverse:
            g = gate_ref[0, pl.ds(st, HALO), :]
            cdf = 0.5 * (1.0 + jnp.tanh(np.float32(np.sqrt(2.0 / np.pi)) * (g + 0.044715 * (g * g * g))))
            out_ref[0, pl.ds(st, HALO), :] = (g * cdf) * (hf_ref[0, pl.ds(st, HALO), :] + h)
            return jnp.broadcast_to(h[0:1], h.shape)
        out_ref[0, pl.ds(st, HALO), :] = h
        return jnp.broadcast_to(h[HALO - 1:HALO], h.shape)

    carry_ref[...] = lax.fori_loop(0, nt, body, carry_ref[...], unroll=4)


def _rglru(pa3, hf, cw, cb, wa, ba, wx, bx, lam, reverse):
    B, S, _ = pa3.shape
    ts = min(MIX_TILE, S)
    nc = S // ts
    blk_of = (lambda c: nc - 1 - c) if reverse else (lambda c: c)
    cur, prev, nxt = _halo_specs(ts, GROUP_W, S, blk_of)
    tile = pl.BlockSpec((1, ts, GROUP_W), lambda b, c: (b, blk_of(c), 0))
    const = lambda shape: pl.BlockSpec(shape, lambda b, c: (0,) * len(shape))
    in_specs = [cur, prev, nxt]
    args = [pa3, pa3, pa3]
    if reverse:
        in_specs += [pl.BlockSpec((1, ts, GROUP_W), lambda b, c: (b, blk_of(c), 1)), tile]
        args += [pa3, hf]
    in_specs += [const((4, GROUP_W)), const((1, GROUP_W)), const((GROUP_W, GROUP_W)), const((1, GROUP_W)),
                 const((GROUP_W, GROUP_W)), const((1, GROUP_W)), const((1, GROUP_W))]
    args += [cw, cb, wa, ba, wx, bx, lam]
    return pl.pallas_call(
        functools.partial(_rglru_kernel, reverse=reverse, ts=ts, nc=nc),
        out_shape=jax.ShapeDtypeStruct((B, S, GROUP_W), F32),
        grid=(B, nc),
        in_specs=in_specs,
        out_specs=tile,
        scratch_shapes=[pltpu.VMEM((ts, GROUP_W), F32), pltpu.VMEM((ts, GROUP_W), F32),
                        pltpu.VMEM((HALO, GROUP_W), F32)],
        compiler_params=_cparams(("parallel", "arbitrary")),
        name="rglru_bwd" if reverse else "rglru_fwd",
    )(*args)


def _pool_kernel(cur_ref, prev_ref, next_ref, w_ref, sc_ref, out_ref, *, ts, nc, S):
    c = pl.program_id(1)
    e = _with_halo(cur_ref, prev_ref, next_ref, c == 0, c == nc - 1)
    sl = slice(HALO, HALO + ts)
    a2 = e + _shift_rows(e, 1)
    a4 = a2 + _shift_rows(a2, 2)
    a8 = a4 + _shift_rows(a4, 4)
    a16 = a8 + _shift_rows(a8, 8)
    sums = [_shift_rows(a, -(w // 2))[sl] for a, w in zip((a2, a4, a8, a16), POOL_WINDOWS)]
    gi = lax.broadcasted_iota(I32, (ts, GROUP_W), 1) // (GROUP_W // len(POOL_WINDOWS))
    ssum = jnp.where(gi == 0, sums[0], jnp.where(gi == 1, sums[1], jnp.where(gi == 2, sums[2], sums[3])))
    hw = jnp.where(gi == 0, 1, jnp.where(gi == 1, 2, jnp.where(gi == 2, 4, 8)))
    t = c * ts + lax.broadcasted_iota(I32, (ts, GROUP_W), 0)
    cnt = (jnp.minimum(t + hw, S) - jnp.maximum(t - hw, 0)).astype(F32)
    p = ssum / cnt - e[sl]
    out_ref[0] = _dot(p, w_ref[...]) * sc_ref[...]


def _pool(pb3, w_bd, scale):
    B, S, _ = pb3.shape
    ts = min(MIX_TILE, S)
    nc = S // ts
    cur, prev, nxt = _halo_specs(ts, GROUP_W, S, lambda c: c)
    return pl.pallas_call(
        functools.partial(_pool_kernel, ts=ts, nc=nc, S=S),
        out_shape=jax.ShapeDtypeStruct((B, S, GROUP_W), F32),
        grid=(B, nc),
        in_specs=[cur, prev, nxt, pl.BlockSpec((GROUP_W, GROUP_W), lambda b, c: (0, 0)),
                  pl.BlockSpec((1, GROUP_W), lambda b, c: (0, 0))],
        out_specs=pl.BlockSpec((1, ts, GROUP_W), lambda b, c: (b, c, 0)),
        compiler_params=_cparams(("parallel", "parallel")),
        name="pool",
    )(pb3, pb3, pb3, w_bd, scale)


def _head_sum_matrix(n, group):
    r = lax.broadcasted_iota(I32, (n, n), 0) // group
    c = lax.broadcasted_iota(I32, (n, n), 1) // group
    return (r == c).astype(F32)


def _dnprep_kernel(cur_ref, prev_ref, next_ref, cab_ref, cw_ref, alog_ref, dtb_ref, eg_ref, eb_ref,
                   q_ref, k_ref, v_ref, gcf_ref, gcb_ref, bef_ref, beb_ref, gcn_ref, *, ts, nc):
    c = pl.program_id(1)
    e = _with_halo(cur_ref, prev_ref, next_ref, c == 0, c == nc - 1)
    cw = cw_ref[...]
    sl = slice(HALO, HALO + ts)
    y = (cw[0:1] * _shift_rows(e, -2)[sl] + cw[1:2] * _shift_rows(e, -1)[sl] + cw[2:3] * e[sl]
         + cw[3:4] * _shift_rows(e, 1)[sl])
    y = y * _sigmoid(y)
    q, k = y[:, 0:GROUP_W], y[:, GROUP_W:2 * GROUP_W]
    hs = _head_sum_matrix(GROUP_W, DN_DK)
    q_ref[0] = q * lax.rsqrt(_dot_sel(q * q, hs, 2) + EPS) * (DN_DK ** -0.5)
    k_ref[0] = k * lax.rsqrt(_dot_sel(k * k, hs, 2) + EPS)
    v_ref[0] = y[:, 2 * GROUP_W:3 * GROUP_W]
    cab = cab_ref[0]
    g = -jnp.exp(alog_ref[...]) * _softplus(cab + dtb_ref[...])
    beta = _sigmoid(cab)
    bef_ref[0] = _dot_sel(beta, eb_ref[0], 2)
    beb_ref[0] = _dot_sel(beta, eb_ref[1], 2)
    pos = lax.broadcasted_iota(I32, (ts, 128), 0) % DN_CHUNK
    fwd = rev = g
    d = 1
    while d < DN_CHUNK:
        fwd = fwd + jnp.where(pos >= d, _shift_rows(fwd, -d), 0.0)
        rev = rev + jnp.where(pos < DN_CHUNK - d, _shift_rows(rev, d), 0.0)
        d *= 2
    gcn = jnp.where(lax.broadcasted_iota(I32, (ts, 128), 1) < DN_HEADS, fwd, rev)
    gcn_ref[0] = gcn
    gcf_ref[0] = _dot_sel(gcn, eg_ref[0])
    gcb_ref[0] = _dot_sel(gcn, eg_ref[1])


def _dnprep(pq3, pcab3, cw, alog_row, dtb_row, eg, eb):
    B, S, _ = pq3.shape
    ts = min(ROW_TILE, S)
    nc = S // ts
    cur, prev, nxt = _halo_specs(ts, 3 * GROUP_W, S, lambda c: c)
    tile = pl.BlockSpec((1, ts, GROUP_W), lambda b, c: (b, c, 0))
    const = lambda shape: pl.BlockSpec(shape, lambda b, c: (0,) * len(shape))
    return pl.pallas_call(
        functools.partial(_dnprep_kernel, ts=ts, nc=nc),
        out_shape=[jax.ShapeDtypeStruct((B, S, GROUP_W), F32)] * 7 + [jax.ShapeDtypeStruct((B, S, 128), F32)],
        grid=(B, nc),
        in_specs=[cur, prev, nxt, pl.BlockSpec((1, ts, 128), lambda b, c: (b, c, 0)),
                  const((4, 3 * GROUP_W)), const((1, 128)), const((1, 128)),
                  const((2, 128, GROUP_W)), const((2, 128, GROUP_W))],
        out_specs=[tile] * 7 + [pl.BlockSpec((1, ts, 128), lambda b, c: (b, c, 0))],
        compiler_params=_cparams(("parallel", "parallel")),
        name="dnprep",
    )(pq3, pq3, pq3, pcab3, cw, alog_row, dtb_row, eg, eb)


def _delta_kernel(qf, kf, vf, gcf, bef, grf, qb, kb, vb, gcb, beb, grb, bd_ref, of_ref, ob_ref,
                  s_ref, p_scr, x_scr, wq_scr, u_scr, at_scr, kd_scr, vb_scr, kg_scr, *, cb):
    j = pl.program_id(1)

    @pl.when(j == 0)
    def _():
        s_ref[...] = jnp.zeros_like(s_ref)

    C = DN_CHUNK
    bd = bd_ref[...] > 0.0
    c_idx = lax.broadcasted_iota(I32, (C, GROUP_W), 0)
    m_idx = lax.broadcasted_iota(I32, (C, GROUP_W), 1) % C
    eye = (c_idx == m_idx).astype(F32)

    def blockdiag(x):
        return jnp.where(bd, jnp.tile(x.astype(BF16), (DN_HEADS, 1)), jnp.zeros((), BF16))

    dirs = ((qf, kf, vf, gcf, bef, grf, of_ref), (qb, kb, vb, gcb, beb, grb, ob_ref))
    units = [(d, ci) for ci in range(cb) for d in range(2)]

    def chunk_of(d, ci):
        return cb - 1 - ci if d == 1 else ci

    def prepare(ui):
        d, ci = units[ui]
        q_r, k_r, v_r, gc_r, be_r, gr_r, _ = dirs[d]
        rev = d == 1
        tril = (c_idx <= m_idx) if rev else (c_idx >= m_idx)
        strict = (c_idx < m_idx) if rev else (c_idx > m_idx)
        last = 0 if rev else C - 1
        cc = chunk_of(d, ci)
        rows = slice(cc * C, (cc + 1) * C)
        q, k, v = q_r[0, rows, :], k_r[0, rows, :], v_r[0, rows, :]
        gc, be, gr = gc_r[0, rows, :], be_r[0, rows, :], gr_r[0, cc]
        eg = jnp.exp(gc)
        kbeta = k * be
        kkqk = _dot_nt(jnp.concatenate([kbeta, q], axis=0), blockdiag(k))
        decay = jnp.exp(jnp.where(tril, gc - gr, -jnp.inf))
        at_scr[ui] = (kkqk[C:] * decay).astype(BF16)
        p = -jnp.where(strict, kkqk[:C] * decay, 0.0)
        p_scr[ui] = p.astype(BF16)
        x_scr[ui] = eye + p
        wq_scr[ui, C:2 * C, :] = (q * eg).astype(BF16)
        kd_scr[ui] = (k * jnp.exp(gc[last:last + 1] - gc)).astype(BF16)
        vb_scr[ui] = (v * be).astype(BF16)
        kg_scr[ui] = (kbeta * eg).astype(BF16)

    def solve(group):
        for ui in group:
            p = p_scr[ui]
            p_scr[ui] = _dot(p, blockdiag(p)).astype(BF16)
        for _ in range(4):
            for ui in group:
                p, x = p_scr[ui], x_scr[ui]
                px = _dot(jnp.concatenate([p, x.astype(BF16)], axis=0), blockdiag(p))
                p_scr[ui] = px[:C].astype(BF16)
                x_scr[ui] = x + px[C:]
        for ui in group:
            x = x_scr[ui]
            x = x + _dot(x, blockdiag(p_scr[ui]))
            u_scr[ui] = _dot(x, blockdiag(vb_scr[ui]))
            wq_scr[ui, 0:C, :] = _dot(x, blockdiag(kg_scr[ui])).astype(BF16)

    def recur(ui):
        d, ci = units[ui]
        gc_r, o_r = dirs[d][3], dirs[d][6]
        cc = chunk_of(d, ci)
        last = cc * C + (0 if d == 1 else C - 1)
        state = s_ref[d]
        ws_qs = _dot(wq_scr[ui], state)
        v_new = u_scr[ui] - ws_qs[:C]
        o_r[0, cc * C:(cc + 1) * C, :] = ws_qs[C:] + _dot(at_scr[ui], blockdiag(v_new))
        s_ref[d] = (state * jnp.exp(gc_r[0, last:last + 1, :])
                    + jnp.where(bd, _dot_tn(kd_scr[ui], v_new), 0.0))

    for ui in range(len(units)):
        prepare(ui)
    solve(range(len(units)))
    for ui in range(len(units)):
        recur(ui)


def _delta(qn, kn, vv, gcf, gcb, bef, beb, grf, grb, bdmask):
    B, S, _ = qn.shape
    cb = DELTA_CB
    rb = cb * DN_CHUNK
    nb = S // rb
    f = lambda b, j: (b, j, 0)
    r = lambda b, j: (b, nb - 1 - j, 0)
    tf = pl.BlockSpec((1, rb, GROUP_W), f)
    tr = pl.BlockSpec((1, rb, GROUP_W), r)
    gf = pl.BlockSpec((1, cb, 1, GROUP_W), lambda b, j: (b, j, 0, 0))
    gr = pl.BlockSpec((1, cb, 1, GROUP_W), lambda b, j: (b, nb - 1 - j, 0, 0))
    return pl.pallas_call(
        functools.partial(_delta_kernel, cb=cb),
        out_shape=[jax.ShapeDtypeStruct((B, S, GROUP_W), F32)] * 2,
        grid=(B, nb),
        in_specs=[tf, tf, tf, tf, tf, gf, tr, tr, tr, tr, tr, gr,
                  pl.BlockSpec((GROUP_W, GROUP_W), lambda b, j: (0, 0))],
        out_specs=[tf, tr],
        scratch_shapes=[pltpu.VMEM((2, GROUP_W, GROUP_W), F32)]
                       + [pltpu.VMEM((2 * cb, n * DN_CHUNK, GROUP_W), dt) for n, dt in
                          ((1, BF16), (1, F32), (2, BF16), (1, F32), (1, BF16), (1, BF16), (1, BF16), (1, BF16))],
        compiler_params=_cparams(("parallel", "arbitrary")),
        name="delta",
    )(qn, kn, vv, gcf, bef, grf, qn, kn, vv, gcb, beb, grb, bdmask)


def _attn_kernel(q_ref, k_ref, v_ref, out_ref, m_ref, l_ref, o_ref, *, S):
    head0 = lax.broadcasted_iota(I32, (1, 128), 1) < ATT_HD

    def rows(start, n, dil):
        return pl.ds(start, n) if dil == 1 else pl.ds(start, n, stride=dil)

    for bi, (win, dil) in enumerate(zip(ATT_WINDOWS, ATT_DILATIONS)):
        half = win // (2 * dil)
        L = S // dil
        qb_n = min(ATT_QB, L)
        kw = min(L, qb_n + 2 * half)
        nqb = L // qb_n

        def block_stats(idx, dil=dil, half=half, L=L, qb_n=qb_n, kw=kw, nqb=nqb):
            r = idx // nqb
            m0 = (idx % nqb) * qb_n
            ks = jnp.clip(m0 - half, 0, L - kw)
            qsel = rows(r + m0 * dil, qb_n, dil)
            ksel = rows(r + ks * dil, kw, dil)
            q = q_ref[0, qsel, :]
            kk = k_ref[0, ksel, :].astype(BF16)
            vv = v_ref[0, ksel, :].astype(BF16)
            rel = (lax.broadcasted_iota(I32, (qb_n, kw), 1) - lax.broadcasted_iota(I32, (qb_n, kw), 0)
                   + (ks - m0 + half))
            valid = rel.astype(jnp.uint32) <= 2 * half
            q2 = jnp.concatenate([jnp.where(head0, q, 0.0), jnp.where(head0, 0.0, q)], axis=0)
            s = jnp.where(jnp.concatenate([valid, valid], axis=0), _dot_nt(q2, kk), NEG)
            m = jnp.max(s, axis=-1, keepdims=True)
            p = jnp.exp(s - m)
            l = jnp.sum(p, axis=-1, keepdims=True)
            o = _dot(p, vv)
            return (qsel, jnp.where(head0, m[:qb_n], m[qb_n:]), jnp.where(head0, l[:qb_n], l[qb_n:]),
                    jnp.where(head0, o[:qb_n], o[qb_n:]))

        unroll = math.gcd(ATT_UNROLL, dil * nqb)

        def body(it, carry, bi=bi, unroll=unroll):
            stats = [block_stats(it * unroll + s) for s in range(unroll)]
            if bi > 0:
                olds = [(m_ref[qsel, :], l_ref[qsel, :], o_ref[qsel, :]) for qsel, _, _, _ in stats]
                merged = []
                for (qsel, m, l, o), (m_old, l_old, o_old) in zip(stats, olds):
                    m_new = jnp.maximum(m_old, m)
                    w_old, w_cur = jnp.exp(m_old - m_new), jnp.exp(m - m_new)
                    merged.append((qsel, m_new, w_old * l_old + w_cur * l, w_old * o_old + w_cur * o))
                stats = merged
            for qsel, m, l, o in stats:
                if bi == len(ATT_WINDOWS) - 1:
                    out_ref[0, qsel, :] = o / l
                else:
                    m_ref[qsel, :] = m
                    l_ref[qsel, :] = l
                    o_ref[qsel, :] = o
            return carry

        lax.fori_loop(0, dil * nqb // unroll, body, 0)


def _attention(pd3):
    B, S, _ = pd3.shape
    spec = lambda off: pl.BlockSpec((1, S, 128), lambda b, p: (b, 0, off + p))
    return pl.pallas_call(
        functools.partial(_attn_kernel, S=S),
        out_shape=jax.ShapeDtypeStruct((B, S, GROUP_W), F32),
        grid=(B, 2),
        in_specs=[spec(0), spec(2), spec(4)],
        out_specs=pl.BlockSpec((1, S, 128), lambda b, p: (b, 0, p)),
        scratch_shapes=[pltpu.VMEM((S, 128), F32)] * 3,
        compiler_params=_cparams(("parallel", "parallel")),
        name="attention",
    )(pd3, pd3, pd3)


def _outproj_kernel(ya_ref, yb_ref, of_ref, ob_ref, cg_ref, yd_ref, x_ref, dng_ref, mg_ref, wo_ref, g2_ref, rw_ref,
                    x2_ref, h2_ref, aff_ref):
    o = of_ref[...] + ob_ref[...]
    ms = _dot_sel(o * o, _head_sum_matrix(GROUP_W, DN_DK), 2) * (1.0 / DN_DK)
    cg = cg_ref[...]
    yc = (o * lax.rsqrt(ms + EPS) * dng_ref[...]) * (cg * _sigmoid(cg))
    acc = x_ref[...]
    for gi, y in enumerate((ya_ref[...], yb_ref[...], yc, yd_ref[...])):
        sl = slice(gi * GROUP_W, (gi + 1) * GROUP_W)
        mix = y * lax.rsqrt(jnp.mean(y * y, axis=-1, keepdims=True) + EPS) * mg_ref[:, sl]
        acc = acc + _dot(mix, wo_ref[sl, :])
    x2_ref[...] = acc
    h2 = acc * lax.rsqrt(jnp.mean(acc * acc, axis=-1, keepdims=True) + EPS) * g2_ref[...]
    h2_ref[...] = h2.astype(BF16)
    h_hi, h_lo = _split(h2, 2)
    z = _dot_nt(rw_ref[...], h_hi) + _dot_nt(rw_ref[...], h_lo)
    logits = z[0:N_EXPERTS] + z[N_EXPERTS:2 * N_EXPERTS]
    ex = jnp.exp(logits - jnp.max(logits, axis=0, keepdims=True))
    aff_ref[...] = ex / jnp.sum(ex, axis=0, keepdims=True)


def _outproj(ya, yb, of, ob, cg, yd, xt, dng, mg, wo, g2, rw):
    T = xt.shape[0]
    tm = ROW_TILE
    row = lambda w: pl.BlockSpec((tm, w), lambda i: (i, 0))
    const = lambda shape: pl.BlockSpec(shape, lambda i: (0, 0))
    return pl.pallas_call(
        _outproj_kernel,
        out_shape=[jax.ShapeDtypeStruct((T, D_MODEL), F32), jax.ShapeDtypeStruct((T, D_MODEL), BF16),
                   jax.ShapeDtypeStruct((N_EXPERTS, T), F32)],
        grid=(T // tm,),
        in_specs=[row(GROUP_W)] * 6 + [row(D_MODEL), const((1, GROUP_W)), const((1, D_MODEL)),
                                       const((D_MODEL, D_MODEL)), const((1, D_MODEL)),
                                       const((2 * N_EXPERTS, D_MODEL))],
        out_specs=[row(D_MODEL), row(D_MODEL), pl.BlockSpec((N_EXPERTS, tm), lambda i: (0, i))],
        compiler_params=_cparams(("parallel",)),
        name="outproj",
    )(ya, yb, of, ob, cg, yd, xt, dng, mg, wo, g2, rw)


def _strict_upper(n):
    return (lax.broadcasted_iota(I32, (n, n), 0) < lax.broadcasted_iota(I32, (n, n), 1)).astype(BF16)


def _route_kernel(aff_ref, mask_ref, pos_ref, wsel_ref, offs_ref, *, cap, nblk):
    keys = pltpu.bitcast(aff_ref[...], I32)

    def bit_body(i, thr):
        cand = thr | lax.shift_left(jnp.int32(1), 30 - i)
        cnt = jnp.sum((keys >= cand).astype(F32), axis=1, keepdims=True)
        return jnp.where(cnt >= cap, cand, thr)

    thr = lax.fori_loop(0, 31, bit_body, jnp.zeros((N_EXPERTS, 1), I32))
    need = cap - jnp.sum((keys > thr).astype(F32), axis=1, keepdims=True)
    su = _strict_upper(TOK_BLK)

    def blk_body(j, carry):
        ceq, csel = carry
        st = pl.multiple_of(j * TOK_BLK, TOK_BLK)
        kb = pltpu.bitcast(aff_ref[:, pl.ds(st, TOK_BLK)], I32)
        eqf = (kb == thr).astype(F32)
        rank = ceq + _dot(eqf, su)
        sel = ((kb > thr) | ((kb == thr) & (rank < need))).astype(F32)
        mask_ref[:, pl.ds(st, TOK_BLK)] = sel
        wsel_ref[:, pl.ds(st, TOK_BLK)] = sel * aff_ref[:, pl.ds(st, TOK_BLK)]
        pos_ref[:, pl.ds(st, TOK_BLK)] = _dot(sel, su)
        offs_ref[j] = jnp.broadcast_to(csel.astype(I32), (N_EXPERTS, 128))
        return (ceq + jnp.sum(eqf, axis=1, keepdims=True), csel + jnp.sum(sel, axis=1, keepdims=True))

    zero = jnp.zeros((N_EXPERTS, 1), F32)
    lax.fori_loop(0, nblk, blk_body, (zero, zero))


def _route(aff_t, cap):
    E, T = aff_t.shape
    nblk = T // TOK_BLK
    full = lambda shape: pl.BlockSpec(shape, lambda i: (0,) * len(shape))
    return pl.pallas_call(
        functools.partial(_route_kernel, cap=cap, nblk=nblk),
        out_shape=[jax.ShapeDtypeStruct((E, T), F32)] * 3 + [jax.ShapeDtypeStruct((nblk, E, 128), I32)],
        grid=(1,),
        in_specs=[full((E, T))],
        out_specs=[full((E, T))] * 3 + [full((nblk, E, 128))],
        compiler_params=_cparams(("arbitrary",)),
        name="route",
    )(aff_t)


def _gather_kernel(offs_ref, x_ref, m_ref, p_ref, out_ref):
    g, j = pl.program_id(0), pl.program_id(1)

    @pl.when(j == 0)
    def _():
        out_ref[...] = jnp.zeros_like(out_ref)

    W = GATHER_W
    slot0 = lax.broadcasted_iota(I32, (W, TOK_BLK), 0).astype(F32)
    for jb in range(GATHER_JB):
        blk = j * GATHER_JB + jb
        cols = slice(jb * TOK_BLK, (jb + 1) * TOK_BLK)
        targets, bases, passes = [], [], []
        for k in range(GATHER_E):
            off = offs_ref[blk * N_EXPERTS + g * GATHER_E + k]
            cnt = offs_ref[(blk + 1) * N_EXPERTS + g * GATHER_E + k] - off
            base = (off // 16) * 16
            targets.append(jnp.where(m_ref[k, :, cols] > 0.0, p_ref[k, :, cols] + (off - base).astype(F32), -1.0))
            bases.append(base)
            passes.append((off - base + cnt + W - 1) // W)
        onehot = jnp.concatenate([jnp.where(slot0 == t, 1.0, 0.0) for t in targets], axis=0).astype(BF16)
        picked = jnp.dot(onehot, x_ref[cols, :], preferred_element_type=F32).astype(BF16)
        for k in range(GATHER_E):
            out_ref[k, pl.ds(pl.multiple_of(bases[k], 16), W), :] += picked[k * W:(k + 1) * W]
        for k in range(GATHER_E):
            def more(ps, carry, k=k, cols=cols):
                hot = jnp.where(slot0 + (ps * W).astype(F32) == targets[k], 1.0, 0.0).astype(BF16)
                extra = jnp.dot(hot, x_ref[cols, :], preferred_element_type=F32).astype(BF16)
                out_ref[k, pl.ds(pl.multiple_of(bases[k] + ps * W, 16), W), :] += extra
                return carry

            lax.fori_loop(1, passes[k], more, 0)


def _gather(h2, mask3, pos3, offs, cap):
    T = h2.shape[0]
    nblk = T // TOK_BLK
    cap_x = cap + 2 * GATHER_W
    rows = GATHER_JB * TOK_BLK
    grid_spec = pltpu.PrefetchScalarGridSpec(
        num_scalar_prefetch=1,
        grid=(N_EXPERTS // GATHER_E, nblk // GATHER_JB),
        in_specs=[pl.BlockSpec((rows, D_MODEL), lambda g, j, o: (j, 0)),
                  pl.BlockSpec((GATHER_E, 1, rows), lambda g, j, o: (g, 0, j)),
                  pl.BlockSpec((GATHER_E, 1, rows), lambda g, j, o: (g, 0, j))],
        out_specs=pl.BlockSpec((GATHER_E, cap_x, D_MODEL), lambda g, j, o: (g, 0, 0),
                               pipeline_mode=pl.Buffered(1)),
    )
    return pl.pallas_call(
        _gather_kernel,
        out_shape=jax.ShapeDtypeStruct((N_EXPERTS, cap_x, D_MODEL), BF16),
        grid_spec=grid_spec,
        compiler_params=_cparams(("parallel", "arbitrary")),
        name="gather",
    )(offs, h2, mask3, pos3)


def _ffn_kernel(*refs, bounds):
    n = len(bounds) - 1
    xs, w_refs, outs, w_scr = refs[:n], refs[n:n + 3], refs[n + 3:2 * n + 3], refs[2 * n + 3]
    i = pl.program_id(1)

    @pl.when(i == 0)
    def _():
        for k, w_ref in enumerate(w_refs):
            w_scr[k] = w_ref[0, 0].astype(BF16)

    for b in range(n):
        @pl.when(jnp.logical_and(i >= bounds[b], i < bounds[b + 1]))
        def _(b=b):
            x = xs[b][0]
            g = jnp.dot(x, w_scr[0], preferred_element_type=F32)
            u = jnp.dot(x, w_scr[1], preferred_element_type=F32)
            h = (g * _sigmoid(g)) * u
            outs[b][0] = _dot(h, w_scr[2]).astype(BF16)


def _ffn(xes, wg, wu, wd, layer, caps):
    tf = min(512, *caps)
    bounds = [0]
    for cap in caps:
        bounds.append(bounds[-1] + cap // tf)

    def tile_spec(b):
        return pl.BlockSpec((1, tf, D_MODEL),
                            lambda e, i: (e, jnp.clip(i - bounds[b], 0, caps[b] // tf - 1), 0))

    wspec = pl.BlockSpec((1, 1, D_MODEL, D_MODEL), lambda e, i: (layer, e, 0, 0))
    return pl.pallas_call(
        functools.partial(_ffn_kernel, bounds=tuple(bounds)),
        out_shape=[jax.ShapeDtypeStruct((N_EXPERTS, cap, D_MODEL), BF16) for cap in caps],
        grid=(N_EXPERTS, bounds[-1]),
        in_specs=[tile_spec(b) for b in range(len(caps))] + [wspec] * 3,
        out_specs=[tile_spec(b) for b in range(len(caps))],
        scratch_shapes=[pltpu.VMEM((3, D_MODEL, D_MODEL), BF16)],
        compiler_params=_cparams(("parallel", "arbitrary")),
        name="ffn",
    )(*xes, wg, wu, wd)


def _slot_window_start(off, cap):
    return jnp.minimum((off // TOK_BLK) * TOK_BLK, cap - 2 * TOK_BLK)


def _scatter_kernel(offs_ref, x2_ref, w_ref, p_ref, gf_ref, *rest, cap, final):
    ye, out_ref = rest[:N_EXPERTS], rest[N_EXPERTS]
    j = pl.program_id(0)
    ps, ws = p_ref[...].T, w_ref[...].T
    offs = [offs_ref[j * N_EXPERTS + e] for e in range(N_EXPERTS)]
    cnts = [offs_ref[(j + 1) * N_EXPERTS + e] - offs[e] for e in range(N_EXPERTS)]
    rels = [offs[e] - _slot_window_start(offs[e], cap) for e in range(N_EXPERTS)]
    weights = [ws[:, e:e + 1] for e in range(N_EXPERTS)]
    SUB = 16
    most = cnts[0]
    for e in range(1, N_EXPERTS):
        most = jnp.maximum(most, cnts[e])

    def onehot(e, start, width):
        slot = lax.broadcasted_iota(I32, (TOK_BLK, width), 1).astype(F32)
        return jnp.where(slot == ps[:, e:e + 1] + (rels[e] - start).astype(F32), weights[e], 0.0).astype(BF16)

    W = SCATTER_W

    @pl.when(most <= W - SUB)
    def _():
        acc = x2_ref[...]
        for g0 in range(0, N_EXPERTS, 2):
            group = (g0, g0 + 1)
            r0 = {e: pl.multiple_of(jnp.minimum((rels[e] // SUB) * SUB, 2 * TOK_BLK - W), SUB) for e in group}
            acc = acc + jnp.dot(jnp.concatenate([onehot(e, r0[e], W) for e in group], axis=1),
                                jnp.concatenate([ye[e][0, pl.ds(r0[e], W), :] for e in group], axis=0),
                                preferred_element_type=F32)
        out_ref[...] = acc

    @pl.when(most > W - SUB)
    def _():
        out_ref[...] = x2_ref[...]
        for e in range(N_EXPERTS):
            @pl.when(cnts[e] > 0)
            def _(e=e):
                out_ref[...] += jnp.dot(onehot(e, 0, 2 * TOK_BLK), ye[e][0], preferred_element_type=F32)

    if final:
        x = out_ref[...]
        out_ref[...] = x * lax.rsqrt(jnp.mean(x * x, axis=-1, keepdims=True) + EPS) * gf_ref[...]


def _scatter(x2, wsel, pos, ye, offs, gf, cap, final):
    T = x2.shape[0]
    nblk = T // TOK_BLK
    assert cap >= 2 * TOK_BLK and cap % TOK_BLK == 0

    def ye_spec(e):
        return pl.BlockSpec((pl.Element(1), pl.Element(2 * TOK_BLK), pl.Element(D_MODEL)),
                            lambda j, o: (e, pl.multiple_of(_slot_window_start(o[j * N_EXPERTS + e], cap), TOK_BLK),
                                          0))

    row = lambda w: pl.BlockSpec((TOK_BLK, w), lambda j, o: (j, 0))
    col = pl.BlockSpec((N_EXPERTS, TOK_BLK), lambda j, o: (0, j))
    grid_spec = pltpu.PrefetchScalarGridSpec(
        num_scalar_prefetch=1,
        grid=(nblk,),
        in_specs=[row(D_MODEL), col, col, pl.BlockSpec((1, D_MODEL), lambda j, o: (0, 0))]
                 + [ye_spec(e) for e in range(N_EXPERTS)],
        out_specs=row(D_MODEL),
    )
    return pl.pallas_call(
        functools.partial(_scatter_kernel, cap=cap, final=final),
        out_shape=jax.ShapeDtypeStruct((T, D_MODEL), F32),
        grid_spec=grid_spec,
        compiler_params=_cparams(("arbitrary",)),
        name="scatter",
    )(offs, x2, wsel, pos, gf, *([ye] * N_EXPERTS))


def _block_diag(w):
    n, a, b = w.shape
    out = jnp.zeros((n * a, n * b), w.dtype)
    for i in range(n):
        out = out.at[i * a:(i + 1) * a, i * b:(i + 1) * b].set(w[i])
    return out


def _rope_tables(S):
    inv = ROPE_THETA ** (-jnp.arange(0, ROT_DIM, 2, dtype=F32) / ROT_DIM)
    ang = jnp.arange(S, dtype=F32)[:, None] * inv[None, :]
    cos, sin = jnp.cos(ang), jnp.sin(ang)
    half = ROT_DIM // 2
    ones, zeros = jnp.ones((S, ATT_HD - ROT_DIM), F32), jnp.zeros((S, ATT_HD - ROT_DIM), F32)
    zh = jnp.zeros((S, half), F32)
    c = jnp.concatenate([cos, cos, ones], axis=1)
    s1 = jnp.concatenate([-sin, zh, zeros], axis=1)
    s2 = jnp.concatenate([zh, sin, zeros], axis=1)
    return tuple(jnp.tile(t, (1, 2)) for t in (c, s1, s2))


def _expanders():
    eg = np.zeros((2, 128, GROUP_W), np.float32)
    eb = np.zeros((2, 128, GROUP_W), np.float32)
    for d in range(2):
        for h in range(DN_HEADS):
            eg[d, d * DN_HEADS + h, h * DN_DK:(h + 1) * DN_DK] = 1.0
            eb[d, 2 * DN_HEADS + d * DN_HEADS + h, h * DN_DK:(h + 1) * DN_DK] = 1.0
    hd = np.arange(GROUP_W) // DN_DK
    bd = (hd[:, None] == hd[None, :]).astype(np.float32)
    return jnp.asarray(eg), jnp.asarray(eb), jnp.asarray(bd)


def _layer_params(l, norm1_g, w_in, conv_a_w, conv_a_b, rg_wa, rg_ba, rg_wx, rg_bx, rg_lambda, pool_w, pool_scale,
                  dn_conv_w, dn_A_log, dn_dt_bias, dn_norm_g, mix_norm_g, w_out, norm2_g, router_w,
                  exp_w_gate, exp_w_up, exp_w_down):
    w = w_in[l]
    w_cat = jnp.concatenate([w[:, 0:1536], w[:, 1552:2576], w[:, 1536:1552],
                             jnp.zeros((D_MODEL, COL_END - COL_CAB - 16), F32)], axis=1).astype(BF16)
    pad8 = lambda v: jnp.concatenate([v.reshape(1, 2 * DN_HEADS), jnp.zeros((1, 120), F32)], axis=1)
    return dict(
        g1=norm1_g[l].reshape(1, D_MODEL), w_cat=w_cat,
        conv_a_w=conv_a_w[l], conv_a_b=conv_a_b[l].reshape(1, GROUP_W),
        wa=[_block_diag(rg_wa[l, d]).astype(BF16) for d in range(2)],
        wx=[_block_diag(rg_wx[l, d]).astype(BF16) for d in range(2)],
        ba=[rg_ba[l, d].reshape(1, GROUP_W) for d in range(2)],
        bx=[rg_bx[l, d].reshape(1, GROUP_W) for d in range(2)],
        lam=[rg_lambda[l, d].reshape(1, GROUP_W) for d in range(2)],
        pool_w=_block_diag(pool_w[l]).astype(BF16), pool_scale=pool_scale[l].reshape(1, GROUP_W),
        dn_conv_w=dn_conv_w[l], alog=pad8(dn_A_log[l]), dtb=pad8(dn_dt_bias[l]),
        dng=jnp.tile(dn_norm_g[l], DN_HEADS).reshape(1, GROUP_W),
        mg=mix_norm_g[l].reshape(1, D_MODEL), wo=w_out[l].astype(BF16),
        g2=norm2_g[l].reshape(1, D_MODEL),
        rw=jnp.concatenate(_split(router_w[l].T, 2), axis=0),
        wg=exp_w_gate, wu=exp_w_up, wd=exp_w_down,
    )


def _chunk_rows(gcn, d):
    B, S, _ = gcn.shape
    n = S // DN_CHUNK
    g = gcn[:, :, d * DN_HEADS:(d + 1) * DN_HEADS].reshape(B, n, DN_CHUNK, DN_HEADS)
    return g.transpose(0, 1, 3, 2).reshape(B, n, 1, GROUP_W)


def _capacity(T):
    return max(1, EC_CAPACITY * T // N_EXPERTS)


def _mix_and_route(xt, B, S, p, consts, ropes):
    T = B * S
    cap = _capacity(T)
    eg, eb, bdmask = consts
    pa, pb, pq, pg, pd, pcab = _inproj(xt, p["g1"], p["w_cat"], *ropes, S)
    pa3 = pa.reshape(B, S, 2 * GROUP_W)
    rg = lambda d: (p["conv_a_w"], p["conv_a_b"], p["wa"][d], p["ba"][d], p["wx"][d], p["bx"][d], p["lam"][d])
    hf = _rglru(pa3, None, *rg(0), reverse=False)
    ya = _rglru(pa3, hf, *rg(1), reverse=True)
    yb = _pool(pb.reshape(B, S, GROUP_W), p["pool_w"], p["pool_scale"])
    qn, kn, vv, gcf, gcb, bef, beb, gcn = _dnprep(pq.reshape(B, S, 3 * GROUP_W), pcab.reshape(B, S, 128),
                                                  p["dn_conv_w"], p["alog"], p["dtb"], eg, eb)
    of, ob = _delta(qn, kn, vv, gcf, gcb, bef, beb, _chunk_rows(gcn, 0), _chunk_rows(gcn, 1), bdmask)
    yd = _attention(pd.reshape(B, S, 3 * GROUP_W))
    flat = lambda a: a.reshape(T, GROUP_W)
    x2, h2, aff_t = _outproj(flat(ya), flat(yb), flat(of), flat(ob), pg, flat(yd), xt,
                             p["dng"], p["mg"], p["wo"], p["g2"], p["rw"])
    mask, pos, wsel, offs3 = _route(aff_t, cap)
    offs = jnp.concatenate([offs3[:, :, 0].reshape(-1), jnp.full((N_EXPERTS,), cap, I32)])
    xe = _gather(h2, mask.reshape(N_EXPERTS, 1, T), pos.reshape(N_EXPERTS, 1, T), offs, cap)
    return dict(x2=x2, wsel=wsel, pos=pos, offs=offs, xe=xe, cap=cap)


def kernel(x_prompt, x_sample, norm1_g, w_in, conv_a_w, conv_a_b, rg_wa, rg_ba, rg_wx, rg_bx, rg_lambda, pool_w, pool_scale, dn_conv_w, dn_A_log, dn_dt_bias, dn_norm_g, mix_norm_g, w_out, norm2_g, router_w, exp_w_gate, exp_w_up, exp_w_down, final_norm_g):
    layers = [_layer_params(l, norm1_g, w_in, conv_a_w, conv_a_b, rg_wa, rg_ba, rg_wx, rg_bx, rg_lambda, pool_w,
                            pool_scale, dn_conv_w, dn_A_log, dn_dt_bias, dn_norm_g, mix_norm_g, w_out, norm2_g,
                            router_w, exp_w_gate, exp_w_up, exp_w_down) for l in range(DEPTH)]
    consts = _expanders()
    final_g = final_norm_g.reshape(1, D_MODEL)
    batches = (x_prompt, x_sample)
    shapes = [x.shape for x in batches]
    ropes = [_rope_tables(S) for _, S, _ in shapes]
    xts = [x.reshape(B * S, D) for x, (B, S, D) in zip(batches, shapes)]
    for l, p in enumerate(layers):
        st = [_mix_and_route(xt, B, S, p, consts, rp) for xt, (B, S, _), rp in zip(xts, shapes, ropes)]
        yes = _ffn([s["xe"] for s in st], p["wg"], p["wu"], p["wd"], l, [s["cap"] for s in st])
        xts = [_scatter(s["x2"], s["wsel"], s["pos"], ye, s["offs"], final_g, s["cap"], final=(l == DEPTH - 1))
               for s, ye in zip(st, yes)]
    return tuple(xt.reshape(shape) for xt, shape in zip(xts, shapes))
```
